```python
import math
import jax, jax.numpy as jnp
from jax import lax
import numpy as np

D_MODEL = 1024
BATCH = 32
SEQ = 256
DEPTH = 2
DEC_BATCH = 2
DEC_SEQ = 4096
PAST_LEN = 512

GRID_W = 64
EPS = 1e-6
N_BRANCH = 3
HY_DIM = 512
HY_CONV = 3
HY_EMB = 33
HY_BANDS = (HY_EMB - 1) // 2
HY_FF = 64
SSD_INNER = 512
SSD_HEADDIM = 64
SSD_HEADS = SSD_INNER // SSD_HEADDIM
SSD_GROUPS = 2
SSD_STATE = 128
SSD_CONV = 3
SSD_CHUNK = 128
SSD_CONV_CH = SSD_INNER + 2 * SSD_GROUPS * SSD_STATE
N_HEADS = 8
N_KV_HEADS = 2
Q_PER_KV = N_HEADS // N_KV_HEADS
HEAD_DIM = 64
ATT_Q = N_HEADS * HEAD_DIM
ATT_KV = N_KV_HEADS * HEAD_DIM
ROPE_THETA = 10000.0
Q_BLOCK = 128
IN_SIZES = (3 * HY_DIM, SSD_INNER, SSD_CONV_CH, 2 * SSD_HEADS, ATT_Q, ATT_KV, ATT_KV)
IN_COLS = sum(IN_SIZES)
N_EXPERTS = 32
TOP_K = 4
MOE_FF = 1024
SWIGLU_ALPHA = 1.702
SWIGLU_LIMIT = 7.0

kernel_name = "hybrid_diffusion_trunk_step"

F32 = jnp.float32


def rms_norm(x, w):
    xf = x.astype(F32)
    y = xf * lax.rsqrt(jnp.mean(xf * xf, axis=-1, keepdims=True) + EPS)
    return (y * w.astype(F32)).astype(x.dtype)


def short_conv(u, w, b):
    width = w.shape[0]
    left = width // 2
    seq_len = u.shape[1]
    up = jnp.pad(u, ((0, 0), (left, width - 1 - left), (0, 0)))
    out = b
    for i in range(width):
        out = out + up[:, i:i + seq_len] * w[i]
    return out


def _time_flip(t, rev):
    return t[:, ::-1] if rev else t


def hyena_filter(seq_len, w1, b1, w2, b2, w3, freq, decay):
    t = jnp.linspace(0.0, 1.0, seq_len, dtype=F32)[:, None]
    bands = jnp.linspace(1e-4, HY_BANDS - 1, HY_BANDS, dtype=F32)[None, :]
    ang = (2.0 * math.pi / seq_len) * jnp.arange(seq_len, dtype=F32)[:, None] * bands
    z = jnp.concatenate([t, jnp.cos(ang), -jnp.sin(ang)], axis=-1)
    freq = freq.astype(F32)
    hid = jnp.sin(freq * (z @ w1.astype(F32) + b1.astype(F32)))
    hid = jnp.sin(freq * (hid @ w2.astype(F32) + b2.astype(F32)))
    filt = hid @ w3.astype(F32)
    return filt * jnp.exp(-t * jnp.abs(decay.astype(F32)))


def bidir_long_conv(u, filt, bias):
    seq_len, ch = u.shape[1], u.shape[2]
    h_fwd, h_bwd = filt[:, :ch], filt[:, ch:]
    k = jnp.concatenate([h_fwd, jnp.zeros((1, ch), F32), h_bwd[1:][::-1]], axis=0)
    uf = u.astype(F32)
    kf = jnp.fft.rfft(k, n=2 * seq_len, axis=0)
    yf = jnp.fft.rfft(uf, n=2 * seq_len, axis=1) * kf[None]
    y = jnp.fft.irfft(yf, n=2 * seq_len, axis=1)[:, :seq_len]
    return (y + uf * bias.astype(F32)).astype(u.dtype)


def hyena_mixer(u_proj, p):
    u = short_conv(u_proj, p["hy_conv_w"], p["hy_conv_b"])
    x0, x1, v = jnp.split(u, 3, axis=-1)
    filt = hyena_filter(u.shape[1], p["hy_w1"], p["hy_b1"], p["hy_w2"], p["hy_b2"],
                        p["hy_w3"], p["hy_freq"], p["hy_decay"])
    return x0 * bidir_long_conv(x1 * v, filt, p["hy_bias"])


def segsum(a):
    cs = jnp.cumsum(a, axis=-1)
    diff = cs[..., :, None] - cs[..., None, :]
    n = a.shape[-1]
    mask = jnp.tril(jnp.ones((n, n), dtype=bool))
    return jnp.where(mask, diff, -jnp.inf)


def ssd_scan(x, a, b, c, init):
    bsz, seq_len, n_h, hd = x.shape
    n_g, n_s = b.shape[2], b.shape[3]
    n_r = n_h // n_g
    nc, cl = seq_len // SSD_CHUNK, SSD_CHUNK
    x = x.reshape(bsz, nc, cl, n_g, n_r, hd)
    a = a.reshape(bsz, nc, cl, n_g, n_r).transpose(0, 3, 4, 1, 2)
    b = b.reshape(bsz, nc, cl, n_g, n_s)
    c = c.reshape(bsz, nc, cl, n_g, n_s)
    a_cs = jnp.cumsum(a, axis=-1)
    lmat = jnp.exp(segsum(a))
    cb = jnp.einsum("bclgn,bcsgn->bcgls", c, b)
    y_diag = jnp.einsum("bcgls,bgrcls,bcsgrp->bclgrp", cb, lmat, x)
    decay_states = jnp.exp(a_cs[..., -1:] - a_cs)
    states = jnp.einsum("bclgn,bgrcl,bclgrp->bcgrpn", b, decay_states, x)
    init_g = init.reshape(bsz, n_g, n_r, hd, n_s)
    states = jnp.concatenate([init_g[:, None], states], axis=1)
    chunk_a = jnp.pad(a_cs[..., -1], ((0, 0), (0, 0), (0, 0), (1, 0)))
    decay_chunk = jnp.exp(segsum(chunk_a))
    new_states = jnp.einsum("bgrzc,bcgrpn->bzgrpn", decay_chunk, states)
    states, final = new_states[:, :-1], new_states[:, -1]
    y_off = jnp.einsum("bclgn,bcgrpn,bgrcl->bclgrp", c, states, jnp.exp(a_cs))
    y = (y_diag + y_off).reshape(bsz, seq_len, n_h, hd)
    return y, final.reshape(bsz, n_h, hd, n_s)


def ssd_mixer(z, xbc, dt_raw, p, init):
    bsz, seq_len = z.shape[:2]
    xbc = jax.nn.silu(short_conv(xbc, p["ssd_conv_w"], p["ssd_conv_b"]))
    xs, bm, cm = jnp.split(xbc, [SSD_INNER, SSD_INNER + SSD_GROUPS * SSD_STATE], axis=-1)
    xs = xs.reshape(bsz, seq_len, SSD_HEADS, SSD_HEADDIM).astype(F32)
    bm = bm.reshape(bsz, seq_len, SSD_GROUPS, SSD_STATE).astype(F32)
    cm = cm.reshape(bsz, seq_len, SSD_GROUPS, SSD_STATE).astype(F32)
    dt = jax.nn.softplus(dt_raw.reshape(bsz, seq_len, 2, SSD_HEADS).astype(F32)
                         + p["ssd_dt_bias"].astype(F32))
    a_cont = -jnp.exp(p["ssd_a_log"].astype(F32))
    d_skip = p["ssd_d"].astype(F32)
    ys, finals = [], []
    for d in range(2):
        rev = d == 1
        dt_d = dt[:, :, d]
        y_d, fin_d = ssd_scan(_time_flip(xs * dt_d[..., None], rev), _time_flip(a_cont[d] * dt_d, rev),
                              _time_flip(bm, rev), _time_flip(cm, rev), init[:, d].astype(F32))
        ys.append(_time_flip(y_d, rev) + d_skip[d][:, None] * xs)
        finals.append(fin_d)
    y = (ys[0] + ys[1]).reshape(bsz, seq_len, SSD_INNER)
    y = rms_norm(y * jax.nn.silu(z.astype(F32)), p["ssd_norm_w"])
    return y.astype(z.dtype), jnp.stack(finals, axis=1).astype(z.dtype)


def axial_rope_angles(seq_len):
    n_rows = seq_len // GRID_W
    row = jnp.repeat(jnp.arange(n_rows), GRID_W).astype(F32)
    col = jnp.tile(jnp.arange(GRID_W), n_rows).astype(F32)
    n_freq = HEAD_DIM // 4
    inv = ROPE_THETA ** (-jnp.arange(n_freq, dtype=F32) / n_freq)
    ang = jnp.concatenate([row[:, None] * inv, col[:, None] * inv], axis=-1)
    return jnp.cos(ang), jnp.sin(ang)


def apply_rope(x, cos, sin):
    xf = x.astype(F32).reshape(*x.shape[:-1], HEAD_DIM // 2, 2)
    x1, x2 = xf[..., 0], xf[..., 1]
    cs, sn = cos[None, :, None, :], sin[None, :, None, :]
    out = jnp.stack([x1 * cs - x2 * sn, x1 * sn + x2 * cs], axis=-1).reshape(x.shape)
    return out.astype(x.dtype)


def block_attention(q, k, v):
    bsz, lq = q.shape[:2]
    nb = lq // Q_BLOCK
    qb = q.reshape(bsz, nb, Q_BLOCK, N_KV_HEADS, Q_PER_KV, HEAD_DIM).transpose(1, 0, 2, 3, 4, 5)
    kf, vf = k.astype(F32), v.astype(F32)
    scale = HEAD_DIM ** -0.5

    def one_block(q_blk):
        s = jnp.einsum("bqkgd,bskd->bkgqs", q_blk.astype(F32), kf) * scale
        pr = jax.nn.softmax(s, axis=-1)
        return jnp.einsum("bkgqs,bskd->bqkgd", pr, vf).astype(q.dtype)

    out = lax.map(one_block, qb)
    return out.transpose(1, 0, 2, 3, 4, 5).reshape(bsz, lq, ATT_Q)


def attention_mixer(q, k, v, p, rope, ctx_kv):
    bsz, seq_len = q.shape[:2]
    q = rms_norm(q.reshape(bsz, seq_len, N_HEADS, HEAD_DIM), p["q_norm_w"])
    k = rms_norm(k.reshape(bsz, seq_len, N_KV_HEADS, HEAD_DIM), p["k_norm_w"])
    v = v.reshape(bsz, seq_len, N_KV_HEADS, HEAD_DIM)
    if rope is not None:
        q = apply_rope(q, rope[0], rope[1])
        k = apply_rope(k, rope[0], rope[1])
    k_all, v_all = k, v
    if ctx_kv is not None:
        k_all = jnp.concatenate([k, ctx_kv[0].astype(k.dtype)], axis=1)
        v_all = jnp.concatenate([v, ctx_kv[1].astype(v.dtype)], axis=1)
    out = block_attention(q.reshape(bsz, seq_len, N_KV_HEADS, Q_PER_KV, HEAD_DIM), k_all, v_all)
    return out, k, v


def token_mixing(h, p, rope, ctx_kv, ssd_init):
    proj = h @ p["w_in"]
    cuts = np.cumsum(IN_SIZES)[:-1].tolist()
    hy_u, ssd_z, ssd_xbc, ssd_dt, q, k, v = jnp.split(proj, cuts, axis=-1)
    y_hy = hyena_mixer(hy_u, p)
    y_ssd, ssd_final = ssd_mixer(ssd_z, ssd_xbc, ssd_dt, p, ssd_init)
    y_att, k_c, v_c = attention_mixer(q, k, v, p, rope, ctx_kv)
    gates = jax.nn.sigmoid((h @ p["w_gate"] + p["b_gate"]).astype(F32)).astype(h.dtype)
    g_hy, g_ssd, g_att = jnp.split(gates, N_BRANCH, axis=-1)
    merged = (g_hy * (y_hy @ p["w_br_hy"]) + g_ssd * (y_ssd @ p["w_br_ssd"])
              + g_att * (y_att @ p["w_br_att"]))
    return merged @ p["w_out"], k_c, v_c, ssd_final


def moe(h, p):
    bsz, seq_len, dm = h.shape
    tok = h.reshape(-1, dm)
    logits = (tok @ p["w_router"] + p["b_router"]).astype(F32)
    top_v, top_i = lax.top_k(logits, TOP_K)
    top_w = jax.nn.softmax(top_v, axis=-1)
    combine = jnp.sum(jax.nn.one_hot(top_i, N_EXPERTS, dtype=F32) * top_w[..., None], axis=1)

    def expert_step(acc, ep):
        w1, b1, w2, b2, cw = ep
        gu = tok @ w1 + b1
        gate, up = gu[..., ::2], gu[..., 1::2]
        gate = jnp.minimum(gate, SWIGLU_LIMIT)
        up = jnp.clip(up, -SWIGLU_LIMIT, SWIGLU_LIMIT)
        glu = gate * jax.nn.sigmoid(SWIGLU_ALPHA * gate)
        out = ((up + 1.0) * glu) @ w2 + b2
        return acc + cw[:, None].astype(tok.dtype) * out, None

    acc, _ = lax.scan(expert_step, jnp.zeros_like(tok),
                      (p["w_e1"], p["b_e1"], p["w_e2"], p["b_e2"], combine.T))
    return acc.reshape(bsz, seq_len, dm)


def modulation(cvec, w_mod, b_mod):
    m = jax.nn.silu(cvec) @ w_mod + b_mod
    return jnp.split(m[..., None, :], 6, axis=-1)


def trunk_layer(x, mods, p, rope, ctx_kv, ssd_init):
    sh1, sc1, g1, sh2, sc2, g2 = mods
    h = rms_norm(x, p["norm1_w"]) * (1.0 + sc1) + sh1
    mix, k_c, v_c, ssd_final = token_mixing(h, p, rope, ctx_kv, ssd_init)
    x = x + g1 * mix
    h = rms_norm(x, p["norm2_w"]) * (1.0 + sc2) + sh2
    x = x + g2 * moe(h, p)
    return x, k_c, v_c, ssd_final


def setup_inputs(seed: int = 0) -> dict:
    key = jax.random.key(seed)
    ks = iter(jax.random.split(key, 64))

    def nrm(shape, scale=1.0):
        return jax.random.normal(next(ks), shape, F32) * scale

    def gain(shape):
        return 1.0 + nrm(shape, 0.05)

    def unif(shape, lo, hi):
        return jax.random.uniform(next(ks), shape, F32, lo, hi)

    dt0 = jnp.exp(unif((DEPTH, 2, SSD_HEADS), math.log(1e-3), math.log(1e-1)))
    return {
        "x_prompt": nrm((BATCH, SEQ, D_MODEL)),
        "x_sample": nrm((DEC_BATCH, DEC_SEQ, D_MODEL)),
        "c": nrm((DEC_BATCH, D_MODEL)),
        "cache_k": nrm((DEC_BATCH, DEPTH, PAST_LEN, N_KV_HEADS, HEAD_DIM)),
        "cache_v": nrm((DEC_BATCH, DEPTH, PAST_LEN, N_KV_HEADS, HEAD_DIM)),
        "state_ssd": nrm((DEC_BATCH, DEPTH, 2, SSD_HEADS, SSD_HEADDIM, SSD_STATE), 0.5),
        "c_ctx": nrm((D_MODEL,)),
        "norm1_w": gain((DEPTH, D_MODEL)),
        "norm2_w": gain((DEPTH, D_MODEL)),
        "w_mod": nrm((DEPTH, D_MODEL, 6 * D_MODEL), 0.5 * D_MODEL ** -0.5),
        "b_mod": nrm((DEPTH, 6 * D_MODEL), 0.02),
        "w_in": nrm((DEPTH, D_MODEL, IN_COLS), D_MODEL ** -0.5),
        "w_gate": nrm((DEPTH, D_MODEL, N_BRANCH * D_MODEL), D_MODEL ** -0.5),
        "b_gate": nrm((DEPTH, N_BRANCH * D_MODEL), 0.02),
        "hy_conv_w": nrm((DEPTH, HY_CONV, 3 * HY_DIM), 0.5),
        "hy_conv_b": nrm((DEPTH, 3 * HY_DIM), 0.02),
        "hy_w1": nrm((DEPTH, HY_EMB, HY_FF), HY_EMB ** -0.5),
        "hy_b1": nrm((DEPTH, HY_FF), 0.02),
        "hy_w2": nrm((DEPTH, HY_FF, HY_FF), HY_FF ** -0.5),
        "hy_b2": nrm((DEPTH, HY_FF), 0.02),
        "hy_w3": nrm((DEPTH, HY_FF, 2 * HY_DIM), 0.05 * HY_FF ** -0.5),
        "hy_freq": 1.0 + nrm((DEPTH, HY_FF), 0.1),
        "hy_decay": unif((DEPTH, 2 * HY_DIM), 3.0, 15.0),
        "hy_bias": nrm((DEPTH, HY_DIM), 0.5),
        "ssd_conv_w": nrm((DEPTH, SSD_CONV, SSD_CONV_CH), 0.5),
        "ssd_conv_b": nrm((DEPTH, SSD_CONV_CH), 0.02),
        "ssd_a_log": jnp.log(unif((DEPTH, 2, SSD_HEADS), 1.0, 16.0)),
        "ssd_dt_bias": dt0 + jnp.log(-jnp.expm1(-dt0)),
        "ssd_d": 1.0 + nrm((DEPTH, 2, SSD_HEADS), 0.1),
        "ssd_norm_w": gain((DEPTH, SSD_INNER)),
        "q_norm_w": gain((DEPTH, HEAD_DIM)),
        "k_norm_w": gain((DEPTH, HEAD_DIM)),
        "w_br_hy": nrm((DEPTH, HY_DIM, D_MODEL), HY_DIM ** -0.5),
        "w_br_ssd": nrm((DEPTH, SSD_INNER, D_MODEL), SSD_INNER ** -0.5),
        "w_br_att": nrm((DEPTH, ATT_Q, D_MODEL), ATT_Q ** -0.5),
        "w_out": nrm((DEPTH, D_MODEL, D_MODEL), D_MODEL ** -0.5),
        "w_router": nrm((DEPTH, D_MODEL, N_EXPERTS), D_MODEL ** -0.5),
        "b_router": nrm((DEPTH, N_EXPERTS), 0.01),
        "w_e1": nrm((DEPTH, N_EXPERTS, D_MODEL, 2 * MOE_FF), D_MODEL ** -0.5),
        "b_e1": nrm((DEPTH, N_EXPERTS, 2 * MOE_FF), 0.01),
        "w_e2": nrm((DEPTH, N_EXPERTS, MOE_FF, D_MODEL), MOE_FF ** -0.5),
        "b_e2": nrm((DEPTH, N_EXPERTS, D_MODEL), 0.01),
    }


def reference(x_prompt, x_sample, c, cache_k, cache_v, state_ssd, c_ctx,
              norm1_w, norm2_w, w_mod, b_mod, w_in, w_gate, b_gate,
              hy_conv_w, hy_conv_b, hy_w1, hy_b1, hy_w2, hy_b2, hy_w3, hy_freq, hy_decay, hy_bias,
              ssd_conv_w, ssd_conv_b, ssd_a_log, ssd_dt_bias, ssd_d, ssd_norm_w,
              q_norm_w, k_norm_w, w_br_hy, w_br_ssd, w_br_att, w_out,
              w_router, b_router, w_e1, b_e1, w_e2, b_e2):
    rope = axial_rope_angles(x_sample.shape[1])
    ctx_init = jnp.zeros((x_prompt.shape[0], 2, SSD_HEADS, SSD_HEADDIM, SSD_STATE), x_prompt.dtype)
    y_prompt, y_sample = x_prompt, x_sample
    new_k, new_v, new_s = [], [], []
    for l in range(DEPTH):
        p = {
            "norm1_w": norm1_w[l], "norm2_w": norm2_w[l], "w_in": w_in[l],
            "w_gate": w_gate[l], "b_gate": b_gate[l],
            "hy_conv_w": hy_conv_w[l], "hy_conv_b": hy_conv_b[l], "hy_w1": hy_w1[l], "hy_b1": hy_b1[l],
            "hy_w2": hy_w2[l], "hy_b2": hy_b2[l], "hy_w3": hy_w3[l], "hy_freq": hy_freq[l],
            "hy_decay": hy_decay[l], "hy_bias": hy_bias[l],
            "ssd_conv_w": ssd_conv_w[l], "ssd_conv_b": ssd_conv_b[l], "ssd_a_log": ssd_a_log[l],
            "ssd_dt_bias": ssd_dt_bias[l], "ssd_d": ssd_d[l], "ssd_norm_w": ssd_norm_w[l],
            "q_norm_w": q_norm_w[l], "k_norm_w": k_norm_w[l],
            "w_br_hy": w_br_hy[l], "w_br_ssd": w_br_ssd[l], "w_br_att": w_br_att[l], "w_out": w_out[l],
            "w_router": w_router[l], "b_router": b_router[l],
            "w_e1": w_e1[l], "b_e1": b_e1[l], "w_e2": w_e2[l], "b_e2": b_e2[l],
        }
        mods_ctx = modulation(c_ctx, w_mod[l], b_mod[l])
        mods_lat = modulation(c, w_mod[l], b_mod[l])
        y_prompt, k_l, v_l, s_l = trunk_layer(y_prompt, mods_ctx, p, None, None, ctx_init)
        y_sample, _, _, _ = trunk_layer(y_sample, mods_lat, p, rope,
                                        (cache_k[:, l], cache_v[:, l]), state_ssd[:, l])
        new_k.append(k_l)
        new_v.append(v_l)
        new_s.append(s_l)
    new_k_all = jnp.stack(new_k, axis=1)
    new_v_all = jnp.stack(new_v, axis=1)
    new_ssd_all = jnp.stack(new_s, axis=1)
    return (y_prompt, y_sample, new_k_all, new_v_all, new_ssd_all)
```

```python
import functools
import math

import numpy as np
import jax
import jax.numpy as jnp
from jax import lax
from jax.experimental import pallas as pl
from jax.experimental.pallas import tpu as pltpu

F32 = jnp.float32
BF16 = jnp.bfloat16
HI = lax.Precision.HIGHEST

EPS = 1e-6
GRID_W = 64
HY_DIM = 512
SSD_INNER = 512
SSD_HEADDIM = 64
SSD_HEADS = 8
SSD_GROUPS = 2
SSD_STATE = 128
SSD_CHUNK = 128
N_HEADS = 8
N_KV_HEADS = 2
HEAD_DIM = 64
ROPE_THETA = 10000.0
N_EXPERTS = 32
TOP_K = 4
SWIGLU_ALPHA = 1.702
SWIGLU_LIMIT = 7.0

P_HY, P_Z, P_XBC, P_Q, P_K, P_V, P_DT, P_COLS = 0, 1536, 2048, 3072, 3584, 3712, 3840, 3968

VMEM_LIMIT = 56 * 1024 * 1024
LANES = 128
ROW_TILE = 256
MOE_TILE = 512
DFT_N1, DFT_N2 = 64, 128


def _cp(sem, vmem=VMEM_LIMIT):
    return pltpu.CompilerParams(dimension_semantics=sem, vmem_limit_bytes=vmem)


def _sigmoid(x):
    return 1.0 / (1.0 + jnp.exp(-x))


def _silu(x):
    return x * _sigmoid(x)


def _softplus(x):
    return jnp.maximum(x, 0.0) + jnp.log(1.0 + jnp.exp(-jnp.abs(x)))


def _mod_kernel(c_ref, w_ref, b_ref, o_ref):
    s = _silu(c_ref[...])
    o_ref[0] = jnp.dot(s, w_ref[0], precision=HI, preferred_element_type=F32) + b_ref[0]


def modulation_all(cvec, w_mod, b_mod):
    depth, d, n = w_mod.shape
    tn = 1536
    return pl.pallas_call(
        _mod_kernel,
        out_shape=jax.ShapeDtypeStruct((depth, 8, n), F32),
        grid=(depth, n // tn),
        in_specs=[pl.BlockSpec((8, d), lambda l, j: (0, 0)),
                  pl.BlockSpec((1, d, tn), lambda l, j: (l, 0, j)),
                  pl.BlockSpec((1, 1, tn), lambda l, j: (l, 0, j))],
        out_specs=pl.BlockSpec((1, 8, tn), lambda l, j: (l, 0, j)),
        compiler_params=_cp(("arbitrary", "arbitrary")),
        name="modulation",
    )(cvec, w_mod, b_mod.reshape(depth, 1, n))


def _mod_row(i, tm, t_ctx, l_lat):
    n_ctx = t_ctx // tm
    per = l_lat // tm
    return jnp.where(i < n_ctx, 0, 1 + (i - n_ctx) // per)


def _mod_spec(k, tm, t_ctx, l_lat):
    return pl.BlockSpec((1, 1, 1024), lambda i: (_mod_row(i, tm, t_ctx, l_lat) * 6 + k, 0, 0))


def _nmm_kernel(x_ref, nw_ref, sh_ref, sc_ref, w_ref, b_ref, o_ref, *, sigmoid):
    x = x_ref[...]
    ms = jnp.mean(x * x, axis=-1, keepdims=True)
    h = x * lax.rsqrt(ms + EPS) * nw_ref[...]
    h = h * (1.0 + sc_ref[0]) + sh_ref[0]
    acc = jnp.dot(h.astype(BF16), w_ref[...], preferred_element_type=F32) + b_ref[...]
    if sigmoid:
        acc = _sigmoid(acc)
    o_ref[...] = acc.astype(o_ref.dtype)


def norm_mod_matmul(x, nw, mods, w, b, *, t_ctx, l_lat, sigmoid, out_dtype, tm=256):
    t, d = x.shape
    n = w.shape[1]
    return pl.pallas_call(
        functools.partial(_nmm_kernel, sigmoid=sigmoid),
        out_shape=jax.ShapeDtypeStruct((t, n), out_dtype),
        grid=(t // tm,),
        in_specs=[pl.BlockSpec((tm, d), lambda i: (i, 0)),
                  pl.BlockSpec((1, d), lambda i: (0, 0)),
                  _mod_spec(0, tm, t_ctx, l_lat),
                  _mod_spec(1, tm, t_ctx, l_lat),
                  pl.BlockSpec((d, n), lambda i: (0, 0)),
                  pl.BlockSpec((1, n), lambda i: (0, 0))],
        out_specs=pl.BlockSpec((tm, n), lambda i: (i, 0)),
        compiler_params=_cp(("arbitrary",)),
        name="norm_mod_matmul",
    )(x, nw, mods, mods, w, b)


def _seq_edges(i, tr, t_ctx, l_ctx, l_lat):
    tok = i * tr
    pos = jnp.where(tok < t_ctx, tok % l_ctx, (tok - t_ctx) % l_lat)
    length = jnp.where(tok < t_ctx, l_ctx, l_lat)
    return pos == 0, pos + tr == length


def _conv3(x, prev8, next8, w_ref, b_ref, first, last):
    tr = x.shape[0]
    row = lax.broadcasted_iota(jnp.int32, x.shape, 0)
    pm = jnp.where(first, 0.0, 1.0)
    nm = jnp.where(last, 0.0, 1.0)
    xm1 = jnp.where(row == 0, prev8[7:8, :] * pm, pltpu.roll(x, 1, axis=0))
    xp1 = jnp.where(row == tr - 1, next8[0:1, :] * nm, pltpu.roll(x, tr - 1, axis=0))
    return b_ref[...] + xm1 * w_ref[0:1, :] + x * w_ref[1:2, :] + xp1 * w_ref[2:3, :]


def _conv_specs(tr, tc, col_blk, n_rows):
    r8 = tr // 8
    last8 = n_rows // 8 - 1
    return [pl.BlockSpec((tr, tc), lambda i, j: (i, col_blk(j))),
            pl.BlockSpec((8, tc), lambda i, j: (jnp.maximum(i * r8 - 1, 0), col_blk(j))),
            pl.BlockSpec((8, tc), lambda i, j: (jnp.minimum((i + 1) * r8, last8), col_blk(j)))]


def _ssd_conv_kernel(x_ref, p_ref, n_ref, w_ref, b_ref, o_ref, *, tr, t_ctx, l_ctx, l_lat):
    first, last = _seq_edges(pl.program_id(0), tr, t_ctx, l_ctx, l_lat)
    o_ref[...] = _silu(_conv3(x_ref[...], p_ref[...], n_ref[...], w_ref, b_ref, first, last))


def ssd_conv(proj, w, b, *, t_ctx, l_ctx, l_lat):
    t = proj.shape[0]
    tr, tc = ROW_TILE, 512
    c = w.shape[1]
    off = P_XBC // tc
    return pl.pallas_call(
        functools.partial(_ssd_conv_kernel, tr=tr, t_ctx=t_ctx, l_ctx=l_ctx, l_lat=l_lat),
        out_shape=jax.ShapeDtypeStruct((t, c), F32),
        grid=(t // tr, c // tc),
        in_specs=_conv_specs(tr, tc, lambda j: off + j, t) + [
            pl.BlockSpec((3, tc), lambda i, j: (0, j)),
            pl.BlockSpec((1, tc), lambda i, j: (0, j))],
        out_specs=pl.BlockSpec((tr, tc), lambda i, j: (i, j)),
        compiler_params=_cp(("arbitrary", "arbitrary")),
        name="ssd_conv",
    )(proj, proj, proj, w, b.reshape(1, c))


def _hy_pre_kernel(*refs, tr, t_ctx, l_ctx, l_lat):
    (x0, x0p, x0n, x1, x1p, x1n, xv, xvp, xvn, w0, w1, wv, b0, b1, bv, o0_ref, op_ref) = refs
    first, last = _seq_edges(pl.program_id(0), tr, t_ctx, l_ctx, l_lat)
    o0_ref[...] = _conv3(x0[...], x0p[...], x0n[...], w0, b0, first, last)
    u1 = _conv3(x1[...], x1p[...], x1n[...], w1, b1, first, last)
    uv = _conv3(xv[...], xvp[...], xvn[...], wv, bv, first, last)
    op_ref[...] = u1 * uv


def hyena_pre(proj, w, b, *, t_ctx, l_ctx, l_lat):
    t = proj.shape[0]
    tr, tc = ROW_TILE, HY_DIM
    nb = HY_DIM // tc
    b2 = b.reshape(1, 3 * HY_DIM)
    specs = []
    for s in range(3):
        specs += _conv_specs(tr, tc, lambda j, s=s: P_HY // tc + s * nb + j, t)
    specs += [pl.BlockSpec((3, tc), lambda i, j, s=s: (0, s * nb + j)) for s in range(3)]
    specs += [pl.BlockSpec((1, tc), lambda i, j, s=s: (0, s * nb + j)) for s in range(3)]
    return pl.pallas_call(
        functools.partial(_hy_pre_kernel, tr=tr, t_ctx=t_ctx, l_ctx=l_ctx, l_lat=l_lat),
        out_shape=(jax.ShapeDtypeStruct((t, HY_DIM), F32), jax.ShapeDtypeStruct((t, HY_DIM), F32)),
        grid=(t // tr, nb),
        in_specs=specs,
        out_specs=(pl.BlockSpec((tr, tc), lambda i, j: (i, j)), pl.BlockSpec((tr, tc), lambda i, j: (i, j))),
        compiler_params=_cp(("arbitrary", "arbitrary")),
        name="hyena_pre",
    )(*([proj] * 9), w, w, w, b2, b2, b2)


def _filter_kernel(z_ref, w1_ref, b1_ref, w2_ref, b2_ref, w3_ref, fr_ref, dec_ref, o_ref):
    z = z_ref[...]
    fr = fr_ref[...]
    h = jnp.sin(fr * (jnp.dot(z, w1_ref[...], precision=HI, preferred_element_type=F32) + b1_ref[...]))
    h = jnp.sin(fr * (jnp.dot(h, w2_ref[...], precision=HI, preferred_element_type=F32) + b2_ref[...]))
    f = jnp.dot(h, w3_ref[...], precision=HI, preferred_element_type=F32)
    o_ref[...] = f * jnp.exp(-z[:, 0:1] * jnp.abs(dec_ref[...]))


def _filter_embedding(seq_len, emb):
    bands_n = (emb - 1) // 2
    t = jnp.linspace(0.0, 1.0, seq_len, dtype=F32)[:, None]
    bands = jnp.linspace(1e-4, bands_n - 1, bands_n, dtype=F32)[None, :]
    ang = (2.0 * math.pi / seq_len) * jnp.arange(seq_len, dtype=F32)[:, None] * bands
    z = jnp.concatenate([t, jnp.cos(ang), -jnp.sin(ang)], axis=-1)
    return jnp.pad(z, ((0, 0), (0, LANES - emb)))


def hyena_filter(seq_len, w1, b1, w2, b2, w3, freq, decay):
    emb, ff = w1.shape
    n = w3.shape[1]
    z = _filter_embedding(seq_len, emb)
    padc = LANES - ff
    w1p = jnp.pad(w1, ((0, LANES - emb), (0, padc)))
    w2p = jnp.pad(w2, ((0, padc), (0, padc)))
    w3p = jnp.pad(w3, ((0, padc), (0, 0)))
    row = lambda v: jnp.pad(v, (0, padc)).reshape(1, LANES)
    tr = 256
    full = lambda shape: pl.BlockSpec(shape, lambda i: (0, 0))
    return pl.pallas_call(
        _filter_kernel,
        out_shape=jax.ShapeDtypeStruct((seq_len, n), F32),
        grid=(seq_len // tr,),
        in_specs=[pl.BlockSpec((tr, LANES), lambda i: (i, 0)), full((LANES, LANES)), full((1, LANES)),
                  full((LANES, LANES)), full((1, LANES)), full((LANES, n)), full((1, LANES)), full((1, n))],
        out_specs=pl.BlockSpec((tr, n), lambda i: (i, 0)),
        compiler_params=_cp(("arbitrary",)),
        name="hyena_filter",
    )(z, w1p, row(b1), w2p, row(b2), w3p, row(freq), decay.reshape(1, n))


def _circular_filter(filt):
    ch = filt.shape[1] // 2
    h_fwd, h_bwd = filt[:, :ch], filt[:, ch:]
    return jnp.concatenate([h_fwd, jnp.zeros((1, ch), F32), h_bwd[1:][::-1]], axis=0)


def _cs(n_rows, n_cols, period):
    ang = 2.0 * np.pi * (np.outer(np.arange(n_rows), np.arange(n_cols)) % period) / period
    return np.cos(ang), np.sin(ang)


def _mm_kernel(a_ref, b_ref, o_ref):
    o_ref[0] = jnp.dot(a_ref[...], b_ref[0], precision=HI, preferred_element_type=F32)


def const_matmul(a, b, tn):
    m, k = a.shape
    bsz, _, n = b.shape
    return pl.pallas_call(
        _mm_kernel,
        out_shape=jax.ShapeDtypeStruct((bsz, m, n), F32),
        grid=(bsz, n // tn),
        in_specs=[pl.BlockSpec((m, k), lambda s, j: (0, 0)), pl.BlockSpec((1, k, tn), lambda s, j: (s, 0, j))],
        out_specs=pl.BlockSpec((1, m, tn), lambda s, j: (s, 0, j)),
        compiler_params=_cp(("arbitrary", "arbitrary")),
        name="const_matmul",
    )(a, b)


def _hy_ctx_kernel(x0_ref, p_ref, kr_ref, ki_ref, fw_ref, iv_ref, bias_ref, o_ref, *, n):
    p = p_ref[...]
    xf = jnp.dot(fw_ref[...], p, precision=HI, preferred_element_type=F32)
    xr, xi = xf[:n], xf[n:]
    kr, ki = kr_ref[...], ki_ref[...]
    yr = xr * kr - xi * ki
    yi = xr * ki + xi * kr
    y = jnp.dot(iv_ref[...], jnp.concatenate([yr, yi], axis=0), precision=HI, preferred_element_type=F32)
    o_ref[...] = x0_ref[...] * (y + p * bias_ref[...])


def hyena_ctx(x0, p, filt, bias, *, n_seq, seq_len):
    n = 2 * seq_len
    ch = p.shape[1]
    c_full, s_full = _cs(n, n, n)
    fw_full = jnp.asarray(np.concatenate([c_full, -s_full], axis=0), F32)
    fw_half = fw_full[:, :seq_len]
    iv = jnp.asarray(np.concatenate([c_full[:seq_len], -s_full[:seq_len]], axis=1) / n, F32)
    kf = const_matmul(fw_full, _circular_filter(filt)[None], ch)[0]
    kr, ki = kf[:n], kf[n:]
    full = lambda shape: pl.BlockSpec(shape, lambda s: (0, 0))
    return pl.pallas_call(
        functools.partial(_hy_ctx_kernel, n=n),
        out_shape=jax.ShapeDtypeStruct((n_seq * seq_len, ch), F32),
        grid=(n_seq,),
        in_specs=[pl.BlockSpec((seq_len, ch), lambda s: (s, 0)), pl.BlockSpec((seq_len, ch), lambda s: (s, 0)),
                  full((n, ch)), full((n, ch)), full((2 * n, seq_len)), full((seq_len, 2 * n)), full((1, ch))],
        out_specs=pl.BlockSpec((seq_len, ch), lambda s: (s, 0)),
        compiler_params=_cp(("arbitrary",)),
        name="hyena_ctx",
    )(x0, p, kr, ki, fw_half, iv, bias.reshape(1, ch))


def _stage2_kernel(a_ref, twr_ref, twi_ref, m_ref, *rest, conv):
    if conv:
        kf_ref, mi_ref, o_ref = rest
    else:
        (o_ref,) = rest
    ar, ai = a_ref[0, 0, 0], a_ref[0, 1, 0]
    twr, twi = twr_ref[0], twi_ref[0]
    br = ar * twr - ai * twi
    bi = ar * twi + ai * twr
    x = jnp.dot(m_ref[...], jnp.concatenate([br, bi], axis=0), precision=HI, preferred_element_type=F32)
    n2 = ar.shape[0]
    xr, xi = x[:n2], x[n2:]
    if not conv:
        o_ref[0, 0, 0] = xr
        o_ref[0, 1, 0] = xi
        return
    kr, ki = kf_ref[0, 0, 0], kf_ref[0, 1, 0]
    yr = xr * kr - xi * ki
    yi = xr * ki + xi * kr
    pq = jnp.dot(mi_ref[...], jnp.concatenate([yr, yi], axis=0), precision=HI, preferred_element_type=F32)
    pr, pi = pq[:n2], pq[n2:]
    o_ref[0, 0, 0] = pr * twr + pi * twi
    o_ref[0, 1, 0] = pi * twr - pr * twi


def _stage2(a, kf, ch):
    bsz = a.shape[0]
    n1, n2 = DFT_N1, DFT_N2
    n = n1 * n2
    tw_ang = 2.0 * np.pi * np.outer(np.arange(n1), np.arange(n2)) / n
    twr = jnp.asarray(np.cos(tw_ang), F32).reshape(n1, n2, 1)
    twi = jnp.asarray(-np.sin(tw_ang), F32).reshape(n1, n2, 1)
    c2, s2 = _cs(n2, n2, n2)
    m_fwd = jnp.asarray(np.block([[c2, s2], [-s2, c2]]), F32)
    m_inv = jnp.asarray(np.block([[c2, -s2], [s2, c2]]), F32)
    conv = kf is not None
    blk = pl.BlockSpec((1, 2, 1, n2, ch), lambda s, k: (s, 0, k, 0, 0))
    specs = [blk, pl.BlockSpec((1, n2, 1), lambda s, k: (k, 0, 0)), pl.BlockSpec((1, n2, 1), lambda s, k: (k, 0, 0)),
             pl.BlockSpec((2 * n2, 2 * n2), lambda s, k: (0, 0))]
    args = [a, twr, twi, m_fwd]
    if conv:
        specs += [pl.BlockSpec((1, 2, 1, n2, ch), lambda s, k: (0, 0, k, 0, 0)),
                  pl.BlockSpec((2 * n2, 2 * n2), lambda s, k: (0, 0))]
        args += [kf, m_inv]
    return pl.pallas_call(
        functools.partial(_stage2_kernel, conv=conv),
        out_shape=jax.ShapeDtypeStruct(a.shape, F32),
        grid=(bsz, n1),
        in_specs=specs,
        out_specs=blk,
        compiler_params=_cp(("arbitrary", "arbitrary")),
        name="hyena_stage2",
    )(*args)


def _hy_post_kernel(g_ref, q_ref, x0_ref, p_ref, bias_ref, o_ref):
    y = jnp.dot(g_ref[...], q_ref[0], precision=HI, preferred_element_type=F32)
    o_ref[0] = x0_ref[0] * (y + p_ref[0] * bias_ref[...])


def hyena_lat(x0, p, filt, bias, *, n_seq, seq_len):
    n1, n2 = DFT_N1, DFT_N2
    n = n1 * n2
    assert n == 2 * seq_len
    ch = p.shape[1]
    h1 = n1 // 2
    wide = n2 * ch
    c1, s1 = _cs(n1, n1, n1)
    f1_full = jnp.asarray(np.concatenate([c1, -s1], axis=0), F32)
    f1_half = f1_full[:, :h1]
    g1 = jnp.asarray(np.concatenate([c1[:h1], -s1[:h1]], axis=1) / n, F32)
    tn = 8192
    kcirc = _circular_filter(filt).reshape(1, n1, wide)
    kf = _stage2(const_matmul(f1_full, kcirc, tn).reshape(1, 2, n1, n2, ch), None, ch)
    a = const_matmul(f1_half, p.reshape(n_seq, h1, wide), tn).reshape(n_seq, 2, n1, n2, ch)
    q = _stage2(a, kf, ch).reshape(n_seq, 2 * n1, wide)
    bias_w = jnp.tile(bias, n2).reshape(1, wide)
    out = pl.pallas_call(
        _hy_post_kernel,
        out_shape=jax.ShapeDtypeStruct((n_seq, h1, wide), F32),
        grid=(n_seq, wide // tn),
        in_specs=[pl.BlockSpec((h1, 2 * n1), lambda s, j: (0, 0)),
                  pl.BlockSpec((1, 2 * n1, tn), lambda s, j: (s, 0, j)),
                  pl.BlockSpec((1, h1, tn), lambda s, j: (s, 0, j)),
                  pl.BlockSpec((1, h1, tn), lambda s, j: (s, 0, j)),
                  pl.BlockSpec((1, tn), lambda s, j: (0, j))],
        out_specs=pl.BlockSpec((1, h1, tn), lambda s, j: (s, 0, j)),
        compiler_params=_cp(("arbitrary", "arbitrary")),
        name="hyena_post",
    )(g1, q, x0.reshape(n_seq, h1, wide), p.reshape(n_seq, h1, wide), bias_w)
    return out.reshape(n_seq * seq_len, ch)


def _ssd_dir(xbc, dtraw, st_ref, d, consts, y_ref):
    tri, expand, dtb, acont, dskip = consts
    cl = xbc.shape[0]
    xs = xbc[:, :SSD_INNER]
    dt = _softplus(dtraw + dtb)
    a = dt * acont
    tri_d = tri if d == 0 else tri.T
    cs = jnp.dot(tri_d, a, precision=HI, preferred_element_type=F32)
    cs_t = jnp.dot(a.T, tri_d.T, precision=HI, preferred_element_type=F32)
    dt_x = jnp.dot(dt, expand[d], precision=HI, preferred_element_type=F32)
    xdt = xs * dt_x
    row = lax.broadcasted_iota(jnp.int32, (cl, cl), 0)
    col = lax.broadcasted_iota(jnp.int32, (cl, cl), 1)
    keep = (col <= row) if d == 0 else (col >= row)
    lane = lax.broadcasted_iota(jnp.int32, (cl, LANES), 1)
    low = lane < SSD_HEADDIM
    edge = cl - 1 if d == 0 else 0
    for g in range(SSD_GROUPS):
        bg = xbc[:, SSD_INNER + g * SSD_STATE: SSD_INNER + (g + 1) * SSD_STATE]
        cg = xbc[:, SSD_INNER + (SSD_GROUPS + g) * SSD_STATE: SSD_INNER + (SSD_GROUPS + g + 1) * SSD_STATE]
        bg16, cg16 = bg.astype(BF16), cg.astype(BF16)
        cb = lax.dot_general(cg16, bg16, (((1,), (1,)), ((), ())), preferred_element_type=F32)
        for pr in range(2):
            pair = g * 2 + pr
            h0 = 2 * pair
            ms = []
            for h in (h0, h0 + 1):
                ln = d * SSD_HEADS + h
                diff = cs[:, ln:ln + 1] - cs_t[ln:ln + 1, :]
                ms.append(jnp.where(keep, cb * jnp.exp(jnp.minimum(diff, 0.0)), 0.0).astype(BF16))
            xp = xdt[:, pair * LANES:(pair + 1) * LANES]
            xs_p = xs[:, pair * LANES:(pair + 1) * LANES]
            xlo = jnp.where(low, xp, 0.0).astype(BF16)
            xhi = jnp.where(low, 0.0, xp).astype(BF16)
            y_diag = (jnp.dot(ms[0], xlo, preferred_element_type=F32)
                      + jnp.dot(ms[1], xhi, preferred_element_type=F32))
            l0, l1 = d * SSD_HEADS + h0, d * SSD_HEADS + h0 + 1
            e_cs = jnp.where(low, jnp.exp(cs[:, l0:l0 + 1]), jnp.exp(cs[:, l1:l1 + 1]))
            st = st_ref[pair]
            y_off = jnp.dot(cg16, st.astype(BF16), preferred_element_type=F32) * e_cs
            y_ref[:, pair * LANES:(pair + 1) * LANES] = (
                y_diag + y_off + xs_p * dskip[d:d + 1, pair * LANES:(pair + 1) * LANES])
            tot0, tot1 = cs[edge:edge + 1, l0:l0 + 1], cs[edge:edge + 1, l1:l1 + 1]
            dec = jnp.where(low, jnp.exp(tot0 - cs[:, l0:l0 + 1]), jnp.exp(tot1 - cs[:, l1:l1 + 1]))
            upd = lax.dot_general(bg16, (xp * dec).astype(BF16), (((0,), (0,)), ((), ())),
                                  preferred_element_type=F32)
            st_ref[pair] = st * jnp.where(low[0:1], jnp.exp(tot0), jnp.exp(tot1)) + upd


def _ssd_kernel(xf_ref, dtf_ref, xb_ref, dtb_ref, init_ref, tri_ref, exp_ref, dtbias_ref, acont_ref, dskip_ref,
                yf_ref, yb_ref, fin_ref, st_ref, *, has_init):
    s = pl.program_id(1)

    @pl.when(s == 0)
    def _():
        if has_init:
            st_ref[...] = init_ref[0]
        else:
            st_ref[...] = jnp.zeros_like(st_ref)

    consts = (tri_ref[...], (exp_ref[0], exp_ref[1]), dtbias_ref[...], acont_ref[...], dskip_ref[...])
    _ssd_dir(xf_ref[...], dtf_ref[...], st_ref.at[0], 0, consts, yf_ref)
    _ssd_dir(xb_ref[...], dtb_ref[...], st_ref.at[1], 1, consts, yb_ref)

    @pl.when(s == pl.num_programs(1) - 1)
    def _():
        fin_ref[0] = st_ref[...]


def _pair_states(s):
    b = s.shape[0]
    s = s.reshape(b, 2, SSD_HEADS // 2, 2, SSD_HEADDIM, SSD_STATE)
    return s.transpose(0, 1, 2, 5, 3, 4).reshape(b, 2, SSD_HEADS // 2, SSD_STATE, 2 * SSD_HEADDIM)


def _unpair_states(s):
    b = s.shape[0]
    s = s.reshape(b, 2, SSD_HEADS // 2, SSD_STATE, 2, SSD_HEADDIM)
    return s.transpose(0, 1, 2, 4, 5, 3).reshape(b, 2, SSD_HEADS, SSD_HEADDIM, SSD_STATE)


def ssd_scan(xbc, proj, init, dt_bias, a_log, d_skip, *, row0, n_seq, seq_len):
    cl = SSD_CHUNK
    nc = seq_len // cl
    base = row0 // cl
    has_init = init is not None
    hp = SSD_HEADS // 2
    init_p = _pair_states(init) if has_init else jnp.zeros((1, 2, hp, SSD_STATE, LANES), F32)
    tri = jnp.asarray(np.tril(np.ones((cl, cl))), F32)
    expand = np.zeros((2, LANES, SSD_INNER), np.float32)
    for d in range(2):
        for h in range(SSD_HEADS):
            expand[d, d * SSD_HEADS + h, h * SSD_HEADDIM:(h + 1) * SSD_HEADDIM] = 1.0
    pad16 = lambda v: jnp.pad(v.reshape(1, 2 * SSD_HEADS), ((0, 0), (0, LANES - 2 * SSD_HEADS)))
    acont = pad16(-jnp.exp(a_log))
    dtb = pad16(dt_bias)
    dskip = jnp.repeat(d_skip, SSD_HEADDIM, axis=1)
    cxbc = xbc.shape[1]
    dtblk = P_DT // LANES
    full2 = lambda shape: pl.BlockSpec(shape, lambda b, s: (0,) * len(shape))
    fwd = lambda b, s: base + b * nc + s
    bwd = lambda b, s: base + b * nc + (nc - 1 - s)
    st_spec = pl.BlockSpec((1, 2, hp, SSD_STATE, LANES), lambda b, s: (b if has_init else 0, 0, 0, 0, 0))
    yf, yb, fin = pl.pallas_call(
        functools.partial(_ssd_kernel, has_init=has_init),
        out_shape=(jax.ShapeDtypeStruct((n_seq * seq_len, SSD_INNER), F32),
                   jax.ShapeDtypeStruct((n_seq * seq_len, SSD_INNER), F32),
                   jax.ShapeDtypeStruct((n_seq, 2, hp, SSD_STATE, LANES), F32)),
        grid=(n_seq, nc),
        in_specs=[pl.BlockSpec((cl, cxbc), lambda b, s: (fwd(b, s), 0)),
                  pl.BlockSpec((cl, LANES), lambda b, s: (fwd(b, s), dtblk)),
                  pl.BlockSpec((cl, cxbc), lambda b, s: (bwd(b, s), 0)),
                  pl.BlockSpec((cl, LANES), lambda b, s: (bwd(b, s), dtblk)),
                  st_spec, full2((cl, cl)), full2((2, LANES, SSD_INNER)), full2((1, LANES)), full2((1, LANES)),
                  full2((2, SSD_INNER))],
        out_specs=(pl.BlockSpec((cl, SSD_INNER), lambda b, s: (b * nc + s, 0)),
                   pl.BlockSpec((cl, SSD_INNER), lambda b, s: (b * nc + (nc - 1 - s), 0)),
                   pl.BlockSpec((1, 2, hp, SSD_STATE, LANES), lambda b, s: (b, 0, 0, 0, 0))),
        scratch_shapes=[pltpu.VMEM((2, hp, SSD_STATE, LANES), F32)],
        compiler_params=_cp(("arbitrary", "arbitrary")),
        name="ssd_scan",
    )(xbc, proj, xbc, proj, init_p, tri, jnp.asarray(expand), dtb, acont, dskip)
    return yf, yb, _unpair_states(fin)


def _headnorm(x, g_ref, w_ref):
    ms = jnp.dot(x * x, g_ref[...], precision=HI, preferred_element_type=F32)
    return x * lax.rsqrt(ms + EPS) * w_ref[...]


def _rope(x, cos, sin_signed):
    lane = lax.broadcasted_iota(jnp.int32, x.shape, 1)
    w = x.shape[1]
    swapped = jnp.where(lane % 2 == 0, pltpu.roll(x, w - 1, axis=1), pltpu.roll(x, 1, axis=1))
    return x * cos + swapped * sin_signed


def _qk_kernel(q_ref, k_ref, cos_ref, sin_ref, gq_ref, gk_ref, qw_ref, kw_ref, qo_ref, ko_ref, *, n_ctx_tiles):
    q = _headnorm(q_ref[...], gq_ref, qw_ref)
    k = _headnorm(k_ref[...], gk_ref, kw_ref)
    is_lat = pl.program_id(0) >= n_ctx_tiles

    @pl.when(is_lat)
    def _():
        cos, sin = cos_ref[...], sin_ref[...]
        qo_ref[...] = (_rope(q, cos, sin) * HEAD_DIM ** -0.5).astype(qo_ref.dtype)
        ko_ref[...] = _rope(k, cos[:, :k.shape[1]], sin[:, :k.shape[1]])

    @pl.when(jnp.logical_not(is_lat))
    def _():
        qo_ref[...] = (q * HEAD_DIM ** -0.5).astype(qo_ref.dtype)
        ko_ref[...] = k


def _rope_tables(seq_len):
    n_rows = seq_len // GRID_W
    row = jnp.repeat(jnp.arange(n_rows), GRID_W).astype(F32)
    col = jnp.tile(jnp.arange(GRID_W), n_rows).astype(F32)
    n_freq = HEAD_DIM // 4
    inv = ROPE_THETA ** (-jnp.arange(n_freq, dtype=F32) / n_freq)
    ang = jnp.concatenate([row[:, None] * inv, col[:, None] * inv], axis=-1)
    cos = jnp.repeat(jnp.cos(ang), 2, axis=1)
    sin = jnp.repeat(jnp.sin(ang), 2, axis=1) * jnp.tile(jnp.asarray([-1.0, 1.0], F32), HEAD_DIM // 2)
    return jnp.tile(cos, (1, N_HEADS)), jnp.tile(sin, (1, N_HEADS))


def qk_prep(proj, q_norm_w, k_norm_w, *, t_ctx, l_lat):
    t = proj.shape[0]
    tr = ROW_TILE
    qc, kc = N_HEADS * HEAD_DIM, N_KV_HEADS * HEAD_DIM
    cos, sin = _rope_tables(l_lat)
    group = lambda c: jnp.asarray(np.kron(np.eye(c // HEAD_DIM), np.ones((HEAD_DIM, HEAD_DIM))) / HEAD_DIM, F32)
    n_ctx_tiles = t_ctx // tr
    per = l_lat // tr
    tab = pl.BlockSpec((tr, qc), lambda i: (jnp.maximum(i - n_ctx_tiles, 0) % per, 0))
    full = lambda shape: pl.BlockSpec(shape, lambda i: (0, 0))
    return pl.pallas_call(
        functools.partial(_qk_kernel, n_ctx_tiles=n_ctx_tiles),
        out_shape=(jax.ShapeDtypeStruct((t, qc), BF16), jax.ShapeDtypeStruct((t, kc), F32)),
        grid=(t // tr,),
        in_specs=[pl.BlockSpec((tr, qc), lambda i: (i, P_Q // qc)), pl.BlockSpec((tr, kc), lambda i: (i, P_K // kc)),
                  tab, tab, full((qc, qc)), full((kc, kc)), full((1, qc)), full((1, kc))],
        out_specs=(pl.BlockSpec((tr, qc), lambda i: (i, 0)), pl.BlockSpec((tr, kc), lambda i: (i, 0))),
        compiler_params=_cp(("arbitrary",)),
        name="qk_prep",
    )(proj, proj, cos, sin, group(qc), group(kc), jnp.tile(q_norm_w, N_HEADS).reshape(1, qc),
      jnp.tile(k_norm_w, N_KV_HEADS).reshape(1, kc))


def _flash_kernel(q_ref, k_ref, v_ref, o_ref, qs_ref, m_ref, l_ref, acc_ref, *, tq):
    ki = pl.program_id(2)
    rep = N_HEADS // N_KV_HEADS

    @pl.when(ki == 0)
    def _():
        q = q_ref[0]
        for g in range(N_KV_HEADS):
            for r in range(rep):
                h = g * rep + r
                qs_ref[g, r * tq:(r + 1) * tq, :] = q[:, h * HEAD_DIM:(h + 1) * HEAD_DIM]
        m_ref[...] = jnp.full_like(m_ref, -jnp.inf)
        l_ref[...] = jnp.zeros_like(l_ref)
        acc_ref[...] = jnp.zeros_like(acc_ref)

    k = k_ref[0]
    v = v_ref[0]
    for g in range(N_KV_HEADS):
        kg = k[:, g * HEAD_DIM:(g + 1) * HEAD_DIM]
        vg = v[:, g * HEAD_DIM:(g + 1) * HEAD_DIM]
        s = lax.dot_general(qs_ref[g], kg, (((1,), (1,)), ((), ())), preferred_element_type=F32)
        m_old = m_ref[g]
        m_new = jnp.maximum(m_old, jnp.max(s, axis=-1, keepdims=True))
        alpha = jnp.exp(m_old - m_new)
        p = jnp.exp(s - m_new)
        l_ref[g] = alpha * l_ref[g] + jnp.sum(p, axis=-1, keepdims=True)
        acc_ref[g] = alpha * acc_ref[g] + jnp.dot(p.astype(BF16), vg, preferred_element_type=F32)
        m_ref[g] = m_new

    @pl.when(ki == pl.num_programs(2) - 1)
    def _():
        for g in range(N_KV_HEADS):
            out = acc_ref[g] / l_ref[g]
            for r in range(rep):
                h = g * rep + r
                o_ref[0, :, h * HEAD_DIM:(h + 1) * HEAD_DIM] = out[r * tq:(r + 1) * tq].astype(o_ref.dtype)


def flash_attention(q, k, v, *, tq, tk):
    b, lq, qc = q.shape
    lk, kc = k.shape[1], k.shape[2]
    rep = N_HEADS // N_KV_HEADS
    return pl.pallas_call(
        functools.partial(_flash_kernel, tq=tq),
        out_shape=jax.ShapeDtypeStruct((b, lq, qc), BF16),
        grid=(b, lq // tq, lk // tk),
        in_specs=[pl.BlockSpec((1, tq, qc), lambda s, i, j: (s, i, 0)),
                  pl.BlockSpec((1, tk, kc), lambda s, i, j: (s, j, 0)),
                  pl.BlockSpec((1, tk, kc), lambda s, i, j: (s, j, 0))],
        out_specs=pl.BlockSpec((1, tq, qc), lambda s, i, j: (s, i, 0)),
        scratch_shapes=[pltpu.VMEM((N_KV_HEADS, rep * tq, HEAD_DIM), BF16),
                        pltpu.VMEM((N_KV_HEADS, rep * tq, 1), F32),
                        pltpu.VMEM((N_KV_HEADS, rep * tq, 1), F32),
                        pltpu.VMEM((N_KV_HEADS, rep * tq, HEAD_DIM), F32)],
        compiler_params=_cp(("arbitrary", "arbitrary", "arbitrary")),
        name="flash_attention",
    )(q, k, v)


def _merge_kernel(x_ref, yhy_ref, yf_ref, yb_ref, z_ref, yatt_ref, gate_ref, snw_ref, whb_ref, wsb_ref, wab_ref,
                  wout_ref, g1_ref, n2w_ref, sh2_ref, sc2_ref, wr_ref, br_ref,
                  xo_ref, h2_ref, ti_ref, tw_ref):
    d = x_ref.shape[1]
    ys = (yf_ref[...] + yb_ref[...]) * _silu(z_ref[...])
    ys = ys * lax.rsqrt(jnp.mean(ys * ys, axis=-1, keepdims=True) + EPS) * snw_ref[...]
    gate = gate_ref[...].astype(F32)
    merged = (gate[:, :d] * jnp.dot(yhy_ref[...].astype(BF16), whb_ref[...], preferred_element_type=F32)
              + gate[:, d:2 * d] * jnp.dot(ys.astype(BF16), wsb_ref[...], preferred_element_type=F32)
              + gate[:, 2 * d:] * jnp.dot(yatt_ref[...], wab_ref[...], preferred_element_type=F32))
    mix = jnp.dot(merged.astype(BF16), wout_ref[...], preferred_element_type=F32)
    x = x_ref[...] + g1_ref[0] * mix
    xo_ref[...] = x
    h = x * lax.rsqrt(jnp.mean(x * x, axis=-1, keepdims=True) + EPS) * n2w_ref[...]
    h = h * (1.0 + sc2_ref[0]) + sh2_ref[0]
    h2_ref[...] = h.astype(h2_ref.dtype)
    logits = jnp.dot(h, wr_ref[...], precision=HI, preferred_element_type=F32) + br_ref[...]
    lane = lax.broadcasted_iota(jnp.int32, logits.shape, 1)
    work = jnp.where(lane < N_EXPERTS, logits, -jnp.inf)
    idx_out = jnp.zeros(logits.shape, jnp.int32)
    val_out = jnp.full(logits.shape, -jnp.inf, F32)
    for j in range(TOP_K):
        mx = jnp.max(work, axis=-1, keepdims=True)
        am = jnp.min(jnp.where(work == mx, lane, LANES), axis=-1, keepdims=True)
        idx_out = jnp.where(lane == j, am, idx_out)
        val_out = jnp.where(lane == j, mx, val_out)
        work = jnp.where(lane == am, -jnp.inf, work)
    e = jnp.exp(val_out - jnp.max(val_out, axis=-1, keepdims=True))
    ti_ref[...] = idx_out
    tw_ref[...] = e / jnp.sum(e, axis=-1, keepdims=True)


def merge_router(x, y_hy, yf, yb, proj, y_att, gate, ssd_norm_w, whb, wsb, wab, wout, mods, n2w, wr, br,
                 *, t_ctx, l_lat, tm=256):
    t, d = x.shape
    row = lambda tc, c0=0: pl.BlockSpec((tm, tc), lambda i: (i, c0))
    full = lambda shape: pl.BlockSpec(shape, lambda i: (0, 0))
    return pl.pallas_call(
        _merge_kernel,
        out_shape=(jax.ShapeDtypeStruct((t, d), F32), jax.ShapeDtypeStruct((t, d), BF16),
                   jax.ShapeDtypeStruct((t, LANES), jnp.int32), jax.ShapeDtypeStruct((t, LANES), F32)),
        grid=(t // tm,),
        in_specs=[row(d), row(HY_DIM), row(SSD_INNER), row(SSD_INNER), row(SSD_INNER, P_Z // SSD_INNER),
                  row(N_HEADS * HEAD_DIM), row(3 * d), full((1, SSD_INNER)),
                  full(whb.shape), full(wsb.shape), full(wab.shape), full(wout.shape),
                  _mod_spec(2, tm, t_ctx, l_lat), full((1, d)), _mod_spec(3, tm, t_ctx, l_lat),
                  _mod_spec(4, tm, t_ctx, l_lat), full((d, LANES)), full((1, LANES))],
        out_specs=(row(d), row(d), row(LANES), row(LANES)),
        compiler_params=_cp(("arbitrary",)),
        name="merge_router",
    )(x, y_hy, yf, yb, proj, y_att, gate, ssd_norm_w.reshape(1, -1), whb, wsb, wab, wout, mods, n2w, mods, mods,
      wr, br)


def _moe_kernel(te_ref, nv_ref, x_ref, w1g_ref, w1u_ref, b1g_ref, b1u_ref, w2_ref, b2_ref, o_ref):
    @pl.when(pl.program_id(0) < nv_ref[0])
    def _():
        x = x_ref[...]
        gate = jnp.dot(x, w1g_ref[0], preferred_element_type=F32) + b1g_ref[0]
        up = jnp.dot(x, w1u_ref[0], preferred_element_type=F32) + b1u_ref[0]
        gate = jnp.minimum(gate, SWIGLU_LIMIT)
        up = jnp.clip(up, -SWIGLU_LIMIT, SWIGLU_LIMIT)
        act = (up + 1.0) * (gate * _sigmoid(SWIGLU_ALPHA * gate))
        o_ref[...] = jnp.dot(act.astype(BF16), w2_ref[0], preferred_element_type=F32) + b2_ref[0]


def moe_experts(xs, tile_expert, n_valid, w1g, w1u, b1g, b1u, w2, b2):
    npad, d = xs.shape
    tm = MOE_TILE
    ff = w1g.shape[2]
    n_tiles = npad // tm
    rows = lambda i, te, nv: (jnp.minimum(i, nv[0] - 1), 0)
    wsel = lambda i, te, nv: (te[i], 0, 0)
    return pl.pallas_call(
        _moe_kernel,
        out_shape=jax.ShapeDtypeStruct((npad, d), F32),
        grid_spec=pltpu.PrefetchScalarGridSpec(
            num_scalar_prefetch=2,
            grid=(n_tiles,),
            in_specs=[pl.BlockSpec((tm, d), rows),
                      pl.BlockSpec((1, d, ff), wsel), pl.BlockSpec((1, d, ff), wsel),
                      pl.BlockSpec((1, 1, ff), wsel), pl.BlockSpec((1, 1, ff), wsel),
                      pl.BlockSpec((1, ff, d), wsel), pl.BlockSpec((1, 1, d), wsel)],
            out_specs=pl.BlockSpec((tm, d), rows)),
        compiler_params=_cp(("arbitrary",)),
        name="moe_experts",
    )(tile_expert, n_valid, xs, w1g, w1u, b1g, b1u, w2, b2)


def _combine_kernel(x_ref, o_ref_in, tw_ref, g2_ref, y_ref):
    d = x_ref.shape[1]
    tw = tw_ref[...]
    acc = tw[:, 0:1] * o_ref_in[:, 0:d]
    for s in range(1, TOP_K):
        acc = acc + tw[:, s:s + 1] * o_ref_in[:, s * d:(s + 1) * d]
    y_ref[...] = x_ref[...] + g2_ref[0] * acc


def moe_combine(x, slots, top_w, mods, *, t_ctx, l_lat, tm=256):
    t, d = x.shape
    return pl.pallas_call(
        _combine_kernel,
        out_shape=jax.ShapeDtypeStruct((t, d), F32),
        grid=(t // tm,),
        in_specs=[pl.BlockSpec((tm, d), lambda i: (i, 0)), pl.BlockSpec((tm, TOP_K * d), lambda i: (i, 0)),
                  pl.BlockSpec((tm, LANES), lambda i: (i, 0)), _mod_spec(5, tm, t_ctx, l_lat)],
        out_specs=pl.BlockSpec((tm, d), lambda i: (i, 0)),
        compiler_params=_cp(("arbitrary",)),
        name="moe_combine",
    )(x, slots, top_w, mods)


def _dispatch_plan(top_i, tm):
    t = top_i.shape[0]
    e = top_i.reshape(-1)
    onehot = (e[:, None] == jnp.arange(N_EXPERTS, dtype=jnp.int32)[None, :]).astype(jnp.int32)
    rank = jnp.sum((jnp.cumsum(onehot, axis=0) - 1) * onehot, axis=1)
    counts = jnp.sum(onehot, axis=0)
    tiles = (counts + tm - 1) // tm
    tile_end = jnp.cumsum(tiles)
    start = (tile_end - tiles) * tm
    dest = start[e] + rank
    n_tiles = (t * TOP_K) // tm + N_EXPERTS
    tok = jnp.zeros((n_tiles * tm,), jnp.int32).at[dest].set(jnp.arange(t * TOP_K, dtype=jnp.int32) // TOP_K)
    tile_expert = jnp.minimum(jnp.searchsorted(tile_end, jnp.arange(n_tiles, dtype=jnp.int32), side="right"),
                              N_EXPERTS - 1).astype(jnp.int32)
    return tok, dest, tile_expert, tile_end[-1:].astype(jnp.int32)


def kernel(x_prompt, x_sample, c, cache_k, cache_v, state_ssd, c_ctx, norm1_w, norm2_w, w_mod, b_mod, w_in, w_gate,
           b_gate, hy_conv_w, hy_conv_b, hy_w1, hy_b1, hy_w2, hy_b2, hy_w3, hy_freq, hy_decay, hy_bias, ssd_conv_w,
           ssd_conv_b, ssd_a_log, ssd_dt_bias, ssd_d, ssd_norm_w, q_norm_w, k_norm_w, w_br_hy, w_br_ssd, w_br_att,
           w_out, w_router, b_router, w_e1, b_e1, w_e2, b_e2):
    n_ctx, l_ctx, d = x_prompt.shape
    n_lat, l_lat, _ = x_sample.shape
    depth = w_in.shape[0]
    t_ctx, t_lat = n_ctx * l_ctx, n_lat * l_lat
    t = t_ctx + t_lat
    kc = N_KV_HEADS * HEAD_DIM
    geo = dict(t_ctx=t_ctx, l_lat=l_lat)

    x = jnp.concatenate([x_prompt.reshape(t_ctx, d), x_sample.reshape(t_lat, d)], axis=0)
    cvec = jnp.zeros((8, d), F32).at[0].set(c_ctx).at[1:1 + n_lat].set(c)
    mods_all = modulation_all(cvec, w_mod, b_mod)

    new_k, new_v, new_s = [], [], []
    for l in range(depth):
        mods = mods_all[l].reshape(8 * 6, 1, d)
        wi = w_in[l]
        w_proj = jnp.concatenate([wi[:, 0:3072], wi[:, 3088:3856], wi[:, 3072:3088],
                                  jnp.zeros((d, P_COLS - 3856), F32)], axis=1).astype(BF16)
        nw1 = norm1_w[l].reshape(1, d)
        proj = norm_mod_matmul(x, nw1, mods, w_proj, jnp.zeros((1, P_COLS), F32), sigmoid=False, out_dtype=F32,
                               **geo)
        gate = norm_mod_matmul(x, nw1, mods, w_gate[l].astype(BF16), b_gate[l].reshape(1, -1), sigmoid=True,
                               out_dtype=BF16, **geo)

        x0, p = hyena_pre(proj, hy_conv_w[l], hy_conv_b[l], t_ctx=t_ctx, l_ctx=l_ctx, l_lat=l_lat)
        hy_args = (hy_w1[l], hy_b1[l], hy_w2[l], hy_b2[l], hy_w3[l], hy_freq[l], hy_decay[l])
        y_hy_ctx = hyena_ctx(x0, p, hyena_filter(l_ctx, *hy_args), hy_bias[l], n_seq=n_ctx, seq_len=l_ctx)
        y_hy_lat = hyena_lat(x0[t_ctx:], p[t_ctx:], hyena_filter(l_lat, *hy_args), hy_bias[l], n_seq=n_lat,
                             seq_len=l_lat)
        y_hy = jnp.concatenate([y_hy_ctx, y_hy_lat], axis=0)

        xbc = ssd_conv(proj, ssd_conv_w[l], ssd_conv_b[l], t_ctx=t_ctx, l_ctx=l_ctx, l_lat=l_lat)
        ssd_args = (ssd_dt_bias[l], ssd_a_log[l], ssd_d[l])
        yf_c, yb_c, fin_c = ssd_scan(xbc, proj, None, *ssd_args, row0=0, n_seq=n_ctx, seq_len=l_ctx)
        yf_l, yb_l, _ = ssd_scan(xbc, proj, state_ssd[:, l], *ssd_args, row0=t_ctx, n_seq=n_lat, seq_len=l_lat)
        yf = jnp.concatenate([yf_c, yf_l], axis=0)
        yb = jnp.concatenate([yb_c, yb_l], axis=0)

        qn, kn = qk_prep(proj, q_norm_w[l], k_norm_w[l], **geo)
        v_all = proj[:, P_V:P_V + kc]
        k_ctx = kn[:t_ctx].reshape(n_ctx, l_ctx, kc)
        v_ctx = v_all[:t_ctx].reshape(n_ctx, l_ctx, kc)
        att_ctx = flash_attention(qn[:t_ctx].reshape(n_ctx, l_ctx, -1), k_ctx.astype(BF16), v_ctx.astype(BF16),
                                  tq=l_ctx, tk=l_ctx)
        k_lat = jnp.concatenate([kn[t_ctx:].reshape(n_lat, l_lat, kc), cache_k[:, l].reshape(n_lat, -1, kc)], axis=1)
        v_lat = jnp.concatenate([v_all[t_ctx:].reshape(n_lat, l_lat, kc), cache_v[:, l].reshape(n_lat, -1, kc)],
                                axis=1)
        att_lat = flash_attention(qn[t_ctx:].reshape(n_lat, l_lat, -1), k_lat.astype(BF16), v_lat.astype(BF16),
                                  tq=256, tk=512)
        y_att = jnp.concatenate([att_ctx.reshape(t_ctx, -1), att_lat.reshape(t_lat, -1)], axis=0)

        wr = jnp.pad(w_router[l], ((0, 0), (0, LANES - N_EXPERTS)))
        br = jnp.pad(b_router[l], (0, LANES - N_EXPERTS)).reshape(1, LANES)
        x, h2, top_i, top_w = merge_router(
            x, y_hy, yf, yb, proj, y_att, gate, ssd_norm_w[l], w_br_hy[l].astype(BF16), w_br_ssd[l].astype(BF16),
            w_br_att[l].astype(BF16), w_out[l].astype(BF16), mods, norm2_w[l].reshape(1, d), wr, br, **geo)

        tok, dest, tile_expert, n_valid = _dispatch_plan(top_i[:, :TOP_K], MOE_TILE)
        w1 = w_e1[l]
        out_sorted = moe_experts(
            h2[tok], tile_expert, n_valid, w1[:, :, 0::2].astype(BF16), w1[:, :, 1::2].astype(BF16),
            b_e1[l][:, None, 0::2], b_e1[l][:, None, 1::2], w_e2[l].astype(BF16), b_e2[l][:, None, :])
        slots = out_sorted[dest].reshape(t, TOP_K * d)
        x = moe_combine(x, slots, top_w, mods, **geo)

        new_k.append(k_ctx.reshape(n_ctx, l_ctx, N_KV_HEADS, HEAD_DIM))
        new_v.append(v_ctx.reshape(n_ctx, l_ctx, N_KV_HEADS, HEAD_DIM))
        new_s.append(fin_c)

    y_prompt = x[:t_ctx].reshape(n_ctx, l_ctx, d)
    y_sample = x[t_ctx:].reshape(n_lat, l_lat, d)
    return (y_prompt, y_sample, jnp.stack(new_k, axis=1), jnp.stack(new_v, axis=1), jnp.stack(new_s, axis=1))
```

```python
import functools
import math

import numpy as np
import jax
import jax.numpy as jnp
from jax import lax
from jax.experimental import pallas as pl
from jax.experimental.pallas import tpu as pltpu

F32 = jnp.float32
BF16 = jnp.bfloat16
HI = lax.Precision.HIGHEST

EPS = 1e-6
GRID_W = 64
HY_DIM = 512
SSD_INNER = 512
SSD_HEADDIM = 64
SSD_HEADS = 8
SSD_GROUPS = 2
SSD_STATE = 128
SSD_CHUNK = 128
N_HEADS = 8
N_KV_HEADS = 2
HEAD_DIM = 64
ROPE_THETA = 10000.0
N_EXPERTS = 32
TOP_K = 4
SWIGLU_ALPHA = 1.702
SWIGLU_LIMIT = 7.0

P_HY, P_Z, P_XBC, P_Q, P_K, P_V, P_DT, P_COLS = 0, 1536, 2048, 3072, 3584, 3712, 3840, 3968

VMEM_LIMIT = 56 * 1024 * 1024
LANES = 128
ROW_TILE = 256
MOE_TILE = 512
DFT_N1, DFT_N2 = 64, 128


def _cp(sem, vmem=VMEM_LIMIT):
    return pltpu.CompilerParams(dimension_semantics=sem, vmem_limit_bytes=vmem)


def _sigmoid(x):
    return 1.0 / (1.0 + jnp.exp(-x))


def _silu(x):
    return x * _sigmoid(x)


def _softplus(x):
    return jnp.maximum(x, 0.0) + jnp.log(1.0 + jnp.exp(-jnp.abs(x)))


def _mod_kernel(c_ref, w_ref, b_ref, o_ref):
    s = _silu(c_ref[...])
    o_ref[0] = jnp.dot(s, w_ref[0], precision=HI, preferred_element_type=F32) + b_ref[0]


def modulation_all(cvec, w_mod, b_mod):
    depth, d, n = w_mod.shape
    tn = 1536
    return pl.pallas_call(
        _mod_kernel,
        out_shape=jax.ShapeDtypeStruct((depth, 8, n), F32),
        grid=(depth, n // tn),
        in_specs=[pl.BlockSpec((8, d), lambda l, j: (0, 0)),
                  pl.BlockSpec((1, d, tn), lambda l, j: (l, 0, j)),
                  pl.BlockSpec((1, 1, tn), lambda l, j: (l, 0, j))],
        out_specs=pl.BlockSpec((1, 8, tn), lambda l, j: (l, 0, j)),
        compiler_params=_cp(("arbitrary", "arbitrary")),
        name="modulation",
    )(cvec, w_mod, b_mod.reshape(depth, 1, n))


def _mod_row(i, tm, t_ctx, l_lat):
    n_ctx = t_ctx // tm
    per = l_lat // tm
    return jnp.where(i < n_ctx, 0, 1 + (i - n_ctx) // per)


def _mod_spec(k, tm, t_ctx, l_lat):
    return pl.BlockSpec((1, 1, 1024), lambda i: (_mod_row(i, tm, t_ctx, l_lat) * 6 + k, 0, 0))


def _nmm_kernel(x_ref, nw_ref, sh_ref, sc_ref, w_ref, b_ref, o_ref, *, sigmoid):
    x = x_ref[...]
    ms = jnp.mean(x * x, axis=-1, keepdims=True)
    h = x * lax.rsqrt(ms + EPS) * nw_ref[...]
    h = h * (1.0 + sc_ref[0]) + sh_ref[0]
    acc = jnp.dot(h.astype(BF16), w_ref[...], preferred_element_type=F32) + b_ref[...]
    if sigmoid:
        acc = _sigmoid(acc)
    o_ref[...] = acc.astype(o_ref.dtype)


def norm_mod_matmul(x, nw, mods, w, b, *, t_ctx, l_lat, sigmoid, out_dtype, tm=256):
    t, d = x.shape
    n = w.shape[1]
    return pl.pallas_call(
        functools.partial(_nmm_kernel, sigmoid=sigmoid),
        out_shape=jax.ShapeDtypeStruct((t, n), out_dtype),
        grid=(t // tm,),
        in_specs=[pl.BlockSpec((tm, d), lambda i: (i, 0)),
                  pl.BlockSpec((1, d), lambda i: (0, 0)),
                  _mod_spec(0, tm, t_ctx, l_lat),
                  _mod_spec(1, tm, t_ctx, l_lat),
                  pl.BlockSpec((d, n), lambda i: (0, 0)),
                  pl.BlockSpec((1, n), lambda i: (0, 0))],
        out_specs=pl.BlockSpec((tm, n), lambda i: (i, 0)),
        compiler_params=_cp(("arbitrary",)),
        name="norm_mod_matmul",
    )(x, nw, mods, mods, w, b)


def _seq_edges(i, tr, t_ctx, l_ctx, l_lat):
    tok = i * tr
    pos = jnp.where(tok < t_ctx, tok % l_ctx, (tok - t_ctx) % l_lat)
    length = jnp.where(tok < t_ctx, l_ctx, l_lat)
    return pos == 0, pos + tr == length


def _conv3(x, prev8, next8, w_ref, b_ref, first, last):
    tr = x.shape[0]
    row = lax.broadcasted_iota(jnp.int32, x.shape, 0)
    pm = jnp.where(first, 0.0, 1.0)
    nm = jnp.where(last, 0.0, 1.0)
    xm1 = jnp.where(row == 0, prev8[7:8, :] * pm, pltpu.roll(x, 1, axis=0))
    xp1 = jnp.where(row == tr - 1, next8[0:1, :] * nm, pltpu.roll(x, tr - 1, axis=0))
    return b_ref[...] + xm1 * w_ref[0:1, :] + x * w_ref[1:2, :] + xp1 * w_ref[2:3, :]


def _conv_specs(tr, tc, col_blk, n_rows):
    r8 = tr // 8
    last8 = n_rows // 8 - 1
    return [pl.BlockSpec((tr, tc), lambda i, j: (i, col_blk(j))),
            pl.BlockSpec((8, tc), lambda i, j: (jnp.maximum(i * r8 - 1, 0), col_blk(j))),
            pl.BlockSpec((8, tc), lambda i, j: (jnp.minimum((i + 1) * r8, last8), col_blk(j)))]


def _ssd_conv_kernel(x_ref, p_ref, n_ref, w_ref, b_ref, o_ref, *, tr, t_ctx, l_ctx, l_lat):
    first, last = _seq_edges(pl.program_id(0), tr, t_ctx, l_ctx, l_lat)
    o_ref[...] = _silu(_conv3(x_ref[...], p_ref[...], n_ref[...], w_ref, b_ref, first, last))


def ssd_conv(proj, w, b, *, t_ctx, l_ctx, l_lat):
    t = proj.shape[0]
    tr, tc = ROW_TILE, 512
    c = w.shape[1]
    off = P_XBC // tc
    return pl.pallas_call(
        functools.partial(_ssd_conv_kernel, tr=tr, t_ctx=t_ctx, l_ctx=l_ctx, l_lat=l_lat),
        out_shape=jax.ShapeDtypeStruct((t, c), F32),
        grid=(t // tr, c // tc),
        in_specs=_conv_specs(tr, tc, lambda j: off + j, t) + [
            pl.BlockSpec((3, tc), lambda i, j: (0, j)),
            pl.BlockSpec((1, tc), lambda i, j: (0, j))],
        out_specs=pl.BlockSpec((tr, tc), lambda i, j: (i, j)),
        compiler_params=_cp(("arbitrary", "arbitrary")),
        name="ssd_conv",
    )(proj, proj, proj, w, b.reshape(1, c))


def _hy_pre_kernel(*refs, tr, t_ctx, l_ctx, l_lat):
    (x0, x0p, x0n, x1, x1p, x1n, xv, xvp, xvn, w0, w1, wv, b0, b1, bv, o0_ref, op_ref) = refs
    first, last = _seq_edges(pl.program_id(0), tr, t_ctx, l_ctx, l_lat)
    o0_ref[...] = _conv3(x0[...], x0p[...], x0n[...], w0, b0, first, last)
    u1 = _conv3(x1[...], x1p[...], x1n[...], w1, b1, first, last)
    uv = _conv3(xv[...], xvp[...], xvn[...], wv, bv, first, last)
    op_ref[...] = u1 * uv


def hyena_pre(proj, w, b, *, t_ctx, l_ctx, l_lat):
    t = proj.shape[0]
    tr, tc = ROW_TILE, HY_DIM
    nb = HY_DIM // tc
    b2 = b.reshape(1, 3 * HY_DIM)
    specs = []
    for s in range(3):
        specs += _conv_specs(tr, tc, lambda j, s=s: P_HY // tc + s * nb + j, t)
    specs += [pl.BlockSpec((3, tc), lambda i, j, s=s: (0, s * nb + j)) for s in range(3)]
    specs += [pl.BlockSpec((1, tc), lambda i, j, s=s: (0, s * nb + j)) for s in range(3)]
    return pl.pallas_call(
        functools.partial(_hy_pre_kernel, tr=tr, t_ctx=t_ctx, l_ctx=l_ctx, l_lat=l_lat),
        out_shape=(jax.ShapeDtypeStruct((t, HY_DIM), F32), jax.ShapeDtypeStruct((t, HY_DIM), F32)),
        grid=(t // tr, nb),
        in_specs=specs,
        out_specs=(pl.BlockSpec((tr, tc), lambda i, j: (i, j)), pl.BlockSpec((tr, tc), lambda i, j: (i, j))),
        compiler_params=_cp(("arbitrary", "arbitrary")),
        name="hyena_pre",
    )(*([proj] * 9), w, w, w, b2, b2, b2)


def _filter_kernel(z_ref, w1_ref, b1_ref, w2_ref, b2_ref, w3_ref, fr_ref, dec_ref, o_ref):
    z = z_ref[...]
    fr = fr_ref[...]
    h = jnp.sin(fr * (jnp.dot(z, w1_ref[...], precision=HI, preferred_element_type=F32) + b1_ref[...]))
    h = jnp.sin(fr * (jnp.dot(h, w2_ref[...], precision=HI, preferred_element_type=F32) + b2_ref[...]))
    f = jnp.dot(h, w3_ref[...], precision=HI, preferred_element_type=F32)
    o_ref[...] = f * jnp.exp(-z[:, 0:1] * jnp.abs(dec_ref[...]))


def _filter_embedding(seq_len, emb):
    bands_n = (emb - 1) // 2
    t = jnp.linspace(0.0, 1.0, seq_len, dtype=F32)[:, None]
    bands = jnp.linspace(1e-4, bands_n - 1, bands_n, dtype=F32)[None, :]
    ang = (2.0 * math.pi / seq_len) * jnp.arange(seq_len, dtype=F32)[:, None] * bands
    z = jnp.concatenate([t, jnp.cos(ang), -jnp.sin(ang)], axis=-1)
    return jnp.pad(z, ((0, 0), (0, LANES - emb)))


def hyena_filter(seq_len, w1, b1, w2, b2, w3, freq, decay):
    emb, ff = w1.shape
    n = w3.shape[1]
    z = _filter_embedding(seq_len, emb)
    padc = LANES - ff
    w1p = jnp.pad(w1, ((0, LANES - emb), (0, padc)))
    w2p = jnp.pad(w2, ((0, padc), (0, padc)))
    w3p = jnp.pad(w3, ((0, padc), (0, 0)))
    row = lambda v: jnp.pad(v, (0, padc)).reshape(1, LANES)
    tr = 256
    full = lambda shape: pl.BlockSpec(shape, lambda i: (0, 0))
    return pl.pallas_call(
        _filter_kernel,
        out_shape=jax.ShapeDtypeStruct((seq_len, n), F32),
        grid=(seq_len // tr,),
        in_specs=[pl.BlockSpec((tr, LANES), lambda i: (i, 0)), full((LANES, LANES)), full((1, LANES)),
                  full((LANES, LANES)), full((1, LANES)), full((LANES, n)), full((1, LANES)), full((1, n))],
        out_specs=pl.BlockSpec((tr, n), lambda i: (i, 0)),
        compiler_params=_cp(("arbitrary",)),
        name="hyena_filter",
    )(z, w1p, row(b1), w2p, row(b2), w3p, row(freq), decay.reshape(1, n))


def _circular_filter(filt):
    ch = filt.shape[1] // 2
    h_fwd, h_bwd = filt[:, :ch], filt[:, ch:]
    return jnp.concatenate([h_fwd, jnp.zeros((1, ch), F32), h_bwd[1:][::-1]], axis=0)


def _cs(n_rows, n_cols, period):
    ang = 2.0 * np.pi * (np.outer(np.arange(n_rows), np.arange(n_cols)) % period) / period
    return np.cos(ang), np.sin(ang)


def _mm_kernel(a_ref, b_ref, o_ref):
    o_ref[0] = jnp.dot(a_ref[...], b_ref[0], precision=HI, preferred_element_type=F32)


def const_matmul(a, b, tn):
    m, k = a.shape
    bsz, _, n = b.shape
    return pl.pallas_call(
        _mm_kernel,
        out_shape=jax.ShapeDtypeStruct((bsz, m, n), F32),
        grid=(bsz, n // tn),
        in_specs=[pl.BlockSpec((m, k), lambda s, j: (0, 0)), pl.BlockSpec((1, k, tn), lambda s, j: (s, 0, j))],
        out_specs=pl.BlockSpec((1, m, tn), lambda s, j: (s, 0, j)),
        compiler_params=_cp(("arbitrary", "arbitrary")),
        name="const_matmul",
    )(a, b)


def _hy_ctx_kernel(x0_ref, p_ref, kr_ref, ki_ref, fw_ref, iv_ref, bias_ref, o_ref, *, n):
    p = p_ref[...]
    xf = jnp.dot(fw_ref[...], p, precision=HI, preferred_element_type=F32)
    xr, xi = xf[:n], xf[n:]
    kr, ki = kr_ref[...], ki_ref[...]
    yr = xr * kr - xi * ki
    yi = xr * ki + xi * kr
    y = jnp.dot(iv_ref[...], jnp.concatenate([yr, yi], axis=0), precision=HI, preferred_element_type=F32)
    o_ref[...] = x0_ref[...] * (y + p * bias_ref[...])


def hyena_ctx(x0, p, filt, bias, *, n_seq, seq_len):
    n = 2 * seq_len
    ch = p.shape[1]
    c_full, s_full = _cs(n, n, n)
    fw_full = jnp.asarray(np.concatenate([c_full, -s_full], axis=0), F32)
    fw_half = fw_full[:, :seq_len]
    iv = jnp.asarray(np.concatenate([c_full[:seq_len], -s_full[:seq_len]], axis=1) / n, F32)
    kf = const_matmul(fw_full, _circular_filter(filt)[None], ch)[0]
    kr, ki = kf[:n], kf[n:]
    full = lambda shape: pl.BlockSpec(shape, lambda s: (0, 0))
    return pl.pallas_call(
        functools.partial(_hy_ctx_kernel, n=n),
        out_shape=jax.ShapeDtypeStruct((n_seq * seq_len, ch), F32),
        grid=(n_seq,),
        in_specs=[pl.BlockSpec((seq_len, ch), lambda s: (s, 0)), pl.BlockSpec((seq_len, ch), lambda s: (s, 0)),
                  full((n, ch)), full((n, ch)), full((2 * n, seq_len)), full((seq_len, 2 * n)), full((1, ch))],
        out_specs=pl.BlockSpec((seq_len, ch), lambda s: (s, 0)),
        compiler_params=_cp(("arbitrary",)),
        name="hyena_ctx",
    )(x0, p, kr, ki, fw_half, iv, bias.reshape(1, ch))


def _stage2_kernel(a_ref, twr_ref, twi_ref, m_ref, *rest, conv):
    if conv:
        kf_ref, mi_ref, o_ref = rest
    else:
        (o_ref,) = rest
    ar, ai = a_ref[0, 0, 0], a_ref[0, 1, 0]
    twr, twi = twr_ref[0], twi_ref[0]
    br = ar * twr - ai * twi
    bi = ar * twi + ai * twr
    x = jnp.dot(m_ref[...], jnp.concatenate([br, bi], axis=0), precision=HI, preferred_element_type=F32)
    n2 = ar.shape[0]
    xr, xi = x[:n2], x[n2:]
    if not conv:
        o_ref[0, 0, 0] = xr
        o_ref[0, 1, 0] = xi
        return
    kr, ki = kf_ref[0, 0, 0], kf_ref[0, 1, 0]
    yr = xr * kr - xi * ki
    yi = xr * ki + xi * kr
    pq = jnp.dot(mi_ref[...], jnp.concatenate([yr, yi], axis=0), precision=HI, preferred_element_type=F32)
    pr, pi = pq[:n2], pq[n2:]
    o_ref[0, 0, 0] = pr * twr + pi * twi
    o_ref[0, 1, 0] = pi * twr - pr * twi


def _stage2(a, kf, ch):
    bsz = a.shape[0]
    n1, n2 = DFT_N1, DFT_N2
    n = n1 * n2
    tw_ang = 2.0 * np.pi * np.outer(np.arange(n1), np.arange(n2)) / n
    twr = jnp.asarray(np.cos(tw_ang), F32).reshape(n1, n2, 1)
    twi = jnp.asarray(-np.sin(tw_ang), F32).reshape(n1, n2, 1)
    c2, s2 = _cs(n2, n2, n2)
    m_fwd = jnp.asarray(np.block([[c2, s2], [-s2, c2]]), F32)
    m_inv = jnp.asarray(np.block([[c2, -s2], [s2, c2]]), F32)
    conv = kf is not None
    blk = pl.BlockSpec((1, 2, 1, n2, ch), lambda s, k: (s, 0, k, 0, 0))
    specs = [blk, pl.BlockSpec((1, n2, 1), lambda s, k: (k, 0, 0)), pl.BlockSpec((1, n2, 1), lambda s, k: (k, 0, 0)),
             pl.BlockSpec((2 * n2, 2 * n2), lambda s, k: (0, 0))]
    args = [a, twr, twi, m_fwd]
    if conv:
        specs += [pl.BlockSpec((1, 2, 1, n2, ch), lambda s, k: (0, 0, k, 0, 0)),
                  pl.BlockSpec((2 * n2, 2 * n2), lambda s, k: (0, 0))]
        args += [kf, m_inv]
    return pl.pallas_call(
        functools.partial(_stage2_kernel, conv=conv),
        out_shape=jax.ShapeDtypeStruct(a.shape, F32),
        grid=(bsz, n1),
        in_specs=specs,
        out_specs=blk,
        compiler_params=_cp(("arbitrary", "arbitrary")),
        name="hyena_stage2",
    )(*args)


def _hy_post_kernel(g_ref, q_ref, x0_ref, p_ref, bias_ref, o_ref):
    y = jnp.dot(g_ref[...], q_ref[0], precision=HI, preferred_element_type=F32)
    o_ref[0] = x0_ref[0] * (y + p_ref[0] * bias_ref[...])


def hyena_lat(x0, p, filt, bias, *, n_seq, seq_len):
    n1, n2 = DFT_N1, DFT_N2
    n = n1 * n2
    assert n == 2 * seq_len
    ch = p.shape[1]
    h1 = n1 // 2
    wide = n2 * ch
    c1, s1 = _cs(n1, n1, n1)
    f1_full = jnp.asarray(np.concatenate([c1, -s1], axis=0), F32)
    f1_half = f1_full[:, :h1]
    g1 = jnp.asarray(np.concatenate([c1[:h1], -s1[:h1]], axis=1) / n, F32)
    tn = 8192
    kcirc = _circular_filter(filt).reshape(1, n1, wide)
    kf = _stage2(const_matmul(f1_full, kcirc, tn).reshape(1, 2, n1, n2, ch), None, ch)
    a = const_matmul(f1_half, p.reshape(n_seq, h1, wide), tn).reshape(n_seq, 2, n1, n2, ch)
    q = _stage2(a, kf, ch).reshape(n_seq, 2 * n1, wide)
    bias_w = jnp.tile(bias, n2).reshape(1, wide)
    out = pl.pallas_call(
        _hy_post_kernel,
        out_shape=jax.ShapeDtypeStruct((n_seq, h1, wide), F32),
        grid=(n_seq, wide // tn),
        in_specs=[pl.BlockSpec((h1, 2 * n1), lambda s, j: (0, 0)),
                  pl.BlockSpec((1, 2 * n1, tn), lambda s, j: (s, 0, j)),
                  pl.BlockSpec((1, h1, tn), lambda s, j: (s, 0, j)),
                  pl.BlockSpec((1, h1, tn), lambda s, j: (s, 0, j)),
                  pl.BlockSpec((1, tn), lambda s, j: (0, j))],
        out_specs=pl.BlockSpec((1, h1, tn), lambda s, j: (s, 0, j)),
        compiler_params=_cp(("arbitrary", "arbitrary")),
        name="hyena_post",
    )(g1, q, x0.reshape(n_seq, h1, wide), p.reshape(n_seq, h1, wide), bias_w)
    return out.reshape(n_seq * seq_len, ch)


def _ssd_dir(xbc, dtraw, st_ref, d, consts, y_ref):
    tri, expand, dtb, acont, dskip = consts
    cl = xbc.shape[0]
    xs = xbc[:, :SSD_INNER]
    dt = _softplus(dtraw + dtb)
    a = dt * acont
    tri_d = tri if d == 0 else tri.T
    cs = jnp.dot(tri_d, a, precision=HI, preferred_element_type=F32)
    cs_t = jnp.dot(a.T, tri_d.T, precision=HI, preferred_element_type=F32)
    dt_x = jnp.dot(dt, expand[d], precision=HI, preferred_element_type=F32)
    xdt = xs * dt_x
    row = lax.broadcasted_iota(jnp.int32, (cl, cl), 0)
    col = lax.broadcasted_iota(jnp.int32, (cl, cl), 1)
    keep = (col <= row) if d == 0 else (col >= row)
    lane = lax.broadcasted_iota(jnp.int32, (cl, LANES), 1)
    low = lane < SSD_HEADDIM
    edge = cl - 1 if d == 0 else 0
    for g in range(SSD_GROUPS):
        bg = xbc[:, SSD_INNER + g * SSD_STATE: SSD_INNER + (g + 1) * SSD_STATE]
        cg = xbc[:, SSD_INNER + (SSD_GROUPS + g) * SSD_STATE: SSD_INNER + (SSD_GROUPS + g + 1) * SSD_STATE]
        bg16, cg16 = bg.astype(BF16), cg.astype(BF16)
        cb = lax.dot_general(cg16, bg16, (((1,), (1,)), ((), ())), preferred_element_type=F32)
        for pr in range(2):
            pair = g * 2 + pr
            h0 = 2 * pair
            ms = []
            for h in (h0, h0 + 1):
                ln = d * SSD_HEADS + h
                diff = cs[:, ln:ln + 1] - cs_t[ln:ln + 1, :]
                ms.append(jnp.where(keep, cb * jnp.exp(jnp.minimum(diff, 0.0)), 0.0).astype(BF16))
            xp = xdt[:, pair * LANES:(pair + 1) * LANES]
            xs_p = xs[:, pair * LANES:(pair + 1) * LANES]
            xlo = jnp.where(low, xp, 0.0).astype(BF16)
            xhi = jnp.where(low, 0.0, xp).astype(BF16)
            y_diag = (jnp.dot(ms[0], xlo, preferred_element_type=F32)
                      + jnp.dot(ms[1], xhi, preferred_element_type=F32))
            l0, l1 = d * SSD_HEADS + h0, d * SSD_HEADS + h0 + 1
            e_cs = jnp.where(low, jnp.exp(cs[:, l0:l0 + 1]), jnp.exp(cs[:, l1:l1 + 1]))
            st = st_ref[pair]
            y_off = jnp.dot(cg16, st.astype(BF16), preferred_element_type=F32) * e_cs
            y_ref[:, pair * LANES:(pair + 1) * LANES] = (
                y_diag + y_off + xs_p * dskip[d:d + 1, pair * LANES:(pair + 1) * LANES])
            tot0, tot1 = cs[edge:edge + 1, l0:l0 + 1], cs[edge:edge + 1, l1:l1 + 1]
            dec = jnp.where(low, jnp.exp(tot0 - cs[:, l0:l0 + 1]), jnp.exp(tot1 - cs[:, l1:l1 + 1]))
            upd = lax.dot_general(bg16, (xp * dec).astype(BF16), (((0,), (0,)), ((), ())),
                                  preferred_element_type=F32)
            st_ref[pair] = st * jnp.where(low[0:1], jnp.exp(tot0), jnp.exp(tot1)) + upd


def _ssd_kernel(xf_ref, dtf_ref, xb_ref, dtb_ref, init_ref, tri_ref, exp_ref, dtbias_ref, acont_ref, dskip_ref,
                yf_ref, yb_ref, fin_ref, st_ref, *, has_init):
    s = pl.program_id(1)

    @pl.when(s == 0)
    def _():
        if has_init:
            st_ref[...] = init_ref[0]
        else:
            st_ref[...] = jnp.zeros_like(st_ref)

    consts = (tri_ref[...], (exp_ref[0], exp_ref[1]), dtbias_ref[...], acont_ref[...], dskip_ref[...])
    _ssd_dir(xf_ref[...], dtf_ref[...], st_ref.at[0], 0, consts, yf_ref)
    _ssd_dir(xb_ref[...], dtb_ref[...], st_ref.at[1], 1, consts, yb_ref)

    @pl.when(s == pl.num_programs(1) - 1)
    def _():
        fin_ref[0] = st_ref[...]


def _pair_states(s):
    b = s.shape[0]
    s = s.reshape(b, 2, SSD_HEADS // 2, 2, SSD_HEADDIM, SSD_STATE)
    return s.transpose(0, 1, 2, 5, 3, 4).reshape(b, 2, SSD_HEADS // 2, SSD_STATE, 2 * SSD_HEADDIM)


def _unpair_states(s):
    b = s.shape[0]
    s = s.reshape(b, 2, SSD_HEADS // 2, SSD_STATE, 2, SSD_HEADDIM)
    return s.transpose(0, 1, 2, 4, 5, 3).reshape(b, 2, SSD_HEADS, SSD_HEADDIM, SSD_STATE)


def ssd_scan(xbc, proj, init, dt_bias, a_log, d_skip, *, row0, n_seq, seq_len):
    cl = SSD_CHUNK
    nc = seq_len // cl
    base = row0 // cl
    has_init = init is not None
    hp = SSD_HEADS // 2
    init_p = _pair_states(init) if has_init else jnp.zeros((1, 2, hp, SSD_STATE, LANES), F32)
    tri = jnp.asarray(np.tril(np.ones((cl, cl))), F32)
    expand = np.zeros((2, LANES, SSD_INNER), np.float32)
    for d in range(2):
        for h in range(SSD_HEADS):
            expand[d, d * SSD_HEADS + h, h * SSD_HEADDIM:(h + 1) * SSD_HEADDIM] = 1.0
    pad16 = lambda v: jnp.pad(v.reshape(1, 2 * SSD_HEADS), ((0, 0), (0, LANES - 2 * SSD_HEADS)))
    acont = pad16(-jnp.exp(a_log))
    dtb = pad16(dt_bias)
    dskip = jnp.repeat(d_skip, SSD_HEADDIM, axis=1)
    cxbc = xbc.shape[1]
    dtblk = P_DT // LANES
    full2 = lambda shape: pl.BlockSpec(shape, lambda b, s: (0,) * len(shape))
    fwd = lambda b, s: base + b * nc + s
    bwd = lambda b, s: base + b * nc + (nc - 1 - s)
    st_spec = pl.BlockSpec((1, 2, hp, SSD_STATE, LANES), lambda b, s: (b if has_init else 0, 0, 0, 0, 0))
    yf, yb, fin = pl.pallas_call(
        functools.partial(_ssd_kernel, has_init=has_init),
        out_shape=(jax.ShapeDtypeStruct((n_seq * seq_len, SSD_INNER), F32),
                   jax.ShapeDtypeStruct((n_seq * seq_len, SSD_INNER), F32),
                   jax.ShapeDtypeStruct((n_seq, 2, hp, SSD_STATE, LANES), F32)),
        grid=(n_seq, nc),
        in_specs=[pl.BlockSpec((cl, cxbc), lambda b, s: (fwd(b, s), 0)),
                  pl.BlockSpec((cl, LANES), lambda b, s: (fwd(b, s), dtblk)),
                  pl.BlockSpec((cl, cxbc), lambda b, s: (bwd(b, s), 0)),
                  pl.BlockSpec((cl, LANES), lambda b, s: (bwd(b, s), dtblk)),
                  st_spec, full2((cl, cl)), full2((2, LANES, SSD_INNER)), full2((1, LANES)), full2((1, LANES)),
                  full2((2, SSD_INNER))],
        out_specs=(pl.BlockSpec((cl, SSD_INNER), lambda b, s: (b * nc + s, 0)),
                   pl.BlockSpec((cl, SSD_INNER), lambda b, s: (b * nc + (nc - 1 - s), 0)),
                   pl.BlockSpec((1, 2, hp, SSD_STATE, LANES), lambda b, s: (b, 0, 0, 0, 0))),
        scratch_shapes=[pltpu.VMEM((2, hp, SSD_STATE, LANES), F32)],
        compiler_params=_cp(("arbitrary", "arbitrary")),
        name="ssd_scan",
    )(xbc, proj, xbc, proj, init_p, tri, jnp.asarray(expand), dtb, acont, dskip)
    return yf, yb, _unpair_states(fin)


def _headnorm(x, g_ref, w_ref):
    ms = jnp.dot(x * x, g_ref[...], precision=HI, preferred_element_type=F32)
    return x * lax.rsqrt(ms + EPS) * w_ref[...]


def _rope(x, cos, sin_signed):
    lane = lax.broadcasted_iota(jnp.int32, x.shape, 1)
    w = x.shape[1]
    swapped = jnp.where(lane % 2 == 0, pltpu.roll(x, w - 1, axis=1), pltpu.roll(x, 1, axis=1))
    return x * cos + swapped * sin_signed


def _qk_kernel(q_ref, k_ref, cos_ref, sin_ref, gq_ref, gk_ref, qw_ref, kw_ref, qo_ref, ko_ref, *, n_ctx_tiles):
    q = _headnorm(q_ref[...], gq_ref, qw_ref)
    k = _headnorm(k_ref[...], gk_ref, kw_ref)
    is_lat = pl.program_id(0) >= n_ctx_tiles

    @pl.when(is_lat)
    def _():
        cos, sin = cos_ref[...], sin_ref[...]
        qo_ref[...] = (_rope(q, cos, sin) * HEAD_DIM ** -0.5).astype(qo_ref.dtype)
        ko_ref[...] = _rope(k, cos[:, :k.shape[1]], sin[:, :k.shape[1]])

    @pl.when(jnp.logical_not(is_lat))
    def _():
        qo_ref[...] = (q * HEAD_DIM ** -0.5).astype(qo_ref.dtype)
        ko_ref[...] = k


def _rope_tables(seq_len):
    n_rows = seq_len // GRID_W
    row = jnp.repeat(jnp.arange(n_rows), GRID_W).astype(F32)
    col = jnp.tile(jnp.arange(GRID_W), n_rows).astype(F32)
    n_freq = HEAD_DIM // 4
    inv = ROPE_THETA ** (-jnp.arange(n_freq, dtype=F32) / n_freq)
    ang = jnp.concatenate([row[:, None] * inv, col[:, None] * inv], axis=-1)
    cos = jnp.repeat(jnp.cos(ang), 2, axis=1)
    sin = jnp.repeat(jnp.sin(ang), 2, axis=1) * jnp.tile(jnp.asarray([-1.0, 1.0], F32), HEAD_DIM // 2)
    return jnp.tile(cos, (1, N_HEADS)), jnp.tile(sin, (1, N_HEADS))


def qk_prep(proj, q_norm_w, k_norm_w, *, t_ctx, l_lat):
    t = proj.shape[0]
    tr = ROW_TILE
    qc, kc = N_HEADS * HEAD_DIM, N_KV_HEADS * HEAD_DIM
    cos, sin = _rope_tables(l_lat)
    group = lambda c: jnp.asarray(np.kron(np.eye(c // HEAD_DIM), np.ones((HEAD_DIM, HEAD_DIM))) / HEAD_DIM, F32)
    n_ctx_tiles = t_ctx // tr
    per = l_lat // tr
    tab = pl.BlockSpec((tr, qc), lambda i: (jnp.maximum(i - n_ctx_tiles, 0) % per, 0))
    full = lambda shape: pl.BlockSpec(shape, lambda i: (0, 0))
    return pl.pallas_call(
        functools.partial(_qk_kernel, n_ctx_tiles=n_ctx_tiles),
        out_shape=(jax.ShapeDtypeStruct((t, qc), BF16), jax.ShapeDtypeStruct((t, kc), F32)),
        grid=(t // tr,),
        in_specs=[pl.BlockSpec((tr, qc), lambda i: (i, P_Q // qc)), pl.BlockSpec((tr, kc), lambda i: (i, P_K // kc)),
                  tab, tab, full((qc, qc)), full((kc, kc)), full((1, qc)), full((1, kc))],
        out_specs=(pl.BlockSpec((tr, qc), lambda i: (i, 0)), pl.BlockSpec((tr, kc), lambda i: (i, 0))),
        compiler_params=_cp(("arbitrary",)),
        name="qk_prep",
    )(proj, proj, cos, sin, group(qc), group(kc), jnp.tile(q_norm_w, N_HEADS).reshape(1, qc),
      jnp.tile(k_norm_w, N_KV_HEADS).reshape(1, kc))


def _flash_kernel(q_ref, k_ref, v_ref, o_ref, qs_ref, m_ref, l_ref, acc_ref, *, tq):
    ki = pl.program_id(2)
    rep = N_HEADS // N_KV_HEADS

    @pl.when(ki == 0)
    def _():
        q = q_ref[0]
        for g in range(N_KV_HEADS):
            for r in range(rep):
                h = g * rep + r
                qs_ref[g, r * tq:(r + 1) * tq, :] = q[:, h * HEAD_DIM:(h + 1) * HEAD_DIM]
        m_ref[...] = jnp.full_like(m_ref, -jnp.inf)
        l_ref[...] = jnp.zeros_like(l_ref)
        acc_ref[...] = jnp.zeros_like(acc_ref)

    k = k_ref[0]
    v = v_ref[0]
    for g in range(N_KV_HEADS):
        kg = k[:, g * HEAD_DIM:(g + 1) * HEAD_DIM]
        vg = v[:, g * HEAD_DIM:(g + 1) * HEAD_DIM]
        s = lax.dot_general(qs_ref[g], kg, (((1,), (1,)), ((), ())), preferred_element_type=F32)
        m_old = m_ref[g]
        m_new = jnp.maximum(m_old, jnp.max(s, axis=-1, keepdims=True))
        alpha = jnp.exp(m_old - m_new)
        p = jnp.exp(s - m_new)
        l_ref[g] = alpha * l_ref[g] + jnp.sum(p, axis=-1, keepdims=True)
        acc_ref[g] = alpha * acc_ref[g] + jnp.dot(p.astype(BF16), vg, preferred_element_type=F32)
        m_ref[g] = m_new

    @pl.when(ki == pl.num_programs(2) - 1)
    def _():
        for g in range(N_KV_HEADS):
            out = acc_ref[g] / l_ref[g]
            for r in range(rep):
                h = g * rep + r
                o_ref[0, :, h * HEAD_DIM:(h + 1) * HEAD_DIM] = out[r * tq:(r + 1) * tq].astype(o_ref.dtype)


def flash_attention(q, k, v, *, tq, tk):
    b, lq, qc = q.shape
    lk, kc = k.shape[1], k.shape[2]
    rep = N_HEADS // N_KV_HEADS
    return pl.pallas_call(
        functools.partial(_flash_kernel, tq=tq),
        out_shape=jax.ShapeDtypeStruct((b, lq, qc), BF16),
        grid=(b, lq // tq, lk // tk),
        in_specs=[pl.BlockSpec((1, tq, qc), lambda s, i, j: (s, i, 0)),
                  pl.BlockSpec((1, tk, kc), lambda s, i, j: (s, j, 0)),
                  pl.BlockSpec((1, tk, kc), lambda s, i, j: (s, j, 0))],
        out_specs=pl.BlockSpec((1, tq, qc), lambda s, i, j: (s, i, 0)),
        scratch_shapes=[pltpu.VMEM((N_KV_HEADS, rep * tq, HEAD_DIM), BF16),
                        pltpu.VMEM((N_KV_HEADS, rep * tq, 1), F32),
                        pltpu.VMEM((N_KV_HEADS, rep * tq, 1), F32),
                        pltpu.VMEM((N_KV_HEADS, rep * tq, HEAD_DIM), F32)],
        compiler_params=_cp(("arbitrary", "arbitrary", "arbitrary")),
        name="flash_attention",
    )(q, k, v)


def _merge_kernel(x_ref, yhy_ref, yf_ref, yb_ref, z_ref, yatt_ref, gate_ref, snw_ref, whb_ref, wsb_ref, wab_ref,
                  wout_ref, g1_ref, n2w_ref, sh2_ref, sc2_ref, wr_ref, br_ref,
                  xo_ref, h2_ref, ti_ref, tw_ref):
    d = x_ref.shape[1]
    ys = (yf_ref[...] + yb_ref[...]) * _silu(z_ref[...])
    ys = ys * lax.rsqrt(jnp.mean(ys * ys, axis=-1, keepdims=True) + EPS) * snw_ref[...]
    gate = gate_ref[...].astype(F32)
    merged = (gate[:, :d] * jnp.dot(yhy_ref[...].astype(BF16), whb_ref[...], preferred_element_type=F32)
              + gate[:, d:2 * d] * jnp.dot(ys.astype(BF16), wsb_ref[...], preferred_element_type=F32)
              + gate[:, 2 * d:] * jnp.dot(yatt_ref[...], wab_ref[...], preferred_element_type=F32))
    mix = jnp.dot(merged.astype(BF16), wout_ref[...], preferred_element_type=F32)
    x = x_ref[...] + g1_ref[0] * mix
    xo_ref[...] = x
    h = x * lax.rsqrt(jnp.mean(x * x, axis=-1, keepdims=True) + EPS) * n2w_ref[...]
    h = h * (1.0 + sc2_ref[0]) + sh2_ref[0]
    _split_rows(h2_ref, h)
    logits = jnp.dot(h, wr_ref[...], precision=HI, preferred_element_type=F32) + br_ref[...]
    lane = lax.broadcasted_iota(jnp.int32, logits.shape, 1)
    work = jnp.where(lane < N_EXPERTS, logits, -jnp.inf)
    idx_out = jnp.zeros(logits.shape, jnp.int32)
    val_out = jnp.full(logits.shape, -jnp.inf, F32)
    for j in range(TOP_K):
        mx = jnp.max(work, axis=-1, keepdims=True)
        am = jnp.min(jnp.where(work == mx, lane, LANES), axis=-1, keepdims=True)
        idx_out = jnp.where(lane == j, am, idx_out)
        val_out = jnp.where(lane == j, mx, val_out)
        work = jnp.where(lane == am, -jnp.inf, work)
    e = jnp.exp(val_out - jnp.max(val_out, axis=-1, keepdims=True))
    ti_ref[...] = idx_out
    tw_ref[...] = e / jnp.sum(e, axis=-1, keepdims=True)


def merge_router(x, y_hy, yf, yb, proj, y_att, gate, ssd_norm_w, whb, wsb, wab, wout, mods, n2w, wr, br,
                 *, t_ctx, l_lat, tm=256):
    t, d = x.shape
    row = lambda tc, c0=0: pl.BlockSpec((tm, tc), lambda i: (i, c0))
    full = lambda shape: pl.BlockSpec(shape, lambda i: (0, 0))
    return pl.pallas_call(
        _merge_kernel,
        out_shape=(jax.ShapeDtypeStruct((t, d), F32), jax.ShapeDtypeStruct((t, ROW_SUB, LANES), F32),
                   jax.ShapeDtypeStruct((t, LANES), jnp.int32), jax.ShapeDtypeStruct((t, LANES), F32)),
        grid=(t // tm,),
        in_specs=[row(d), row(HY_DIM), row(SSD_INNER), row(SSD_INNER), row(SSD_INNER, P_Z // SSD_INNER),
                  row(N_HEADS * HEAD_DIM), row(3 * d), full((1, SSD_INNER)),
                  full(whb.shape), full(wsb.shape), full(wab.shape), full(wout.shape),
                  _mod_spec(2, tm, t_ctx, l_lat), full((1, d)), _mod_spec(3, tm, t_ctx, l_lat),
                  _mod_spec(4, tm, t_ctx, l_lat), full((d, LANES)), full((1, LANES))],
        out_specs=(row(d), pl.BlockSpec((tm, ROW_SUB, LANES), lambda i: (i, 0, 0)), row(LANES), row(LANES)),
        compiler_params=_cp(("arbitrary",)),
        name="merge_router",
    )(x, y_hy, yf, yb, proj, y_att, gate, ssd_norm_w.reshape(1, -1), whb, wsb, wab, wout, mods, n2w, mods, mods,
      wr, br)


DEINT_BLOCK = 256


def _deinterleave_kernel(w_ref, perm_ref, g_ref, u_ref):
    half = DEINT_BLOCK // 2
    for blk in range(w_ref.shape[2] // DEINT_BLOCK):
        wb = w_ref[0, :, blk * DEINT_BLOCK:(blk + 1) * DEINT_BLOCK].astype(BF16)
        r = jnp.dot(wb, perm_ref[...], preferred_element_type=F32)
        g_ref[0, :, blk * half:(blk + 1) * half] = r[:, :half].astype(BF16)
        u_ref[0, :, blk * half:(blk + 1) * half] = r[:, half:].astype(BF16)


def deinterleave_experts(w1):
    e, d, f2 = w1.shape
    perm = np.zeros((DEINT_BLOCK, DEINT_BLOCK), np.float32)
    half = DEINT_BLOCK // 2
    perm[2 * np.arange(half), np.arange(half)] = 1.0
    perm[2 * np.arange(half) + 1, half + np.arange(half)] = 1.0
    out = jax.ShapeDtypeStruct((e, d, f2 // 2), BF16)
    return pl.pallas_call(
        _deinterleave_kernel,
        out_shape=(out, out),
        grid=(e,),
        in_specs=[pl.BlockSpec((1, d, f2), lambda i: (i, 0, 0)),
                  pl.BlockSpec((DEINT_BLOCK, DEINT_BLOCK), lambda i: (0, 0))],
        out_specs=(pl.BlockSpec((1, d, f2 // 2), lambda i: (i, 0, 0)),
                   pl.BlockSpec((1, d, f2 // 2), lambda i: (i, 0, 0))),
        compiler_params=_cp(("arbitrary",)),
        name="deinterleave_experts",
    )(w1, jnp.asarray(perm, BF16))


ROW_SUB = 8
DMA_CHUNK = 512


def _split_rows(ref, x):
    for s in range(ROW_SUB):
        ref[:, s, :] = x[:, s * LANES:(s + 1) * LANES]


def _join_rows(ref, first=0):
    return jnp.concatenate([ref[:, first + s, :] for s in range(ROW_SUB)], axis=1)


def _row_dma_kernel(idx_ref, src_ref, *rest, dispatch):
    dst_ref, sems = rest[-2:]
    c = pl.program_id(0)
    slot = c % 2
    base = c * DMA_CHUNK

    def chunk_wait(s):
        pltpu.make_async_copy(dst_ref.at[pl.ds(0, DMA_CHUNK)], dst_ref.at[pl.ds(0, DMA_CHUNK)], sems.at[s]).wait()

    def issue(r, carry):
        j = idx_ref[0, 0, r]
        if dispatch:
            pltpu.make_async_copy(src_ref.at[(base + r) // TOP_K], dst_ref.at[j], sems.at[slot]).start()
        else:
            pltpu.make_async_copy(src_ref.at[j], dst_ref.at[base + r], sems.at[slot]).start()
        return carry

    lax.fori_loop(0, DMA_CHUNK, issue, 0, unroll=8)

    @pl.when(c > 0)
    def _():
        chunk_wait(1 - slot)

    @pl.when(c == pl.num_programs(0) - 1)
    def _():
        chunk_wait(slot)


def row_dma(idx, src, n_dst, *, dispatch):
    n = idx.shape[0]
    nc = n // DMA_CHUNK
    shape = (n_dst,) + src.shape[1:]
    args = [idx.reshape(nc, 1, DMA_CHUNK), src]
    specs = [pl.BlockSpec((1, 1, DMA_CHUNK), lambda c: (c, 0, 0), memory_space=pltpu.SMEM),
             pl.BlockSpec(memory_space=pl.ANY)]
    if dispatch:
        args.append(jnp.zeros(shape, src.dtype))
        specs.append(pl.BlockSpec(memory_space=pl.ANY))
    return pl.pallas_call(
        functools.partial(_row_dma_kernel, dispatch=dispatch),
        out_shape=jax.ShapeDtypeStruct(shape, src.dtype),
        grid=(nc,),
        in_specs=specs,
        out_specs=pl.BlockSpec(memory_space=pl.ANY),
        scratch_shapes=[pltpu.SemaphoreType.DMA((2,))],
        input_output_aliases={2: 0} if dispatch else {},
        compiler_params=_cp(("arbitrary",)),
        name="moe_dispatch" if dispatch else "moe_return",
    )(*args)


def _moe_kernel(te_ref, nv_ref, x_ref, w1g_ref, w1u_ref, b1g_ref, b1u_ref, w2_ref, b2_ref, o_ref):
    @pl.when(pl.program_id(0) < nv_ref[0])
    def _():
        x = _join_rows(x_ref).astype(BF16)
        gate = jnp.dot(x, w1g_ref[0], preferred_element_type=F32) + b1g_ref[0]
        up = jnp.dot(x, w1u_ref[0], preferred_element_type=F32) + b1u_ref[0]
        gate = jnp.minimum(gate, SWIGLU_LIMIT)
        up = jnp.clip(up, -SWIGLU_LIMIT, SWIGLU_LIMIT)
        act = (up + 1.0) * (gate * _sigmoid(SWIGLU_ALPHA * gate))
        _split_rows(o_ref, jnp.dot(act.astype(BF16), w2_ref[0], preferred_element_type=F32) + b2_ref[0])

    @pl.when(pl.program_id(0) >= nv_ref[0])
    def _():
        o_ref[...] = jnp.zeros_like(o_ref)


def moe_experts(xs, tile_expert, n_valid, w1g, w1u, b1g, b1u, w2, b2):
    npad = xs.shape[0]
    tm = MOE_TILE
    d, ff = w1g.shape[1:]
    n_tiles = npad // tm
    rows = lambda i, te, nv: (jnp.minimum(i, nv[0] - 1), 0, 0)
    wsel = lambda i, te, nv: (te[i], 0, 0)
    return pl.pallas_call(
        _moe_kernel,
        out_shape=jax.ShapeDtypeStruct(xs.shape, F32),
        grid_spec=pltpu.PrefetchScalarGridSpec(
            num_scalar_prefetch=2,
            grid=(n_tiles,),
            in_specs=[pl.BlockSpec((tm, ROW_SUB, LANES), rows),
                      pl.BlockSpec((1, d, ff), wsel), pl.BlockSpec((1, d, ff), wsel),
                      pl.BlockSpec((1, 1, ff), wsel), pl.BlockSpec((1, 1, ff), wsel),
                      pl.BlockSpec((1, ff, d), wsel), pl.BlockSpec((1, 1, d), wsel)],
            out_specs=pl.BlockSpec((tm, ROW_SUB, LANES), lambda i, te, nv: (i, 0, 0))),
        compiler_params=_cp(("arbitrary",)),
        name="moe_experts",
    )(tile_expert, n_valid, xs, w1g, w1u, b1g, b1u, w2, b2)


def _combine_kernel(x_ref, slots_ref, tw_ref, g2_ref, y_ref):
    tw = tw_ref[...]
    acc = tw[:, 0:1] * _join_rows(slots_ref, 0)
    for s in range(1, TOP_K):
        acc = acc + tw[:, s:s + 1] * _join_rows(slots_ref, s * ROW_SUB)
    y_ref[...] = x_ref[...] + g2_ref[0] * acc


def moe_combine(x, slots, top_w, mods, *, t_ctx, l_lat, tm=256):
    t, d = x.shape
    return pl.pallas_call(
        _combine_kernel,
        out_shape=jax.ShapeDtypeStruct((t, d), F32),
        grid=(t // tm,),
        in_specs=[pl.BlockSpec((tm, d), lambda i: (i, 0)),
                  pl.BlockSpec((tm, TOP_K * ROW_SUB, LANES), lambda i: (i, 0, 0)),
                  pl.BlockSpec((tm, LANES), lambda i: (i, 0)), _mod_spec(5, tm, t_ctx, l_lat)],
        out_specs=pl.BlockSpec((tm, d), lambda i: (i, 0)),
        compiler_params=_cp(("arbitrary",)),
        name="moe_combine",
    )(x, slots, top_w, mods)


def _dispatch_plan(top_i, tm):
    t = top_i.shape[0]
    e = top_i.reshape(-1)
    onehot = (e[:, None] == jnp.arange(N_EXPERTS, dtype=jnp.int32)[None, :]).astype(jnp.int32)
    rank = jnp.sum((jnp.cumsum(onehot, axis=0) - 1) * onehot, axis=1)
    counts = jnp.sum(onehot, axis=0)
    tiles = (counts + tm - 1) // tm
    tile_end = jnp.cumsum(tiles)
    start = (tile_end - tiles) * tm
    dest = (start[e] + rank).astype(jnp.int32)
    n_tiles = (t * TOP_K) // tm + N_EXPERTS
    tile_ids = jnp.arange(n_tiles, dtype=jnp.int32)
    tile_expert = jnp.minimum(jnp.sum((tile_ids[:, None] >= tile_end[None, :]).astype(jnp.int32), axis=1),
                              N_EXPERTS - 1).astype(jnp.int32)
    return dest, n_tiles, tile_expert, tile_end[-1:].astype(jnp.int32)


def kernel(x_prompt, x_sample, c, cache_k, cache_v, state_ssd, c_ctx, norm1_w, norm2_w, w_mod, b_mod, w_in, w_gate,
           b_gate, hy_conv_w, hy_conv_b, hy_w1, hy_b1, hy_w2, hy_b2, hy_w3, hy_freq, hy_decay, hy_bias, ssd_conv_w,
           ssd_conv_b, ssd_a_log, ssd_dt_bias, ssd_d, ssd_norm_w, q_norm_w, k_norm_w, w_br_hy, w_br_ssd, w_br_att,
           w_out, w_router, b_router, w_e1, b_e1, w_e2, b_e2):
    n_ctx, l_ctx, d = x_prompt.shape
    n_lat, l_lat, _ = x_sample.shape
    depth = w_in.shape[0]
    t_ctx, t_lat = n_ctx * l_ctx, n_lat * l_lat
    t = t_ctx + t_lat
    kc = N_KV_HEADS * HEAD_DIM
    geo = dict(t_ctx=t_ctx, l_lat=l_lat)

    x = jnp.concatenate([x_prompt.reshape(t_ctx, d), x_sample.reshape(t_lat, d)], axis=0)
    cvec = jnp.zeros((8, d), F32).at[0].set(c_ctx).at[1:1 + n_lat].set(c)
    mods_all = modulation_all(cvec, w_mod, b_mod)
    n_exp, _, ff2 = w_e1.shape[1:]
    w1g_all, w1u_all = deinterleave_experts(w_e1.reshape(depth * n_exp, d, ff2))
    w1g_all = w1g_all.reshape(depth, n_exp, d, ff2 // 2)
    w1u_all = w1u_all.reshape(depth, n_exp, d, ff2 // 2)

    new_k, new_v, new_s = [], [], []
    for l in range(depth):
        mods = mods_all[l].reshape(8 * 6, 1, d)
        wi = w_in[l]
        w_proj = jnp.concatenate([wi[:, 0:3072], wi[:, 3088:3856], wi[:, 3072:3088],
                                  jnp.zeros((d, P_COLS - 3856), F32)], axis=1).astype(BF16)
        nw1 = norm1_w[l].reshape(1, d)
        proj = norm_mod_matmul(x, nw1, mods, w_proj, jnp.zeros((1, P_COLS), F32), sigmoid=False, out_dtype=F32,
                               **geo)
        gate = norm_mod_matmul(x, nw1, mods, w_gate[l].astype(BF16), b_gate[l].reshape(1, -1), sigmoid=True,
                               out_dtype=BF16, **geo)

        x0, p = hyena_pre(proj, hy_conv_w[l], hy_conv_b[l], t_ctx=t_ctx, l_ctx=l_ctx, l_lat=l_lat)
        hy_args = (hy_w1[l], hy_b1[l], hy_w2[l], hy_b2[l], hy_w3[l], hy_freq[l], hy_decay[l])
        y_hy_ctx = hyena_ctx(x0, p, hyena_filter(l_ctx, *hy_args), hy_bias[l], n_seq=n_ctx, seq_len=l_ctx)
        y_hy_lat = hyena_lat(x0[t_ctx:], p[t_ctx:], hyena_filter(l_lat, *hy_args), hy_bias[l], n_seq=n_lat,
                             seq_len=l_lat)
        y_hy = jnp.concatenate([y_hy_ctx, y_hy_lat], axis=0)

        xbc = ssd_conv(proj, ssd_conv_w[l], ssd_conv_b[l], t_ctx=t_ctx, l_ctx=l_ctx, l_lat=l_lat)
        ssd_args = (ssd_dt_bias[l], ssd_a_log[l], ssd_d[l])
        yf_c, yb_c, fin_c = ssd_scan(xbc, proj, None, *ssd_args, row0=0, n_seq=n_ctx, seq_len=l_ctx)
        yf_l, yb_l, _ = ssd_scan(xbc, proj, state_ssd[:, l], *ssd_args, row0=t_ctx, n_seq=n_lat, seq_len=l_lat)
        yf = jnp.concatenate([yf_c, yf_l], axis=0)
        yb = jnp.concatenate([yb_c, yb_l], axis=0)

        qn, kn = qk_prep(proj, q_norm_w[l], k_norm_w[l], **geo)
        v_all = proj[:, P_V:P_V + kc]
        k_ctx = kn[:t_ctx].reshape(n_ctx, l_ctx, kc)
        v_ctx = v_all[:t_ctx].reshape(n_ctx, l_ctx, kc)
        att_ctx = flash_attention(qn[:t_ctx].reshape(n_ctx, l_ctx, -1), k_ctx.astype(BF16), v_ctx.astype(BF16),
                                  tq=l_ctx, tk=l_ctx)
        k_lat = jnp.concatenate([kn[t_ctx:].reshape(n_lat, l_lat, kc), cache_k[:, l].reshape(n_lat, -1, kc)], axis=1)
        v_lat = jnp.concatenate([v_all[t_ctx:].reshape(n_lat, l_lat, kc), cache_v[:, l].reshape(n_lat, -1, kc)],
                                axis=1)
        att_lat = flash_attention(qn[t_ctx:].reshape(n_lat, l_lat, -1), k_lat.astype(BF16), v_lat.astype(BF16),
                                  tq=256, tk=512)
        y_att = jnp.concatenate([att_ctx.reshape(t_ctx, -1), att_lat.reshape(t_lat, -1)], axis=0)

        wr = jnp.pad(w_router[l], ((0, 0), (0, LANES - N_EXPERTS)))
        br = jnp.pad(b_router[l], (0, LANES - N_EXPERTS)).reshape(1, LANES)
        x, h2, top_i, top_w = merge_router(
            x, y_hy, yf, yb, proj, y_att, gate, ssd_norm_w[l], w_br_hy[l].astype(BF16), w_br_ssd[l].astype(BF16),
            w_br_att[l].astype(BF16), w_out[l].astype(BF16), mods, norm2_w[l].reshape(1, d), wr, br, **geo)

        dest, n_tiles, tile_expert, n_valid = _dispatch_plan(top_i[:, :TOP_K], MOE_TILE)
        xs = row_dma(dest, h2, n_tiles * MOE_TILE, dispatch=True)
        out_sorted = moe_experts(
            xs, tile_expert, n_valid, w1g_all[l], w1u_all[l],
            b_e1[l][:, None, 0::2], b_e1[l][:, None, 1::2], w_e2[l].astype(BF16), b_e2[l][:, None, :])
        slots = row_dma(dest, out_sorted, t * TOP_K, dispatch=False)
        x = moe_combine(x, slots.reshape(t, TOP_K * ROW_SUB, LANES), top_w, mods, **geo)

        new_k.append(k_ctx.reshape(n_ctx, l_ctx, N_KV_HEADS, HEAD_DIM))
        new_v.append(v_ctx.reshape(n_ctx, l_ctx, N_KV_HEADS, HEAD_DIM))
        new_s.append(fin_c)

    y_prompt = x[:t_ctx].reshape(n_ctx, l_ctx, d)
    y_sample = x[t_ctx:].reshape(n_lat, l_lat, d)
    return (y_prompt, y_sample, jnp.stack(new_k, axis=1), jnp.stack(new_v, axis=1), jnp.stack(new_s, axis=1))
```

```python
import functools
import math

import numpy as np
import jax
import jax.numpy as jnp
from jax import lax
from jax.experimental import pallas as pl
from jax.experimental.pallas import tpu as pltpu

F32 = jnp.float32
BF16 = jnp.bfloat16
HI = lax.Precision.HIGHEST

EPS = 1e-6
GRID_W = 64
HY_DIM = 512
SSD_INNER = 512
SSD_HEADDIM = 64
SSD_HEADS = 8
SSD_GROUPS = 2
SSD_STATE = 128
SSD_CHUNK = 128
N_HEADS = 8
N_KV_HEADS = 2
HEAD_DIM = 64
ROPE_THETA = 10000.0
N_EXPERTS = 32
TOP_K = 4
TOP_K_SHIFT = 2
SWIGLU_ALPHA = 1.702
SWIGLU_LIMIT = 7.0

P_HY, P_Z, P_XBC, P_Q, P_K, P_V, P_DT, P_COLS = 0, 1536, 2048, 3072, 3584, 3712, 3840, 3968

VMEM_LIMIT = 56 * 1024 * 1024
LANES = 128
ROW_TILE = 256
MOE_TILE = 512
DFT_N1, DFT_N2 = 64, 128


def _cp(sem, vmem=VMEM_LIMIT):
    return pltpu.CompilerParams(dimension_semantics=sem, vmem_limit_bytes=vmem)


def _sigmoid(x):
    return 1.0 / (1.0 + jnp.exp(-x))


def _silu(x):
    return x * _sigmoid(x)


def _softplus(x):
    return jnp.maximum(x, 0.0) + jnp.log(1.0 + jnp.exp(-jnp.abs(x)))


def _mod_kernel(c_ref, w_ref, b_ref, o_ref):
    s = _silu(c_ref[...])
    o_ref[0] = jnp.dot(s, w_ref[0], precision=HI, preferred_element_type=F32) + b_ref[0]


def modulation_all(cvec, w_mod, b_mod):
    depth, d, n = w_mod.shape
    tn = 1536
    return pl.pallas_call(
        _mod_kernel,
        out_shape=jax.ShapeDtypeStruct((depth, 8, n), F32),
        grid=(depth, n // tn),
        in_specs=[pl.BlockSpec((8, d), lambda l, j: (0, 0)),
                  pl.BlockSpec((1, d, tn), lambda l, j: (l, 0, j)),
                  pl.BlockSpec((1, 1, tn), lambda l, j: (l, 0, j))],
        out_specs=pl.BlockSpec((1, 8, tn), lambda l, j: (l, 0, j)),
        compiler_params=_cp(("arbitrary", "arbitrary")),
        name="modulation",
    )(cvec, w_mod, b_mod.reshape(depth, 1, n))


def _mod_row(i, tm, t_ctx, l_lat):
    n_ctx = t_ctx // tm
    per = l_lat // tm
    return jnp.where(i < n_ctx, 0, 1 + (i - n_ctx) // per)


def _mod_spec(k, tm, t_ctx, l_lat):
    return pl.BlockSpec((1, 1, 1024), lambda i: (_mod_row(i, tm, t_ctx, l_lat) * 6 + k, 0, 0))


def _nmm_kernel(x_ref, nw_ref, sh_ref, sc_ref, w_ref, b_ref, o_ref, *, sigmoid):
    x = x_ref[...]
    ms = jnp.mean(x * x, axis=-1, keepdims=True)
    h = x * lax.rsqrt(ms + EPS) * nw_ref[...]
    h = h * (1.0 + sc_ref[0]) + sh_ref[0]
    acc = jnp.dot(h.astype(BF16), w_ref[...], preferred_element_type=F32) + b_ref[...]
    if sigmoid:
        acc = _sigmoid(acc)
    o_ref[...] = acc.astype(o_ref.dtype)


def norm_mod_matmul(x, nw, mods, w, b, *, t_ctx, l_lat, sigmoid, out_dtype, tm=256):
    t, d = x.shape
    n = w.shape[1]
    return pl.pallas_call(
        functools.partial(_nmm_kernel, sigmoid=sigmoid),
        out_shape=jax.ShapeDtypeStruct((t, n), out_dtype),
        grid=(t // tm,),
        in_specs=[pl.BlockSpec((tm, d), lambda i: (i, 0)),
                  pl.BlockSpec((1, d), lambda i: (0, 0)),
                  _mod_spec(0, tm, t_ctx, l_lat),
                  _mod_spec(1, tm, t_ctx, l_lat),
                  pl.BlockSpec((d, n), lambda i: (0, 0)),
                  pl.BlockSpec((1, n), lambda i: (0, 0))],
        out_specs=pl.BlockSpec((tm, n), lambda i: (i, 0)),
        compiler_params=_cp(("arbitrary",)),
        name="norm_mod_matmul",
    )(x, nw, mods, mods, w, b)


def _seq_edges(i, tr, t_ctx, l_ctx, l_lat):
    tok = i * tr
    pos = jnp.where(tok < t_ctx, tok % l_ctx, (tok - t_ctx) % l_lat)
    length = jnp.where(tok < t_ctx, l_ctx, l_lat)
    return pos == 0, pos + tr == length


def _conv3(x, prev8, next8, w_ref, b_ref, first, last):
    tr = x.shape[0]
    row = lax.broadcasted_iota(jnp.int32, x.shape, 0)
    pm = jnp.where(first, 0.0, 1.0)
    nm = jnp.where(last, 0.0, 1.0)
    xm1 = jnp.where(row == 0, prev8[7:8, :] * pm, pltpu.roll(x, 1, axis=0))
    xp1 = jnp.where(row == tr - 1, next8[0:1, :] * nm, pltpu.roll(x, tr - 1, axis=0))
    return b_ref[...] + xm1 * w_ref[0:1, :] + x * w_ref[1:2, :] + xp1 * w_ref[2:3, :]


def _conv_specs(tr, tc, col_blk, n_rows):
    r8 = tr // 8
    last8 = n_rows // 8 - 1
    return [pl.BlockSpec((tr, tc), lambda i, j: (i, col_blk(j))),
            pl.BlockSpec((8, tc), lambda i, j: (jnp.maximum(i * r8 - 1, 0), col_blk(j))),
            pl.BlockSpec((8, tc), lambda i, j: (jnp.minimum((i + 1) * r8, last8), col_blk(j)))]


def _ssd_conv_kernel(x_ref, p_ref, n_ref, w_ref, b_ref, o_ref, *, tr, t_ctx, l_ctx, l_lat):
    first, last = _seq_edges(pl.program_id(0), tr, t_ctx, l_ctx, l_lat)
    o_ref[...] = _silu(_conv3(x_ref[...], p_ref[...], n_ref[...], w_ref, b_ref, first, last))


def ssd_conv(proj, w, b, *, t_ctx, l_ctx, l_lat):
    t = proj.shape[0]
    tr, tc = ROW_TILE, 512
    c = w.shape[1]
    off = P_XBC // tc
    return pl.pallas_call(
        functools.partial(_ssd_conv_kernel, tr=tr, t_ctx=t_ctx, l_ctx=l_ctx, l_lat=l_lat),
        out_shape=jax.ShapeDtypeStruct((t, c), F32),
        grid=(t // tr, c // tc),
        in_specs=_conv_specs(tr, tc, lambda j: off + j, t) + [
            pl.BlockSpec((3, tc), lambda i, j: (0, j)),
            pl.BlockSpec((1, tc), lambda i, j: (0, j))],
        out_specs=pl.BlockSpec((tr, tc), lambda i, j: (i, j)),
        compiler_params=_cp(("arbitrary", "arbitrary")),
        name="ssd_conv",
    )(proj, proj, proj, w, b.reshape(1, c))


def _hy_pre_kernel(*refs, tr, t_ctx, l_ctx, l_lat):
    (x0, x0p, x0n, x1, x1p, x1n, xv, xvp, xvn, w0, w1, wv, b0, b1, bv, o0_ref, op_ref) = refs
    first, last = _seq_edges(pl.program_id(0), tr, t_ctx, l_ctx, l_lat)
    o0_ref[...] = _conv3(x0[...], x0p[...], x0n[...], w0, b0, first, last)
    u1 = _conv3(x1[...], x1p[...], x1n[...], w1, b1, first, last)
    uv = _conv3(xv[...], xvp[...], xvn[...], wv, bv, first, last)
    op_ref[...] = u1 * uv


def hyena_pre(proj, w, b, *, t_ctx, l_ctx, l_lat):
    t = proj.shape[0]
    tr, tc = ROW_TILE, HY_DIM
    nb = HY_DIM // tc
    b2 = b.reshape(1, 3 * HY_DIM)
    specs = []
    for s in range(3):
        specs += _conv_specs(tr, tc, lambda j, s=s: P_HY // tc + s * nb + j, t)
    specs += [pl.BlockSpec((3, tc), lambda i, j, s=s: (0, s * nb + j)) for s in range(3)]
    specs += [pl.BlockSpec((1, tc), lambda i, j, s=s: (0, s * nb + j)) for s in range(3)]
    return pl.pallas_call(
        functools.partial(_hy_pre_kernel, tr=tr, t_ctx=t_ctx, l_ctx=l_ctx, l_lat=l_lat),
        out_shape=(jax.ShapeDtypeStruct((t, HY_DIM), F32), jax.ShapeDtypeStruct((t, HY_DIM), F32)),
        grid=(t // tr, nb),
        in_specs=specs,
        out_specs=(pl.BlockSpec((tr, tc), lambda i, j: (i, j)), pl.BlockSpec((tr, tc), lambda i, j: (i, j))),
        compiler_params=_cp(("arbitrary", "arbitrary")),
        name="hyena_pre",
    )(*([proj] * 9), w, w, w, b2, b2, b2)


def _filter_kernel(z_ref, w1_ref, b1_ref, w2_ref, b2_ref, w3_ref, fr_ref, dec_ref, o_ref):
    z = z_ref[...]
    fr = fr_ref[...]
    h = jnp.sin(fr * (jnp.dot(z, w1_ref[...], precision=HI, preferred_element_type=F32) + b1_ref[...]))
    h = jnp.sin(fr * (jnp.dot(h, w2_ref[...], precision=HI, preferred_element_type=F32) + b2_ref[...]))
    f = jnp.dot(h, w3_ref[...], precision=HI, preferred_element_type=F32)
    o_ref[...] = f * jnp.exp(-z[:, 0:1] * jnp.abs(dec_ref[...]))


def _filter_embedding(seq_len, emb):
    bands_n = (emb - 1) // 2
    t = jnp.linspace(0.0, 1.0, seq_len, dtype=F32)[:, None]
    bands = jnp.linspace(1e-4, bands_n - 1, bands_n, dtype=F32)[None, :]
    ang = (2.0 * math.pi / seq_len) * jnp.arange(seq_len, dtype=F32)[:, None] * bands
    z = jnp.concatenate([t, jnp.cos(ang), -jnp.sin(ang)], axis=-1)
    return jnp.pad(z, ((0, 0), (0, LANES - emb)))


def hyena_filter(seq_len, w1, b1, w2, b2, w3, freq, decay):
    emb, ff = w1.shape
    n = w3.shape[1]
    z = _filter_embedding(seq_len, emb)
    padc = LANES - ff
    w1p = jnp.pad(w1, ((0, LANES - emb), (0, padc)))
    w2p = jnp.pad(w2, ((0, padc), (0, padc)))
    w3p = jnp.pad(w3, ((0, padc), (0, 0)))
    row = lambda v: jnp.pad(v, (0, padc)).reshape(1, LANES)
    tr = 256
    full = lambda shape: pl.BlockSpec(shape, lambda i: (0, 0))
    return pl.pallas_call(
        _filter_kernel,
        out_shape=jax.ShapeDtypeStruct((seq_len, n), F32),
        grid=(seq_len // tr,),
        in_specs=[pl.BlockSpec((tr, LANES), lambda i: (i, 0)), full((LANES, LANES)), full((1, LANES)),
                  full((LANES, LANES)), full((1, LANES)), full((LANES, n)), full((1, LANES)), full((1, n))],
        out_specs=pl.BlockSpec((tr, n), lambda i: (i, 0)),
        compiler_params=_cp(("arbitrary",)),
        name="hyena_filter",
    )(z, w1p, row(b1), w2p, row(b2), w3p, row(freq), decay.reshape(1, n))


def _circular_filter(filt):
    ch = filt.shape[1] // 2
    h_fwd, h_bwd = filt[:, :ch], filt[:, ch:]
    return jnp.concatenate([h_fwd, jnp.zeros((1, ch), F32), h_bwd[1:][::-1]], axis=0)


def _cs(n_rows, n_cols, period):
    ang = 2.0 * np.pi * (np.outer(np.arange(n_rows), np.arange(n_cols)) % period) / period
    return np.cos(ang), np.sin(ang)


def _mm_kernel(a_ref, b_ref, o_ref):
    o_ref[0] = jnp.dot(a_ref[...], b_ref[0], precision=HI, preferred_element_type=F32)


def const_matmul(a, b, tn):
    m, k = a.shape
    bsz, _, n = b.shape
    return pl.pallas_call(
        _mm_kernel,
        out_shape=jax.ShapeDtypeStruct((bsz, m, n), F32),
        grid=(bsz, n // tn),
        in_specs=[pl.BlockSpec((m, k), lambda s, j: (0, 0)), pl.BlockSpec((1, k, tn), lambda s, j: (s, 0, j))],
        out_specs=pl.BlockSpec((1, m, tn), lambda s, j: (s, 0, j)),
        compiler_params=_cp(("arbitrary", "arbitrary")),
        name="const_matmul",
    )(a, b)


def _hy_ctx_kernel(x0_ref, p_ref, kr_ref, ki_ref, fw_ref, iv_ref, bias_ref, o_ref, *, n):
    p = p_ref[...]
    xf = jnp.dot(fw_ref[...], p, precision=HI, preferred_element_type=F32)
    xr, xi = xf[:n], xf[n:]
    kr, ki = kr_ref[...], ki_ref[...]
    yr = xr * kr - xi * ki
    yi = xr * ki + xi * kr
    y = jnp.dot(iv_ref[...], jnp.concatenate([yr, yi], axis=0), precision=HI, preferred_element_type=F32)
    o_ref[...] = x0_ref[...] * (y + p * bias_ref[...])


def hyena_ctx(x0, p, filt, bias, *, n_seq, seq_len):
    n = 2 * seq_len
    ch = p.shape[1]
    c_full, s_full = _cs(n, n, n)
    fw_full = jnp.asarray(np.concatenate([c_full, -s_full], axis=0), F32)
    fw_half = fw_full[:, :seq_len]
    iv = jnp.asarray(np.concatenate([c_full[:seq_len], -s_full[:seq_len]], axis=1) / n, F32)
    kf = const_matmul(fw_full, _circular_filter(filt)[None], ch)[0]
    kr, ki = kf[:n], kf[n:]
    full = lambda shape: pl.BlockSpec(shape, lambda s: (0, 0))
    return pl.pallas_call(
        functools.partial(_hy_ctx_kernel, n=n),
        out_shape=jax.ShapeDtypeStruct((n_seq * seq_len, ch), F32),
        grid=(n_seq,),
        in_specs=[pl.BlockSpec((seq_len, ch), lambda s: (s, 0)), pl.BlockSpec((seq_len, ch), lambda s: (s, 0)),
                  full((n, ch)), full((n, ch)), full((2 * n, seq_len)), full((seq_len, 2 * n)), full((1, ch))],
        out_specs=pl.BlockSpec((seq_len, ch), lambda s: (s, 0)),
        compiler_params=_cp(("arbitrary",)),
        name="hyena_ctx",
    )(x0, p, kr, ki, fw_half, iv, bias.reshape(1, ch))


def _stage2_kernel(a_ref, twr_ref, twi_ref, m_ref, *rest, conv):
    if conv:
        kf_ref, mi_ref, o_ref = rest
    else:
        (o_ref,) = rest
    ar, ai = a_ref[0, 0, 0], a_ref[0, 1, 0]
    twr, twi = twr_ref[0], twi_ref[0]
    br = ar * twr - ai * twi
    bi = ar * twi + ai * twr
    x = jnp.dot(m_ref[...], jnp.concatenate([br, bi], axis=0), precision=HI, preferred_element_type=F32)
    n2 = ar.shape[0]
    xr, xi = x[:n2], x[n2:]
    if not conv:
        o_ref[0, 0, 0] = xr
        o_ref[0, 1, 0] = xi
        return
    kr, ki = kf_ref[0, 0, 0], kf_ref[0, 1, 0]
    yr = xr * kr - xi * ki
    yi = xr * ki + xi * kr
    pq = jnp.dot(mi_ref[...], jnp.concatenate([yr, yi], axis=0), precision=HI, preferred_element_type=F32)
    pr, pi = pq[:n2], pq[n2:]
    o_ref[0, 0, 0] = pr * twr + pi * twi
    o_ref[0, 1, 0] = pi * twr - pr * twi


def _stage2(a, kf, ch):
    bsz = a.shape[0]
    n1, n2 = DFT_N1, DFT_N2
    n = n1 * n2
    tw_ang = 2.0 * np.pi * np.outer(np.arange(n1), np.arange(n2)) / n
    twr = jnp.asarray(np.cos(tw_ang), F32).reshape(n1, n2, 1)
    twi = jnp.asarray(-np.sin(tw_ang), F32).reshape(n1, n2, 1)
    c2, s2 = _cs(n2, n2, n2)
    m_fwd = jnp.asarray(np.block([[c2, s2], [-s2, c2]]), F32)
    m_inv = jnp.asarray(np.block([[c2, -s2], [s2, c2]]), F32)
    conv = kf is not None
    blk = pl.BlockSpec((1, 2, 1, n2, ch), lambda s, k: (s, 0, k, 0, 0))
    specs = [blk, pl.BlockSpec((1, n2, 1), lambda s, k: (k, 0, 0)), pl.BlockSpec((1, n2, 1), lambda s, k: (k, 0, 0)),
             pl.BlockSpec((2 * n2, 2 * n2), lambda s, k: (0, 0))]
    args = [a, twr, twi, m_fwd]
    if conv:
        specs += [pl.BlockSpec((1, 2, 1, n2, ch), lambda s, k: (0, 0, k, 0, 0)),
                  pl.BlockSpec((2 * n2, 2 * n2), lambda s, k: (0, 0))]
        args += [kf, m_inv]
    return pl.pallas_call(
        functools.partial(_stage2_kernel, conv=conv),
        out_shape=jax.ShapeDtypeStruct(a.shape, F32),
        grid=(bsz, n1),
        in_specs=specs,
        out_specs=blk,
        compiler_params=_cp(("arbitrary", "arbitrary")),
        name="hyena_stage2",
    )(*args)


def _hy_post_kernel(g_ref, q_ref, x0_ref, p_ref, bias_ref, o_ref):
    y = jnp.dot(g_ref[...], q_ref[0], precision=HI, preferred_element_type=F32)
    o_ref[0] = x0_ref[0] * (y + p_ref[0] * bias_ref[...])


def hyena_lat(x0, p, filt, bias, *, n_seq, seq_len):
    n1, n2 = DFT_N1, DFT_N2
    n = n1 * n2
    assert n == 2 * seq_len
    ch = p.shape[1]
    h1 = n1 // 2
    wide = n2 * ch
    c1, s1 = _cs(n1, n1, n1)
    f1_full = jnp.asarray(np.concatenate([c1, -s1], axis=0), F32)
    f1_half = f1_full[:, :h1]
    g1 = jnp.asarray(np.concatenate([c1[:h1], -s1[:h1]], axis=1) / n, F32)
    tn = 8192
    kcirc = _circular_filter(filt).reshape(1, n1, wide)
    kf = _stage2(const_matmul(f1_full, kcirc, tn).reshape(1, 2, n1, n2, ch), None, ch)
    a = const_matmul(f1_half, p.reshape(n_seq, h1, wide), tn).reshape(n_seq, 2, n1, n2, ch)
    q = _stage2(a, kf, ch).reshape(n_seq, 2 * n1, wide)
    bias_w = jnp.tile(bias, n2).reshape(1, wide)
    out = pl.pallas_call(
        _hy_post_kernel,
        out_shape=jax.ShapeDtypeStruct((n_seq, h1, wide), F32),
        grid=(n_seq, wide // tn),
        in_specs=[pl.BlockSpec((h1, 2 * n1), lambda s, j: (0, 0)),
                  pl.BlockSpec((1, 2 * n1, tn), lambda s, j: (s, 0, j)),
                  pl.BlockSpec((1, h1, tn), lambda s, j: (s, 0, j)),
                  pl.BlockSpec((1, h1, tn), lambda s, j: (s, 0, j)),
                  pl.BlockSpec((1, tn), lambda s, j: (0, j))],
        out_specs=pl.BlockSpec((1, h1, tn), lambda s, j: (s, 0, j)),
        compiler_params=_cp(("arbitrary", "arbitrary")),
        name="hyena_post",
    )(g1, q, x0.reshape(n_seq, h1, wide), p.reshape(n_seq, h1, wide), bias_w)
    return out.reshape(n_seq * seq_len, ch)


def _ssd_dir(xbc, dtraw, st_ref, d, consts, y_ref):
    tri, expand, dtb, acont, dskip = consts
    cl = xbc.shape[0]
    xs = xbc[:, :SSD_INNER]
    dt = _softplus(dtraw + dtb)
    a = dt * acont
    tri_d = tri if d == 0 else tri.T
    cs = jnp.dot(tri_d, a, precision=HI, preferred_element_type=F32)
    cs_t = jnp.dot(a.T, tri_d.T, precision=HI, preferred_element_type=F32)
    dt_x = jnp.dot(dt, expand[d], precision=HI, preferred_element_type=F32)
    xdt = xs * dt_x
    row = lax.broadcasted_iota(jnp.int32, (cl, cl), 0)
    col = lax.broadcasted_iota(jnp.int32, (cl, cl), 1)
    keep = (col <= row) if d == 0 else (col >= row)
    lane = lax.broadcasted_iota(jnp.int32, (cl, LANES), 1)
    low = lane < SSD_HEADDIM
    edge = cl - 1 if d == 0 else 0
    for g in range(SSD_GROUPS):
        bg = xbc[:, SSD_INNER + g * SSD_STATE: SSD_INNER + (g + 1) * SSD_STATE]
        cg = xbc[:, SSD_INNER + (SSD_GROUPS + g) * SSD_STATE: SSD_INNER + (SSD_GROUPS + g + 1) * SSD_STATE]
        bg16, cg16 = bg.astype(BF16), cg.astype(BF16)
        cb = lax.dot_general(cg16, bg16, (((1,), (1,)), ((), ())), preferred_element_type=F32)
        for pr in range(2):
            pair = g * 2 + pr
            h0 = 2 * pair
            ms = []
            for h in (h0, h0 + 1):
                ln = d * SSD_HEADS + h
                diff = cs[:, ln:ln + 1] - cs_t[ln:ln + 1, :]
                ms.append(jnp.where(keep, cb * jnp.exp(jnp.minimum(diff, 0.0)), 0.0).astype(BF16))
            xp = xdt[:, pair * LANES:(pair + 1) * LANES]
            xs_p = xs[:, pair * LANES:(pair + 1) * LANES]
            xlo = jnp.where(low, xp, 0.0).astype(BF16)
            xhi = jnp.where(low, 0.0, xp).astype(BF16)
            y_diag = (jnp.dot(ms[0], xlo, preferred_element_type=F32)
                      + jnp.dot(ms[1], xhi, preferred_element_type=F32))
            l0, l1 = d * SSD_HEADS + h0, d * SSD_HEADS + h0 + 1
            e_cs = jnp.where(low, jnp.exp(cs[:, l0:l0 + 1]), jnp.exp(cs[:, l1:l1 + 1]))
            st = st_ref[pair]
            y_off = jnp.dot(cg16, st.astype(BF16), preferred_element_type=F32) * e_cs
            y_ref[:, pair * LANES:(pair + 1) * LANES] = (
                y_diag + y_off + xs_p * dskip[d:d + 1, pair * LANES:(pair + 1) * LANES])
            tot0, tot1 = cs[edge:edge + 1, l0:l0 + 1], cs[edge:edge + 1, l1:l1 + 1]
            dec = jnp.where(low, jnp.exp(tot0 - cs[:, l0:l0 + 1]), jnp.exp(tot1 - cs[:, l1:l1 + 1]))
            upd = lax.dot_general(bg16, (xp * dec).astype(BF16), (((0,), (0,)), ((), ())),
                                  preferred_element_type=F32)
            st_ref[pair] = st * jnp.where(low[0:1], jnp.exp(tot0), jnp.exp(tot1)) + upd


def _ssd_kernel(xf_ref, dtf_ref, xb_ref, dtb_ref, init_ref, tri_ref, exp_ref, dtbias_ref, acont_ref, dskip_ref,
                yf_ref, yb_ref, fin_ref, st_ref, *, has_init):
    s = pl.program_id(1)

    @pl.when(s == 0)
    def _():
        if has_init:
            st_ref[...] = init_ref[0]
        else:
            st_ref[...] = jnp.zeros_like(st_ref)

    consts = (tri_ref[...], (exp_ref[0], exp_ref[1]), dtbias_ref[...], acont_ref[...], dskip_ref[...])
    _ssd_dir(xf_ref[...], dtf_ref[...], st_ref.at[0], 0, consts, yf_ref)
    _ssd_dir(xb_ref[...], dtb_ref[...], st_ref.at[1], 1, consts, yb_ref)

    @pl.when(s == pl.num_programs(1) - 1)
    def _():
        fin_ref[0] = st_ref[...]


def _pair_states(s):
    b = s.shape[0]
    s = s.reshape(b, 2, SSD_HEADS // 2, 2, SSD_HEADDIM, SSD_STATE)
    return s.transpose(0, 1, 2, 5, 3, 4).reshape(b, 2, SSD_HEADS // 2, SSD_STATE, 2 * SSD_HEADDIM)


def _unpair_states(s):
    b = s.shape[0]
    s = s.reshape(b, 2, SSD_HEADS // 2, SSD_STATE, 2, SSD_HEADDIM)
    return s.transpose(0, 1, 2, 4, 5, 3).reshape(b, 2, SSD_HEADS, SSD_HEADDIM, SSD_STATE)


def ssd_scan(xbc, proj, init, dt_bias, a_log, d_skip, *, row0, n_seq, seq_len):
    cl = SSD_CHUNK
    nc = seq_len // cl
    base = row0 // cl
    has_init = init is not None
    hp = SSD_HEADS // 2
    init_p = _pair_states(init) if has_init else jnp.zeros((1, 2, hp, SSD_STATE, LANES), F32)
    tri = jnp.asarray(np.tril(np.ones((cl, cl))), F32)
    expand = np.zeros((2, LANES, SSD_INNER), np.float32)
    for d in range(2):
        for h in range(SSD_HEADS):
            expand[d, d * SSD_HEADS + h, h * SSD_HEADDIM:(h + 1) * SSD_HEADDIM] = 1.0
    pad16 = lambda v: jnp.pad(v.reshape(1, 2 * SSD_HEADS), ((0, 0), (0, LANES - 2 * SSD_HEADS)))
    acont = pad16(-jnp.exp(a_log))
    dtb = pad16(dt_bias)
    dskip = jnp.repeat(d_skip, SSD_HEADDIM, axis=1)
    cxbc = xbc.shape[1]
    dtblk = P_DT // LANES
    full2 = lambda shape: pl.BlockSpec(shape, lambda b, s: (0,) * len(shape))
    fwd = lambda b, s: base + b * nc + s
    bwd = lambda b, s: base + b * nc + (nc - 1 - s)
    st_spec = pl.BlockSpec((1, 2, hp, SSD_STATE, LANES), lambda b, s: (b if has_init else 0, 0, 0, 0, 0))
    yf, yb, fin = pl.pallas_call(
        functools.partial(_ssd_kernel, has_init=has_init),
        out_shape=(jax.ShapeDtypeStruct((n_seq * seq_len, SSD_INNER), F32),
                   jax.ShapeDtypeStruct((n_seq * seq_len, SSD_INNER), F32),
                   jax.ShapeDtypeStruct((n_seq, 2, hp, SSD_STATE, LANES), F32)),
        grid=(n_seq, nc),
        in_specs=[pl.BlockSpec((cl, cxbc), lambda b, s: (fwd(b, s), 0)),
                  pl.BlockSpec((cl, LANES), lambda b, s: (fwd(b, s), dtblk)),
                  pl.BlockSpec((cl, cxbc), lambda b, s: (bwd(b, s), 0)),
                  pl.BlockSpec((cl, LANES), lambda b, s: (bwd(b, s), dtblk)),
                  st_spec, full2((cl, cl)), full2((2, LANES, SSD_INNER)), full2((1, LANES)), full2((1, LANES)),
                  full2((2, SSD_INNER))],
        out_specs=(pl.BlockSpec((cl, SSD_INNER), lambda b, s: (b * nc + s, 0)),
                   pl.BlockSpec((cl, SSD_INNER), lambda b, s: (b * nc + (nc - 1 - s), 0)),
                   pl.BlockSpec((1, 2, hp, SSD_STATE, LANES), lambda b, s: (b, 0, 0, 0, 0))),
        scratch_shapes=[pltpu.VMEM((2, hp, SSD_STATE, LANES), F32)],
        compiler_params=_cp(("arbitrary", "arbitrary")),
        name="ssd_scan",
    )(xbc, proj, xbc, proj, init_p, tri, jnp.asarray(expand), dtb, acont, dskip)
    return yf, yb, _unpair_states(fin)


def _headnorm(x, g_ref, w_ref):
    ms = jnp.dot(x * x, g_ref[...], precision=HI, preferred_element_type=F32)
    return x * lax.rsqrt(ms + EPS) * w_ref[...]


def _rope(x, cos, sin_signed):
    lane = lax.broadcasted_iota(jnp.int32, x.shape, 1)
    w = x.shape[1]
    swapped = jnp.where(lane % 2 == 0, pltpu.roll(x, w - 1, axis=1), pltpu.roll(x, 1, axis=1))
    return x * cos + swapped * sin_signed


def _qk_kernel(q_ref, k_ref, cos_ref, sin_ref, gq_ref, gk_ref, qw_ref, kw_ref, qo_ref, ko_ref, *, n_ctx_tiles):
    q = _headnorm(q_ref[...], gq_ref, qw_ref)
    k = _headnorm(k_ref[...], gk_ref, kw_ref)
    is_lat = pl.program_id(0) >= n_ctx_tiles

    @pl.when(is_lat)
    def _():
        cos, sin = cos_ref[...], sin_ref[...]
        qo_ref[...] = (_rope(q, cos, sin) * HEAD_DIM ** -0.5).astype(qo_ref.dtype)
        ko_ref[...] = _rope(k, cos[:, :k.shape[1]], sin[:, :k.shape[1]])

    @pl.when(jnp.logical_not(is_lat))
    def _():
        qo_ref[...] = (q * HEAD_DIM ** -0.5).astype(qo_ref.dtype)
        ko_ref[...] = k


def _rope_tables(seq_len):
    n_rows = seq_len // GRID_W
    row = jnp.repeat(jnp.arange(n_rows), GRID_W).astype(F32)
    col = jnp.tile(jnp.arange(GRID_W), n_rows).astype(F32)
    n_freq = HEAD_DIM // 4
    inv = ROPE_THETA ** (-jnp.arange(n_freq, dtype=F32) / n_freq)
    ang = jnp.concatenate([row[:, None] * inv, col[:, None] * inv], axis=-1)
    cos = jnp.repeat(jnp.cos(ang), 2, axis=1)
    sin = jnp.repeat(jnp.sin(ang), 2, axis=1) * jnp.tile(jnp.asarray([-1.0, 1.0], F32), HEAD_DIM // 2)
    return jnp.tile(cos, (1, N_HEADS)), jnp.tile(sin, (1, N_HEADS))


def qk_prep(proj, q_norm_w, k_norm_w, *, t_ctx, l_lat):
    t = proj.shape[0]
    tr = ROW_TILE
    qc, kc = N_HEADS * HEAD_DIM, N_KV_HEADS * HEAD_DIM
    cos, sin = _rope_tables(l_lat)
    group = lambda c: jnp.asarray(np.kron(np.eye(c // HEAD_DIM), np.ones((HEAD_DIM, HEAD_DIM))) / HEAD_DIM, F32)
    n_ctx_tiles = t_ctx // tr
    per = l_lat // tr
    tab = pl.BlockSpec((tr, qc), lambda i: (jnp.maximum(i - n_ctx_tiles, 0) % per, 0))
    full = lambda shape: pl.BlockSpec(shape, lambda i: (0, 0))
    return pl.pallas_call(
        functools.partial(_qk_kernel, n_ctx_tiles=n_ctx_tiles),
        out_shape=(jax.ShapeDtypeStruct((t, qc), BF16), jax.ShapeDtypeStruct((t, kc), F32)),
        grid=(t // tr,),
        in_specs=[pl.BlockSpec((tr, qc), lambda i: (i, P_Q // qc)), pl.BlockSpec((tr, kc), lambda i: (i, P_K // kc)),
                  tab, tab, full((qc, qc)), full((kc, kc)), full((1, qc)), full((1, kc))],
        out_specs=(pl.BlockSpec((tr, qc), lambda i: (i, 0)), pl.BlockSpec((tr, kc), lambda i: (i, 0))),
        compiler_params=_cp(("arbitrary",)),
        name="qk_prep",
    )(proj, proj, cos, sin, group(qc), group(kc), jnp.tile(q_norm_w, N_HEADS).reshape(1, qc),
      jnp.tile(k_norm_w, N_KV_HEADS).reshape(1, kc))


def _flash_kernel(q_ref, k_ref, v_ref, o_ref, qs_ref, m_ref, l_ref, acc_ref, *, tq):
    ki = pl.program_id(2)
    rep = N_HEADS // N_KV_HEADS

    @pl.when(ki == 0)
    def _():
        q = q_ref[0]
        for g in range(N_KV_HEADS):
            for r in range(rep):
                h = g * rep + r
                qs_ref[g, r * tq:(r + 1) * tq, :] = q[:, h * HEAD_DIM:(h + 1) * HEAD_DIM]
        m_ref[...] = jnp.full_like(m_ref, -jnp.inf)
        l_ref[...] = jnp.zeros_like(l_ref)
        acc_ref[...] = jnp.zeros_like(acc_ref)

    k = k_ref[0]
    v = v_ref[0]
    for g in range(N_KV_HEADS):
        kg = k[:, g * HEAD_DIM:(g + 1) * HEAD_DIM]
        vg = v[:, g * HEAD_DIM:(g + 1) * HEAD_DIM]
        s = lax.dot_general(qs_ref[g], kg, (((1,), (1,)), ((), ())), preferred_element_type=F32)
        m_old = m_ref[g]
        m_new = jnp.maximum(m_old, jnp.max(s, axis=-1, keepdims=True))
        alpha = jnp.exp(m_old - m_new)
        p = jnp.exp(s - m_new)
        l_ref[g] = alpha * l_ref[g] + jnp.sum(p, axis=-1, keepdims=True)
        acc_ref[g] = alpha * acc_ref[g] + jnp.dot(p.astype(BF16), vg, preferred_element_type=F32)
        m_ref[g] = m_new

    @pl.when(ki == pl.num_programs(2) - 1)
    def _():
        for g in range(N_KV_HEADS):
            out = acc_ref[g] / l_ref[g]
            for r in range(rep):
                h = g * rep + r
                o_ref[0, :, h * HEAD_DIM:(h + 1) * HEAD_DIM] = out[r * tq:(r + 1) * tq].astype(o_ref.dtype)


def flash_attention(q, k, v, *, tq, tk):
    b, lq, qc = q.shape
    lk, kc = k.shape[1], k.shape[2]
    rep = N_HEADS // N_KV_HEADS
    return pl.pallas_call(
        functools.partial(_flash_kernel, tq=tq),
        out_shape=jax.ShapeDtypeStruct((b, lq, qc), BF16),
        grid=(b, lq // tq, lk // tk),
        in_specs=[pl.BlockSpec((1, tq, qc), lambda s, i, j: (s, i, 0)),
                  pl.BlockSpec((1, tk, kc), lambda s, i, j: (s, j, 0)),
                  pl.BlockSpec((1, tk, kc), lambda s, i, j: (s, j, 0))],
        out_specs=pl.BlockSpec((1, tq, qc), lambda s, i, j: (s, i, 0)),
        scratch_shapes=[pltpu.VMEM((N_KV_HEADS, rep * tq, HEAD_DIM), BF16),
                        pltpu.VMEM((N_KV_HEADS, rep * tq, 1), F32),
                        pltpu.VMEM((N_KV_HEADS, rep * tq, 1), F32),
                        pltpu.VMEM((N_KV_HEADS, rep * tq, HEAD_DIM), F32)],
        compiler_params=_cp(("arbitrary", "arbitrary", "arbitrary")),
        name="flash_attention",
    )(q, k, v)


def _merge_kernel(x_ref, yhy_ref, yf_ref, yb_ref, z_ref, yatt_ref, gate_ref, snw_ref, whb_ref, wsb_ref, wab_ref,
                  wout_ref, g1_ref, n2w_ref, sh2_ref, sc2_ref, wr_ref, br_ref,
                  xo_ref, h2_ref, ti_ref, tw_ref):
    d = x_ref.shape[1]
    ys = (yf_ref[...] + yb_ref[...]) * _silu(z_ref[...])
    ys = ys * lax.rsqrt(jnp.mean(ys * ys, axis=-1, keepdims=True) + EPS) * snw_ref[...]
    gate = gate_ref[...].astype(F32)
    merged = (gate[:, :d] * jnp.dot(yhy_ref[...].astype(BF16), whb_ref[...], preferred_element_type=F32)
              + gate[:, d:2 * d] * jnp.dot(ys.astype(BF16), wsb_ref[...], preferred_element_type=F32)
              + gate[:, 2 * d:] * jnp.dot(yatt_ref[...], wab_ref[...], preferred_element_type=F32))
    mix = jnp.dot(merged.astype(BF16), wout_ref[...], preferred_element_type=F32)
    x = x_ref[...] + g1_ref[0] * mix
    xo_ref[...] = x
    h = x * lax.rsqrt(jnp.mean(x * x, axis=-1, keepdims=True) + EPS) * n2w_ref[...]
    h = h * (1.0 + sc2_ref[0]) + sh2_ref[0]
    _split_rows(h2_ref, h)
    logits = jnp.dot(h, wr_ref[...], precision=HI, preferred_element_type=F32) + br_ref[...]
    lane = lax.broadcasted_iota(jnp.int32, logits.shape, 1)
    work = jnp.where(lane < N_EXPERTS, logits, -jnp.inf)
    idx_out = jnp.zeros(logits.shape, jnp.int32)
    val_out = jnp.full(logits.shape, -jnp.inf, F32)
    for j in range(TOP_K):
        mx = jnp.max(work, axis=-1, keepdims=True)
        am = jnp.min(jnp.where(work == mx, lane, LANES), axis=-1, keepdims=True)
        idx_out = jnp.where(lane == j, am, idx_out)
        val_out = jnp.where(lane == j, mx, val_out)
        work = jnp.where(lane == am, -jnp.inf, work)
    e = jnp.exp(val_out - jnp.max(val_out, axis=-1, keepdims=True))
    ti_ref[...] = idx_out
    tw_ref[...] = e / jnp.sum(e, axis=-1, keepdims=True)


def merge_router(x, y_hy, yf, yb, proj, y_att, gate, ssd_norm_w, whb, wsb, wab, wout, mods, n2w, wr, br,
                 *, t_ctx, l_lat, tm=256):
    t, d = x.shape
    row = lambda tc, c0=0: pl.BlockSpec((tm, tc), lambda i: (i, c0))
    full = lambda shape: pl.BlockSpec(shape, lambda i: (0, 0))
    return pl.pallas_call(
        _merge_kernel,
        out_shape=(jax.ShapeDtypeStruct((t, d), F32), jax.ShapeDtypeStruct((t, ROW_SUB, LANES), F32),
                   jax.ShapeDtypeStruct((t, LANES), jnp.int32), jax.ShapeDtypeStruct((t, LANES), F32)),
        grid=(t // tm,),
        in_specs=[row(d), row(HY_DIM), row(SSD_INNER), row(SSD_INNER), row(SSD_INNER, P_Z // SSD_INNER),
                  row(N_HEADS * HEAD_DIM), row(3 * d), full((1, SSD_INNER)),
                  full(whb.shape), full(wsb.shape), full(wab.shape), full(wout.shape),
                  _mod_spec(2, tm, t_ctx, l_lat), full((1, d)), _mod_spec(3, tm, t_ctx, l_lat),
                  _mod_spec(4, tm, t_ctx, l_lat), full((d, LANES)), full((1, LANES))],
        out_specs=(row(d), pl.BlockSpec((tm, ROW_SUB, LANES), lambda i: (i, 0, 0)), row(LANES), row(LANES)),
        compiler_params=_cp(("arbitrary",)),
        name="merge_router",
    )(x, y_hy, yf, yb, proj, y_att, gate, ssd_norm_w.reshape(1, -1), whb, wsb, wab, wout, mods, n2w, mods, mods,
      wr, br)


DEINT_BLOCK = 256


def _deinterleave_kernel(w_ref, perm_ref, g_ref, u_ref):
    half = DEINT_BLOCK // 2
    for blk in range(w_ref.shape[2] // DEINT_BLOCK):
        wb = w_ref[0, :, blk * DEINT_BLOCK:(blk + 1) * DEINT_BLOCK].astype(BF16)
        r = jnp.dot(wb, perm_ref[...], preferred_element_type=F32)
        g_ref[0, :, blk * half:(blk + 1) * half] = r[:, :half].astype(BF16)
        u_ref[0, :, blk * half:(blk + 1) * half] = r[:, half:].astype(BF16)


def deinterleave_experts(w1):
    e, d, f2 = w1.shape
    perm = np.zeros((DEINT_BLOCK, DEINT_BLOCK), np.float32)
    half = DEINT_BLOCK // 2
    perm[2 * np.arange(half), np.arange(half)] = 1.0
    perm[2 * np.arange(half) + 1, half + np.arange(half)] = 1.0
    out = jax.ShapeDtypeStruct((e, d, f2 // 2), BF16)
    return pl.pallas_call(
        _deinterleave_kernel,
        out_shape=(out, out),
        grid=(e,),
        in_specs=[pl.BlockSpec((1, d, f2), lambda i: (i, 0, 0)),
                  pl.BlockSpec((DEINT_BLOCK, DEINT_BLOCK), lambda i: (0, 0))],
        out_specs=(pl.BlockSpec((1, d, f2 // 2), lambda i: (i, 0, 0)),
                   pl.BlockSpec((1, d, f2 // 2), lambda i: (i, 0, 0))),
        compiler_params=_cp(("arbitrary",)),
        name="deinterleave_experts",
    )(w1, jnp.asarray(perm, BF16))


ROW_SUB = 8
DMA_CHUNK = 512


def _split_rows(ref, x):
    for s in range(ROW_SUB):
        ref[:, s, :] = x[:, s * LANES:(s + 1) * LANES]


def _join_rows(ref, first=0):
    return jnp.concatenate([ref[:, first + s, :] for s in range(ROW_SUB)], axis=1)


def _chunk_wait(row_ref, dst_ref, sem):
    del row_ref
    pltpu.make_async_copy(dst_ref.at[pl.ds(0, DMA_CHUNK)], dst_ref.at[pl.ds(0, DMA_CHUNK)], sem).wait()


def _dispatch_kernel(idx_ref, x_ref, zero_ref, dst_ref, sem):
    del zero_ref

    def issue(r, carry):
        tok = lax.shift_right_logical(r, TOP_K_SHIFT)
        pltpu.make_async_copy(x_ref.at[tok], dst_ref.at[idx_ref[0, 0, r]], sem).start()
        return carry

    lax.fori_loop(0, DMA_CHUNK, issue, 0, unroll=8)
    _chunk_wait(x_ref, dst_ref, sem)


def moe_dispatch(dest, h2, n_dst):
    n = dest.shape[0]
    nc = n // DMA_CHUNK
    tok = DMA_CHUNK // TOP_K
    shape = (n_dst,) + h2.shape[1:]
    return pl.pallas_call(
        _dispatch_kernel,
        out_shape=jax.ShapeDtypeStruct(shape, h2.dtype),
        grid=(nc,),
        in_specs=[pl.BlockSpec((1, 1, DMA_CHUNK), lambda c: (c, 0, 0), memory_space=pltpu.SMEM),
                  pl.BlockSpec((tok, ROW_SUB, LANES), lambda c: (c, 0, 0)),
                  pl.BlockSpec(memory_space=pl.ANY)],
        out_specs=pl.BlockSpec(memory_space=pl.ANY),
        scratch_shapes=[pltpu.SemaphoreType.DMA(())],
        input_output_aliases={2: 0},
        compiler_params=_cp(("arbitrary",)),
        name="moe_dispatch",
    )(dest.reshape(nc, 1, DMA_CHUNK), h2, jnp.zeros(shape, h2.dtype))


def _moe_kernel(te_ref, nv_ref, x_ref, w1g_ref, w1u_ref, b1g_ref, b1u_ref, w2_ref, b2_ref, o_ref):
    @pl.when(pl.program_id(0) < nv_ref[0])
    def _():
        x = _join_rows(x_ref).astype(BF16)
        gate = jnp.dot(x, w1g_ref[0], preferred_element_type=F32) + b1g_ref[0]
        up = jnp.dot(x, w1u_ref[0], preferred_element_type=F32) + b1u_ref[0]
        gate = jnp.minimum(gate, SWIGLU_LIMIT)
        up = jnp.clip(up, -SWIGLU_LIMIT, SWIGLU_LIMIT)
        act = (up + 1.0) * (gate * _sigmoid(SWIGLU_ALPHA * gate))
        _split_rows(o_ref, jnp.dot(act.astype(BF16), w2_ref[0], preferred_element_type=F32) + b2_ref[0])

    @pl.when(pl.program_id(0) >= nv_ref[0])
    def _():
        o_ref[...] = jnp.zeros_like(o_ref)


def moe_experts(xs, tile_expert, n_valid, w1g, w1u, b1g, b1u, w2, b2):
    npad = xs.shape[0]
    tm = MOE_TILE
    d, ff = w1g.shape[1:]
    n_tiles = npad // tm
    rows = lambda i, te, nv: (jnp.minimum(i, nv[0] - 1), 0, 0)
    wsel = lambda i, te, nv: (te[i], 0, 0)
    return pl.pallas_call(
        _moe_kernel,
        out_shape=jax.ShapeDtypeStruct(xs.shape, F32),
        grid_spec=pltpu.PrefetchScalarGridSpec(
            num_scalar_prefetch=2,
            grid=(n_tiles,),
            in_specs=[pl.BlockSpec((tm, ROW_SUB, LANES), rows),
                      pl.BlockSpec((1, d, ff), wsel), pl.BlockSpec((1, d, ff), wsel),
                      pl.BlockSpec((1, 1, ff), wsel), pl.BlockSpec((1, 1, ff), wsel),
                      pl.BlockSpec((1, ff, d), wsel), pl.BlockSpec((1, 1, d), wsel)],
            out_specs=pl.BlockSpec((tm, ROW_SUB, LANES), lambda i, te, nv: (i, 0, 0))),
        compiler_params=_cp(("arbitrary",)),
        name="moe_experts",
    )(tile_expert, n_valid, xs, w1g, w1u, b1g, b1u, w2, b2)


def _combine_kernel(idx_ref, nxt_ref, x_ref, rows_ref, tw_ref, g2_ref, y_ref, buf_ref, sems):
    i = pl.program_id(0)
    slot = i % 2

    def gather(ids_ref, s):
        def issue(r, carry):
            tok = lax.shift_right_logical(r, TOP_K_SHIFT)
            sub = pl.multiple_of((r & (TOP_K - 1)) * ROW_SUB, ROW_SUB)
            dst = buf_ref.at[s, tok, pl.ds(sub, ROW_SUB), :]
            pltpu.make_async_copy(rows_ref.at[ids_ref[0, 0, r]], dst, sems.at[s]).start()
            return carry
        lax.fori_loop(0, DMA_CHUNK, issue, 0, unroll=8)

    @pl.when(i == 0)
    def _():
        gather(idx_ref, 0)

    @pl.when(i + 1 < pl.num_programs(0))
    def _():
        gather(nxt_ref, 1 - slot)

    _chunk_wait(None, rows_ref, sems.at[slot])
    tw = tw_ref[...]
    cur = buf_ref.at[slot]
    acc = tw[:, 0:1] * _join_rows(cur, 0)
    for s in range(1, TOP_K):
        acc = acc + tw[:, s:s + 1] * _join_rows(cur, s * ROW_SUB)
    y_ref[...] = x_ref[...] + g2_ref[0] * acc


def moe_combine(x, dest, rows, top_w, mods, *, t_ctx, l_lat):
    t, d = x.shape
    tm = DMA_CHUNK // TOP_K
    nt = t // tm
    ids = dest.reshape(nt, 1, DMA_CHUNK)
    smem = lambda fn: pl.BlockSpec((1, 1, DMA_CHUNK), fn, memory_space=pltpu.SMEM)
    return pl.pallas_call(
        _combine_kernel,
        out_shape=jax.ShapeDtypeStruct((t, d), F32),
        grid=(nt,),
        in_specs=[smem(lambda i: (i, 0, 0)), smem(lambda i: (jnp.minimum(i + 1, nt - 1), 0, 0)),
                  pl.BlockSpec((tm, d), lambda i: (i, 0)), pl.BlockSpec(memory_space=pl.ANY),
                  pl.BlockSpec((tm, LANES), lambda i: (i, 0)), _mod_spec(5, tm, t_ctx, l_lat)],
        out_specs=pl.BlockSpec((tm, d), lambda i: (i, 0)),
        scratch_shapes=[pltpu.VMEM((2, tm, TOP_K * ROW_SUB, LANES), F32), pltpu.SemaphoreType.DMA((2,))],
        compiler_params=_cp(("arbitrary",)),
        name="moe_combine",
    )(ids, ids, x, rows, top_w, mods)


def _dispatch_plan(top_i, tm):
    t = top_i.shape[0]
    e = top_i.reshape(-1)
    onehot = (e[:, None] == jnp.arange(N_EXPERTS, dtype=jnp.int32)[None, :]).astype(jnp.int32)
    rank = jnp.sum((jnp.cumsum(onehot, axis=0) - 1) * onehot, axis=1)
    counts = jnp.sum(onehot, axis=0)
    tiles = (counts + tm - 1) // tm
    tile_end = jnp.cumsum(tiles)
    start = (tile_end - tiles) * tm
    dest = (start[e] + rank).astype(jnp.int32)
    n_tiles = (t * TOP_K) // tm + N_EXPERTS
    tile_ids = jnp.arange(n_tiles, dtype=jnp.int32)
    tile_expert = jnp.minimum(jnp.sum((tile_ids[:, None] >= tile_end[None, :]).astype(jnp.int32), axis=1),
                              N_EXPERTS - 1).astype(jnp.int32)
    return dest, n_tiles, tile_expert, tile_end[-1:].astype(jnp.int32)


def kernel(x_prompt, x_sample, c, cache_k, cache_v, state_ssd, c_ctx, norm1_w, norm2_w, w_mod, b_mod, w_in, w_gate,
           b_gate, hy_conv_w, hy_conv_b, hy_w1, hy_b1, hy_w2, hy_b2, hy_w3, hy_freq, hy_decay, hy_bias, ssd_conv_w,
           ssd_conv_b, ssd_a_log, ssd_dt_bias, ssd_d, ssd_norm_w, q_norm_w, k_norm_w, w_br_hy, w_br_ssd, w_br_att,
           w_out, w_router, b_router, w_e1, b_e1, w_e2, b_e2):
    n_ctx, l_ctx, d = x_prompt.shape
    n_lat, l_lat, _ = x_sample.shape
    depth = w_in.shape[0]
    t_ctx, t_lat = n_ctx * l_ctx, n_lat * l_lat
    t = t_ctx + t_lat
    kc = N_KV_HEADS * HEAD_DIM
    geo = dict(t_ctx=t_ctx, l_lat=l_lat)

    x = jnp.concatenate([x_prompt.reshape(t_ctx, d), x_sample.reshape(t_lat, d)], axis=0)
    cvec = jnp.zeros((8, d), F32).at[0].set(c_ctx).at[1:1 + n_lat].set(c)
    mods_all = modulation_all(cvec, w_mod, b_mod)
    n_exp, _, ff2 = w_e1.shape[1:]
    w1g_all, w1u_all = deinterleave_experts(w_e1.reshape(depth * n_exp, d, ff2))
    w1g_all = w1g_all.reshape(depth, n_exp, d, ff2 // 2)
    w1u_all = w1u_all.reshape(depth, n_exp, d, ff2 // 2)

    new_k, new_v, new_s = [], [], []
    for l in range(depth):
        mods = mods_all[l].reshape(8 * 6, 1, d)
        wi = w_in[l]
        w_proj = jnp.concatenate([wi[:, 0:3072], wi[:, 3088:3856], wi[:, 3072:3088],
                                  jnp.zeros((d, P_COLS - 3856), F32)], axis=1).astype(BF16)
        nw1 = norm1_w[l].reshape(1, d)
        proj = norm_mod_matmul(x, nw1, mods, w_proj, jnp.zeros((1, P_COLS), F32), sigmoid=False, out_dtype=F32,
                               **geo)
        gate = norm_mod_matmul(x, nw1, mods, w_gate[l].astype(BF16), b_gate[l].reshape(1, -1), sigmoid=True,
                               out_dtype=BF16, **geo)

        x0, p = hyena_pre(proj, hy_conv_w[l], hy_conv_b[l], t_ctx=t_ctx, l_ctx=l_ctx, l_lat=l_lat)
        hy_args = (hy_w1[l], hy_b1[l], hy_w2[l], hy_b2[l], hy_w3[l], hy_freq[l], hy_decay[l])
        y_hy_ctx = hyena_ctx(x0, p, hyena_filter(l_ctx, *hy_args), hy_bias[l], n_seq=n_ctx, seq_len=l_ctx)
        y_hy_lat = hyena_lat(x0[t_ctx:], p[t_ctx:], hyena_filter(l_lat, *hy_args), hy_bias[l], n_seq=n_lat,
                             seq_len=l_lat)
        y_hy = jnp.concatenate([y_hy_ctx, y_hy_lat], axis=0)

        xbc = ssd_conv(proj, ssd_conv_w[l], ssd_conv_b[l], t_ctx=t_ctx, l_ctx=l_ctx, l_lat=l_lat)
        ssd_args = (ssd_dt_bias[l], ssd_a_log[l], ssd_d[l])
        yf_c, yb_c, fin_c = ssd_scan(xbc, proj, None, *ssd_args, row0=0, n_seq=n_ctx, seq_len=l_ctx)
        yf_l, yb_l, _ = ssd_scan(xbc, proj, state_ssd[:, l], *ssd_args, row0=t_ctx, n_seq=n_lat, seq_len=l_lat)
        yf = jnp.concatenate([yf_c, yf_l], axis=0)
        yb = jnp.concatenate([yb_c, yb_l], axis=0)

        qn, kn = qk_prep(proj, q_norm_w[l], k_norm_w[l], **geo)
        v_all = proj[:, P_V:P_V + kc]
        k_ctx = kn[:t_ctx].reshape(n_ctx, l_ctx, kc)
        v_ctx = v_all[:t_ctx].reshape(n_ctx, l_ctx, kc)
        att_ctx = flash_attention(qn[:t_ctx].reshape(n_ctx, l_ctx, -1), k_ctx.astype(BF16), v_ctx.astype(BF16),
                                  tq=l_ctx, tk=l_ctx)
        k_lat = jnp.concatenate([kn[t_ctx:].reshape(n_lat, l_lat, kc), cache_k[:, l].reshape(n_lat, -1, kc)], axis=1)
        v_lat = jnp.concatenate([v_all[t_ctx:].reshape(n_lat, l_lat, kc), cache_v[:, l].reshape(n_lat, -1, kc)],
                                axis=1)
        att_lat = flash_attention(qn[t_ctx:].reshape(n_lat, l_lat, -1), k_lat.astype(BF16), v_lat.astype(BF16),
                                  tq=256, tk=512)
        y_att = jnp.concatenate([att_ctx.reshape(t_ctx, -1), att_lat.reshape(t_lat, -1)], axis=0)

        wr = jnp.pad(w_router[l], ((0, 0), (0, LANES - N_EXPERTS)))
        br = jnp.pad(b_router[l], (0, LANES - N_EXPERTS)).reshape(1, LANES)
        x, h2, top_i, top_w = merge_router(
            x, y_hy, yf, yb, proj, y_att, gate, ssd_norm_w[l], w_br_hy[l].astype(BF16), w_br_ssd[l].astype(BF16),
            w_br_att[l].astype(BF16), w_out[l].astype(BF16), mods, norm2_w[l].reshape(1, d), wr, br, **geo)

        dest, n_tiles, tile_expert, n_valid = _dispatch_plan(top_i[:, :TOP_K], MOE_TILE)
        xs = moe_dispatch(dest, h2, n_tiles * MOE_TILE)
        out_sorted = moe_experts(
            xs, tile_expert, n_valid, w1g_all[l], w1u_all[l],
            b_e1[l][:, None, 0::2], b_e1[l][:, None, 1::2], w_e2[l].astype(BF16), b_e2[l][:, None, :])
        x = moe_combine(x, dest, out_sorted, top_w, mods, **geo)

        new_k.append(k_ctx.reshape(n_ctx, l_ctx, N_KV_HEADS, HEAD_DIM))
        new_v.append(v_ctx.reshape(n_ctx, l_ctx, N_KV_HEADS, HEAD_DIM))
        new_s.append(fin_c)

    y_prompt = x[:t_ctx].reshape(n_ctx, l_ctx, d)
    y_sample = x[t_ctx:].reshape(n_lat, l_lat, d)
    return (y_prompt, y_sample, jnp.stack(new_k, axis=1), jnp.stack(new_v, axis=1), jnp.stack(new_s, axis=1))
```

```python
import functools
import math

import numpy as np
import jax
import jax.numpy as jnp
from jax import lax
from jax.experimental import pallas as pl
from jax.experimental.pallas import tpu as pltpu

F32 = jnp.float32
BF16 = jnp.bfloat16
HI = lax.Precision.HIGHEST

EPS = 1e-6
GRID_W = 64
HY_DIM = 512
SSD_INNER = 512
SSD_HEADDIM = 64
SSD_HEADS = 8
SSD_GROUPS = 2
SSD_STATE = 128
SSD_CHUNK = 128
N_HEADS = 8
N_KV_HEADS = 2
HEAD_DIM = 64
ROPE_THETA = 10000.0
N_EXPERTS = 32
TOP_K = 4
TOP_K_SHIFT = 2
SWIGLU_ALPHA = 1.702
SWIGLU_LIMIT = 7.0

P_HY, P_Z, P_XBC, P_Q, P_K, P_V, P_DT, P_COLS = 0, 1536, 2048, 3072, 3584, 3712, 3840, 3968

VMEM_LIMIT = 56 * 1024 * 1024
LANES = 128
ROW_TILE = 256
MOE_TILE = 512
DFT_N1, DFT_N2 = 64, 128


def _cp(sem, vmem=VMEM_LIMIT):
    return pltpu.CompilerParams(dimension_semantics=sem, vmem_limit_bytes=vmem)


def _sigmoid(x):
    return 1.0 / (1.0 + jnp.exp(-x))


def _silu(x):
    return x * _sigmoid(x)


def _softplus(x):
    return jnp.maximum(x, 0.0) + jnp.log(1.0 + jnp.exp(-jnp.abs(x)))


def _mod_kernel(c_ref, w_ref, b_ref, o_ref):
    s = _silu(c_ref[...])
    o_ref[0] = jnp.dot(s, w_ref[0], precision=HI, preferred_element_type=F32) + b_ref[0]


def modulation_all(cvec, w_mod, b_mod):
    depth, d, n = w_mod.shape
    tn = 1536
    return pl.pallas_call(
        _mod_kernel,
        out_shape=jax.ShapeDtypeStruct((depth, 8, n), F32),
        grid=(depth, n // tn),
        in_specs=[pl.BlockSpec((8, d), lambda l, j: (0, 0)),
                  pl.BlockSpec((1, d, tn), lambda l, j: (l, 0, j)),
                  pl.BlockSpec((1, 1, tn), lambda l, j: (l, 0, j))],
        out_specs=pl.BlockSpec((1, 8, tn), lambda l, j: (l, 0, j)),
        compiler_params=_cp(("arbitrary", "arbitrary")),
        name="modulation",
    )(cvec, w_mod, b_mod.reshape(depth, 1, n))


def _mod_row(i, tm, t_ctx, l_lat):
    n_ctx = t_ctx // tm
    per = l_lat // tm
    return jnp.where(i < n_ctx, 0, 1 + (i - n_ctx) // per)


def _mod_spec(k, tm, t_ctx, l_lat):
    return pl.BlockSpec((1, 1, 1024), lambda i: (_mod_row(i, tm, t_ctx, l_lat) * 6 + k, 0, 0))


def _nmm_kernel(x_ref, nw_ref, sh_ref, sc_ref, w_ref, b_ref, o_ref, *, sigmoid):
    x = x_ref[...]
    ms = jnp.mean(x * x, axis=-1, keepdims=True)
    h = x * lax.rsqrt(ms + EPS) * nw_ref[...]
    h = h * (1.0 + sc_ref[0]) + sh_ref[0]
    acc = jnp.dot(h.astype(BF16), w_ref[...], preferred_element_type=F32) + b_ref[...]
    if sigmoid:
        acc = _sigmoid(acc)
    o_ref[...] = acc.astype(o_ref.dtype)


def norm_mod_matmul(x, nw, mods, w, b, *, t_ctx, l_lat, sigmoid, out_dtype, tm=256):
    t, d = x.shape
    n = w.shape[1]
    return pl.pallas_call(
        functools.partial(_nmm_kernel, sigmoid=sigmoid),
        out_shape=jax.ShapeDtypeStruct((t, n), out_dtype),
        grid=(t // tm,),
        in_specs=[pl.BlockSpec((tm, d), lambda i: (i, 0)),
                  pl.BlockSpec((1, d), lambda i: (0, 0)),
                  _mod_spec(0, tm, t_ctx, l_lat),
                  _mod_spec(1, tm, t_ctx, l_lat),
                  pl.BlockSpec((d, n), lambda i: (0, 0)),
                  pl.BlockSpec((1, n), lambda i: (0, 0))],
        out_specs=pl.BlockSpec((tm, n), lambda i: (i, 0)),
        compiler_params=_cp(("arbitrary",)),
        name="norm_mod_matmul",
    )(x, nw, mods, mods, w, b)


def _seq_edges(i, tr, t_ctx, l_ctx, l_lat):
    tok = i * tr
    pos = jnp.where(tok < t_ctx, tok % l_ctx, (tok - t_ctx) % l_lat)
    length = jnp.where(tok < t_ctx, l_ctx, l_lat)
    return pos == 0, pos + tr == length


def _conv3(x, prev8, next8, w_ref, b_ref, first, last):
    tr = x.shape[0]
    row = lax.broadcasted_iota(jnp.int32, x.shape, 0)
    pm = jnp.where(first, 0.0, 1.0)
    nm = jnp.where(last, 0.0, 1.0)
    xm1 = jnp.where(row == 0, prev8[7:8, :] * pm, pltpu.roll(x, 1, axis=0))
    xp1 = jnp.where(row == tr - 1, next8[0:1, :] * nm, pltpu.roll(x, tr - 1, axis=0))
    return b_ref[...] + xm1 * w_ref[0:1, :] + x * w_ref[1:2, :] + xp1 * w_ref[2:3, :]


def _conv_specs(tr, tc, col_blk, n_rows):
    r8 = tr // 8
    last8 = n_rows // 8 - 1
    return [pl.BlockSpec((tr, tc), lambda i, j: (i, col_blk(j))),
            pl.BlockSpec((8, tc), lambda i, j: (jnp.maximum(i * r8 - 1, 0), col_blk(j))),
            pl.BlockSpec((8, tc), lambda i, j: (jnp.minimum((i + 1) * r8, last8), col_blk(j)))]


def _ssd_conv_kernel(x_ref, p_ref, n_ref, w_ref, b_ref, o_ref, *, tr, t_ctx, l_ctx, l_lat):
    first, last = _seq_edges(pl.program_id(0), tr, t_ctx, l_ctx, l_lat)
    o_ref[...] = _silu(_conv3(x_ref[...], p_ref[...], n_ref[...], w_ref, b_ref, first, last))


def ssd_conv(proj, w, b, *, t_ctx, l_ctx, l_lat):
    t = proj.shape[0]
    tr, tc = ROW_TILE, 512
    c = w.shape[1]
    off = P_XBC // tc
    return pl.pallas_call(
        functools.partial(_ssd_conv_kernel, tr=tr, t_ctx=t_ctx, l_ctx=l_ctx, l_lat=l_lat),
        out_shape=jax.ShapeDtypeStruct((t, c), F32),
        grid=(t // tr, c // tc),
        in_specs=_conv_specs(tr, tc, lambda j: off + j, t) + [
            pl.BlockSpec((3, tc), lambda i, j: (0, j)),
            pl.BlockSpec((1, tc), lambda i, j: (0, j))],
        out_specs=pl.BlockSpec((tr, tc), lambda i, j: (i, j)),
        compiler_params=_cp(("arbitrary", "arbitrary")),
        name="ssd_conv",
    )(proj, proj, proj, w, b.reshape(1, c))


def _hy_pre_kernel(*refs, tr, t_ctx, l_ctx, l_lat):
    (x0, x0p, x0n, x1, x1p, x1n, xv, xvp, xvn, w0, w1, wv, b0, b1, bv, o0_ref, op_ref) = refs
    first, last = _seq_edges(pl.program_id(0), tr, t_ctx, l_ctx, l_lat)
    o0_ref[...] = _conv3(x0[...], x0p[...], x0n[...], w0, b0, first, last)
    u1 = _conv3(x1[...], x1p[...], x1n[...], w1, b1, first, last)
    uv = _conv3(xv[...], xvp[...], xvn[...], wv, bv, first, last)
    op_ref[...] = u1 * uv


def hyena_pre(proj, w, b, *, t_ctx, l_ctx, l_lat):
    t = proj.shape[0]
    tr, tc = ROW_TILE, HY_DIM
    nb = HY_DIM // tc
    b2 = b.reshape(1, 3 * HY_DIM)
    specs = []
    for s in range(3):
        specs += _conv_specs(tr, tc, lambda j, s=s: P_HY // tc + s * nb + j, t)
    specs += [pl.BlockSpec((3, tc), lambda i, j, s=s: (0, s * nb + j)) for s in range(3)]
    specs += [pl.BlockSpec((1, tc), lambda i, j, s=s: (0, s * nb + j)) for s in range(3)]
    return pl.pallas_call(
        functools.partial(_hy_pre_kernel, tr=tr, t_ctx=t_ctx, l_ctx=l_ctx, l_lat=l_lat),
        out_shape=(jax.ShapeDtypeStruct((t, HY_DIM), F32), jax.ShapeDtypeStruct((t, HY_DIM), F32)),
        grid=(t // tr, nb),
        in_specs=specs,
        out_specs=(pl.BlockSpec((tr, tc), lambda i, j: (i, j)), pl.BlockSpec((tr, tc), lambda i, j: (i, j))),
        compiler_params=_cp(("arbitrary", "arbitrary")),
        name="hyena_pre",
    )(*([proj] * 9), w, w, w, b2, b2, b2)


def _filter_kernel(z_ref, w1_ref, b1_ref, w2_ref, b2_ref, w3_ref, fr_ref, dec_ref, o_ref):
    z = z_ref[...]
    fr = fr_ref[...]
    h = jnp.sin(fr * (jnp.dot(z, w1_ref[...], precision=HI, preferred_element_type=F32) + b1_ref[...]))
    h = jnp.sin(fr * (jnp.dot(h, w2_ref[...], precision=HI, preferred_element_type=F32) + b2_ref[...]))
    f = jnp.dot(h, w3_ref[...], precision=HI, preferred_element_type=F32)
    o_ref[...] = f * jnp.exp(-z[:, 0:1] * jnp.abs(dec_ref[...]))


def _filter_embedding(seq_len, emb):
    bands_n = (emb - 1) // 2
    t = jnp.linspace(0.0, 1.0, seq_len, dtype=F32)[:, None]
    bands = jnp.linspace(1e-4, bands_n - 1, bands_n, dtype=F32)[None, :]
    ang = (2.0 * math.pi / seq_len) * jnp.arange(seq_len, dtype=F32)[:, None] * bands
    z = jnp.concatenate([t, jnp.cos(ang), -jnp.sin(ang)], axis=-1)
    return jnp.pad(z, ((0, 0), (0, LANES - emb)))


def hyena_filter(seq_len, w1, b1, w2, b2, w3, freq, decay):
    emb, ff = w1.shape
    n = w3.shape[1]
    z = _filter_embedding(seq_len, emb)
    padc = LANES - ff
    w1p = jnp.pad(w1, ((0, LANES - emb), (0, padc)))
    w2p = jnp.pad(w2, ((0, padc), (0, padc)))
    w3p = jnp.pad(w3, ((0, padc), (0, 0)))
    row = lambda v: jnp.pad(v, (0, padc)).reshape(1, LANES)
    tr = 256
    full = lambda shape: pl.BlockSpec(shape, lambda i: (0, 0))
    return pl.pallas_call(
        _filter_kernel,
        out_shape=jax.ShapeDtypeStruct((seq_len, n), F32),
        grid=(seq_len // tr,),
        in_specs=[pl.BlockSpec((tr, LANES), lambda i: (i, 0)), full((LANES, LANES)), full((1, LANES)),
                  full((LANES, LANES)), full((1, LANES)), full((LANES, n)), full((1, LANES)), full((1, n))],
        out_specs=pl.BlockSpec((tr, n), lambda i: (i, 0)),
        compiler_params=_cp(("arbitrary",)),
        name="hyena_filter",
    )(z, w1p, row(b1), w2p, row(b2), w3p, row(freq), decay.reshape(1, n))


def _circular_filter(filt):
    ch = filt.shape[1] // 2
    h_fwd, h_bwd = filt[:, :ch], filt[:, ch:]
    return jnp.concatenate([h_fwd, jnp.zeros((1, ch), F32), h_bwd[1:][::-1]], axis=0)


def _cs(n_rows, n_cols, period):
    ang = 2.0 * np.pi * (np.outer(np.arange(n_rows), np.arange(n_cols)) % period) / period
    return np.cos(ang), np.sin(ang)


def _mm_kernel(a_ref, b_ref, o_ref):
    o_ref[0] = jnp.dot(a_ref[...], b_ref[0], precision=HI, preferred_element_type=F32)


def const_matmul(a, b, tn):
    m, k = a.shape
    bsz, _, n = b.shape
    return pl.pallas_call(
        _mm_kernel,
        out_shape=jax.ShapeDtypeStruct((bsz, m, n), F32),
        grid=(bsz, n // tn),
        in_specs=[pl.BlockSpec((m, k), lambda s, j: (0, 0)), pl.BlockSpec((1, k, tn), lambda s, j: (s, 0, j))],
        out_specs=pl.BlockSpec((1, m, tn), lambda s, j: (s, 0, j)),
        compiler_params=_cp(("arbitrary", "arbitrary")),
        name="const_matmul",
    )(a, b)


def _hy_ctx_kernel(x0_ref, p_ref, kr_ref, ki_ref, fw_ref, iv_ref, bias_ref, o_ref, *, n):
    p = p_ref[...]
    xf = jnp.dot(fw_ref[...], p, precision=HI, preferred_element_type=F32)
    xr, xi = xf[:n], xf[n:]
    kr, ki = kr_ref[...], ki_ref[...]
    yr = xr * kr - xi * ki
    yi = xr * ki + xi * kr
    y = jnp.dot(iv_ref[...], jnp.concatenate([yr, yi], axis=0), precision=HI, preferred_element_type=F32)
    o_ref[...] = x0_ref[...] * (y + p * bias_ref[...])


def hyena_ctx(x0, p, filt, bias, *, n_seq, seq_len):
    n = 2 * seq_len
    ch = p.shape[1]
    c_full, s_full = _cs(n, n, n)
    fw_full = jnp.asarray(np.concatenate([c_full, -s_full], axis=0), F32)
    fw_half = fw_full[:, :seq_len]
    iv = jnp.asarray(np.concatenate([c_full[:seq_len], -s_full[:seq_len]], axis=1) / n, F32)
    kf = const_matmul(fw_full, _circular_filter(filt)[None], ch)[0]
    kr, ki = kf[:n], kf[n:]
    full = lambda shape: pl.BlockSpec(shape, lambda s: (0, 0))
    return pl.pallas_call(
        functools.partial(_hy_ctx_kernel, n=n),
        out_shape=jax.ShapeDtypeStruct((n_seq * seq_len, ch), F32),
        grid=(n_seq,),
        in_specs=[pl.BlockSpec((seq_len, ch), lambda s: (s, 0)), pl.BlockSpec((seq_len, ch), lambda s: (s, 0)),
                  full((n, ch)), full((n, ch)), full((2 * n, seq_len)), full((seq_len, 2 * n)), full((1, ch))],
        out_specs=pl.BlockSpec((seq_len, ch), lambda s: (s, 0)),
        compiler_params=_cp(("arbitrary",)),
        name="hyena_ctx",
    )(x0, p, kr, ki, fw_half, iv, bias.reshape(1, ch))


def _stage2_kernel(a_ref, twr_ref, twi_ref, m_ref, *rest, conv):
    if conv:
        kf_ref, mi_ref, o_ref = rest
    else:
        (o_ref,) = rest
    ar, ai = a_ref[0, 0, 0], a_ref[0, 1, 0]
    twr, twi = twr_ref[0], twi_ref[0]
    br = ar * twr - ai * twi
    bi = ar * twi + ai * twr
    x = jnp.dot(m_ref[...], jnp.concatenate([br, bi], axis=0), precision=HI, preferred_element_type=F32)
    n2 = ar.shape[0]
    xr, xi = x[:n2], x[n2:]
    if not conv:
        o_ref[0, 0, 0] = xr
        o_ref[0, 1, 0] = xi
        return
    kr, ki = kf_ref[0, 0, 0], kf_ref[0, 1, 0]
    yr = xr * kr - xi * ki
    yi = xr * ki + xi * kr
    pq = jnp.dot(mi_ref[...], jnp.concatenate([yr, yi], axis=0), precision=HI, preferred_element_type=F32)
    pr, pi = pq[:n2], pq[n2:]
    o_ref[0, 0, 0] = pr * twr + pi * twi
    o_ref[0, 1, 0] = pi * twr - pr * twi


def _stage2(a, kf, ch):
    bsz = a.shape[0]
    n1, n2 = DFT_N1, DFT_N2
    n = n1 * n2
    tw_ang = 2.0 * np.pi * np.outer(np.arange(n1), np.arange(n2)) / n
    twr = jnp.asarray(np.cos(tw_ang), F32).reshape(n1, n2, 1)
    twi = jnp.asarray(-np.sin(tw_ang), F32).reshape(n1, n2, 1)
    c2, s2 = _cs(n2, n2, n2)
    m_fwd = jnp.asarray(np.block([[c2, s2], [-s2, c2]]), F32)
    m_inv = jnp.asarray(np.block([[c2, -s2], [s2, c2]]), F32)
    conv = kf is not None
    blk = pl.BlockSpec((1, 2, 1, n2, ch), lambda s, k: (s, 0, k, 0, 0))
    specs = [blk, pl.BlockSpec((1, n2, 1), lambda s, k: (k, 0, 0)), pl.BlockSpec((1, n2, 1), lambda s, k: (k, 0, 0)),
             pl.BlockSpec((2 * n2, 2 * n2), lambda s, k: (0, 0))]
    args = [a, twr, twi, m_fwd]
    if conv:
        specs += [pl.BlockSpec((1, 2, 1, n2, ch), lambda s, k: (0, 0, k, 0, 0)),
                  pl.BlockSpec((2 * n2, 2 * n2), lambda s, k: (0, 0))]
        args += [kf, m_inv]
    return pl.pallas_call(
        functools.partial(_stage2_kernel, conv=conv),
        out_shape=jax.ShapeDtypeStruct(a.shape, F32),
        grid=(bsz, n1),
        in_specs=specs,
        out_specs=blk,
        compiler_params=_cp(("arbitrary", "arbitrary")),
        name="hyena_stage2",
    )(*args)


def _hy_post_kernel(g_ref, q_ref, x0_ref, p_ref, bias_ref, o_ref):
    y = jnp.dot(g_ref[...], q_ref[0], precision=HI, preferred_element_type=F32)
    o_ref[0] = x0_ref[0] * (y + p_ref[0] * bias_ref[...])


def hyena_lat(x0, p, filt, bias, *, n_seq, seq_len):
    n1, n2 = DFT_N1, DFT_N2
    n = n1 * n2
    assert n == 2 * seq_len
    ch = p.shape[1]
    h1 = n1 // 2
    wide = n2 * ch
    c1, s1 = _cs(n1, n1, n1)
    f1_full = jnp.asarray(np.concatenate([c1, -s1], axis=0), F32)
    f1_half = f1_full[:, :h1]
    g1 = jnp.asarray(np.concatenate([c1[:h1], -s1[:h1]], axis=1) / n, F32)
    tn = 8192
    kcirc = _circular_filter(filt).reshape(1, n1, wide)
    kf = _stage2(const_matmul(f1_full, kcirc, tn).reshape(1, 2, n1, n2, ch), None, ch)
    a = const_matmul(f1_half, p.reshape(n_seq, h1, wide), tn).reshape(n_seq, 2, n1, n2, ch)
    q = _stage2(a, kf, ch).reshape(n_seq, 2 * n1, wide)
    bias_w = jnp.tile(bias, n2).reshape(1, wide)
    out = pl.pallas_call(
        _hy_post_kernel,
        out_shape=jax.ShapeDtypeStruct((n_seq, h1, wide), F32),
        grid=(n_seq, wide // tn),
        in_specs=[pl.BlockSpec((h1, 2 * n1), lambda s, j: (0, 0)),
                  pl.BlockSpec((1, 2 * n1, tn), lambda s, j: (s, 0, j)),
                  pl.BlockSpec((1, h1, tn), lambda s, j: (s, 0, j)),
                  pl.BlockSpec((1, h1, tn), lambda s, j: (s, 0, j)),
                  pl.BlockSpec((1, tn), lambda s, j: (0, j))],
        out_specs=pl.BlockSpec((1, h1, tn), lambda s, j: (s, 0, j)),
        compiler_params=_cp(("arbitrary", "arbitrary")),
        name="hyena_post",
    )(g1, q, x0.reshape(n_seq, h1, wide), p.reshape(n_seq, h1, wide), bias_w)
    return out.reshape(n_seq * seq_len, ch)


def _ssd_dir(xbc, dtraw, st_ref, d, consts, y_ref):
    tri, expand, dtb, acont, dskip = consts
    cl = xbc.shape[0]
    xs = xbc[:, :SSD_INNER]
    dt = _softplus(dtraw + dtb)
    a = dt * acont
    tri_d = tri if d == 0 else tri.T
    cs = jnp.dot(tri_d, a, precision=HI, preferred_element_type=F32)
    cs_t = jnp.dot(a.T, tri_d.T, precision=HI, preferred_element_type=F32)
    dt_x = jnp.dot(dt, expand[d], precision=HI, preferred_element_type=F32)
    xdt = xs * dt_x
    row = lax.broadcasted_iota(jnp.int32, (cl, cl), 0)
    col = lax.broadcasted_iota(jnp.int32, (cl, cl), 1)
    keep = (col <= row) if d == 0 else (col >= row)
    lane = lax.broadcasted_iota(jnp.int32, (cl, LANES), 1)
    low = lane < SSD_HEADDIM
    edge = cl - 1 if d == 0 else 0
    for g in range(SSD_GROUPS):
        bg = xbc[:, SSD_INNER + g * SSD_STATE: SSD_INNER + (g + 1) * SSD_STATE]
        cg = xbc[:, SSD_INNER + (SSD_GROUPS + g) * SSD_STATE: SSD_INNER + (SSD_GROUPS + g + 1) * SSD_STATE]
        bg16, cg16 = bg.astype(BF16), cg.astype(BF16)
        cb = lax.dot_general(cg16, bg16, (((1,), (1,)), ((), ())), preferred_element_type=F32)
        for pr in range(2):
            pair = g * 2 + pr
            h0 = 2 * pair
            ms = []
            for h in (h0, h0 + 1):
                ln = d * SSD_HEADS + h
                diff = cs[:, ln:ln + 1] - cs_t[ln:ln + 1, :]
                ms.append(jnp.where(keep, cb * jnp.exp(jnp.minimum(diff, 0.0)), 0.0).astype(BF16))
            xp = xdt[:, pair * LANES:(pair + 1) * LANES]
            xs_p = xs[:, pair * LANES:(pair + 1) * LANES]
            xlo = jnp.where(low, xp, 0.0).astype(BF16)
            xhi = jnp.where(low, 0.0, xp).astype(BF16)
            y_diag = (jnp.dot(ms[0], xlo, preferred_element_type=F32)
                      + jnp.dot(ms[1], xhi, preferred_element_type=F32))
            l0, l1 = d * SSD_HEADS + h0, d * SSD_HEADS + h0 + 1
            e_cs = jnp.where(low, jnp.exp(cs[:, l0:l0 + 1]), jnp.exp(cs[:, l1:l1 + 1]))
            st = st_ref[pair]
            y_off = jnp.dot(cg16, st.astype(BF16), preferred_element_type=F32) * e_cs
            y_ref[:, pair * LANES:(pair + 1) * LANES] = (
                y_diag + y_off + xs_p * dskip[d:d + 1, pair * LANES:(pair + 1) * LANES])
            tot0, tot1 = cs[edge:edge + 1, l0:l0 + 1], cs[edge:edge + 1, l1:l1 + 1]
            dec = jnp.where(low, jnp.exp(tot0 - cs[:, l0:l0 + 1]), jnp.exp(tot1 - cs[:, l1:l1 + 1]))
            upd = lax.dot_general(bg16, (xp * dec).astype(BF16), (((0,), (0,)), ((), ())),
                                  preferred_element_type=F32)
            st_ref[pair] = st * jnp.where(low[0:1], jnp.exp(tot0), jnp.exp(tot1)) + upd


def _ssd_kernel(xf_ref, dtf_ref, xb_ref, dtb_ref, init_ref, tri_ref, exp_ref, dtbias_ref, acont_ref, dskip_ref,
                yf_ref, yb_ref, fin_ref, st_ref, *, has_init):
    s = pl.program_id(1)

    @pl.when(s == 0)
    def _():
        if has_init:
            st_ref[...] = init_ref[0]
        else:
            st_ref[...] = jnp.zeros_like(st_ref)

    consts = (tri_ref[...], (exp_ref[0], exp_ref[1]), dtbias_ref[...], acont_ref[...], dskip_ref[...])
    _ssd_dir(xf_ref[...], dtf_ref[...], st_ref.at[0], 0, consts, yf_ref)
    _ssd_dir(xb_ref[...], dtb_ref[...], st_ref.at[1], 1, consts, yb_ref)

    @pl.when(s == pl.num_programs(1) - 1)
    def _():
        fin_ref[0] = st_ref[...]


def _pair_states(s):
    b = s.shape[0]
    s = s.reshape(b, 2, SSD_HEADS // 2, 2, SSD_HEADDIM, SSD_STATE)
    return s.transpose(0, 1, 2, 5, 3, 4).reshape(b, 2, SSD_HEADS // 2, SSD_STATE, 2 * SSD_HEADDIM)


def _unpair_states(s):
    b = s.shape[0]
    s = s.reshape(b, 2, SSD_HEADS // 2, SSD_STATE, 2, SSD_HEADDIM)
    return s.transpose(0, 1, 2, 4, 5, 3).reshape(b, 2, SSD_HEADS, SSD_HEADDIM, SSD_STATE)


def ssd_scan(xbc, proj, init, dt_bias, a_log, d_skip, *, row0, n_seq, seq_len):
    cl = SSD_CHUNK
    nc = seq_len // cl
    base = row0 // cl
    has_init = init is not None
    hp = SSD_HEADS // 2
    init_p = _pair_states(init) if has_init else jnp.zeros((1, 2, hp, SSD_STATE, LANES), F32)
    tri = jnp.asarray(np.tril(np.ones((cl, cl))), F32)
    expand = np.zeros((2, LANES, SSD_INNER), np.float32)
    for d in range(2):
        for h in range(SSD_HEADS):
            expand[d, d * SSD_HEADS + h, h * SSD_HEADDIM:(h + 1) * SSD_HEADDIM] = 1.0
    pad16 = lambda v: jnp.pad(v.reshape(1, 2 * SSD_HEADS), ((0, 0), (0, LANES - 2 * SSD_HEADS)))
    acont = pad16(-jnp.exp(a_log))
    dtb = pad16(dt_bias)
    dskip = jnp.repeat(d_skip, SSD_HEADDIM, axis=1)
    cxbc = xbc.shape[1]
    dtblk = P_DT // LANES
    full2 = lambda shape: pl.BlockSpec(shape, lambda b, s: (0,) * len(shape))
    fwd = lambda b, s: base + b * nc + s
    bwd = lambda b, s: base + b * nc + (nc - 1 - s)
    st_spec = pl.BlockSpec((1, 2, hp, SSD_STATE, LANES), lambda b, s: (b if has_init else 0, 0, 0, 0, 0))
    yf, yb, fin = pl.pallas_call(
        functools.partial(_ssd_kernel, has_init=has_init),
        out_shape=(jax.ShapeDtypeStruct((n_seq * seq_len, SSD_INNER), F32),
                   jax.ShapeDtypeStruct((n_seq * seq_len, SSD_INNER), F32),
                   jax.ShapeDtypeStruct((n_seq, 2, hp, SSD_STATE, LANES), F32)),
        grid=(n_seq, nc),
        in_specs=[pl.BlockSpec((cl, cxbc), lambda b, s: (fwd(b, s), 0)),
                  pl.BlockSpec((cl, LANES), lambda b, s: (fwd(b, s), dtblk)),
                  pl.BlockSpec((cl, cxbc), lambda b, s: (bwd(b, s), 0)),
                  pl.BlockSpec((cl, LANES), lambda b, s: (bwd(b, s), dtblk)),
                  st_spec, full2((cl, cl)), full2((2, LANES, SSD_INNER)), full2((1, LANES)), full2((1, LANES)),
                  full2((2, SSD_INNER))],
        out_specs=(pl.BlockSpec((cl, SSD_INNER), lambda b, s: (b * nc + s, 0)),
                   pl.BlockSpec((cl, SSD_INNER), lambda b, s: (b * nc + (nc - 1 - s), 0)),
                   pl.BlockSpec((1, 2, hp, SSD_STATE, LANES), lambda b, s: (b, 0, 0, 0, 0))),
        scratch_shapes=[pltpu.VMEM((2, hp, SSD_STATE, LANES), F32)],
        compiler_params=_cp(("arbitrary", "arbitrary")),
        name="ssd_scan",
    )(xbc, proj, xbc, proj, init_p, tri, jnp.asarray(expand), dtb, acont, dskip)
    return yf, yb, _unpair_states(fin)


def _headnorm(x, g_ref, w_ref):
    ms = jnp.dot(x * x, g_ref[...], precision=HI, preferred_element_type=F32)
    return x * lax.rsqrt(ms + EPS) * w_ref[...]


def _rope(x, cos, sin_signed):
    lane = lax.broadcasted_iota(jnp.int32, x.shape, 1)
    w = x.shape[1]
    swapped = jnp.where(lane % 2 == 0, pltpu.roll(x, w - 1, axis=1), pltpu.roll(x, 1, axis=1))
    return x * cos + swapped * sin_signed


Q_SCALE = HEAD_DIM ** -0.5 * math.log2(math.e)


def _store_padded_heads(qo_ref, q):
    rep = N_HEADS // N_KV_HEADS
    lane = lax.broadcasted_iota(jnp.int32, (q.shape[0], LANES), 1)
    for h in range(N_HEADS):
        g = h // rep
        chunk = q[:, (h // 2) * LANES:(h // 2 + 1) * LANES]
        if h % 2 != g:
            chunk = pltpu.roll(chunk, HEAD_DIM, axis=1)
        keep = (lane >= g * HEAD_DIM) & (lane < (g + 1) * HEAD_DIM)
        qo_ref[:, h * LANES:(h + 1) * LANES] = jnp.where(keep, chunk, 0.0).astype(qo_ref.dtype)


def _qk_kernel(q_ref, k_ref, cos_ref, sin_ref, gq_ref, gk_ref, qw_ref, kw_ref, qo_ref, ko_ref, *, n_ctx_tiles):
    q = _headnorm(q_ref[...], gq_ref, qw_ref)
    k = _headnorm(k_ref[...], gk_ref, kw_ref)
    is_lat = pl.program_id(0) >= n_ctx_tiles

    @pl.when(is_lat)
    def _():
        cos, sin = cos_ref[...], sin_ref[...]
        _store_padded_heads(qo_ref, _rope(q, cos, sin) * Q_SCALE)
        ko_ref[...] = _rope(k, cos[:, :k.shape[1]], sin[:, :k.shape[1]])

    @pl.when(jnp.logical_not(is_lat))
    def _():
        _store_padded_heads(qo_ref, q * Q_SCALE)
        ko_ref[...] = k


def _rope_tables(seq_len):
    n_rows = seq_len // GRID_W
    row = jnp.repeat(jnp.arange(n_rows), GRID_W).astype(F32)
    col = jnp.tile(jnp.arange(GRID_W), n_rows).astype(F32)
    n_freq = HEAD_DIM // 4
    inv = ROPE_THETA ** (-jnp.arange(n_freq, dtype=F32) / n_freq)
    ang = jnp.concatenate([row[:, None] * inv, col[:, None] * inv], axis=-1)
    cos = jnp.repeat(jnp.cos(ang), 2, axis=1)
    sin = jnp.repeat(jnp.sin(ang), 2, axis=1) * jnp.tile(jnp.asarray([-1.0, 1.0], F32), HEAD_DIM // 2)
    return jnp.tile(cos, (1, N_HEADS)), jnp.tile(sin, (1, N_HEADS))


def qk_prep(proj, q_norm_w, k_norm_w, *, t_ctx, l_lat):
    t = proj.shape[0]
    tr = ROW_TILE
    qc, kc = N_HEADS * HEAD_DIM, N_KV_HEADS * HEAD_DIM
    cos, sin = _rope_tables(l_lat)
    group = lambda c: jnp.asarray(np.kron(np.eye(c // HEAD_DIM), np.ones((HEAD_DIM, HEAD_DIM))) / HEAD_DIM, F32)
    n_ctx_tiles = t_ctx // tr
    per = l_lat // tr
    tab = pl.BlockSpec((tr, qc), lambda i: (jnp.maximum(i - n_ctx_tiles, 0) % per, 0))
    full = lambda shape: pl.BlockSpec(shape, lambda i: (0, 0))
    return pl.pallas_call(
        functools.partial(_qk_kernel, n_ctx_tiles=n_ctx_tiles),
        out_shape=(jax.ShapeDtypeStruct((t, N_HEADS * LANES), BF16), jax.ShapeDtypeStruct((t, kc), F32)),
        grid=(t // tr,),
        in_specs=[pl.BlockSpec((tr, qc), lambda i: (i, P_Q // qc)), pl.BlockSpec((tr, kc), lambda i: (i, P_K // kc)),
                  tab, tab, full((qc, qc)), full((kc, kc)), full((1, qc)), full((1, kc))],
        out_specs=(pl.BlockSpec((tr, N_HEADS * LANES), lambda i: (i, 0)), pl.BlockSpec((tr, kc), lambda i: (i, 0))),
        compiler_params=_cp(("arbitrary",)),
        name="qk_prep",
    )(proj, proj, cos, sin, group(qc), group(kc), jnp.tile(q_norm_w, N_HEADS).reshape(1, qc),
      jnp.tile(k_norm_w, N_KV_HEADS).reshape(1, kc))


def _attention_kernel(q_ref, k_ref, v_ref, o_ref, *, tq):
    rep = N_HEADS // N_KV_HEADS
    k = k_ref[0]
    v = v_ref[0]
    for g in range(N_KV_HEADS):
        qs = jnp.concatenate([q_ref[0, :, (g * rep + r) * LANES:(g * rep + r + 1) * LANES] for r in range(rep)],
                             axis=0)
        s = lax.dot_general(qs, k, (((1,), (1,)), ((), ())), preferred_element_type=F32)
        p = jnp.exp2(s - jnp.max(s, axis=-1, keepdims=True))
        out = jnp.dot(p.astype(BF16), v, preferred_element_type=F32) / jnp.sum(p, axis=-1, keepdims=True)
        for r in range(rep):
            h = g * rep + r
            o_ref[0, :, h * HEAD_DIM:(h + 1) * HEAD_DIM] = (
                out[r * tq:(r + 1) * tq, g * HEAD_DIM:(g + 1) * HEAD_DIM].astype(o_ref.dtype))


def attention(q, k, v, *, tq):
    b, lq, qc = q.shape
    lk, kc = k.shape[1], k.shape[2]
    return pl.pallas_call(
        functools.partial(_attention_kernel, tq=tq),
        out_shape=jax.ShapeDtypeStruct((b, lq, N_HEADS * HEAD_DIM), BF16),
        grid=(b, lq // tq),
        in_specs=[pl.BlockSpec((1, tq, qc), lambda s, i: (s, i, 0)),
                  pl.BlockSpec((1, lk, kc), lambda s, i: (s, 0, 0)),
                  pl.BlockSpec((1, lk, kc), lambda s, i: (s, 0, 0))],
        out_specs=pl.BlockSpec((1, tq, N_HEADS * HEAD_DIM), lambda s, i: (s, i, 0)),
        compiler_params=_cp(("arbitrary", "arbitrary")),
        name="attention",
    )(q, k, v)


def _merge_kernel(x_ref, yhy_ref, yf_ref, yb_ref, z_ref, yatt_ref, gate_ref, snw_ref, whb_ref, wsb_ref, wab_ref,
                  wout_ref, g1_ref, n2w_ref, sh2_ref, sc2_ref, wr_ref, br_ref, ltri_ref,
                  xo_ref, h2_ref, ti_ref, tw_ref, hist_ref):
    d = x_ref.shape[1]
    ys = (yf_ref[...] + yb_ref[...]) * _silu(z_ref[...])
    ys = ys * lax.rsqrt(jnp.mean(ys * ys, axis=-1, keepdims=True) + EPS) * snw_ref[...]
    gate = gate_ref[...].astype(F32)
    merged = (gate[:, :d] * jnp.dot(yhy_ref[...].astype(BF16), whb_ref[...], preferred_element_type=F32)
              + gate[:, d:2 * d] * jnp.dot(ys.astype(BF16), wsb_ref[...], preferred_element_type=F32)
              + gate[:, 2 * d:] * jnp.dot(yatt_ref[...], wab_ref[...], preferred_element_type=F32))
    mix = jnp.dot(merged.astype(BF16), wout_ref[...], preferred_element_type=F32)
    x = x_ref[...] + g1_ref[0] * mix
    xo_ref[...] = x
    h = x * lax.rsqrt(jnp.mean(x * x, axis=-1, keepdims=True) + EPS) * n2w_ref[...]
    h = h * (1.0 + sc2_ref[0]) + sh2_ref[0]
    h2_ref[...] = h
    logits = jnp.dot(h, wr_ref[...], precision=HI, preferred_element_type=F32) + br_ref[...]
    lane = lax.broadcasted_iota(jnp.int32, logits.shape, 1)
    work = jnp.where(lane < N_EXPERTS, logits, -jnp.inf)
    idx_out = jnp.zeros(logits.shape, jnp.int32)
    val_out = jnp.full(logits.shape, -jnp.inf, F32)
    picks = []
    for j in range(TOP_K):
        mx = jnp.max(work, axis=-1, keepdims=True)
        am = jnp.min(jnp.where(work == mx, lane, LANES), axis=-1, keepdims=True)
        idx_out = jnp.where(lane == j, am, idx_out)
        val_out = jnp.where(lane == j, mx, val_out)
        picks.append(lane == am)
        work = jnp.where(picks[-1], -jnp.inf, work)
    e = jnp.exp(val_out - jnp.max(val_out, axis=-1, keepdims=True))
    tw_ref[...] = e / jnp.sum(e, axis=-1, keepdims=True)
    chosen = jnp.where(picks[0] | picks[1] | picks[2] | picks[3], 1.0, 0.0)
    before = jnp.dot(ltri_ref[...], chosen.astype(BF16), preferred_element_type=F32)
    rank_out = jnp.zeros(logits.shape, F32)
    for j in range(TOP_K):
        rank_j = jnp.sum(jnp.where(picks[j], before, 0.0), axis=-1, keepdims=True)
        rank_out = jnp.where(lane == TOP_K + j, rank_j, rank_out)
    ti_ref[...] = idx_out + rank_out.astype(jnp.int32)
    hist_ref[...] = jnp.broadcast_to(jnp.sum(chosen, axis=0, keepdims=True), hist_ref.shape)


def merge_router(x, y_hy, yf, yb, proj, y_att, gate, ssd_norm_w, whb, wsb, wab, wout, mods, n2w, wr, br,
                 *, t_ctx, l_lat, tm=256):
    t, d = x.shape
    row = lambda tc, c0=0: pl.BlockSpec((tm, tc), lambda i: (i, c0))
    full = lambda shape: pl.BlockSpec(shape, lambda i: (0, 0))
    ltri = jnp.asarray(np.tril(np.ones((tm, tm)), -1), BF16)
    return pl.pallas_call(
        _merge_kernel,
        out_shape=(jax.ShapeDtypeStruct((t, d), F32), jax.ShapeDtypeStruct((t, d), F32),
                   jax.ShapeDtypeStruct((t, LANES), jnp.int32), jax.ShapeDtypeStruct((t, LANES), F32),
                   jax.ShapeDtypeStruct((t // tm * 8, LANES), F32)),
        grid=(t // tm,),
        in_specs=[row(d), row(HY_DIM), row(SSD_INNER), row(SSD_INNER), row(SSD_INNER, P_Z // SSD_INNER),
                  row(N_HEADS * HEAD_DIM), row(3 * d), full((1, SSD_INNER)),
                  full(whb.shape), full(wsb.shape), full(wab.shape), full(wout.shape),
                  _mod_spec(2, tm, t_ctx, l_lat), full((1, d)), _mod_spec(3, tm, t_ctx, l_lat),
                  _mod_spec(4, tm, t_ctx, l_lat), full((d, LANES)), full((1, LANES)), full((tm, tm))],
        out_specs=(row(d), row(d), row(LANES), row(LANES), pl.BlockSpec((8, LANES), lambda i: (i, 0))),
        compiler_params=_cp(("arbitrary",)),
        name="merge_router",
    )(x, y_hy, yf, yb, proj, y_att, gate, ssd_norm_w.reshape(1, -1), whb, wsb, wab, wout, mods, n2w, mods, mods,
      wr, br, ltri)


DEINT_BLOCK = 256


def _deinterleave_kernel(w_ref, perm_ref, g_ref, u_ref):
    half = DEINT_BLOCK // 2
    for blk in range(w_ref.shape[2] // DEINT_BLOCK):
        wb = w_ref[0, :, blk * DEINT_BLOCK:(blk + 1) * DEINT_BLOCK].astype(BF16)
        r = jnp.dot(wb, perm_ref[...], preferred_element_type=F32)
        g_ref[0, :, blk * half:(blk + 1) * half] = r[:, :half].astype(BF16)
        u_ref[0, :, blk * half:(blk + 1) * half] = r[:, half:].astype(BF16)


def deinterleave_experts(w1):
    e, d, f2 = w1.shape
    perm = np.zeros((DEINT_BLOCK, DEINT_BLOCK), np.float32)
    half = DEINT_BLOCK // 2
    perm[2 * np.arange(half), np.arange(half)] = 1.0
    perm[2 * np.arange(half) + 1, half + np.arange(half)] = 1.0
    out = jax.ShapeDtypeStruct((e, d, f2 // 2), BF16)
    return pl.pallas_call(
        _deinterleave_kernel,
        out_shape=(out, out),
        grid=(e,),
        in_specs=[pl.BlockSpec((1, d, f2), lambda i: (i, 0, 0)),
                  pl.BlockSpec((DEINT_BLOCK, DEINT_BLOCK), lambda i: (0, 0))],
        out_specs=(pl.BlockSpec((1, d, f2 // 2), lambda i: (i, 0, 0)),
                   pl.BlockSpec((1, d, f2 // 2), lambda i: (i, 0, 0))),
        compiler_params=_cp(("arbitrary",)),
        name="deinterleave_experts",
    )(w1, jnp.asarray(perm, BF16))


DMA_CHUNK = 512
DMA_UNROLL = 8


def _chunk_wait(ref, sem):
    pltpu.make_async_copy(ref.at[pl.ds(0, DMA_CHUNK)], ref.at[pl.ds(0, DMA_CHUNK)], sem).wait()


def _issue_rows(copy_of_row):
    def body(it, carry):
        for u in range(DMA_UNROLL):
            copy_of_row(it * DMA_UNROLL + u).start(priority=u % 2)
        return carry
    lax.fori_loop(0, DMA_CHUNK // DMA_UNROLL, body, 0)


def _dispatch_kernel(idx_ref, x_ref, zero_ref, dst_ref, sem):
    del zero_ref

    def copy_of_row(r):
        tok = lax.shift_right_logical(r, TOP_K_SHIFT)
        return pltpu.make_async_copy(x_ref.at[pl.ds(tok, 1)], dst_ref.at[pl.ds(idx_ref[0, 0, r], 1)], sem)

    _issue_rows(copy_of_row)
    _chunk_wait(dst_ref, sem)


def moe_dispatch(dest, h2, n_dst):
    n = dest.shape[0]
    nc = n // DMA_CHUNK
    tok = DMA_CHUNK // TOP_K
    d = h2.shape[1]
    return pl.pallas_call(
        _dispatch_kernel,
        out_shape=jax.ShapeDtypeStruct((n_dst, d), h2.dtype),
        grid=(nc,),
        in_specs=[pl.BlockSpec((1, 1, DMA_CHUNK), lambda c: (c, 0, 0), memory_space=pltpu.SMEM),
                  pl.BlockSpec((tok, d), lambda c: (c, 0)),
                  pl.BlockSpec(memory_space=pl.ANY)],
        out_specs=pl.BlockSpec(memory_space=pl.ANY),
        scratch_shapes=[pltpu.SemaphoreType.DMA(())],
        input_output_aliases={2: 0},
        compiler_params=_cp(("arbitrary",)),
        name="moe_dispatch",
    )(dest.reshape(nc, 1, DMA_CHUNK), h2, jnp.zeros((n_dst, d), h2.dtype))


def _moe_kernel(te_ref, nv_ref, x_ref, w1g_ref, w1u_ref, b1g_ref, b1u_ref, w2_ref, b2_ref, o_ref):
    @pl.when(pl.program_id(0) < nv_ref[0])
    def _():
        x = x_ref[...].astype(BF16)
        gate = jnp.dot(x, w1g_ref[0], preferred_element_type=F32) + b1g_ref[0]
        up = jnp.dot(x, w1u_ref[0], preferred_element_type=F32) + b1u_ref[0]
        gate = jnp.minimum(gate, SWIGLU_LIMIT)
        up = jnp.clip(up, -SWIGLU_LIMIT, SWIGLU_LIMIT)
        act = (up + 1.0) * (gate * _sigmoid(SWIGLU_ALPHA * gate))
        o_ref[...] = jnp.dot(act.astype(BF16), w2_ref[0], preferred_element_type=F32) + b2_ref[0]

    @pl.when(pl.program_id(0) >= nv_ref[0])
    def _():
        o_ref[...] = jnp.zeros_like(o_ref)


def moe_experts(xs, tile_expert, n_valid, w1g, w1u, b1g, b1u, w2, b2):
    npad, d = xs.shape
    tm = MOE_TILE
    ff = w1g.shape[2]
    n_tiles = npad // tm
    rows = lambda i, te, nv: (jnp.minimum(i, nv[0] - 1), 0)
    wsel = lambda i, te, nv: (te[i], 0, 0)
    return pl.pallas_call(
        _moe_kernel,
        out_shape=jax.ShapeDtypeStruct(xs.shape, F32),
        grid_spec=pltpu.PrefetchScalarGridSpec(
            num_scalar_prefetch=2,
            grid=(n_tiles,),
            in_specs=[pl.BlockSpec((tm, d), rows),
                      pl.BlockSpec((1, d, ff), wsel), pl.BlockSpec((1, d, ff), wsel),
                      pl.BlockSpec((1, 1, ff), wsel), pl.BlockSpec((1, 1, ff), wsel),
                      pl.BlockSpec((1, ff, d), wsel), pl.BlockSpec((1, 1, d), wsel)],
            out_specs=pl.BlockSpec((tm, d), lambda i, te, nv: (i, 0))),
        compiler_params=_cp(("arbitrary",)),
        name="moe_experts",
    )(tile_expert, n_valid, xs, w1g, w1u, b1g, b1u, w2, b2)


def _combine_kernel(idx_ref, nxt_ref, x_ref, rows_ref, tw_ref, g2_ref, y_ref, buf_ref, sems):
    i = pl.program_id(0)
    slot = i % 2
    tm = x_ref.shape[0]

    def gather(ids_ref, s):
        def copy_of_row(r):
            row = (r & (TOP_K - 1)) * tm + lax.shift_right_logical(r, TOP_K_SHIFT)
            return pltpu.make_async_copy(rows_ref.at[pl.ds(ids_ref[0, 0, r], 1)], buf_ref.at[s, pl.ds(row, 1)],
                                         sems.at[s])
        _issue_rows(copy_of_row)

    @pl.when(i == 0)
    def _():
        gather(idx_ref, 0)

    @pl.when(i + 1 < pl.num_programs(0))
    def _():
        gather(nxt_ref, 1 - slot)

    _chunk_wait(rows_ref, sems.at[slot])
    tw = tw_ref[...]
    acc = tw[:, 0:1] * buf_ref[slot, 0:tm, :]
    for s in range(1, TOP_K):
        acc = acc + tw[:, s:s + 1] * buf_ref[slot, s * tm:(s + 1) * tm, :]
    y_ref[...] = x_ref[...] + g2_ref[0] * acc


def moe_combine(x, dest, rows, top_w, mods, *, t_ctx, l_lat):
    t, d = x.shape
    tm = DMA_CHUNK // TOP_K
    nt = t // tm
    ids = dest.reshape(nt, 1, DMA_CHUNK)
    smem = lambda fn: pl.BlockSpec((1, 1, DMA_CHUNK), fn, memory_space=pltpu.SMEM)
    return pl.pallas_call(
        _combine_kernel,
        out_shape=jax.ShapeDtypeStruct((t, d), F32),
        grid=(nt,),
        in_specs=[smem(lambda i: (i, 0, 0)), smem(lambda i: (jnp.minimum(i + 1, nt - 1), 0, 0)),
                  pl.BlockSpec((tm, d), lambda i: (i, 0)), pl.BlockSpec(memory_space=pl.ANY),
                  pl.BlockSpec((tm, LANES), lambda i: (i, 0)), _mod_spec(5, tm, t_ctx, l_lat)],
        out_specs=pl.BlockSpec((tm, d), lambda i: (i, 0)),
        scratch_shapes=[pltpu.VMEM((2, DMA_CHUNK, d), F32), pltpu.SemaphoreType.DMA((2,))],
        compiler_params=_cp(("arbitrary",)),
        name="moe_combine",
    )(ids, ids, x, rows, top_w, mods)


def _dispatch_plan(top_ir, hist, tm):
    t = top_ir.shape[0]
    n_rt = hist.shape[0]
    hist = hist.astype(jnp.int32)
    counts = jnp.sum(hist, axis=0)
    tiles = (counts + tm - 1) // tm
    tile_end = jnp.cumsum(tiles)
    base = (tile_end - tiles)[None, :] * tm + jnp.cumsum(hist, axis=0) - hist
    e = top_ir[:, :TOP_K].reshape(n_rt, -1)
    rank = top_ir[:, TOP_K:].reshape(n_rt, -1)
    dest = (jnp.take_along_axis(base, e, axis=1) + rank).reshape(-1).astype(jnp.int32)
    n_tiles = (t * TOP_K) // tm + N_EXPERTS
    tile_ids = jnp.arange(n_tiles, dtype=jnp.int32)
    tile_expert = jnp.minimum(jnp.sum((tile_ids[:, None] >= tile_end[None, :]).astype(jnp.int32), axis=1),
                              N_EXPERTS - 1).astype(jnp.int32)
    return dest, n_tiles, tile_expert, tile_end[-1:].astype(jnp.int32)


def kernel(x_prompt, x_sample, c, cache_k, cache_v, state_ssd, c_ctx, norm1_w, norm2_w, w_mod, b_mod, w_in, w_gate,
           b_gate, hy_conv_w, hy_conv_b, hy_w1, hy_b1, hy_w2, hy_b2, hy_w3, hy_freq, hy_decay, hy_bias, ssd_conv_w,
           ssd_conv_b, ssd_a_log, ssd_dt_bias, ssd_d, ssd_norm_w, q_norm_w, k_norm_w, w_br_hy, w_br_ssd, w_br_att,
           w_out, w_router, b_router, w_e1, b_e1, w_e2, b_e2):
    n_ctx, l_ctx, d = x_prompt.shape
    n_lat, l_lat, _ = x_sample.shape
    depth = w_in.shape[0]
    t_ctx, t_lat = n_ctx * l_ctx, n_lat * l_lat
    t = t_ctx + t_lat
    kc = N_KV_HEADS * HEAD_DIM
    geo = dict(t_ctx=t_ctx, l_lat=l_lat)

    x = jnp.concatenate([x_prompt.reshape(t_ctx, d), x_sample.reshape(t_lat, d)], axis=0)
    cvec = jnp.zeros((8, d), F32).at[0].set(c_ctx).at[1:1 + n_lat].set(c)
    mods_all = modulation_all(cvec, w_mod, b_mod)
    n_exp, _, ff2 = w_e1.shape[1:]
    w1g_all, w1u_all = deinterleave_experts(w_e1.reshape(depth * n_exp, d, ff2))
    w1g_all = w1g_all.reshape(depth, n_exp, d, ff2 // 2)
    w1u_all = w1u_all.reshape(depth, n_exp, d, ff2 // 2)

    new_k, new_v, new_s = [], [], []
    for l in range(depth):
        mods = mods_all[l].reshape(8 * 6, 1, d)
        wi = w_in[l]
        w_proj = jnp.concatenate([wi[:, 0:3072], wi[:, 3088:3856], wi[:, 3072:3088],
                                  jnp.zeros((d, P_COLS - 3856), F32)], axis=1).astype(BF16)
        nw1 = norm1_w[l].reshape(1, d)
        proj = norm_mod_matmul(x, nw1, mods, w_proj, jnp.zeros((1, P_COLS), F32), sigmoid=False, out_dtype=F32,
                               **geo)
        gate = norm_mod_matmul(x, nw1, mods, w_gate[l].astype(BF16), b_gate[l].reshape(1, -1), sigmoid=True,
                               out_dtype=BF16, **geo)

        x0, p = hyena_pre(proj, hy_conv_w[l], hy_conv_b[l], t_ctx=t_ctx, l_ctx=l_ctx, l_lat=l_lat)
        hy_args = (hy_w1[l], hy_b1[l], hy_w2[l], hy_b2[l], hy_w3[l], hy_freq[l], hy_decay[l])
        y_hy_ctx = hyena_ctx(x0, p, hyena_filter(l_ctx, *hy_args), hy_bias[l], n_seq=n_ctx, seq_len=l_ctx)
        y_hy_lat = hyena_lat(x0[t_ctx:], p[t_ctx:], hyena_filter(l_lat, *hy_args), hy_bias[l], n_seq=n_lat,
                             seq_len=l_lat)
        y_hy = jnp.concatenate([y_hy_ctx, y_hy_lat], axis=0)

        xbc = ssd_conv(proj, ssd_conv_w[l], ssd_conv_b[l], t_ctx=t_ctx, l_ctx=l_ctx, l_lat=l_lat)
        ssd_args = (ssd_dt_bias[l], ssd_a_log[l], ssd_d[l])
        yf_c, yb_c, fin_c = ssd_scan(xbc, proj, None, *ssd_args, row0=0, n_seq=n_ctx, seq_len=l_ctx)
        yf_l, yb_l, _ = ssd_scan(xbc, proj, state_ssd[:, l], *ssd_args, row0=t_ctx, n_seq=n_lat, seq_len=l_lat)
        yf = jnp.concatenate([yf_c, yf_l], axis=0)
        yb = jnp.concatenate([yb_c, yb_l], axis=0)

        qn, kn = qk_prep(proj, q_norm_w[l], k_norm_w[l], **geo)
        v_all = proj[:, P_V:P_V + kc]
        k_ctx = kn[:t_ctx].reshape(n_ctx, l_ctx, kc)
        v_ctx = v_all[:t_ctx].reshape(n_ctx, l_ctx, kc)
        att_ctx = attention(qn[:t_ctx].reshape(n_ctx, l_ctx, -1), k_ctx.astype(BF16), v_ctx.astype(BF16), tq=l_ctx)
        k_lat = jnp.concatenate([kn[t_ctx:].reshape(n_lat, l_lat, kc), cache_k[:, l].reshape(n_lat, -1, kc)], axis=1)
        v_lat = jnp.concatenate([v_all[t_ctx:].reshape(n_lat, l_lat, kc), cache_v[:, l].reshape(n_lat, -1, kc)],
                                axis=1)
        att_lat = attention(qn[t_ctx:].reshape(n_lat, l_lat, -1), k_lat.astype(BF16), v_lat.astype(BF16), tq=128)
        y_att = jnp.concatenate([att_ctx.reshape(t_ctx, -1), att_lat.reshape(t_lat, -1)], axis=0)

        wr = jnp.pad(w_router[l], ((0, 0), (0, LANES - N_EXPERTS)))
        br = jnp.pad(b_router[l], (0, LANES - N_EXPERTS)).reshape(1, LANES)
        x, h2, top_ir, top_w, hist = merge_router(
            x, y_hy, yf, yb, proj, y_att, gate, ssd_norm_w[l], w_br_hy[l].astype(BF16), w_br_ssd[l].astype(BF16),
            w_br_att[l].astype(BF16), w_out[l].astype(BF16), mods, norm2_w[l].reshape(1, d), wr, br, **geo)

        dest, n_tiles, tile_expert, n_valid = _dispatch_plan(top_ir[:, :2 * TOP_K], hist[::8, :N_EXPERTS], MOE_TILE)
        xs = moe_dispatch(dest, h2, n_tiles * MOE_TILE)
        out_sorted = moe_experts(
            xs, tile_expert, n_valid, w1g_all[l], w1u_all[l],
            b_e1[l][:, None, 0::2], b_e1[l][:, None, 1::2], w_e2[l].astype(BF16), b_e2[l][:, None, :])
        x = moe_combine(x, dest, out_sorted, top_w, mods, **geo)

        new_k.append(k_ctx.reshape(n_ctx, l_ctx, N_KV_HEADS, HEAD_DIM))
        new_v.append(v_ctx.reshape(n_ctx, l_ctx, N_KV_HEADS, HEAD_DIM))
        new_s.append(fin_c)

    y_prompt = x[:t_ctx].reshape(n_ctx, l_ctx, d)
    y_sample = x[t_ctx:].reshape(n_lat, l_lat, d)
    return (y_prompt, y_sample, jnp.stack(new_k, axis=1), jnp.stack(new_v, axis=1), jnp.stack(new_s, axis=1))
```

```python
import functools
import math

import numpy as np
import jax
import jax.numpy as jnp
from jax import lax
from jax.experimental import pallas as pl
from jax.experimental.pallas import tpu as pltpu

F32 = jnp.float32
BF16 = jnp.bfloat16
HI = lax.Precision.HIGHEST

EPS = 1e-6
GRID_W = 64
HY_DIM = 512
SSD_INNER = 512
SSD_HEADDIM = 64
SSD_HEADS = 8
SSD_GROUPS = 2
SSD_STATE = 128
SSD_CHUNK = 128
N_HEADS = 8
N_KV_HEADS = 2
HEAD_DIM = 64
ROPE_THETA = 10000.0
N_EXPERTS = 32
TOP_K = 4
TOP_K_SHIFT = 2
SWIGLU_ALPHA = 1.702
SWIGLU_LIMIT = 7.0

P_HY, P_Z, P_XBC, P_Q, P_K, P_V, P_DT, P_COLS = 0, 1536, 2048, 3072, 3584, 3712, 3840, 3968

VMEM_LIMIT = 56 * 1024 * 1024
LANES = 128
ROW_TILE = 256
MOE_TILE = 512
DFT_N1, DFT_N2 = 64, 128


def _cp(sem, vmem=VMEM_LIMIT):
    return pltpu.CompilerParams(dimension_semantics=sem, vmem_limit_bytes=vmem)


def _sigmoid(x):
    return 1.0 / (1.0 + jnp.exp(-x))


def _silu(x):
    return x * _sigmoid(x)


def _softplus(x):
    return jnp.maximum(x, 0.0) + jnp.log(1.0 + jnp.exp(-jnp.abs(x)))


def _dot3(a, b):
    ah = a.astype(BF16)
    al = (a - ah.astype(F32)).astype(BF16)
    bh = b.astype(BF16)
    bl = (b - bh.astype(F32)).astype(BF16)
    dot = lambda u, v: jnp.dot(u, v, preferred_element_type=F32)
    return dot(ah, bh) + (dot(ah, bl) + dot(al, bh))


def _mod_kernel(c_ref, w_ref, b_ref, o_ref):
    s = _silu(c_ref[...])
    o_ref[0] = jnp.dot(s, w_ref[0], precision=HI, preferred_element_type=F32) + b_ref[0]


def modulation_all(cvec, w_mod, b_mod):
    depth, d, n = w_mod.shape
    tn = 1536
    return pl.pallas_call(
        _mod_kernel,
        out_shape=jax.ShapeDtypeStruct((depth, 8, n), F32),
        grid=(depth, n // tn),
        in_specs=[pl.BlockSpec((8, d), lambda l, j: (0, 0)),
                  pl.BlockSpec((1, d, tn), lambda l, j: (l, 0, j)),
                  pl.BlockSpec((1, 1, tn), lambda l, j: (l, 0, j))],
        out_specs=pl.BlockSpec((1, 8, tn), lambda l, j: (l, 0, j)),
        compiler_params=_cp(("arbitrary", "arbitrary")),
        name="modulation",
    )(cvec, w_mod, b_mod.reshape(depth, 1, n))


def _mod_row(i, tm, t_ctx, l_lat):
    n_ctx = t_ctx // tm
    per = l_lat // tm
    return jnp.where(i < n_ctx, 0, 1 + (i - n_ctx) // per)


def _mod_spec(k, tm, t_ctx, l_lat):
    return pl.BlockSpec((1, 1, 1024), lambda i: (_mod_row(i, tm, t_ctx, l_lat) * 6 + k, 0, 0))


def _nmm_kernel(x_ref, nw_ref, sh_ref, sc_ref, w_ref, b_ref, o_ref, *, sigmoid):
    x = x_ref[...]
    ms = jnp.mean(x * x, axis=-1, keepdims=True)
    h = x * lax.rsqrt(ms + EPS) * nw_ref[...]
    h = h * (1.0 + sc_ref[0]) + sh_ref[0]
    acc = jnp.dot(h.astype(BF16), w_ref[...], preferred_element_type=F32) + b_ref[...]
    if sigmoid:
        acc = _sigmoid(acc)
    o_ref[...] = acc.astype(o_ref.dtype)


def norm_mod_matmul(x, nw, mods, w, b, *, t_ctx, l_lat, sigmoid, out_dtype, tm=256):
    t, d = x.shape
    n = w.shape[1]
    return pl.pallas_call(
        functools.partial(_nmm_kernel, sigmoid=sigmoid),
        out_shape=jax.ShapeDtypeStruct((t, n), out_dtype),
        grid=(t // tm,),
        in_specs=[pl.BlockSpec((tm, d), lambda i: (i, 0)),
                  pl.BlockSpec((1, d), lambda i: (0, 0)),
                  _mod_spec(0, tm, t_ctx, l_lat),
                  _mod_spec(1, tm, t_ctx, l_lat),
                  pl.BlockSpec((d, n), lambda i: (0, 0)),
                  pl.BlockSpec((1, n), lambda i: (0, 0))],
        out_specs=pl.BlockSpec((tm, n), lambda i: (i, 0)),
        compiler_params=_cp(("arbitrary",)),
        name="norm_mod_matmul",
    )(x, nw, mods, mods, w, b)


def _seq_edges(i, tr, t_ctx, l_ctx, l_lat):
    tok = i * tr
    pos = jnp.where(tok < t_ctx, tok % l_ctx, (tok - t_ctx) % l_lat)
    length = jnp.where(tok < t_ctx, l_ctx, l_lat)
    return pos == 0, pos + tr == length


def _conv3(x, prev8, next8, w_ref, b_ref, first, last):
    tr = x.shape[0]
    row = lax.broadcasted_iota(jnp.int32, x.shape, 0)
    pm = jnp.where(first, 0.0, 1.0)
    nm = jnp.where(last, 0.0, 1.0)
    xm1 = jnp.where(row == 0, prev8[7:8, :] * pm, pltpu.roll(x, 1, axis=0))
    xp1 = jnp.where(row == tr - 1, next8[0:1, :] * nm, pltpu.roll(x, tr - 1, axis=0))
    return b_ref[...] + xm1 * w_ref[0:1, :] + x * w_ref[1:2, :] + xp1 * w_ref[2:3, :]


def _conv_specs(tr, tc, col_blk, n_rows):
    r8 = tr // 8
    last8 = n_rows // 8 - 1
    return [pl.BlockSpec((tr, tc), lambda i, j: (i, col_blk(j))),
            pl.BlockSpec((8, tc), lambda i, j: (jnp.maximum(i * r8 - 1, 0), col_blk(j))),
            pl.BlockSpec((8, tc), lambda i, j: (jnp.minimum((i + 1) * r8, last8), col_blk(j)))]


def _ssd_conv_kernel(x_ref, p_ref, n_ref, w_ref, b_ref, o_ref, *, tr, t_ctx, l_ctx, l_lat):
    first, last = _seq_edges(pl.program_id(0), tr, t_ctx, l_ctx, l_lat)
    o_ref[...] = _silu(_conv3(x_ref[...], p_ref[...], n_ref[...], w_ref, b_ref, first, last))


def ssd_conv(proj, w, b, *, t_ctx, l_ctx, l_lat):
    t = proj.shape[0]
    tr, tc = ROW_TILE, 512
    c = w.shape[1]
    off = P_XBC // tc
    return pl.pallas_call(
        functools.partial(_ssd_conv_kernel, tr=tr, t_ctx=t_ctx, l_ctx=l_ctx, l_lat=l_lat),
        out_shape=jax.ShapeDtypeStruct((t, c), F32),
        grid=(t // tr, c // tc),
        in_specs=_conv_specs(tr, tc, lambda j: off + j, t) + [
            pl.BlockSpec((3, tc), lambda i, j: (0, j)),
            pl.BlockSpec((1, tc), lambda i, j: (0, j))],
        out_specs=pl.BlockSpec((tr, tc), lambda i, j: (i, j)),
        compiler_params=_cp(("arbitrary", "arbitrary")),
        name="ssd_conv",
    )(proj, proj, proj, w, b.reshape(1, c))


def _hy_pre_kernel(*refs, tr, t_ctx, l_ctx, l_lat):
    (x0, x0p, x0n, x1, x1p, x1n, xv, xvp, xvn, w0, w1, wv, b0, b1, bv, o0_ref, op_ref) = refs
    first, last = _seq_edges(pl.program_id(0), tr, t_ctx, l_ctx, l_lat)
    o0_ref[...] = _conv3(x0[...], x0p[...], x0n[...], w0, b0, first, last)
    u1 = _conv3(x1[...], x1p[...], x1n[...], w1, b1, first, last)
    uv = _conv3(xv[...], xvp[...], xvn[...], wv, bv, first, last)
    op_ref[...] = u1 * uv


def hyena_pre(proj, w, b, *, t_ctx, l_ctx, l_lat):
    t = proj.shape[0]
    tr, tc = ROW_TILE, HY_DIM
    nb = HY_DIM // tc
    b2 = b.reshape(1, 3 * HY_DIM)
    specs = []
    for s in range(3):
        specs += _conv_specs(tr, tc, lambda j, s=s: P_HY // tc + s * nb + j, t)
    specs += [pl.BlockSpec((3, tc), lambda i, j, s=s: (0, s * nb + j)) for s in range(3)]
    specs += [pl.BlockSpec((1, tc), lambda i, j, s=s: (0, s * nb + j)) for s in range(3)]
    return pl.pallas_call(
        functools.partial(_hy_pre_kernel, tr=tr, t_ctx=t_ctx, l_ctx=l_ctx, l_lat=l_lat),
        out_shape=(jax.ShapeDtypeStruct((t, HY_DIM), F32), jax.ShapeDtypeStruct((t, HY_DIM), F32)),
        grid=(t // tr, nb),
        in_specs=specs,
        out_specs=(pl.BlockSpec((tr, tc), lambda i, j: (i, j)), pl.BlockSpec((tr, tc), lambda i, j: (i, j))),
        compiler_params=_cp(("arbitrary", "arbitrary")),
        name="hyena_pre",
    )(*([proj] * 9), w, w, w, b2, b2, b2)


def _filter_kernel(z_ref, w1_ref, b1_ref, w2_ref, b2_ref, w3_ref, fr_ref, dec_ref, o_ref):
    z = z_ref[...]
    fr = fr_ref[...]
    h = jnp.sin(fr * (jnp.dot(z, w1_ref[...], precision=HI, preferred_element_type=F32) + b1_ref[...]))
    h = jnp.sin(fr * (jnp.dot(h, w2_ref[...], precision=HI, preferred_element_type=F32) + b2_ref[...]))
    f = jnp.dot(h, w3_ref[...], precision=HI, preferred_element_type=F32)
    o_ref[...] = f * jnp.exp(-z[:, 0:1] * jnp.abs(dec_ref[...]))


def _filter_embedding(seq_len, emb):
    bands_n = (emb - 1) // 2
    t = jnp.linspace(0.0, 1.0, seq_len, dtype=F32)[:, None]
    bands = jnp.linspace(1e-4, bands_n - 1, bands_n, dtype=F32)[None, :]
    ang = (2.0 * math.pi / seq_len) * jnp.arange(seq_len, dtype=F32)[:, None] * bands
    z = jnp.concatenate([t, jnp.cos(ang), -jnp.sin(ang)], axis=-1)
    return jnp.pad(z, ((0, 0), (0, LANES - emb)))


def hyena_filter(seq_len, w1, b1, w2, b2, w3, freq, decay):
    emb, ff = w1.shape
    n = w3.shape[1]
    z = _filter_embedding(seq_len, emb)
    padc = LANES - ff
    w1p = jnp.pad(w1, ((0, LANES - emb), (0, padc)))
    w2p = jnp.pad(w2, ((0, padc), (0, padc)))
    w3p = jnp.pad(w3, ((0, padc), (0, 0)))
    row = lambda v: jnp.pad(v, (0, padc)).reshape(1, LANES)
    tr = 256
    full = lambda shape: pl.BlockSpec(shape, lambda i: (0, 0))
    return pl.pallas_call(
        _filter_kernel,
        out_shape=jax.ShapeDtypeStruct((seq_len, n), F32),
        grid=(seq_len // tr,),
        in_specs=[pl.BlockSpec((tr, LANES), lambda i: (i, 0)), full((LANES, LANES)), full((1, LANES)),
                  full((LANES, LANES)), full((1, LANES)), full((LANES, n)), full((1, LANES)), full((1, n))],
        out_specs=pl.BlockSpec((tr, n), lambda i: (i, 0)),
        compiler_params=_cp(("arbitrary",)),
        name="hyena_filter",
    )(z, w1p, row(b1), w2p, row(b2), w3p, row(freq), decay.reshape(1, n))


def _circular_filter(filt):
    ch = filt.shape[1] // 2
    h_fwd, h_bwd = filt[:, :ch], filt[:, ch:]
    return jnp.concatenate([h_fwd, jnp.zeros((1, ch), F32), h_bwd[1:][::-1]], axis=0)


def _cs(n_rows, n_cols, period):
    ang = 2.0 * np.pi * (np.outer(np.arange(n_rows), np.arange(n_cols)) % period) / period
    return np.cos(ang), np.sin(ang)


def _mm_kernel(a_ref, b_ref, o_ref):
    o_ref[0] = _dot3(a_ref[...], b_ref[0])


def const_matmul(a, b, tn):
    m, k = a.shape
    bsz, _, n = b.shape
    return pl.pallas_call(
        _mm_kernel,
        out_shape=jax.ShapeDtypeStruct((bsz, m, n), F32),
        grid=(bsz, n // tn),
        in_specs=[pl.BlockSpec((m, k), lambda s, j: (0, 0)), pl.BlockSpec((1, k, tn), lambda s, j: (s, 0, j))],
        out_specs=pl.BlockSpec((1, m, tn), lambda s, j: (s, 0, j)),
        compiler_params=_cp(("arbitrary", "arbitrary")),
        name="const_matmul",
    )(a, b)


def _hy_ctx_kernel(x0_ref, p_ref, kr_ref, ki_ref, fw_ref, iv_ref, bias_ref, o_ref, *, n):
    p = p_ref[...]
    xf = _dot3(fw_ref[...], p)
    xr, xi = xf[:n], xf[n:]
    kr, ki = kr_ref[...], ki_ref[...]
    yr = xr * kr - xi * ki
    yi = xr * ki + xi * kr
    y = _dot3(iv_ref[...], jnp.concatenate([yr, yi], axis=0))
    o_ref[...] = x0_ref[...] * (y + p * bias_ref[...])


def hyena_ctx(x0, p, filt, bias, *, n_seq, seq_len):
    n = 2 * seq_len
    ch = p.shape[1]
    c_full, s_full = _cs(n, n, n)
    fw_full = jnp.asarray(np.concatenate([c_full, -s_full], axis=0), F32)
    fw_half = fw_full[:, :seq_len]
    iv = jnp.asarray(np.concatenate([c_full[:seq_len], -s_full[:seq_len]], axis=1) / n, F32)
    kf = const_matmul(fw_full, _circular_filter(filt)[None], ch)[0]
    kr, ki = kf[:n], kf[n:]
    full = lambda shape: pl.BlockSpec(shape, lambda s: (0, 0))
    return pl.pallas_call(
        functools.partial(_hy_ctx_kernel, n=n),
        out_shape=jax.ShapeDtypeStruct((n_seq * seq_len, ch), F32),
        grid=(n_seq,),
        in_specs=[pl.BlockSpec((seq_len, ch), lambda s: (s, 0)), pl.BlockSpec((seq_len, ch), lambda s: (s, 0)),
                  full((n, ch)), full((n, ch)), full((2 * n, seq_len)), full((seq_len, 2 * n)), full((1, ch))],
        out_specs=pl.BlockSpec((seq_len, ch), lambda s: (s, 0)),
        compiler_params=_cp(("arbitrary",)),
        name="hyena_ctx",
    )(x0, p, kr, ki, fw_half, iv, bias.reshape(1, ch))


def _stage2_kernel(a_ref, twr_ref, twi_ref, m_ref, *rest, conv):
    if conv:
        kf_ref, mi_ref, o_ref = rest
    else:
        (o_ref,) = rest
    ar, ai = a_ref[0, 0, 0], a_ref[0, 1, 0]
    twr, twi = twr_ref[0], twi_ref[0]
    br = ar * twr - ai * twi
    bi = ar * twi + ai * twr
    x = _dot3(m_ref[...], jnp.concatenate([br, bi], axis=0))
    n2 = ar.shape[0]
    xr, xi = x[:n2], x[n2:]
    if not conv:
        o_ref[0, 0, 0] = xr
        o_ref[0, 1, 0] = xi
        return
    kr, ki = kf_ref[0, 0, 0], kf_ref[0, 1, 0]
    yr = xr * kr - xi * ki
    yi = xr * ki + xi * kr
    pq = _dot3(mi_ref[...], jnp.concatenate([yr, yi], axis=0))
    pr, pi = pq[:n2], pq[n2:]
    o_ref[0, 0, 0] = pr * twr + pi * twi
    o_ref[0, 1, 0] = pi * twr - pr * twi


def _stage2(a, kf, ch):
    bsz = a.shape[0]
    n1, n2 = DFT_N1, DFT_N2
    n = n1 * n2
    tw_ang = 2.0 * np.pi * np.outer(np.arange(n1), np.arange(n2)) / n
    twr = jnp.asarray(np.cos(tw_ang), F32).reshape(n1, n2, 1)
    twi = jnp.asarray(-np.sin(tw_ang), F32).reshape(n1, n2, 1)
    c2, s2 = _cs(n2, n2, n2)
    m_fwd = jnp.asarray(np.block([[c2, s2], [-s2, c2]]), F32)
    m_inv = jnp.asarray(np.block([[c2, -s2], [s2, c2]]), F32)
    conv = kf is not None
    blk = pl.BlockSpec((1, 2, 1, n2, ch), lambda s, k: (s, 0, k, 0, 0))
    specs = [blk, pl.BlockSpec((1, n2, 1), lambda s, k: (k, 0, 0)), pl.BlockSpec((1, n2, 1), lambda s, k: (k, 0, 0)),
             pl.BlockSpec((2 * n2, 2 * n2), lambda s, k: (0, 0))]
    args = [a, twr, twi, m_fwd]
    if conv:
        specs += [pl.BlockSpec((1, 2, 1, n2, ch), lambda s, k: (0, 0, k, 0, 0)),
                  pl.BlockSpec((2 * n2, 2 * n2), lambda s, k: (0, 0))]
        args += [kf, m_inv]
    return pl.pallas_call(
        functools.partial(_stage2_kernel, conv=conv),
        out_shape=jax.ShapeDtypeStruct(a.shape, F32),
        grid=(bsz, n1),
        in_specs=specs,
        out_specs=blk,
        compiler_params=_cp(("arbitrary", "arbitrary")),
        name="hyena_stage2",
    )(*args)


def _hy_post_kernel(g_ref, q_ref, x0_ref, p_ref, bias_ref, o_ref):
    y = _dot3(g_ref[...], q_ref[0])
    o_ref[0] = x0_ref[0] * (y + p_ref[0] * bias_ref[...])


def hyena_lat(x0, p, filt, bias, *, n_seq, seq_len):
    n1, n2 = DFT_N1, DFT_N2
    n = n1 * n2
    assert n == 2 * seq_len
    ch = p.shape[1]
    h1 = n1 // 2
    wide = n2 * ch
    c1, s1 = _cs(n1, n1, n1)
    f1_full = jnp.asarray(np.concatenate([c1, -s1], axis=0), F32)
    f1_half = f1_full[:, :h1]
    g1 = jnp.asarray(np.concatenate([c1[:h1], -s1[:h1]], axis=1) / n, F32)
    tn = 8192
    kcirc = _circular_filter(filt).reshape(1, n1, wide)
    kf = _stage2(const_matmul(f1_full, kcirc, tn).reshape(1, 2, n1, n2, ch), None, ch)
    a = const_matmul(f1_half, p.reshape(n_seq, h1, wide), tn).reshape(n_seq, 2, n1, n2, ch)
    q = _stage2(a, kf, ch).reshape(n_seq, 2 * n1, wide)
    bias_w = jnp.tile(bias, n2).reshape(1, wide)
    out = pl.pallas_call(
        _hy_post_kernel,
        out_shape=jax.ShapeDtypeStruct((n_seq, h1, wide), F32),
        grid=(n_seq, wide // tn),
        in_specs=[pl.BlockSpec((h1, 2 * n1), lambda s, j: (0, 0)),
                  pl.BlockSpec((1, 2 * n1, tn), lambda s, j: (s, 0, j)),
                  pl.BlockSpec((1, h1, tn), lambda s, j: (s, 0, j)),
                  pl.BlockSpec((1, h1, tn), lambda s, j: (s, 0, j)),
                  pl.BlockSpec((1, tn), lambda s, j: (0, j))],
        out_specs=pl.BlockSpec((1, h1, tn), lambda s, j: (s, 0, j)),
        compiler_params=_cp(("arbitrary", "arbitrary")),
        name="hyena_post",
    )(g1, q, x0.reshape(n_seq, h1, wide), p.reshape(n_seq, h1, wide), bias_w)
    return out.reshape(n_seq * seq_len, ch)


def _ssd_dir(xbc, dtraw, st_ref, d, consts, y_ref):
    tri, expand, dtb, acont, dskip = consts
    cl = xbc.shape[0]
    xs = xbc[:, :SSD_INNER]
    dt = _softplus(dtraw + dtb)
    a = dt * acont
    tri_d = tri if d == 0 else tri.T
    cs = jnp.dot(tri_d, a, precision=HI, preferred_element_type=F32)
    cs_t = jnp.dot(a.T, tri_d.T, precision=HI, preferred_element_type=F32)
    dt_x = jnp.dot(dt, expand[d], precision=HI, preferred_element_type=F32)
    xdt = xs * dt_x
    row = lax.broadcasted_iota(jnp.int32, (cl, cl), 0)
    col = lax.broadcasted_iota(jnp.int32, (cl, cl), 1)
    keep = (col <= row) if d == 0 else (col >= row)
    lane = lax.broadcasted_iota(jnp.int32, (cl, LANES), 1)
    low = lane < SSD_HEADDIM
    edge = cl - 1 if d == 0 else 0
    for g in range(SSD_GROUPS):
        bg = xbc[:, SSD_INNER + g * SSD_STATE: SSD_INNER + (g + 1) * SSD_STATE]
        cg = xbc[:, SSD_INNER + (SSD_GROUPS + g) * SSD_STATE: SSD_INNER + (SSD_GROUPS + g + 1) * SSD_STATE]
        bg16, cg16 = bg.astype(BF16), cg.astype(BF16)
        cb = lax.dot_general(cg16, bg16, (((1,), (1,)), ((), ())), preferred_element_type=F32)
        for pr in range(2):
            pair = g * 2 + pr
            h0 = 2 * pair
            ms = []
            for h in (h0, h0 + 1):
                ln = d * SSD_HEADS + h
                diff = cs[:, ln:ln + 1] - cs_t[ln:ln + 1, :]
                ms.append(jnp.where(keep, cb * jnp.exp(jnp.minimum(diff, 0.0)), 0.0).astype(BF16))
            xp = xdt[:, pair * LANES:(pair + 1) * LANES]
            xs_p = xs[:, pair * LANES:(pair + 1) * LANES]
            xlo = jnp.where(low, xp, 0.0).astype(BF16)
            xhi = jnp.where(low, 0.0, xp).astype(BF16)
            y_diag = (jnp.dot(ms[0], xlo, preferred_element_type=F32)
                      + jnp.dot(ms[1], xhi, preferred_element_type=F32))
            l0, l1 = d * SSD_HEADS + h0, d * SSD_HEADS + h0 + 1
            e_cs = jnp.where(low, jnp.exp(cs[:, l0:l0 + 1]), jnp.exp(cs[:, l1:l1 + 1]))
            st = st_ref[pair]
            y_off = jnp.dot(cg16, st.astype(BF16), preferred_element_type=F32) * e_cs
            y_ref[:, pair * LANES:(pair + 1) * LANES] = (
                y_diag + y_off + xs_p * dskip[d:d + 1, pair * LANES:(pair + 1) * LANES])
            tot0, tot1 = cs[edge:edge + 1, l0:l0 + 1], cs[edge:edge + 1, l1:l1 + 1]
            dec = jnp.where(low, jnp.exp(tot0 - cs[:, l0:l0 + 1]), jnp.exp(tot1 - cs[:, l1:l1 + 1]))
            upd = lax.dot_general(bg16, (xp * dec).astype(BF16), (((0,), (0,)), ((), ())),
                                  preferred_element_type=F32)
            st_ref[pair] = st * jnp.where(low[0:1], jnp.exp(tot0), jnp.exp(tot1)) + upd


def _ssd_kernel(xf_ref, dtf_ref, xb_ref, dtb_ref, init_ref, tri_ref, exp_ref, dtbias_ref, acont_ref, dskip_ref,
                yf_ref, yb_ref, fin_ref, st_ref, *, has_init):
    s = pl.program_id(1)

    @pl.when(s == 0)
    def _():
        if has_init:
            st_ref[...] = init_ref[0]
        else:
            st_ref[...] = jnp.zeros_like(st_ref)

    consts = (tri_ref[...], (exp_ref[0], exp_ref[1]), dtbias_ref[...], acont_ref[...], dskip_ref[...])
    _ssd_dir(xf_ref[...], dtf_ref[...], st_ref.at[0], 0, consts, yf_ref)
    _ssd_dir(xb_ref[...], dtb_ref[...], st_ref.at[1], 1, consts, yb_ref)

    @pl.when(s == pl.num_programs(1) - 1)
    def _():
        fin_ref[0] = st_ref[...]


def _pair_states(s):
    b = s.shape[0]
    s = s.reshape(b, 2, SSD_HEADS // 2, 2, SSD_HEADDIM, SSD_STATE)
    return s.transpose(0, 1, 2, 5, 3, 4).reshape(b, 2, SSD_HEADS // 2, SSD_STATE, 2 * SSD_HEADDIM)


def _unpair_states(s):
    b = s.shape[0]
    s = s.reshape(b, 2, SSD_HEADS // 2, SSD_STATE, 2, SSD_HEADDIM)
    return s.transpose(0, 1, 2, 4, 5, 3).reshape(b, 2, SSD_HEADS, SSD_HEADDIM, SSD_STATE)


def ssd_scan(xbc, proj, init, dt_bias, a_log, d_skip, *, row0, n_seq, seq_len):
    cl = SSD_CHUNK
    nc = seq_len // cl
    base = row0 // cl
    has_init = init is not None
    hp = SSD_HEADS // 2
    init_p = _pair_states(init) if has_init else jnp.zeros((1, 2, hp, SSD_STATE, LANES), F32)
    tri = jnp.asarray(np.tril(np.ones((cl, cl))), F32)
    expand = np.zeros((2, LANES, SSD_INNER), np.float32)
    for d in range(2):
        for h in range(SSD_HEADS):
            expand[d, d * SSD_HEADS + h, h * SSD_HEADDIM:(h + 1) * SSD_HEADDIM] = 1.0
    pad16 = lambda v: jnp.pad(v.reshape(1, 2 * SSD_HEADS), ((0, 0), (0, LANES - 2 * SSD_HEADS)))
    acont = pad16(-jnp.exp(a_log))
    dtb = pad16(dt_bias)
    dskip = jnp.repeat(d_skip, SSD_HEADDIM, axis=1)
    cxbc = xbc.shape[1]
    dtblk = P_DT // LANES
    full2 = lambda shape: pl.BlockSpec(shape, lambda b, s: (0,) * len(shape))
    fwd = lambda b, s: base + b * nc + s
    bwd = lambda b, s: base + b * nc + (nc - 1 - s)
    st_spec = pl.BlockSpec((1, 2, hp, SSD_STATE, LANES), lambda b, s: (b if has_init else 0, 0, 0, 0, 0))
    yf, yb, fin = pl.pallas_call(
        functools.partial(_ssd_kernel, has_init=has_init),
        out_shape=(jax.ShapeDtypeStruct((n_seq * seq_len, SSD_INNER), F32),
                   jax.ShapeDtypeStruct((n_seq * seq_len, SSD_INNER), F32),
                   jax.ShapeDtypeStruct((n_seq, 2, hp, SSD_STATE, LANES), F32)),
        grid=(n_seq, nc),
        in_specs=[pl.BlockSpec((cl, cxbc), lambda b, s: (fwd(b, s), 0)),
                  pl.BlockSpec((cl, LANES), lambda b, s: (fwd(b, s), dtblk)),
                  pl.BlockSpec((cl, cxbc), lambda b, s: (bwd(b, s), 0)),
                  pl.BlockSpec((cl, LANES), lambda b, s: (bwd(b, s), dtblk)),
                  st_spec, full2((cl, cl)), full2((2, LANES, SSD_INNER)), full2((1, LANES)), full2((1, LANES)),
                  full2((2, SSD_INNER))],
        out_specs=(pl.BlockSpec((cl, SSD_INNER), lambda b, s: (b * nc + s, 0)),
                   pl.BlockSpec((cl, SSD_INNER), lambda b, s: (b * nc + (nc - 1 - s), 0)),
                   pl.BlockSpec((1, 2, hp, SSD_STATE, LANES), lambda b, s: (b, 0, 0, 0, 0))),
        scratch_shapes=[pltpu.VMEM((2, hp, SSD_STATE, LANES), F32)],
        compiler_params=_cp(("arbitrary", "arbitrary")),
        name="ssd_scan",
    )(xbc, proj, xbc, proj, init_p, tri, jnp.asarray(expand), dtb, acont, dskip)
    return yf, yb, _unpair_states(fin)


def _headnorm(x, g_ref, w_ref):
    ms = _dot3(x * x, g_ref[...])
    return x * lax.rsqrt(ms + EPS) * w_ref[...]


def _rope(x, cos, sin_signed):
    lane = lax.broadcasted_iota(jnp.int32, x.shape, 1)
    w = x.shape[1]
    swapped = jnp.where(lane % 2 == 0, pltpu.roll(x, w - 1, axis=1), pltpu.roll(x, 1, axis=1))
    return x * cos + swapped * sin_signed


Q_SCALE = HEAD_DIM ** -0.5 * math.log2(math.e)


def _store_padded_heads(qo_ref, q):
    rep = N_HEADS // N_KV_HEADS
    lane = lax.broadcasted_iota(jnp.int32, (q.shape[0], LANES), 1)
    for h in range(N_HEADS):
        g = h // rep
        chunk = q[:, (h // 2) * LANES:(h // 2 + 1) * LANES]
        if h % 2 != g:
            chunk = pltpu.roll(chunk, HEAD_DIM, axis=1)
        keep = (lane >= g * HEAD_DIM) & (lane < (g + 1) * HEAD_DIM)
        qo_ref[:, h * LANES:(h + 1) * LANES] = jnp.where(keep, chunk, 0.0).astype(qo_ref.dtype)


def _qk_kernel(q_ref, k_ref, cos_ref, sin_ref, gq_ref, gk_ref, qw_ref, kw_ref, qo_ref, ko_ref, *, n_ctx_tiles):
    q = _headnorm(q_ref[...], gq_ref, qw_ref)
    k = _headnorm(k_ref[...], gk_ref, kw_ref)
    is_lat = pl.program_id(0) >= n_ctx_tiles

    @pl.when(is_lat)
    def _():
        cos, sin = cos_ref[...], sin_ref[...]
        _store_padded_heads(qo_ref, _rope(q, cos, sin) * Q_SCALE)
        ko_ref[...] = _rope(k, cos[:, :k.shape[1]], sin[:, :k.shape[1]])

    @pl.when(jnp.logical_not(is_lat))
    def _():
        _store_padded_heads(qo_ref, q * Q_SCALE)
        ko_ref[...] = k


def _rope_tables(seq_len):
    n_rows = seq_len // GRID_W
    row = jnp.repeat(jnp.arange(n_rows), GRID_W).astype(F32)
    col = jnp.tile(jnp.arange(GRID_W), n_rows).astype(F32)
    n_freq = HEAD_DIM // 4
    inv = ROPE_THETA ** (-jnp.arange(n_freq, dtype=F32) / n_freq)
    ang = jnp.concatenate([row[:, None] * inv, col[:, None] * inv], axis=-1)
    cos = jnp.repeat(jnp.cos(ang), 2, axis=1)
    sin = jnp.repeat(jnp.sin(ang), 2, axis=1) * jnp.tile(jnp.asarray([-1.0, 1.0], F32), HEAD_DIM // 2)
    return jnp.tile(cos, (1, N_HEADS)), jnp.tile(sin, (1, N_HEADS))


def qk_prep(proj, q_norm_w, k_norm_w, *, t_ctx, l_lat):
    t = proj.shape[0]
    tr = ROW_TILE
    qc, kc = N_HEADS * HEAD_DIM, N_KV_HEADS * HEAD_DIM
    cos, sin = _rope_tables(l_lat)
    group = lambda c: jnp.asarray(np.kron(np.eye(c // HEAD_DIM), np.ones((HEAD_DIM, HEAD_DIM))) / HEAD_DIM, F32)
    n_ctx_tiles = t_ctx // tr
    per = l_lat // tr
    tab = pl.BlockSpec((tr, qc), lambda i: (jnp.maximum(i - n_ctx_tiles, 0) % per, 0))
    full = lambda shape: pl.BlockSpec(shape, lambda i: (0, 0))
    return pl.pallas_call(
        functools.partial(_qk_kernel, n_ctx_tiles=n_ctx_tiles),
        out_shape=(jax.ShapeDtypeStruct((t, N_HEADS * LANES), BF16), jax.ShapeDtypeStruct((t, kc), F32)),
        grid=(t // tr,),
        in_specs=[pl.BlockSpec((tr, qc), lambda i: (i, P_Q // qc)), pl.BlockSpec((tr, kc), lambda i: (i, P_K // kc)),
                  tab, tab, full((qc, qc)), full((kc, kc)), full((1, qc)), full((1, kc))],
        out_specs=(pl.BlockSpec((tr, N_HEADS * LANES), lambda i: (i, 0)), pl.BlockSpec((tr, kc), lambda i: (i, 0))),
        compiler_params=_cp(("arbitrary",)),
        name="qk_prep",
    )(proj, proj, cos, sin, group(qc), group(kc), jnp.tile(q_norm_w, N_HEADS).reshape(1, qc),
      jnp.tile(k_norm_w, N_KV_HEADS).reshape(1, kc))


def _attention_kernel(q_ref, k_ref, v_ref, o_ref, *, tq):
    rep = N_HEADS // N_KV_HEADS
    k = k_ref[0]
    v = v_ref[0]
    for g in range(N_KV_HEADS):
        qs = jnp.concatenate([q_ref[:, (g * rep + r) * LANES:(g * rep + r + 1) * LANES] for r in range(rep)],
                             axis=0)
        s = lax.dot_general(qs, k, (((1,), (1,)), ((), ())), preferred_element_type=F32)
        p = jnp.exp2(s - jnp.max(s, axis=-1, keepdims=True))
        out = jnp.dot(p.astype(BF16), v, preferred_element_type=F32) / jnp.sum(p, axis=-1, keepdims=True)
        for r in range(rep):
            h = g * rep + r
            o_ref[:, h * HEAD_DIM:(h + 1) * HEAD_DIM] = (
                out[r * tq:(r + 1) * tq, g * HEAD_DIM:(g + 1) * HEAD_DIM].astype(o_ref.dtype))


def attention(q, k, v, *, row0, seq_len, tq):
    qc = q.shape[1]
    b, lk, kc = k.shape
    per = seq_len // tq
    base = row0 // tq
    return pl.pallas_call(
        functools.partial(_attention_kernel, tq=tq),
        out_shape=jax.ShapeDtypeStruct((b * seq_len, N_HEADS * HEAD_DIM), BF16),
        grid=(b, per),
        in_specs=[pl.BlockSpec((tq, qc), lambda s, i: (base + s * per + i, 0)),
                  pl.BlockSpec((1, lk, kc), lambda s, i: (s, 0, 0)),
                  pl.BlockSpec((1, lk, kc), lambda s, i: (s, 0, 0))],
        out_specs=pl.BlockSpec((tq, N_HEADS * HEAD_DIM), lambda s, i: (s * per + i, 0)),
        compiler_params=_cp(("arbitrary", "arbitrary")),
        name="attention",
    )(q, k, v)


def _merge_kernel(x_ref, yhy_c, yhy_l, yf_c, yf_l, yb_c, yb_l, z_ref, yatt_c, yatt_l, gate_ref, snw_ref, whb_ref,
                  wsb_ref, wab_ref, wout_ref, g1_ref, n2w_ref, sh2_ref, sc2_ref, wr_ref, br_ref, ltri_ref,
                  xo_ref, h2_ref, ti_ref, tw_ref, hist_ref, *, n_ctx_tiles):
    d = x_ref.shape[1]
    is_ctx = pl.program_id(0) < n_ctx_tiles
    pick = lambda c_ref, l_ref: jnp.where(is_ctx, c_ref[...], l_ref[...])
    ys = (pick(yf_c, yf_l) + pick(yb_c, yb_l)) * _silu(z_ref[...])
    ys = ys * lax.rsqrt(jnp.mean(ys * ys, axis=-1, keepdims=True) + EPS) * snw_ref[...]
    gate = gate_ref[...].astype(F32)
    merged = (gate[:, :d] * jnp.dot(pick(yhy_c, yhy_l).astype(BF16), whb_ref[...], preferred_element_type=F32)
              + gate[:, d:2 * d] * jnp.dot(ys.astype(BF16), wsb_ref[...], preferred_element_type=F32)
              + gate[:, 2 * d:] * jnp.dot(pick(yatt_c, yatt_l), wab_ref[...], preferred_element_type=F32))
    mix = jnp.dot(merged.astype(BF16), wout_ref[...], preferred_element_type=F32)
    x = x_ref[...] + g1_ref[0] * mix
    xo_ref[...] = x
    h = x * lax.rsqrt(jnp.mean(x * x, axis=-1, keepdims=True) + EPS) * n2w_ref[...]
    h = h * (1.0 + sc2_ref[0]) + sh2_ref[0]
    h2_ref[...] = h
    logits = jnp.dot(h, wr_ref[...], precision=HI, preferred_element_type=F32) + br_ref[...]
    lane = lax.broadcasted_iota(jnp.int32, logits.shape, 1)
    work = jnp.where(lane < N_EXPERTS, logits, -jnp.inf)
    idx_out = jnp.zeros(logits.shape, jnp.int32)
    val_out = jnp.full(logits.shape, -jnp.inf, F32)
    picks = []
    for j in range(TOP_K):
        mx = jnp.max(work, axis=-1, keepdims=True)
        am = jnp.min(jnp.where(work == mx, lane, LANES), axis=-1, keepdims=True)
        idx_out = jnp.where(lane == j, am, idx_out)
        val_out = jnp.where(lane == j, mx, val_out)
        picks.append(lane == am)
        work = jnp.where(picks[-1], -jnp.inf, work)
    e = jnp.exp(val_out - jnp.max(val_out, axis=-1, keepdims=True))
    tw_ref[...] = e / jnp.sum(e, axis=-1, keepdims=True)
    chosen = jnp.where(picks[0] | picks[1] | picks[2] | picks[3], 1.0, 0.0)
    before = jnp.dot(ltri_ref[...], chosen.astype(BF16), preferred_element_type=F32)
    rank_out = jnp.zeros(logits.shape, F32)
    for j in range(TOP_K):
        rank_j = jnp.sum(jnp.where(picks[j], before, 0.0), axis=-1, keepdims=True)
        rank_out = jnp.where(lane == TOP_K + j, rank_j, rank_out)
    ti_ref[...] = idx_out + rank_out.astype(jnp.int32)
    hist_ref[...] = jnp.broadcast_to(jnp.sum(chosen, axis=0, keepdims=True), hist_ref.shape)


def merge_router(x, y_hy, yf, yb, proj, y_att, gate, ssd_norm_w, whb, wsb, wab, wout, mods, n2w, wr, br,
                 *, t_ctx, l_lat, tm=256):
    t, d = x.shape
    nct = t_ctx // tm
    nlt = (t - t_ctx) // tm
    row = lambda tc, c0=0: pl.BlockSpec((tm, tc), lambda i: (i, c0))
    full = lambda shape: pl.BlockSpec(shape, lambda i: (0, 0))
    pair = lambda tc: [pl.BlockSpec((tm, tc), lambda i: (jnp.minimum(i, nct - 1), 0)),
                       pl.BlockSpec((tm, tc), lambda i: (jnp.clip(i - nct, 0, nlt - 1), 0))]
    ltri = jnp.asarray(np.tril(np.ones((tm, tm)), -1), BF16)
    return pl.pallas_call(
        functools.partial(_merge_kernel, n_ctx_tiles=nct),
        out_shape=(jax.ShapeDtypeStruct((t, d), F32), jax.ShapeDtypeStruct((t, d), F32),
                   jax.ShapeDtypeStruct((t, LANES), jnp.int32), jax.ShapeDtypeStruct((t, LANES), F32),
                   jax.ShapeDtypeStruct((t // tm * 8, LANES), F32)),
        grid=(t // tm,),
        in_specs=[row(d)] + pair(HY_DIM) + pair(SSD_INNER) + pair(SSD_INNER) + [row(SSD_INNER, P_Z // SSD_INNER)]
                 + pair(N_HEADS * HEAD_DIM) + [row(3 * d), full((1, SSD_INNER)),
                  full(whb.shape), full(wsb.shape), full(wab.shape), full(wout.shape),
                  _mod_spec(2, tm, t_ctx, l_lat), full((1, d)), _mod_spec(3, tm, t_ctx, l_lat),
                  _mod_spec(4, tm, t_ctx, l_lat), full((d, LANES)), full((1, LANES)), full((tm, tm))],
        out_specs=(row(d), row(d), row(LANES), row(LANES), pl.BlockSpec((8, LANES), lambda i: (i, 0))),
        compiler_params=_cp(("arbitrary",)),
        name="merge_router",
    )(x, *y_hy, *yf, *yb, proj, *y_att, gate, ssd_norm_w.reshape(1, -1), whb, wsb, wab, wout, mods, n2w, mods, mods,
      wr, br, ltri)


DEINT_BLOCK = 256


def _deinterleave_kernel(w_ref, perm_ref, g_ref, u_ref):
    half = DEINT_BLOCK // 2
    for blk in range(w_ref.shape[2] // DEINT_BLOCK):
        wb = w_ref[0, :, blk * DEINT_BLOCK:(blk + 1) * DEINT_BLOCK].astype(BF16)
        r = jnp.dot(wb, perm_ref[...], preferred_element_type=F32)
        g_ref[0, :, blk * half:(blk + 1) * half] = r[:, :half].astype(BF16)
        u_ref[0, :, blk * half:(blk + 1) * half] = r[:, half:].astype(BF16)


def deinterleave_experts(w1):
    e, d, f2 = w1.shape
    perm = np.zeros((DEINT_BLOCK, DEINT_BLOCK), np.float32)
    half = DEINT_BLOCK // 2
    perm[2 * np.arange(half), np.arange(half)] = 1.0
    perm[2 * np.arange(half) + 1, half + np.arange(half)] = 1.0
    out = jax.ShapeDtypeStruct((e, d, f2 // 2), BF16)
    return pl.pallas_call(
        _deinterleave_kernel,
        out_shape=(out, out),
        grid=(e,),
        in_specs=[pl.BlockSpec((1, d, f2), lambda i: (i, 0, 0)),
                  pl.BlockSpec((DEINT_BLOCK, DEINT_BLOCK), lambda i: (0, 0))],
        out_specs=(pl.BlockSpec((1, d, f2 // 2), lambda i: (i, 0, 0)),
                   pl.BlockSpec((1, d, f2 // 2), lambda i: (i, 0, 0))),
        compiler_params=_cp(("arbitrary",)),
        name="deinterleave_experts",
    )(w1, jnp.asarray(perm, BF16))


DMA_CHUNK = 512
DMA_UNROLL = 8


def _chunk_wait(ref, sem):
    pltpu.make_async_copy(ref.at[pl.ds(0, DMA_CHUNK)], ref.at[pl.ds(0, DMA_CHUNK)], sem).wait()


def _issue_rows(copy_of):
    def body(it, carry):
        slot = it & (TOP_K - 1)
        tok0 = pl.multiple_of(lax.shift_right_logical(it, TOP_K_SHIFT) * DMA_UNROLL, DMA_UNROLL)
        for u in range(DMA_UNROLL):
            copy_of(slot, tok0, u).start(priority=u % 2)
        return carry
    lax.fori_loop(0, DMA_CHUNK // DMA_UNROLL, body, 0)


def _dispatch_kernel(idx_ref, x_ref, zero_ref, dst_ref, sem):
    del zero_ref

    def copy_of(slot, tok0, u):
        j = idx_ref[0, 0, (tok0 + u) * TOP_K + slot]
        return pltpu.make_async_copy(x_ref.at[pl.ds(tok0 + u, 1)], dst_ref.at[pl.ds(j, 1)], sem)

    _issue_rows(copy_of)
    _chunk_wait(dst_ref, sem)


def moe_dispatch(dest, h2, n_dst):
    n = dest.shape[0]
    nc = n // DMA_CHUNK
    tok = DMA_CHUNK // TOP_K
    d = h2.shape[1]
    return pl.pallas_call(
        _dispatch_kernel,
        out_shape=jax.ShapeDtypeStruct((n_dst, d), h2.dtype),
        grid=(nc,),
        in_specs=[pl.BlockSpec((1, 1, DMA_CHUNK), lambda c: (c, 0, 0), memory_space=pltpu.SMEM),
                  pl.BlockSpec((tok, d), lambda c: (c, 0)),
                  pl.BlockSpec(memory_space=pl.ANY)],
        out_specs=pl.BlockSpec(memory_space=pl.ANY),
        scratch_shapes=[pltpu.SemaphoreType.DMA(())],
        input_output_aliases={2: 0},
        compiler_params=_cp(("arbitrary",)),
        name="moe_dispatch",
    )(dest.reshape(nc, 1, DMA_CHUNK), h2, jnp.zeros((n_dst, d), h2.dtype))


def _moe_kernel(te_ref, nv_ref, x_ref, w1g_ref, w1u_ref, b1g_ref, b1u_ref, w2_ref, b2_ref, o_ref):
    @pl.when(pl.program_id(0) < nv_ref[0])
    def _():
        x = x_ref[...].astype(BF16)
        gate = jnp.dot(x, w1g_ref[0], preferred_element_type=F32) + b1g_ref[0]
        up = jnp.dot(x, w1u_ref[0], preferred_element_type=F32) + b1u_ref[0]
        gate = jnp.minimum(gate, SWIGLU_LIMIT)
        up = jnp.clip(up, -SWIGLU_LIMIT, SWIGLU_LIMIT)
        act = (up + 1.0) * (gate * _sigmoid(SWIGLU_ALPHA * gate))
        o_ref[...] = jnp.dot(act.astype(BF16), w2_ref[0], preferred_element_type=F32) + b2_ref[0]

    @pl.when(pl.program_id(0) >= nv_ref[0])
    def _():
        o_ref[...] = jnp.zeros_like(o_ref)


def moe_experts(xs, tile_expert, n_valid, w1g, w1u, b1g, b1u, w2, b2):
    npad, d = xs.shape
    tm = MOE_TILE
    ff = w1g.shape[2]
    n_tiles = npad // tm
    rows = lambda i, te, nv: (jnp.minimum(i, nv[0] - 1), 0)
    wsel = lambda i, te, nv: (te[i], 0, 0)
    return pl.pallas_call(
        _moe_kernel,
        out_shape=jax.ShapeDtypeStruct(xs.shape, F32),
        grid_spec=pltpu.PrefetchScalarGridSpec(
            num_scalar_prefetch=2,
            grid=(n_tiles,),
            in_specs=[pl.BlockSpec((tm, d), rows),
                      pl.BlockSpec((1, d, ff), wsel), pl.BlockSpec((1, d, ff), wsel),
                      pl.BlockSpec((1, 1, ff), wsel), pl.BlockSpec((1, 1, ff), wsel),
                      pl.BlockSpec((1, ff, d), wsel), pl.BlockSpec((1, 1, d), wsel)],
            out_specs=pl.BlockSpec((tm, d), lambda i, te, nv: (i, 0))),
        compiler_params=_cp(("arbitrary",)),
        name="moe_experts",
    )(tile_expert, n_valid, xs, w1g, w1u, b1g, b1u, w2, b2)


def _combine_kernel(idx_ref, nxt_ref, x_ref, rows_ref, tw_ref, g2_ref, y_ref, buf_ref, sems):
    i = pl.program_id(0)
    slot = i % 2
    tm = x_ref.shape[0]

    def gather(ids_ref, s):
        def copy_of(slot, tok0, u):
            j = ids_ref[0, 0, (tok0 + u) * TOP_K + slot]
            row0 = pl.multiple_of(slot * tm + tok0, DMA_UNROLL)
            return pltpu.make_async_copy(rows_ref.at[pl.ds(j, 1)], buf_ref.at[s, pl.ds(row0 + u, 1)], sems.at[s])
        _issue_rows(copy_of)

    @pl.when(i == 0)
    def _():
        gather(idx_ref, 0)

    @pl.when(i + 1 < pl.num_programs(0))
    def _():
        gather(nxt_ref, 1 - slot)

    _chunk_wait(rows_ref, sems.at[slot])
    tw = tw_ref[...]
    acc = tw[:, 0:1] * buf_ref[slot, 0:tm, :]
    for s in range(1, TOP_K):
        acc = acc + tw[:, s:s + 1] * buf_ref[slot, s * tm:(s + 1) * tm, :]
    y_ref[...] = x_ref[...] + g2_ref[0] * acc


def moe_combine(x, dest, rows, top_w, mods, *, t_ctx, l_lat):
    t, d = x.shape
    tm = DMA_CHUNK // TOP_K
    nt = t // tm
    ids = dest.reshape(nt, 1, DMA_CHUNK)
    smem = lambda fn: pl.BlockSpec((1, 1, DMA_CHUNK), fn, memory_space=pltpu.SMEM)
    return pl.pallas_call(
        _combine_kernel,
        out_shape=jax.ShapeDtypeStruct((t, d), F32),
        grid=(nt,),
        in_specs=[smem(lambda i: (i, 0, 0)), smem(lambda i: (jnp.minimum(i + 1, nt - 1), 0, 0)),
                  pl.BlockSpec((tm, d), lambda i: (i, 0)), pl.BlockSpec(memory_space=pl.ANY),
                  pl.BlockSpec((tm, LANES), lambda i: (i, 0)), _mod_spec(5, tm, t_ctx, l_lat)],
        out_specs=pl.BlockSpec((tm, d), lambda i: (i, 0)),
        scratch_shapes=[pltpu.VMEM((2, DMA_CHUNK, d), F32), pltpu.SemaphoreType.DMA((2,))],
        compiler_params=_cp(("arbitrary",)),
        name="moe_combine",
    )(ids, ids, x, rows, top_w, mods)


def _dispatch_plan(top_ir, hist, tm):
    t = top_ir.shape[0]
    n_rt = hist.shape[0]
    hist = hist.astype(jnp.int32)
    counts = jnp.sum(hist, axis=0)
    tiles = (counts + tm - 1) // tm
    tile_end = jnp.cumsum(tiles)
    base = (tile_end - tiles)[None, :] * tm + jnp.cumsum(hist, axis=0) - hist
    e = top_ir[:, :TOP_K].reshape(n_rt, -1)
    rank = top_ir[:, TOP_K:].reshape(n_rt, -1)
    pick = e[:, :, None] == jnp.arange(N_EXPERTS, dtype=jnp.int32)[None, None, :]
    dest = (jnp.sum(jnp.where(pick, base[:, None, :], 0), axis=2) + rank).reshape(-1).astype(jnp.int32)
    n_tiles = (t * TOP_K) // tm + N_EXPERTS
    tile_ids = jnp.arange(n_tiles, dtype=jnp.int32)
    tile_expert = jnp.minimum(jnp.sum((tile_ids[:, None] >= tile_end[None, :]).astype(jnp.int32), axis=1),
                              N_EXPERTS - 1).astype(jnp.int32)
    return dest, n_tiles, tile_expert, tile_end[-1:].astype(jnp.int32)


def kernel(x_prompt, x_sample, c, cache_k, cache_v, state_ssd, c_ctx, norm1_w, norm2_w, w_mod, b_mod, w_in, w_gate,
           b_gate, hy_conv_w, hy_conv_b, hy_w1, hy_b1, hy_w2, hy_b2, hy_w3, hy_freq, hy_decay, hy_bias, ssd_conv_w,
           ssd_conv_b, ssd_a_log, ssd_dt_bias, ssd_d, ssd_norm_w, q_norm_w, k_norm_w, w_br_hy, w_br_ssd, w_br_att,
           w_out, w_router, b_router, w_e1, b_e1, w_e2, b_e2):
    n_ctx, l_ctx, d = x_prompt.shape
    n_lat, l_lat, _ = x_sample.shape
    depth = w_in.shape[0]
    t_ctx, t_lat = n_ctx * l_ctx, n_lat * l_lat
    t = t_ctx + t_lat
    kc = N_KV_HEADS * HEAD_DIM
    geo = dict(t_ctx=t_ctx, l_lat=l_lat)

    x = jnp.concatenate([x_prompt.reshape(t_ctx, d), x_sample.reshape(t_lat, d)], axis=0)
    cvec = jnp.zeros((8, d), F32).at[0].set(c_ctx).at[1:1 + n_lat].set(c)
    mods_all = modulation_all(cvec, w_mod, b_mod)
    n_exp, _, ff2 = w_e1.shape[1:]
    w1g_all, w1u_all = deinterleave_experts(w_e1.reshape(depth * n_exp, d, ff2))
    w1g_all = w1g_all.reshape(depth, n_exp, d, ff2 // 2)
    w1u_all = w1u_all.reshape(depth, n_exp, d, ff2 // 2)

    new_k, new_v, new_s = [], [], []
    for l in range(depth):
        mods = mods_all[l].reshape(8 * 6, 1, d)
        wi = w_in[l]
        w_proj = jnp.concatenate([wi[:, 0:3072], wi[:, 3088:3856], wi[:, 3072:3088],
                                  jnp.zeros((d, P_COLS - 3856), F32)], axis=1).astype(BF16)
        nw1 = norm1_w[l].reshape(1, d)
        proj = norm_mod_matmul(x, nw1, mods, w_proj, jnp.zeros((1, P_COLS), F32), sigmoid=False, out_dtype=F32,
                               **geo)
        gate = norm_mod_matmul(x, nw1, mods, w_gate[l].astype(BF16), b_gate[l].reshape(1, -1), sigmoid=True,
                               out_dtype=BF16, **geo)

        x0, p = hyena_pre(proj, hy_conv_w[l], hy_conv_b[l], t_ctx=t_ctx, l_ctx=l_ctx, l_lat=l_lat)
        hy_args = (hy_w1[l], hy_b1[l], hy_w2[l], hy_b2[l], hy_w3[l], hy_freq[l], hy_decay[l])
        y_hy_ctx = hyena_ctx(x0, p, hyena_filter(l_ctx, *hy_args), hy_bias[l], n_seq=n_ctx, seq_len=l_ctx)
        y_hy_lat = hyena_lat(x0[t_ctx:], p[t_ctx:], hyena_filter(l_lat, *hy_args), hy_bias[l], n_seq=n_lat,
                             seq_len=l_lat)

        xbc = ssd_conv(proj, ssd_conv_w[l], ssd_conv_b[l], t_ctx=t_ctx, l_ctx=l_ctx, l_lat=l_lat)
        ssd_args = (ssd_dt_bias[l], ssd_a_log[l], ssd_d[l])
        yf_c, yb_c, fin_c = ssd_scan(xbc, proj, None, *ssd_args, row0=0, n_seq=n_ctx, seq_len=l_ctx)
        yf_l, yb_l, _ = ssd_scan(xbc, proj, state_ssd[:, l], *ssd_args, row0=t_ctx, n_seq=n_lat, seq_len=l_lat)

        qn, kn = qk_prep(proj, q_norm_w[l], k_norm_w[l], **geo)
        v_all = proj[:, P_V:P_V + kc]
        k_ctx = kn[:t_ctx].reshape(n_ctx, l_ctx, kc)
        v_ctx = v_all[:t_ctx].reshape(n_ctx, l_ctx, kc)
        att_ctx = attention(qn, k_ctx.astype(BF16), v_ctx.astype(BF16), row0=0, seq_len=l_ctx, tq=l_ctx)
        k_lat = jnp.concatenate([kn[t_ctx:].reshape(n_lat, l_lat, kc), cache_k[:, l].reshape(n_lat, -1, kc)], axis=1)
        v_lat = jnp.concatenate([v_all[t_ctx:].reshape(n_lat, l_lat, kc), cache_v[:, l].reshape(n_lat, -1, kc)],
                                axis=1)
        att_lat = attention(qn, k_lat.astype(BF16), v_lat.astype(BF16), row0=t_ctx, seq_len=l_lat, tq=128)

        wr = jnp.pad(w_router[l], ((0, 0), (0, LANES - N_EXPERTS)))
        br = jnp.pad(b_router[l], (0, LANES - N_EXPERTS)).reshape(1, LANES)
        x, h2, top_ir, top_w, hist = merge_router(
            x, (y_hy_ctx, y_hy_lat), (yf_c, yf_l), (yb_c, yb_l), proj, (att_ctx, att_lat), gate, ssd_norm_w[l],
            w_br_hy[l].astype(BF16), w_br_ssd[l].astype(BF16),
            w_br_att[l].astype(BF16), w_out[l].astype(BF16), mods, norm2_w[l].reshape(1, d), wr, br, **geo)

        dest, n_tiles, tile_expert, n_valid = _dispatch_plan(top_ir[:, :2 * TOP_K], hist[::8, :N_EXPERTS], MOE_TILE)
        xs = moe_dispatch(dest, h2, n_tiles * MOE_TILE)
        out_sorted = moe_experts(
            xs, tile_expert, n_valid, w1g_all[l], w1u_all[l],
            b_e1[l][:, None, 0::2], b_e1[l][:, None, 1::2], w_e2[l].astype(BF16), b_e2[l][:, None, :])
        x = moe_combine(x, dest, out_sorted, top_w, mods, **geo)

        new_k.append(k_ctx.reshape(n_ctx, l_ctx, N_KV_HEADS, HEAD_DIM))
        new_v.append(v_ctx.reshape(n_ctx, l_ctx, N_KV_HEADS, HEAD_DIM))
        new_s.append(fin_c)

    y_prompt = x[:t_ctx].reshape(n_ctx, l_ctx, d)
    y_sample = x[t_ctx:].reshape(n_lat, l_lat, d)
    return (y_prompt, y_sample, jnp.stack(new_k, axis=1), jnp.stack(new_v, axis=1), jnp.stack(new_s, axis=1))
```

```python
import functools
import math

import numpy as np
import jax
import jax.numpy as jnp
from jax import lax
from jax.experimental import pallas as pl
from jax.experimental.pallas import tpu as pltpu

F32 = jnp.float32
BF16 = jnp.bfloat16
HI = lax.Precision.HIGHEST

EPS = 1e-6
GRID_W = 64
HY_DIM = 512
SSD_INNER = 512
SSD_HEADDIM = 64
SSD_HEADS = 8
SSD_GROUPS = 2
SSD_STATE = 128
SSD_CHUNK = 128
N_HEADS = 8
N_KV_HEADS = 2
HEAD_DIM = 64
ROPE_THETA = 10000.0
N_EXPERTS = 32
TOP_K = 4
TOP_K_SHIFT = 2
SWIGLU_ALPHA = 1.702
SWIGLU_LIMIT = 7.0

P_HY, P_Z, P_XBC, P_Q, P_K, P_V, P_DT, P_COLS = 0, 1536, 2048, 3072, 3584, 3712, 3840, 3968

VMEM_LIMIT = 56 * 1024 * 1024
LANES = 128
ROW_TILE = 256
MOE_TILE = 512
DFT_N1, DFT_N2 = 64, 128


def _cp(sem, vmem=VMEM_LIMIT):
    return pltpu.CompilerParams(dimension_semantics=sem, vmem_limit_bytes=vmem)


def _sigmoid(x):
    return 1.0 / (1.0 + jnp.exp(-x))


def _silu(x):
    return x * _sigmoid(x)


def _softplus(x):
    return jnp.maximum(x, 0.0) + jnp.log(1.0 + jnp.exp(-jnp.abs(x)))


def _dot3(a, b):
    ah = a.astype(BF16)
    al = (a - ah.astype(F32)).astype(BF16)
    bh = b.astype(BF16)
    bl = (b - bh.astype(F32)).astype(BF16)
    dot = lambda u, v: jnp.dot(u, v, preferred_element_type=F32)
    return dot(ah, bh) + (dot(ah, bl) + dot(al, bh))


def _mod_kernel(c_ref, w_ref, b_ref, o_ref):
    s = _silu(c_ref[...])
    o_ref[0] = jnp.dot(s, w_ref[0], precision=HI, preferred_element_type=F32) + b_ref[0]


def modulation_all(cvec, w_mod, b_mod):
    depth, d, n = w_mod.shape
    tn = 1536
    return pl.pallas_call(
        _mod_kernel,
        out_shape=jax.ShapeDtypeStruct((depth, 8, n), F32),
        grid=(depth, n // tn),
        in_specs=[pl.BlockSpec((8, d), lambda l, j: (0, 0)),
                  pl.BlockSpec((1, d, tn), lambda l, j: (l, 0, j)),
                  pl.BlockSpec((1, 1, tn), lambda l, j: (l, 0, j))],
        out_specs=pl.BlockSpec((1, 8, tn), lambda l, j: (l, 0, j)),
        compiler_params=_cp(("arbitrary", "arbitrary")),
        name="modulation",
    )(cvec, w_mod, b_mod.reshape(depth, 1, n))


def _mod_row(i, tm, t_ctx, l_lat):
    n_ctx = t_ctx // tm
    per = l_lat // tm
    return jnp.where(i < n_ctx, 0, 1 + (i - n_ctx) // per)


def _mod_spec(k, tm, t_ctx, l_lat):
    return pl.BlockSpec((1, 1, 1024), lambda i: (_mod_row(i, tm, t_ctx, l_lat) * 6 + k, 0, 0))


def _nmm_kernel(x_ref, nw_ref, sh_ref, sc_ref, w_ref, b_ref, o_ref, *, sigmoid):
    x = x_ref[...]
    ms = jnp.mean(x * x, axis=-1, keepdims=True)
    h = x * lax.rsqrt(ms + EPS) * nw_ref[...]
    h = h * (1.0 + sc_ref[0]) + sh_ref[0]
    acc = jnp.dot(h.astype(BF16), w_ref[...], preferred_element_type=F32) + b_ref[...]
    if sigmoid:
        acc = _sigmoid(acc)
    o_ref[...] = acc.astype(o_ref.dtype)


def norm_mod_matmul(x, nw, mods, w, b, *, t_ctx, l_lat, sigmoid, out_dtype, tm=512):
    t, d = x.shape
    n = w.shape[1]
    return pl.pallas_call(
        functools.partial(_nmm_kernel, sigmoid=sigmoid),
        out_shape=jax.ShapeDtypeStruct((t, n), out_dtype),
        grid=(t // tm,),
        in_specs=[pl.BlockSpec((tm, d), lambda i: (i, 0)),
                  pl.BlockSpec((1, d), lambda i: (0, 0)),
                  _mod_spec(0, tm, t_ctx, l_lat),
                  _mod_spec(1, tm, t_ctx, l_lat),
                  pl.BlockSpec((d, n), lambda i: (0, 0)),
                  pl.BlockSpec((1, n), lambda i: (0, 0))],
        out_specs=pl.BlockSpec((tm, n), lambda i: (i, 0)),
        compiler_params=_cp(("arbitrary",)),
        name="norm_mod_matmul",
    )(x, nw, mods, mods, w, b)


def _seq_edges(i, tr, t_ctx, l_ctx, l_lat):
    tok = i * tr
    pos = jnp.where(tok < t_ctx, tok % l_ctx, (tok - t_ctx) % l_lat)
    length = jnp.where(tok < t_ctx, l_ctx, l_lat)
    return pos == 0, pos + tr == length


def _conv3(x, prev8, next8, w_ref, b_ref, first, last):
    tr = x.shape[0]
    row = lax.broadcasted_iota(jnp.int32, x.shape, 0)
    pm = jnp.where(first, 0.0, 1.0)
    nm = jnp.where(last, 0.0, 1.0)
    xm1 = jnp.where(row == 0, prev8[7:8, :] * pm, pltpu.roll(x, 1, axis=0))
    xp1 = jnp.where(row == tr - 1, next8[0:1, :] * nm, pltpu.roll(x, tr - 1, axis=0))
    return b_ref[...] + xm1 * w_ref[0:1, :] + x * w_ref[1:2, :] + xp1 * w_ref[2:3, :]


def _conv_specs(tr, tc, col_blk, n_rows):
    r8 = tr // 8
    last8 = n_rows // 8 - 1
    return [pl.BlockSpec((tr, tc), lambda i, j: (i, col_blk(j))),
            pl.BlockSpec((8, tc), lambda i, j: (jnp.maximum(i * r8 - 1, 0), col_blk(j))),
            pl.BlockSpec((8, tc), lambda i, j: (jnp.minimum((i + 1) * r8, last8), col_blk(j)))]


def _hy_pre_kernel(*refs, tr, t_ctx, l_ctx, l_lat):
    (x0, x0p, x0n, x1, x1p, x1n, xv, xvp, xvn, w0, w1, wv, b0, b1, bv, o0_ref, op_ref) = refs
    first, last = _seq_edges(pl.program_id(0), tr, t_ctx, l_ctx, l_lat)
    o0_ref[...] = _conv3(x0[...], x0p[...], x0n[...], w0, b0, first, last)
    u1 = _conv3(x1[...], x1p[...], x1n[...], w1, b1, first, last)
    uv = _conv3(xv[...], xvp[...], xvn[...], wv, bv, first, last)
    op_ref[...] = u1 * uv


def hyena_pre(proj, w, b, *, t_ctx, l_ctx, l_lat):
    t = proj.shape[0]
    tr, tc = ROW_TILE, HY_DIM
    nb = HY_DIM // tc
    b2 = b.reshape(1, 3 * HY_DIM)
    specs = []
    for s in range(3):
        specs += _conv_specs(tr, tc, lambda j, s=s: P_HY // tc + s * nb + j, t)
    specs += [pl.BlockSpec((3, tc), lambda i, j, s=s: (0, s * nb + j)) for s in range(3)]
    specs += [pl.BlockSpec((1, tc), lambda i, j, s=s: (0, s * nb + j)) for s in range(3)]
    return pl.pallas_call(
        functools.partial(_hy_pre_kernel, tr=tr, t_ctx=t_ctx, l_ctx=l_ctx, l_lat=l_lat),
        out_shape=(jax.ShapeDtypeStruct((t, HY_DIM), F32), jax.ShapeDtypeStruct((t, HY_DIM), F32)),
        grid=(t // tr, nb),
        in_specs=specs,
        out_specs=(pl.BlockSpec((tr, tc), lambda i, j: (i, j)), pl.BlockSpec((tr, tc), lambda i, j: (i, j))),
        compiler_params=_cp(("arbitrary", "arbitrary")),
        name="hyena_pre",
    )(*([proj] * 9), w, w, w, b2, b2, b2)


def _filter_kernel(z_ref, w1_ref, b1_ref, w2_ref, b2_ref, w3_ref, fr_ref, dec_ref, o_ref):
    z = z_ref[...]
    fr = fr_ref[...]
    h = jnp.sin(fr * (jnp.dot(z, w1_ref[...], precision=HI, preferred_element_type=F32) + b1_ref[...]))
    h = jnp.sin(fr * (jnp.dot(h, w2_ref[...], precision=HI, preferred_element_type=F32) + b2_ref[...]))
    f = jnp.dot(h, w3_ref[...], precision=HI, preferred_element_type=F32)
    o_ref[...] = f * jnp.exp(-z[:, 0:1] * jnp.abs(dec_ref[...]))


def _filter_embedding(seq_len, emb):
    bands_n = (emb - 1) // 2
    t = jnp.linspace(0.0, 1.0, seq_len, dtype=F32)[:, None]
    bands = jnp.linspace(1e-4, bands_n - 1, bands_n, dtype=F32)[None, :]
    ang = (2.0 * math.pi / seq_len) * jnp.arange(seq_len, dtype=F32)[:, None] * bands
    z = jnp.concatenate([t, jnp.cos(ang), -jnp.sin(ang)], axis=-1)
    return jnp.pad(z, ((0, 0), (0, LANES - emb)))


def hyena_filter(seq_len, w1, b1, w2, b2, w3, freq, decay):
    emb, ff = w1.shape
    n = w3.shape[1]
    z = _filter_embedding(seq_len, emb)
    padc = LANES - ff
    w1p = jnp.pad(w1, ((0, LANES - emb), (0, padc)))
    w2p = jnp.pad(w2, ((0, padc), (0, padc)))
    w3p = jnp.pad(w3, ((0, padc), (0, 0)))
    row = lambda v: jnp.pad(v, (0, padc)).reshape(1, LANES)
    tr = 256
    full = lambda shape: pl.BlockSpec(shape, lambda i: (0, 0))
    return pl.pallas_call(
        _filter_kernel,
        out_shape=jax.ShapeDtypeStruct((seq_len, n), F32),
        grid=(seq_len // tr,),
        in_specs=[pl.BlockSpec((tr, LANES), lambda i: (i, 0)), full((LANES, LANES)), full((1, LANES)),
                  full((LANES, LANES)), full((1, LANES)), full((LANES, n)), full((1, LANES)), full((1, n))],
        out_specs=pl.BlockSpec((tr, n), lambda i: (i, 0)),
        compiler_params=_cp(("arbitrary",)),
        name="hyena_filter",
    )(z, w1p, row(b1), w2p, row(b2), w3p, row(freq), decay.reshape(1, n))


def _circular_filter(filt):
    ch = filt.shape[1] // 2
    h_fwd, h_bwd = filt[:, :ch], filt[:, ch:]
    return jnp.concatenate([h_fwd, jnp.zeros((1, ch), F32), h_bwd[1:][::-1]], axis=0)


def _cs(n_rows, n_cols, period):
    ang = 2.0 * np.pi * (np.outer(np.arange(n_rows), np.arange(n_cols)) % period) / period
    return np.cos(ang), np.sin(ang)


def _mm_kernel(a_ref, b_ref, o_ref):
    o_ref[0] = _dot3(a_ref[...], b_ref[0])


def const_matmul(a, b, tn):
    m, k = a.shape
    bsz, _, n = b.shape
    return pl.pallas_call(
        _mm_kernel,
        out_shape=jax.ShapeDtypeStruct((bsz, m, n), F32),
        grid=(bsz, n // tn),
        in_specs=[pl.BlockSpec((m, k), lambda s, j: (0, 0)), pl.BlockSpec((1, k, tn), lambda s, j: (s, 0, j))],
        out_specs=pl.BlockSpec((1, m, tn), lambda s, j: (s, 0, j)),
        compiler_params=_cp(("arbitrary", "arbitrary")),
        name="const_matmul",
    )(a, b)


def _hy_ctx_kernel(x0_ref, p_ref, kr_ref, ki_ref, fw_ref, iv_ref, bias_ref, o_ref, *, n):
    p = p_ref[...]
    xf = _dot3(fw_ref[...], p)
    xr, xi = xf[:n], xf[n:]
    kr, ki = kr_ref[...], ki_ref[...]
    yr = xr * kr - xi * ki
    yi = xr * ki + xi * kr
    y = _dot3(iv_ref[...], jnp.concatenate([yr, yi], axis=0))
    o_ref[...] = x0_ref[...] * (y + p * bias_ref[...])


def hyena_ctx(x0, p, filt, bias, *, n_seq, seq_len):
    n = 2 * seq_len
    ch = p.shape[1]
    c_full, s_full = _cs(n, n, n)
    fw_full = jnp.asarray(np.concatenate([c_full, -s_full], axis=0), F32)
    fw_half = fw_full[:, :seq_len]
    iv = jnp.asarray(np.concatenate([c_full[:seq_len], -s_full[:seq_len]], axis=1) / n, F32)
    kf = const_matmul(fw_full, _circular_filter(filt)[None], ch)[0]
    kr, ki = kf[:n], kf[n:]
    full = lambda shape: pl.BlockSpec(shape, lambda s: (0, 0))
    return pl.pallas_call(
        functools.partial(_hy_ctx_kernel, n=n),
        out_shape=jax.ShapeDtypeStruct((n_seq * seq_len, ch), F32),
        grid=(n_seq,),
        in_specs=[pl.BlockSpec((seq_len, ch), lambda s: (s, 0)), pl.BlockSpec((seq_len, ch), lambda s: (s, 0)),
                  full((n, ch)), full((n, ch)), full((2 * n, seq_len)), full((seq_len, 2 * n)), full((1, ch))],
        out_specs=pl.BlockSpec((seq_len, ch), lambda s: (s, 0)),
        compiler_params=_cp(("arbitrary",)),
        name="hyena_ctx",
    )(x0, p, kr, ki, fw_half, iv, bias.reshape(1, ch))


def _stage2_kernel(a_ref, twr_ref, twi_ref, m_ref, *rest, conv):
    if conv:
        kf_ref, mi_ref, o_ref = rest
    else:
        (o_ref,) = rest
    ar, ai = a_ref[0, 0, 0], a_ref[0, 1, 0]
    twr, twi = twr_ref[0], twi_ref[0]
    br = ar * twr - ai * twi
    bi = ar * twi + ai * twr
    x = _dot3(m_ref[...], jnp.concatenate([br, bi], axis=0))
    n2 = ar.shape[0]
    xr, xi = x[:n2], x[n2:]
    if not conv:
        o_ref[0, 0, 0] = xr
        o_ref[0, 1, 0] = xi
        return
    kr, ki = kf_ref[0, 0, 0], kf_ref[0, 1, 0]
    yr = xr * kr - xi * ki
    yi = xr * ki + xi * kr
    pq = _dot3(mi_ref[...], jnp.concatenate([yr, yi], axis=0))
    pr, pi = pq[:n2], pq[n2:]
    o_ref[0, 0, 0] = pr * twr + pi * twi
    o_ref[0, 1, 0] = pi * twr - pr * twi


def _stage2(a, kf, ch):
    bsz = a.shape[0]
    n1, n2 = DFT_N1, DFT_N2
    n = n1 * n2
    tw_ang = 2.0 * np.pi * np.outer(np.arange(n1), np.arange(n2)) / n
    twr = jnp.asarray(np.cos(tw_ang), F32).reshape(n1, n2, 1)
    twi = jnp.asarray(-np.sin(tw_ang), F32).reshape(n1, n2, 1)
    c2, s2 = _cs(n2, n2, n2)
    m_fwd = jnp.asarray(np.block([[c2, s2], [-s2, c2]]), F32)
    m_inv = jnp.asarray(np.block([[c2, -s2], [s2, c2]]), F32)
    conv = kf is not None
    blk = pl.BlockSpec((1, 2, 1, n2, ch), lambda s, k: (s, 0, k, 0, 0))
    specs = [blk, pl.BlockSpec((1, n2, 1), lambda s, k: (k, 0, 0)), pl.BlockSpec((1, n2, 1), lambda s, k: (k, 0, 0)),
             pl.BlockSpec((2 * n2, 2 * n2), lambda s, k: (0, 0))]
    args = [a, twr, twi, m_fwd]
    if conv:
        specs += [pl.BlockSpec((1, 2, 1, n2, ch), lambda s, k: (0, 0, k, 0, 0)),
                  pl.BlockSpec((2 * n2, 2 * n2), lambda s, k: (0, 0))]
        args += [kf, m_inv]
    return pl.pallas_call(
        functools.partial(_stage2_kernel, conv=conv),
        out_shape=jax.ShapeDtypeStruct(a.shape, F32),
        grid=(bsz, n1),
        in_specs=specs,
        out_specs=blk,
        compiler_params=_cp(("arbitrary", "arbitrary")),
        name="hyena_stage2",
    )(*args)


def _hy_post_kernel(g_ref, q_ref, x0_ref, p_ref, bias_ref, o_ref):
    y = _dot3(g_ref[...], q_ref[0])
    o_ref[0] = x0_ref[0] * (y + p_ref[0] * bias_ref[...])


def hyena_lat(x0, p, filt, bias, *, n_seq, seq_len):
    n1, n2 = DFT_N1, DFT_N2
    n = n1 * n2
    assert n == 2 * seq_len
    ch = p.shape[1]
    h1 = n1 // 2
    wide = n2 * ch
    c1, s1 = _cs(n1, n1, n1)
    f1_full = jnp.asarray(np.concatenate([c1, -s1], axis=0), F32)
    f1_half = f1_full[:, :h1]
    g1 = jnp.asarray(np.concatenate([c1[:h1], -s1[:h1]], axis=1) / n, F32)
    tn = 8192
    kcirc = _circular_filter(filt).reshape(1, n1, wide)
    kf = _stage2(const_matmul(f1_full, kcirc, tn).reshape(1, 2, n1, n2, ch), None, ch)
    a = const_matmul(f1_half, p.reshape(n_seq, h1, wide), tn).reshape(n_seq, 2, n1, n2, ch)
    q = _stage2(a, kf, ch).reshape(n_seq, 2 * n1, wide)
    bias_w = jnp.tile(bias, n2).reshape(1, wide)
    out = pl.pallas_call(
        _hy_post_kernel,
        out_shape=jax.ShapeDtypeStruct((n_seq, h1, wide), F32),
        grid=(n_seq, wide // tn),
        in_specs=[pl.BlockSpec((h1, 2 * n1), lambda s, j: (0, 0)),
                  pl.BlockSpec((1, 2 * n1, tn), lambda s, j: (s, 0, j)),
                  pl.BlockSpec((1, h1, tn), lambda s, j: (s, 0, j)),
                  pl.BlockSpec((1, h1, tn), lambda s, j: (s, 0, j)),
                  pl.BlockSpec((1, tn), lambda s, j: (0, j))],
        out_specs=pl.BlockSpec((1, h1, tn), lambda s, j: (s, 0, j)),
        compiler_params=_cp(("arbitrary", "arbitrary")),
        name="hyena_post",
    )(g1, q, x0.reshape(n_seq, h1, wide), p.reshape(n_seq, h1, wide), bias_w)
    return out.reshape(n_seq * seq_len, ch)


def _dot_01(x, m, *, x_is_lhs):
    x1 = x.astype(BF16)
    r1 = x - x1.astype(F32)
    x2 = r1.astype(BF16)
    x3 = (r1 - x2.astype(F32)).astype(BF16)
    mb = m.astype(BF16)
    if x_is_lhs:
        dot = lambda u: jnp.dot(u, mb, preferred_element_type=F32)
    else:
        dot = lambda u: jnp.dot(mb, u, preferred_element_type=F32)
    return dot(x1) + (dot(x2) + dot(x3))


def _ssd_dir(xbc, dtraw, st_ref, d, consts, y_ref):
    tri, expand, dtb, acont, dskip = consts
    cl = xbc.shape[0]
    xs = xbc[:, :SSD_INNER]
    dt = _softplus(dtraw + dtb)
    a = dt * acont
    tri_d = tri if d == 0 else tri.T
    cs = _dot_01(a, tri_d, x_is_lhs=False)
    cs_t = _dot_01(a.T, tri_d.T, x_is_lhs=True)
    dt_x = _dot_01(dt, expand[d], x_is_lhs=True)
    xdt = xs * dt_x
    row = lax.broadcasted_iota(jnp.int32, (cl, cl), 0)
    col = lax.broadcasted_iota(jnp.int32, (cl, cl), 1)
    keep = (col <= row) if d == 0 else (col >= row)
    lane = lax.broadcasted_iota(jnp.int32, (cl, LANES), 1)
    low = lane < SSD_HEADDIM
    edge = cl - 1 if d == 0 else 0
    for g in range(SSD_GROUPS):
        bg = xbc[:, SSD_INNER + g * SSD_STATE: SSD_INNER + (g + 1) * SSD_STATE]
        cg = xbc[:, SSD_INNER + (SSD_GROUPS + g) * SSD_STATE: SSD_INNER + (SSD_GROUPS + g + 1) * SSD_STATE]
        bg16, cg16 = bg.astype(BF16), cg.astype(BF16)
        cb = lax.dot_general(cg16, bg16, (((1,), (1,)), ((), ())), preferred_element_type=F32)
        for pr in range(2):
            pair = g * 2 + pr
            h0 = 2 * pair
            ms = []
            for h in (h0, h0 + 1):
                ln = d * SSD_HEADS + h
                diff = cs[:, ln:ln + 1] - cs_t[ln:ln + 1, :]
                ms.append(jnp.where(keep, cb * jnp.exp(jnp.minimum(diff, 0.0)), 0.0).astype(BF16))
            xp = xdt[:, pair * LANES:(pair + 1) * LANES]
            xs_p = xs[:, pair * LANES:(pair + 1) * LANES]
            xlo = jnp.where(low, xp, 0.0).astype(BF16)
            xhi = jnp.where(low, 0.0, xp).astype(BF16)
            y_diag = (jnp.dot(ms[0], xlo, preferred_element_type=F32)
                      + jnp.dot(ms[1], xhi, preferred_element_type=F32))
            l0, l1 = d * SSD_HEADS + h0, d * SSD_HEADS + h0 + 1
            e_cs = jnp.where(low, jnp.exp(cs[:, l0:l0 + 1]), jnp.exp(cs[:, l1:l1 + 1]))
            st = st_ref[pair]
            y_off = jnp.dot(cg16, st.astype(BF16), preferred_element_type=F32) * e_cs
            y_ref[:, pair * LANES:(pair + 1) * LANES] = (
                y_diag + y_off + xs_p * dskip[d:d + 1, pair * LANES:(pair + 1) * LANES])
            tot0, tot1 = cs[edge:edge + 1, l0:l0 + 1], cs[edge:edge + 1, l1:l1 + 1]
            dec = jnp.where(low, jnp.exp(tot0 - cs[:, l0:l0 + 1]), jnp.exp(tot1 - cs[:, l1:l1 + 1]))
            upd = lax.dot_general(bg16, (xp * dec).astype(BF16), (((0,), (0,)), ((), ())),
                                  preferred_element_type=F32)
            st_ref[pair] = st * jnp.where(low[0:1], jnp.exp(tot0), jnp.exp(tot1)) + upd


def _ssd_kernel(xf_ref, xfp_ref, xfn_ref, dtf_ref, xb_ref, xbp_ref, xbn_ref, dtb_ref, init_ref, cw_ref, cb_ref,
                tri_ref, exp_ref, dtbias_ref, acont_ref, dskip_ref, yf_ref, yb_ref, fin_ref, st_ref, *, has_init):
    s = pl.program_id(1)
    at_start, at_end = s == 0, s == pl.num_programs(1) - 1

    @pl.when(at_start)
    def _():
        if has_init:
            st_ref[...] = init_ref[0]
        else:
            st_ref[...] = jnp.zeros_like(st_ref)

    consts = (tri_ref[...], (exp_ref[0], exp_ref[1]), dtbias_ref[...], acont_ref[...], dskip_ref[...])
    xbc_f = _silu(_conv3(xf_ref[...], xfp_ref[...], xfn_ref[...], cw_ref, cb_ref, at_start, at_end))
    _ssd_dir(xbc_f, dtf_ref[...], st_ref.at[0], 0, consts, yf_ref)
    xbc_b = _silu(_conv3(xb_ref[...], xbp_ref[...], xbn_ref[...], cw_ref, cb_ref, at_end, at_start))
    _ssd_dir(xbc_b, dtb_ref[...], st_ref.at[1], 1, consts, yb_ref)

    @pl.when(s == pl.num_programs(1) - 1)
    def _():
        fin_ref[0] = st_ref[...]


def _pair_states(s):
    b = s.shape[0]
    s = s.reshape(b, 2, SSD_HEADS // 2, 2, SSD_HEADDIM, SSD_STATE)
    return s.transpose(0, 1, 2, 5, 3, 4).reshape(b, 2, SSD_HEADS // 2, SSD_STATE, 2 * SSD_HEADDIM)


def _unpair_states(s):
    b = s.shape[0]
    s = s.reshape(b, 2, SSD_HEADS // 2, SSD_STATE, 2, SSD_HEADDIM)
    return s.transpose(0, 1, 2, 4, 5, 3).reshape(b, 2, SSD_HEADS, SSD_HEADDIM, SSD_STATE)


def ssd_scan(proj, init, conv_w, conv_b, dt_bias, a_log, d_skip, *, row0, n_seq, seq_len):
    cl = SSD_CHUNK
    nc = seq_len // cl
    base = row0 // cl
    has_init = init is not None
    hp = SSD_HEADS // 2
    init_p = _pair_states(init) if has_init else jnp.zeros((1, 2, hp, SSD_STATE, LANES), F32)
    tri = jnp.asarray(np.tril(np.ones((cl, cl))), F32)
    expand = np.zeros((2, LANES, SSD_INNER), np.float32)
    for d in range(2):
        for h in range(SSD_HEADS):
            expand[d, d * SSD_HEADS + h, h * SSD_HEADDIM:(h + 1) * SSD_HEADDIM] = 1.0
    pad16 = lambda v: jnp.pad(v.reshape(1, 2 * SSD_HEADS), ((0, 0), (0, LANES - 2 * SSD_HEADS)))
    acont = pad16(-jnp.exp(a_log))
    dtb = pad16(dt_bias)
    dskip = jnp.repeat(d_skip, SSD_HEADDIM, axis=1)
    cxbc = conv_w.shape[1]
    xblk = P_XBC // cxbc
    dtblk = P_DT // LANES
    r8 = cl // 8
    last8 = proj.shape[0] // 8 - 1
    full2 = lambda shape: pl.BlockSpec(shape, lambda b, s: (0,) * len(shape))
    fwd = lambda b, s: base + b * nc + s
    bwd = lambda b, s: base + b * nc + (nc - 1 - s)

    def stream(chunk):
        return [pl.BlockSpec((cl, cxbc), lambda b, s: (chunk(b, s), xblk)),
                pl.BlockSpec((8, cxbc), lambda b, s: (jnp.maximum(chunk(b, s) * r8 - 1, 0), xblk)),
                pl.BlockSpec((8, cxbc), lambda b, s: (jnp.minimum((chunk(b, s) + 1) * r8, last8), xblk)),
                pl.BlockSpec((cl, LANES), lambda b, s: (chunk(b, s), dtblk))]

    st_spec = pl.BlockSpec((1, 2, hp, SSD_STATE, LANES), lambda b, s: (b if has_init else 0, 0, 0, 0, 0))
    yf, yb, fin = pl.pallas_call(
        functools.partial(_ssd_kernel, has_init=has_init),
        out_shape=(jax.ShapeDtypeStruct((n_seq * seq_len, SSD_INNER), F32),
                   jax.ShapeDtypeStruct((n_seq * seq_len, SSD_INNER), F32),
                   jax.ShapeDtypeStruct((n_seq, 2, hp, SSD_STATE, LANES), F32)),
        grid=(n_seq, nc),
        in_specs=stream(fwd) + stream(bwd) + [
            st_spec, full2((3, cxbc)), full2((1, cxbc)), full2((cl, cl)), full2((2, LANES, SSD_INNER)),
            full2((1, LANES)), full2((1, LANES)), full2((2, SSD_INNER))],
        out_specs=(pl.BlockSpec((cl, SSD_INNER), lambda b, s: (b * nc + s, 0)),
                   pl.BlockSpec((cl, SSD_INNER), lambda b, s: (b * nc + (nc - 1 - s), 0)),
                   pl.BlockSpec((1, 2, hp, SSD_STATE, LANES), lambda b, s: (b, 0, 0, 0, 0))),
        scratch_shapes=[pltpu.VMEM((2, hp, SSD_STATE, LANES), F32)],
        compiler_params=_cp(("arbitrary", "arbitrary")),
        name="ssd_scan",
    )(*([proj] * 8), init_p, conv_w, conv_b.reshape(1, cxbc), tri, jnp.asarray(expand), dtb, acont, dskip)
    return yf, yb, _unpair_states(fin)


def _headnorm(x, g_ref, w_ref):
    ms = _dot3(x * x, g_ref[...])
    return x * lax.rsqrt(ms + EPS) * w_ref[...]


def _rope(x, cos, sin_signed):
    lane = lax.broadcasted_iota(jnp.int32, x.shape, 1)
    w = x.shape[1]
    swapped = jnp.where(lane % 2 == 0, pltpu.roll(x, w - 1, axis=1), pltpu.roll(x, 1, axis=1))
    return x * cos + swapped * sin_signed


Q_SCALE = HEAD_DIM ** -0.5 * math.log2(math.e)


def _store_padded_heads(qo_ref, q):
    rep = N_HEADS // N_KV_HEADS
    lane = lax.broadcasted_iota(jnp.int32, (q.shape[0], LANES), 1)
    for h in range(N_HEADS):
        g = h // rep
        chunk = q[:, (h // 2) * LANES:(h // 2 + 1) * LANES]
        if h % 2 != g:
            chunk = pltpu.roll(chunk, HEAD_DIM, axis=1)
        keep = (lane >= g * HEAD_DIM) & (lane < (g + 1) * HEAD_DIM)
        qo_ref[:, h * LANES:(h + 1) * LANES] = jnp.where(keep, chunk, 0.0).astype(qo_ref.dtype)


def _qk_kernel(q_ref, k_ref, cos_ref, sin_ref, gq_ref, gk_ref, qw_ref, kw_ref, qo_ref, ko_ref, *, n_ctx_tiles):
    q = _headnorm(q_ref[...], gq_ref, qw_ref)
    k = _headnorm(k_ref[...], gk_ref, kw_ref)
    is_lat = pl.program_id(0) >= n_ctx_tiles

    @pl.when(is_lat)
    def _():
        cos, sin = cos_ref[...], sin_ref[...]
        _store_padded_heads(qo_ref, _rope(q, cos, sin) * Q_SCALE)
        ko_ref[...] = _rope(k, cos[:, :k.shape[1]], sin[:, :k.shape[1]])

    @pl.when(jnp.logical_not(is_lat))
    def _():
        _store_padded_heads(qo_ref, q * Q_SCALE)
        ko_ref[...] = k


def _rope_tables(seq_len):
    n_rows = seq_len // GRID_W
    row = jnp.repeat(jnp.arange(n_rows), GRID_W).astype(F32)
    col = jnp.tile(jnp.arange(GRID_W), n_rows).astype(F32)
    n_freq = HEAD_DIM // 4
    inv = ROPE_THETA ** (-jnp.arange(n_freq, dtype=F32) / n_freq)
    ang = jnp.concatenate([row[:, None] * inv, col[:, None] * inv], axis=-1)
    cos = jnp.repeat(jnp.cos(ang), 2, axis=1)
    sin = jnp.repeat(jnp.sin(ang), 2, axis=1) * jnp.tile(jnp.asarray([-1.0, 1.0], F32), HEAD_DIM // 2)
    return jnp.tile(cos, (1, N_HEADS)), jnp.tile(sin, (1, N_HEADS))


def qk_prep(proj, q_norm_w, k_norm_w, *, t_ctx, l_lat):
    t = proj.shape[0]
    tr = ROW_TILE
    qc, kc = N_HEADS * HEAD_DIM, N_KV_HEADS * HEAD_DIM
    cos, sin = _rope_tables(l_lat)
    group = lambda c: jnp.asarray(np.kron(np.eye(c // HEAD_DIM), np.ones((HEAD_DIM, HEAD_DIM))) / HEAD_DIM, F32)
    n_ctx_tiles = t_ctx // tr
    per = l_lat // tr
    tab = pl.BlockSpec((tr, qc), lambda i: (jnp.maximum(i - n_ctx_tiles, 0) % per, 0))
    full = lambda shape: pl.BlockSpec(shape, lambda i: (0, 0))
    return pl.pallas_call(
        functools.partial(_qk_kernel, n_ctx_tiles=n_ctx_tiles),
        out_shape=(jax.ShapeDtypeStruct((t, N_HEADS * LANES), BF16), jax.ShapeDtypeStruct((t, kc), F32)),
        grid=(t // tr,),
        in_specs=[pl.BlockSpec((tr, qc), lambda i: (i, P_Q // qc)), pl.BlockSpec((tr, kc), lambda i: (i, P_K // kc)),
                  tab, tab, full((qc, qc)), full((kc, kc)), full((1, qc)), full((1, kc))],
        out_specs=(pl.BlockSpec((tr, N_HEADS * LANES), lambda i: (i, 0)), pl.BlockSpec((tr, kc), lambda i: (i, 0))),
        compiler_params=_cp(("arbitrary",)),
        name="qk_prep",
    )(proj, proj, cos, sin, group(qc), group(kc), jnp.tile(q_norm_w, N_HEADS).reshape(1, qc),
      jnp.tile(k_norm_w, N_KV_HEADS).reshape(1, kc))


def _attention_kernel(q_ref, k_ref, vt_ref, o_ref, *, tq):
    rep = N_HEADS // N_KV_HEADS
    k = k_ref[0]
    vt = vt_ref[0]
    for g in range(N_KV_HEADS):
        qs = jnp.concatenate([q_ref[:, (g * rep + r) * LANES:(g * rep + r + 1) * LANES] for r in range(rep)],
                             axis=0)
        s = lax.dot_general(k, qs, (((1,), (1,)), ((), ())), preferred_element_type=F32)
        p = jnp.exp2(s - jnp.max(s, axis=0, keepdims=True))
        l = jnp.sum(p, axis=0, keepdims=True)
        out = (jnp.dot(vt, p.astype(BF16), preferred_element_type=F32) / l).T
        for r in range(rep):
            h = g * rep + r
            o_ref[:, h * HEAD_DIM:(h + 1) * HEAD_DIM] = (
                out[r * tq:(r + 1) * tq, g * HEAD_DIM:(g + 1) * HEAD_DIM].astype(o_ref.dtype))


def attention(q, k, vt, *, row0, seq_len, tq):
    qc = q.shape[1]
    b, lk, kc = k.shape
    per = seq_len // tq
    base = row0 // tq
    return pl.pallas_call(
        functools.partial(_attention_kernel, tq=tq),
        out_shape=jax.ShapeDtypeStruct((b * seq_len, N_HEADS * HEAD_DIM), BF16),
        grid=(b, per),
        in_specs=[pl.BlockSpec((tq, qc), lambda s, i: (base + s * per + i, 0)),
                  pl.BlockSpec((1, lk, kc), lambda s, i: (s, 0, 0)),
                  pl.BlockSpec((1, kc, lk), lambda s, i: (s, 0, 0))],
        out_specs=pl.BlockSpec((tq, N_HEADS * HEAD_DIM), lambda s, i: (s * per + i, 0)),
        compiler_params=_cp(("arbitrary", "arbitrary")),
        name="attention",
    )(q, k, vt)


def _merge_kernel(x_ref, yhy_c, yhy_l, yf_c, yf_l, yb_c, yb_l, z_ref, yatt_c, yatt_l, gate_ref, snw_ref, whb_ref,
                  wsb_ref, wab_ref, wout_ref, g1_ref, n2w_ref, sh2_ref, sc2_ref, wr_ref, br_ref, ltri_ref,
                  xo_ref, h2_ref, ti_ref, tw_ref, hist_ref, *, n_ctx_tiles):
    d = x_ref.shape[1]
    is_ctx = pl.program_id(0) < n_ctx_tiles
    pick = lambda c_ref, l_ref: jnp.where(is_ctx, c_ref[...], l_ref[...])
    ys = (pick(yf_c, yf_l) + pick(yb_c, yb_l)) * _silu(z_ref[...])
    ys = ys * lax.rsqrt(jnp.mean(ys * ys, axis=-1, keepdims=True) + EPS) * snw_ref[...]
    gate = gate_ref[...].astype(F32)
    merged = (gate[:, :d] * jnp.dot(pick(yhy_c, yhy_l).astype(BF16), whb_ref[...], preferred_element_type=F32)
              + gate[:, d:2 * d] * jnp.dot(ys.astype(BF16), wsb_ref[...], preferred_element_type=F32)
              + gate[:, 2 * d:] * jnp.dot(pick(yatt_c, yatt_l), wab_ref[...], preferred_element_type=F32))
    mix = jnp.dot(merged.astype(BF16), wout_ref[...], preferred_element_type=F32)
    x = x_ref[...] + g1_ref[0] * mix
    xo_ref[...] = x
    h = x * lax.rsqrt(jnp.mean(x * x, axis=-1, keepdims=True) + EPS) * n2w_ref[...]
    h = h * (1.0 + sc2_ref[0]) + sh2_ref[0]
    h2_ref[...] = h
    logits = _dot3(h, wr_ref[...]) + br_ref[...]
    lane = lax.broadcasted_iota(jnp.int32, logits.shape, 1)
    work = jnp.where(lane < N_EXPERTS, logits, -jnp.inf)
    idx_out = jnp.zeros(logits.shape, jnp.int32)
    val_out = jnp.full(logits.shape, -jnp.inf, F32)
    picks = []
    for j in range(TOP_K):
        mx = jnp.max(work, axis=-1, keepdims=True)
        am = jnp.min(jnp.where(work == mx, lane, LANES), axis=-1, keepdims=True)
        idx_out = jnp.where(lane == j, am, idx_out)
        val_out = jnp.where(lane == j, mx, val_out)
        picks.append(lane == am)
        work = jnp.where(picks[-1], -jnp.inf, work)
    e = jnp.exp(val_out - jnp.max(val_out, axis=-1, keepdims=True))
    tw_ref[...] = e / jnp.sum(e, axis=-1, keepdims=True)
    chosen = jnp.where(picks[0] | picks[1] | picks[2] | picks[3], 1.0, 0.0)
    before = jnp.dot(ltri_ref[...], chosen.astype(BF16), preferred_element_type=F32)
    rank_out = jnp.zeros(logits.shape, F32)
    for j in range(TOP_K):
        rank_j = jnp.sum(jnp.where(picks[j], before, 0.0), axis=-1, keepdims=True)
        rank_out = jnp.where(lane == TOP_K + j, rank_j, rank_out)
    ti_ref[...] = idx_out + rank_out.astype(jnp.int32)
    hist_ref[...] = jnp.broadcast_to(jnp.sum(chosen, axis=0, keepdims=True), hist_ref.shape)


def merge_router(x, y_hy, yf, yb, proj, y_att, gate, ssd_norm_w, whb, wsb, wab, wout, mods, n2w, wr, br,
                 *, t_ctx, l_lat, tm=512):
    t, d = x.shape
    nct = t_ctx // tm
    nlt = (t - t_ctx) // tm
    row = lambda tc, c0=0: pl.BlockSpec((tm, tc), lambda i: (i, c0))
    full = lambda shape: pl.BlockSpec(shape, lambda i: (0, 0))
    pair = lambda tc: [pl.BlockSpec((tm, tc), lambda i: (jnp.minimum(i, nct - 1), 0)),
                       pl.BlockSpec((tm, tc), lambda i: (jnp.clip(i - nct, 0, nlt - 1), 0))]
    ltri = jnp.asarray(np.tril(np.ones((tm, tm)), -1), BF16)
    return pl.pallas_call(
        functools.partial(_merge_kernel, n_ctx_tiles=nct),
        out_shape=(jax.ShapeDtypeStruct((t, d), F32), jax.ShapeDtypeStruct((t, d), F32),
                   jax.ShapeDtypeStruct((t, LANES), jnp.int32), jax.ShapeDtypeStruct((t, LANES), F32),
                   jax.ShapeDtypeStruct((t // tm * 8, LANES), F32)),
        grid=(t // tm,),
        in_specs=[row(d)] + pair(HY_DIM) + pair(SSD_INNER) + pair(SSD_INNER) + [row(SSD_INNER, P_Z // SSD_INNER)]
                 + pair(N_HEADS * HEAD_DIM) + [row(3 * d), full((1, SSD_INNER)),
                  full(whb.shape), full(wsb.shape), full(wab.shape), full(wout.shape),
                  _mod_spec(2, tm, t_ctx, l_lat), full((1, d)), _mod_spec(3, tm, t_ctx, l_lat),
                  _mod_spec(4, tm, t_ctx, l_lat), full((d, LANES)), full((1, LANES)), full((tm, tm))],
        out_specs=(row(d), row(d), row(LANES), row(LANES), pl.BlockSpec((8, LANES), lambda i: (i, 0))),
        compiler_params=_cp(("arbitrary",)),
        name="merge_router",
    )(x, *y_hy, *yf, *yb, proj, *y_att, gate, ssd_norm_w.reshape(1, -1), whb, wsb, wab, wout, mods, n2w, mods, mods,
      wr, br, ltri)


DEINT_BLOCK = 256


def _deinterleave_kernel(w_ref, perm_ref, g_ref, u_ref):
    half = DEINT_BLOCK // 2
    for blk in range(w_ref.shape[2] // DEINT_BLOCK):
        wb = w_ref[0, :, blk * DEINT_BLOCK:(blk + 1) * DEINT_BLOCK].astype(BF16)
        r = jnp.dot(wb, perm_ref[...], preferred_element_type=F32)
        g_ref[0, :, blk * half:(blk + 1) * half] = r[:, :half].astype(BF16)
        u_ref[0, :, blk * half:(blk + 1) * half] = r[:, half:].astype(BF16)


def deinterleave_experts(w1):
    e, d, f2 = w1.shape
    perm = np.zeros((DEINT_BLOCK, DEINT_BLOCK), np.float32)
    half = DEINT_BLOCK // 2
    perm[2 * np.arange(half), np.arange(half)] = 1.0
    perm[2 * np.arange(half) + 1, half + np.arange(half)] = 1.0
    out = jax.ShapeDtypeStruct((e, d, f2 // 2), BF16)
    return pl.pallas_call(
        _deinterleave_kernel,
        out_shape=(out, out),
        grid=(e,),
        in_specs=[pl.BlockSpec((1, d, f2), lambda i: (i, 0, 0)),
                  pl.BlockSpec((DEINT_BLOCK, DEINT_BLOCK), lambda i: (0, 0))],
        out_specs=(pl.BlockSpec((1, d, f2 // 2), lambda i: (i, 0, 0)),
                   pl.BlockSpec((1, d, f2 // 2), lambda i: (i, 0, 0))),
        compiler_params=_cp(("arbitrary",)),
        name="deinterleave_experts",
    )(w1, jnp.asarray(perm, BF16))


DMA_CHUNK = 512
DMA_UNROLL = 8


def _chunk_wait(ref, sem):
    pltpu.make_async_copy(ref.at[pl.ds(0, DMA_CHUNK)], ref.at[pl.ds(0, DMA_CHUNK)], sem).wait()


def _issue_rows(copy_of):
    def body(it, carry):
        slot = it & (TOP_K - 1)
        tok0 = pl.multiple_of(lax.shift_right_logical(it, TOP_K_SHIFT) * DMA_UNROLL, DMA_UNROLL)
        for u in range(DMA_UNROLL):
            copy_of(slot, tok0, u).start(priority=u % 2)
        return carry
    lax.fori_loop(0, DMA_CHUNK // DMA_UNROLL, body, 0)


def _dispatch_kernel(idx_ref, x_ref, zero_ref, dst_ref, sem):
    del zero_ref

    def copy_of(slot, tok0, u):
        j = idx_ref[0, 0, (tok0 + u) * TOP_K + slot]
        return pltpu.make_async_copy(x_ref.at[pl.ds(tok0 + u, 1)], dst_ref.at[pl.ds(j, 1)], sem)

    _issue_rows(copy_of)
    _chunk_wait(dst_ref, sem)


def moe_dispatch(dest, h2, n_dst):
    n = dest.shape[0]
    nc = n // DMA_CHUNK
    tok = DMA_CHUNK // TOP_K
    d = h2.shape[1]
    return pl.pallas_call(
        _dispatch_kernel,
        out_shape=jax.ShapeDtypeStruct((n_dst, d), h2.dtype),
        grid=(nc,),
        in_specs=[pl.BlockSpec((1, 1, DMA_CHUNK), lambda c: (c, 0, 0), memory_space=pltpu.SMEM),
                  pl.BlockSpec((tok, d), lambda c: (c, 0)),
                  pl.BlockSpec(memory_space=pl.ANY)],
        out_specs=pl.BlockSpec(memory_space=pl.ANY),
        scratch_shapes=[pltpu.SemaphoreType.DMA(())],
        input_output_aliases={2: 0},
        compiler_params=_cp(("arbitrary",)),
        name="moe_dispatch",
    )(dest.reshape(nc, 1, DMA_CHUNK), h2, jnp.zeros((n_dst, d), h2.dtype))


def _moe_kernel(te_ref, nv_ref, x_ref, w1g_ref, w1u_ref, b1g_ref, b1u_ref, w2_ref, b2_ref, o_ref):
    @pl.when(pl.program_id(0) < nv_ref[0])
    def _():
        x = x_ref[...].astype(BF16)
        gate = jnp.dot(x, w1g_ref[0], preferred_element_type=F32) + b1g_ref[0]
        up = jnp.dot(x, w1u_ref[0], preferred_element_type=F32) + b1u_ref[0]
        gate = jnp.minimum(gate, SWIGLU_LIMIT)
        up = jnp.clip(up, -SWIGLU_LIMIT, SWIGLU_LIMIT)
        act = (up + 1.0) * (gate * _sigmoid(SWIGLU_ALPHA * gate))
        o_ref[...] = jnp.dot(act.astype(BF16), w2_ref[0], preferred_element_type=F32) + b2_ref[0]

    @pl.when(pl.program_id(0) >= nv_ref[0])
    def _():
        o_ref[...] = jnp.zeros_like(o_ref)


def moe_experts(xs, tile_expert, n_valid, w1g, w1u, b1g, b1u, w2, b2):
    npad, d = xs.shape
    tm = MOE_TILE
    ff = w1g.shape[2]
    n_tiles = npad // tm
    rows = lambda i, te, nv: (jnp.minimum(i, nv[0] - 1), 0)
    wsel = lambda i, te, nv: (te[i], 0, 0)
    return pl.pallas_call(
        _moe_kernel,
        out_shape=jax.ShapeDtypeStruct(xs.shape, F32),
        grid_spec=pltpu.PrefetchScalarGridSpec(
            num_scalar_prefetch=2,
            grid=(n_tiles,),
            in_specs=[pl.BlockSpec((tm, d), rows),
                      pl.BlockSpec((1, d, ff), wsel), pl.BlockSpec((1, d, ff), wsel),
                      pl.BlockSpec((1, 1, ff), wsel), pl.BlockSpec((1, 1, ff), wsel),
                      pl.BlockSpec((1, ff, d), wsel), pl.BlockSpec((1, 1, d), wsel)],
            out_specs=pl.BlockSpec((tm, d), lambda i, te, nv: (i, 0))),
        compiler_params=_cp(("arbitrary",)),
        name="moe_experts",
    )(tile_expert, n_valid, xs, w1g, w1u, b1g, b1u, w2, b2)


def _combine_kernel(idx_ref, nxt_ref, x_ref, rows_ref, tw_ref, g2_ref, y_ref, buf_ref, sems):
    i = pl.program_id(0)
    slot = i % 2
    tm = x_ref.shape[0]

    def gather(ids_ref, s):
        def copy_of(slot, tok0, u):
            j = ids_ref[0, 0, (tok0 + u) * TOP_K + slot]
            row0 = pl.multiple_of(slot * tm + tok0, DMA_UNROLL)
            return pltpu.make_async_copy(rows_ref.at[pl.ds(j, 1)], buf_ref.at[s, pl.ds(row0 + u, 1)], sems.at[s])
        _issue_rows(copy_of)

    @pl.when(i == 0)
    def _():
        gather(idx_ref, 0)

    @pl.when(i + 1 < pl.num_programs(0))
    def _():
        gather(nxt_ref, 1 - slot)

    _chunk_wait(rows_ref, sems.at[slot])
    tw = tw_ref[...]
    acc = tw[:, 0:1] * buf_ref[slot, 0:tm, :]
    for s in range(1, TOP_K):
        acc = acc + tw[:, s:s + 1] * buf_ref[slot, s * tm:(s + 1) * tm, :]
    y_ref[...] = x_ref[...] + g2_ref[0] * acc


def moe_combine(x, dest, rows, top_w, mods, *, t_ctx, l_lat):
    t, d = x.shape
    tm = DMA_CHUNK // TOP_K
    nt = t // tm
    ids = dest.reshape(nt, 1, DMA_CHUNK)
    smem = lambda fn: pl.BlockSpec((1, 1, DMA_CHUNK), fn, memory_space=pltpu.SMEM)
    return pl.pallas_call(
        _combine_kernel,
        out_shape=jax.ShapeDtypeStruct((t, d), F32),
        grid=(nt,),
        in_specs=[smem(lambda i: (i, 0, 0)), smem(lambda i: (jnp.minimum(i + 1, nt - 1), 0, 0)),
                  pl.BlockSpec((tm, d), lambda i: (i, 0)), pl.BlockSpec(memory_space=pl.ANY),
                  pl.BlockSpec((tm, LANES), lambda i: (i, 0)), _mod_spec(5, tm, t_ctx, l_lat)],
        out_specs=pl.BlockSpec((tm, d), lambda i: (i, 0)),
        scratch_shapes=[pltpu.VMEM((2, DMA_CHUNK, d), F32), pltpu.SemaphoreType.DMA((2,))],
        compiler_params=_cp(("arbitrary",)),
        name="moe_combine",
    )(ids, ids, x, rows, top_w, mods)


def _dispatch_plan(top_ir, hist, tm):
    t = top_ir.shape[0]
    n_rt = hist.shape[0]
    hist = hist.astype(jnp.int32)
    counts = jnp.sum(hist, axis=0)
    tiles = (counts + tm - 1) // tm
    tile_end = jnp.cumsum(tiles)
    base = (tile_end - tiles)[None, :] * tm + jnp.cumsum(hist, axis=0) - hist
    e = top_ir[:, :TOP_K].reshape(n_rt, -1)
    rank = top_ir[:, TOP_K:].reshape(n_rt, -1)
    pick = e[:, :, None] == jnp.arange(N_EXPERTS, dtype=jnp.int32)[None, None, :]
    dest = (jnp.sum(jnp.where(pick, base[:, None, :], 0), axis=2) + rank).reshape(-1).astype(jnp.int32)
    n_tiles = (t * TOP_K) // tm + N_EXPERTS
    tile_ids = jnp.arange(n_tiles, dtype=jnp.int32)
    tile_expert = jnp.minimum(jnp.sum((tile_ids[:, None] >= tile_end[None, :]).astype(jnp.int32), axis=1),
                              N_EXPERTS - 1).astype(jnp.int32)
    return dest, n_tiles, tile_expert, tile_end[-1:].astype(jnp.int32)


def kernel(x_prompt, x_sample, c, cache_k, cache_v, state_ssd, c_ctx, norm1_w, norm2_w, w_mod, b_mod, w_in, w_gate,
           b_gate, hy_conv_w, hy_conv_b, hy_w1, hy_b1, hy_w2, hy_b2, hy_w3, hy_freq, hy_decay, hy_bias, ssd_conv_w,
           ssd_conv_b, ssd_a_log, ssd_dt_bias, ssd_d, ssd_norm_w, q_norm_w, k_norm_w, w_br_hy, w_br_ssd, w_br_att,
           w_out, w_router, b_router, w_e1, b_e1, w_e2, b_e2):
    n_ctx, l_ctx, d = x_prompt.shape
    n_lat, l_lat, _ = x_sample.shape
    depth = w_in.shape[0]
    t_ctx, t_lat = n_ctx * l_ctx, n_lat * l_lat
    t = t_ctx + t_lat
    kc = N_KV_HEADS * HEAD_DIM
    geo = dict(t_ctx=t_ctx, l_lat=l_lat)

    x = jnp.concatenate([x_prompt.reshape(t_ctx, d), x_sample.reshape(t_lat, d)], axis=0)
    cvec = jnp.zeros((8, d), F32).at[0].set(c_ctx).at[1:1 + n_lat].set(c)
    mods_all = modulation_all(cvec, w_mod, b_mod)
    n_exp, _, ff2 = w_e1.shape[1:]
    w1g_all, w1u_all = deinterleave_experts(w_e1.reshape(depth * n_exp, d, ff2))
    w1g_all = w1g_all.reshape(depth, n_exp, d, ff2 // 2)
    w1u_all = w1u_all.reshape(depth, n_exp, d, ff2 // 2)

    new_k, new_v, new_s = [], [], []
    for l in range(depth):
        mods = mods_all[l].reshape(8 * 6, 1, d)
        wi = w_in[l]
        w_proj = jnp.concatenate([wi[:, 0:3072], wi[:, 3088:3856], wi[:, 3072:3088],
                                  jnp.zeros((d, P_COLS - 3856), F32)], axis=1).astype(BF16)
        nw1 = norm1_w[l].reshape(1, d)
        proj = norm_mod_matmul(x, nw1, mods, w_proj, jnp.zeros((1, P_COLS), F32), sigmoid=False, out_dtype=F32,
                               **geo)
        gate = norm_mod_matmul(x, nw1, mods, w_gate[l].astype(BF16), b_gate[l].reshape(1, -1), sigmoid=True,
                               out_dtype=BF16, **geo)

        x0, p = hyena_pre(proj, hy_conv_w[l], hy_conv_b[l], t_ctx=t_ctx, l_ctx=l_ctx, l_lat=l_lat)
        hy_args = (hy_w1[l], hy_b1[l], hy_w2[l], hy_b2[l], hy_w3[l], hy_freq[l], hy_decay[l])
        y_hy_ctx = hyena_ctx(x0, p, hyena_filter(l_ctx, *hy_args), hy_bias[l], n_seq=n_ctx, seq_len=l_ctx)
        y_hy_lat = hyena_lat(x0[t_ctx:], p[t_ctx:], hyena_filter(l_lat, *hy_args), hy_bias[l], n_seq=n_lat,
                             seq_len=l_lat)

        ssd_args = (ssd_conv_w[l], ssd_conv_b[l], ssd_dt_bias[l], ssd_a_log[l], ssd_d[l])
        yf_c, yb_c, fin_c = ssd_scan(proj, None, *ssd_args, row0=0, n_seq=n_ctx, seq_len=l_ctx)
        yf_l, yb_l, _ = ssd_scan(proj, state_ssd[:, l], *ssd_args, row0=t_ctx, n_seq=n_lat, seq_len=l_lat)

        qn, kn = qk_prep(proj, q_norm_w[l], k_norm_w[l], **geo)
        v_all = proj[:, P_V:P_V + kc]
        k_ctx = kn[:t_ctx].reshape(n_ctx, l_ctx, kc)
        v_ctx = v_all[:t_ctx].reshape(n_ctx, l_ctx, kc)
        vt = lambda v: jnp.swapaxes(v, 1, 2).astype(BF16)
        att_ctx = attention(qn, k_ctx.astype(BF16), vt(v_ctx), row0=0, seq_len=l_ctx, tq=l_ctx)
        k_lat = jnp.concatenate([kn[t_ctx:].reshape(n_lat, l_lat, kc), cache_k[:, l].reshape(n_lat, -1, kc)], axis=1)
        v_lat = jnp.concatenate([v_all[t_ctx:].reshape(n_lat, l_lat, kc), cache_v[:, l].reshape(n_lat, -1, kc)],
                                axis=1)
        att_lat = attention(qn, k_lat.astype(BF16), vt(v_lat), row0=t_ctx, seq_len=l_lat, tq=128)

        wr = jnp.pad(w_router[l], ((0, 0), (0, LANES - N_EXPERTS)))
        br = jnp.pad(b_router[l], (0, LANES - N_EXPERTS)).reshape(1, LANES)
        x, h2, top_ir, top_w, hist = merge_router(
            x, (y_hy_ctx, y_hy_lat), (yf_c, yf_l), (yb_c, yb_l), proj, (att_ctx, att_lat), gate, ssd_norm_w[l],
            w_br_hy[l].astype(BF16), w_br_ssd[l].astype(BF16),
            w_br_att[l].astype(BF16), w_out[l].astype(BF16), mods, norm2_w[l].reshape(1, d), wr, br, **geo)

        dest, n_tiles, tile_expert, n_valid = _dispatch_plan(top_ir[:, :2 * TOP_K], hist[::8, :N_EXPERTS], MOE_TILE)
        xs = moe_dispatch(dest, h2, n_tiles * MOE_TILE)
        out_sorted = moe_experts(
            xs, tile_expert, n_valid, w1g_all[l], w1u_all[l],
            b_e1[l][:, None, 0::2], b_e1[l][:, None, 1::2], w_e2[l].astype(BF16), b_e2[l][:, None, :])
        x = moe_combine(x, dest, out_sorted, top_w, mods, **geo)

        new_k.append(k_ctx.reshape(n_ctx, l_ctx, N_KV_HEADS, HEAD_DIM))
        new_v.append(v_ctx.reshape(n_ctx, l_ctx, N_KV_HEADS, HEAD_DIM))
        new_s.append(fin_c)

    y_prompt = x[:t_ctx].reshape(n_ctx, l_ctx, d)
    y_sample = x[t_ctx:].reshape(n_lat, l_lat, d)
    return (y_prompt, y_sample, jnp.stack(new_k, axis=1), jnp.stack(new_v, axis=1), jnp.stack(new_s, axis=1))
```

```python
import functools
import math

import numpy as np
import jax
import jax.numpy as jnp
from jax import lax
from jax.experimental import pallas as pl
from jax.experimental.pallas import tpu as pltpu

F32 = jnp.float32
BF16 = jnp.bfloat16
HI = lax.Precision.HIGHEST

EPS = 1e-6
GRID_W = 64
HY_DIM = 512
SSD_INNER = 512
SSD_HEADDIM = 64
SSD_HEADS = 8
SSD_GROUPS = 2
SSD_STATE = 128
SSD_CHUNK = 128
N_HEADS = 8
N_KV_HEADS = 2
HEAD_DIM = 64
ROPE_THETA = 10000.0
N_EXPERTS = 32
TOP_K = 4
TOP_K_SHIFT = 2
SWIGLU_ALPHA = 1.702
SWIGLU_LIMIT = 7.0

P_HY, P_Z, P_XBC, P_Q, P_K, P_V, P_DT, P_COLS = 0, 1536, 2048, 3072, 3584, 3712, 3840, 3968

VMEM_LIMIT = 56 * 1024 * 1024
LANES = 128
ROW_TILE = 256
MOE_TILE = 512
DFT_N1, DFT_N2 = 64, 128


def _cp(sem, vmem=VMEM_LIMIT):
    return pltpu.CompilerParams(dimension_semantics=sem, vmem_limit_bytes=vmem)


def _sigmoid(x):
    return 1.0 / (1.0 + jnp.exp(-x))


def _silu(x):
    return x * _sigmoid(x)


def _softplus(x):
    return jnp.maximum(x, 0.0) + jnp.log(1.0 + jnp.exp(-jnp.abs(x)))


def _dot3(a, b):
    ah = a.astype(BF16)
    al = (a - ah.astype(F32)).astype(BF16)
    bh = b.astype(BF16)
    bl = (b - bh.astype(F32)).astype(BF16)
    dot = lambda u, v: jnp.dot(u, v, preferred_element_type=F32)
    return dot(ah, bh) + (dot(ah, bl) + dot(al, bh))


def _mod_kernel(c_ref, w_ref, b_ref, o_ref):
    s = _silu(c_ref[...])
    o_ref[0] = jnp.dot(s, w_ref[0], precision=HI, preferred_element_type=F32) + b_ref[0]


def modulation_all(cvec, w_mod, b_mod):
    depth, d, n = w_mod.shape
    tn = 1536
    return pl.pallas_call(
        _mod_kernel,
        out_shape=jax.ShapeDtypeStruct((depth, 8, n), F32),
        grid=(depth, n // tn),
        in_specs=[pl.BlockSpec((8, d), lambda l, j: (0, 0)),
                  pl.BlockSpec((1, d, tn), lambda l, j: (l, 0, j)),
                  pl.BlockSpec((1, 1, tn), lambda l, j: (l, 0, j))],
        out_specs=pl.BlockSpec((1, 8, tn), lambda l, j: (l, 0, j)),
        compiler_params=_cp(("arbitrary", "arbitrary")),
        name="modulation",
    )(cvec, w_mod, b_mod.reshape(depth, 1, n))


def _mod_row(i, tm, t_ctx, l_lat):
    n_ctx = t_ctx // tm
    per = l_lat // tm
    return jnp.where(i < n_ctx, 0, 1 + (i - n_ctx) // per)


def _mod_spec(k, tm, t_ctx, l_lat):
    return pl.BlockSpec((1, 1, 1024), lambda i: (_mod_row(i, tm, t_ctx, l_lat) * 6 + k, 0, 0))


def _nmm_kernel(x_ref, nw_ref, sh_ref, sc_ref, w_ref, b_ref, o_ref, *, sigmoid):
    x = x_ref[...]
    ms = jnp.mean(x * x, axis=-1, keepdims=True)
    h = x * lax.rsqrt(ms + EPS) * nw_ref[...]
    h = h * (1.0 + sc_ref[0]) + sh_ref[0]
    acc = jnp.dot(h.astype(BF16), w_ref[...], preferred_element_type=F32) + b_ref[...]
    if sigmoid:
        acc = _sigmoid(acc)
    o_ref[...] = acc.astype(o_ref.dtype)


def norm_mod_matmul(x, nw, mods, w, b, *, t_ctx, l_lat, sigmoid, out_dtype, tm=512):
    t, d = x.shape
    n = w.shape[1]
    return pl.pallas_call(
        functools.partial(_nmm_kernel, sigmoid=sigmoid),
        out_shape=jax.ShapeDtypeStruct((t, n), out_dtype),
        grid=(t // tm,),
        in_specs=[pl.BlockSpec((tm, d), lambda i: (i, 0)),
                  pl.BlockSpec((1, d), lambda i: (0, 0)),
                  _mod_spec(0, tm, t_ctx, l_lat),
                  _mod_spec(1, tm, t_ctx, l_lat),
                  pl.BlockSpec((d, n), lambda i: (0, 0)),
                  pl.BlockSpec((1, n), lambda i: (0, 0))],
        out_specs=pl.BlockSpec((tm, n), lambda i: (i, 0)),
        compiler_params=_cp(("arbitrary",)),
        name="norm_mod_matmul",
    )(x, nw, mods, mods, w, b)


def _seq_edges(i, tr, t_ctx, l_ctx, l_lat):
    tok = i * tr
    pos = jnp.where(tok < t_ctx, tok % l_ctx, (tok - t_ctx) % l_lat)
    length = jnp.where(tok < t_ctx, l_ctx, l_lat)
    return pos == 0, pos + tr == length


def _conv3(x, prev8, next8, w_ref, b_ref, first, last):
    tr = x.shape[0]
    row = lax.broadcasted_iota(jnp.int32, x.shape, 0)
    pm = jnp.where(first, 0.0, 1.0)
    nm = jnp.where(last, 0.0, 1.0)
    xm1 = jnp.where(row == 0, prev8[7:8, :] * pm, pltpu.roll(x, 1, axis=0))
    xp1 = jnp.where(row == tr - 1, next8[0:1, :] * nm, pltpu.roll(x, tr - 1, axis=0))
    return b_ref[...] + xm1 * w_ref[0:1, :] + x * w_ref[1:2, :] + xp1 * w_ref[2:3, :]


def _conv_specs(tr, tc, col_blk, n_rows):
    r8 = tr // 8
    last8 = n_rows // 8 - 1
    return [pl.BlockSpec((tr, tc), lambda i, j: (i, col_blk(j))),
            pl.BlockSpec((8, tc), lambda i, j: (jnp.maximum(i * r8 - 1, 0), col_blk(j))),
            pl.BlockSpec((8, tc), lambda i, j: (jnp.minimum((i + 1) * r8, last8), col_blk(j)))]


def _hy_pre_kernel(*refs, tr, t_ctx, l_ctx, l_lat):
    (x0, x0p, x0n, x1, x1p, x1n, xv, xvp, xvn, w0, w1, wv, b0, b1, bv, o0_ref, op_ref) = refs
    first, last = _seq_edges(pl.program_id(0), tr, t_ctx, l_ctx, l_lat)
    o0_ref[...] = _conv3(x0[...], x0p[...], x0n[...], w0, b0, first, last)
    u1 = _conv3(x1[...], x1p[...], x1n[...], w1, b1, first, last)
    uv = _conv3(xv[...], xvp[...], xvn[...], wv, bv, first, last)
    op_ref[...] = u1 * uv


def hyena_pre(proj, w, b, *, t_ctx, l_ctx, l_lat):
    t = proj.shape[0]
    tr, tc = ROW_TILE, HY_DIM
    nb = HY_DIM // tc
    b2 = b.reshape(1, 3 * HY_DIM)
    specs = []
    for s in range(3):
        specs += _conv_specs(tr, tc, lambda j, s=s: P_HY // tc + s * nb + j, t)
    specs += [pl.BlockSpec((3, tc), lambda i, j, s=s: (0, s * nb + j)) for s in range(3)]
    specs += [pl.BlockSpec((1, tc), lambda i, j, s=s: (0, s * nb + j)) for s in range(3)]
    return pl.pallas_call(
        functools.partial(_hy_pre_kernel, tr=tr, t_ctx=t_ctx, l_ctx=l_ctx, l_lat=l_lat),
        out_shape=(jax.ShapeDtypeStruct((t, HY_DIM), F32), jax.ShapeDtypeStruct((t, HY_DIM), F32)),
        grid=(t // tr, nb),
        in_specs=specs,
        out_specs=(pl.BlockSpec((tr, tc), lambda i, j: (i, j)), pl.BlockSpec((tr, tc), lambda i, j: (i, j))),
        compiler_params=_cp(("arbitrary", "arbitrary")),
        name="hyena_pre",
    )(*([proj] * 9), w, w, w, b2, b2, b2)


def _filter_kernel(z_ref, w1_ref, b1_ref, w2_ref, b2_ref, w3_ref, fr_ref, dec_ref, o_ref):
    z = z_ref[...]
    fr = fr_ref[...]
    h = jnp.sin(fr * (jnp.dot(z, w1_ref[...], precision=HI, preferred_element_type=F32) + b1_ref[...]))
    h = jnp.sin(fr * (jnp.dot(h, w2_ref[...], precision=HI, preferred_element_type=F32) + b2_ref[...]))
    f = jnp.dot(h, w3_ref[...], precision=HI, preferred_element_type=F32)
    o_ref[...] = f * jnp.exp(-z[:, 0:1] * jnp.abs(dec_ref[...]))


def _filter_embedding(seq_len, emb):
    bands_n = (emb - 1) // 2
    t = jnp.linspace(0.0, 1.0, seq_len, dtype=F32)[:, None]
    bands = jnp.linspace(1e-4, bands_n - 1, bands_n, dtype=F32)[None, :]
    ang = (2.0 * math.pi / seq_len) * jnp.arange(seq_len, dtype=F32)[:, None] * bands
    z = jnp.concatenate([t, jnp.cos(ang), -jnp.sin(ang)], axis=-1)
    return jnp.pad(z, ((0, 0), (0, LANES - emb)))


def hyena_filter(seq_len, w1, b1, w2, b2, w3, freq, decay):
    emb, ff = w1.shape
    n = w3.shape[1]
    z = _filter_embedding(seq_len, emb)
    padc = LANES - ff
    w1p = jnp.pad(w1, ((0, LANES - emb), (0, padc)))
    w2p = jnp.pad(w2, ((0, padc), (0, padc)))
    w3p = jnp.pad(w3, ((0, padc), (0, 0)))
    row = lambda v: jnp.pad(v, (0, padc)).reshape(1, LANES)
    tr = 256
    full = lambda shape: pl.BlockSpec(shape, lambda i: (0, 0))
    return pl.pallas_call(
        _filter_kernel,
        out_shape=jax.ShapeDtypeStruct((seq_len, n), F32),
        grid=(seq_len // tr,),
        in_specs=[pl.BlockSpec((tr, LANES), lambda i: (i, 0)), full((LANES, LANES)), full((1, LANES)),
                  full((LANES, LANES)), full((1, LANES)), full((LANES, n)), full((1, LANES)), full((1, n))],
        out_specs=pl.BlockSpec((tr, n), lambda i: (i, 0)),
        compiler_params=_cp(("arbitrary",)),
        name="hyena_filter",
    )(z, w1p, row(b1), w2p, row(b2), w3p, row(freq), decay.reshape(1, n))


def _circular_filter(filt):
    ch = filt.shape[1] // 2
    h_fwd, h_bwd = filt[:, :ch], filt[:, ch:]
    return jnp.concatenate([h_fwd, jnp.zeros((1, ch), F32), h_bwd[1:][::-1]], axis=0)


def _cs(n_rows, n_cols, period):
    ang = 2.0 * np.pi * (np.outer(np.arange(n_rows), np.arange(n_cols)) % period) / period
    return np.cos(ang), np.sin(ang)


def _mm_kernel(a_ref, b_ref, o_ref):
    o_ref[0] = _dot3(a_ref[...], b_ref[0])


def const_matmul(a, b, tn):
    m, k = a.shape
    bsz, _, n = b.shape
    return pl.pallas_call(
        _mm_kernel,
        out_shape=jax.ShapeDtypeStruct((bsz, m, n), F32),
        grid=(bsz, n // tn),
        in_specs=[pl.BlockSpec((m, k), lambda s, j: (0, 0)), pl.BlockSpec((1, k, tn), lambda s, j: (s, 0, j))],
        out_specs=pl.BlockSpec((1, m, tn), lambda s, j: (s, 0, j)),
        compiler_params=_cp(("arbitrary", "arbitrary")),
        name="const_matmul",
    )(a, b)


def _hy_ctx_kernel(x0_ref, p_ref, kr_ref, ki_ref, fw_ref, iv_ref, bias_ref, o_ref, *, n):
    p = p_ref[...]
    xf = _dot3(fw_ref[...], p)
    xr, xi = xf[:n], xf[n:]
    kr, ki = kr_ref[...], ki_ref[...]
    yr = xr * kr - xi * ki
    yi = xr * ki + xi * kr
    y = _dot3(iv_ref[...], jnp.concatenate([yr, yi], axis=0))
    o_ref[...] = x0_ref[...] * (y + p * bias_ref[...])


def hyena_ctx(x0, p, filt, bias, *, n_seq, seq_len):
    n = 2 * seq_len
    ch = p.shape[1]
    c_full, s_full = _cs(n, n, n)
    fw_full = jnp.asarray(np.concatenate([c_full, -s_full], axis=0), F32)
    fw_half = fw_full[:, :seq_len]
    iv = jnp.asarray(np.concatenate([c_full[:seq_len], -s_full[:seq_len]], axis=1) / n, F32)
    kf = const_matmul(fw_full, _circular_filter(filt)[None], ch)[0]
    kr, ki = kf[:n], kf[n:]
    full = lambda shape: pl.BlockSpec(shape, lambda s: (0, 0))
    return pl.pallas_call(
        functools.partial(_hy_ctx_kernel, n=n),
        out_shape=jax.ShapeDtypeStruct((n_seq * seq_len, ch), F32),
        grid=(n_seq,),
        in_specs=[pl.BlockSpec((seq_len, ch), lambda s: (s, 0)), pl.BlockSpec((seq_len, ch), lambda s: (s, 0)),
                  full((n, ch)), full((n, ch)), full((2 * n, seq_len)), full((seq_len, 2 * n)), full((1, ch))],
        out_specs=pl.BlockSpec((seq_len, ch), lambda s: (s, 0)),
        compiler_params=_cp(("arbitrary",)),
        name="hyena_ctx",
    )(x0, p, kr, ki, fw_half, iv, bias.reshape(1, ch))


def _stage2_kernel(a_ref, twr_ref, twi_ref, m_ref, *rest, conv):
    if conv:
        kf_ref, mi_ref, o_ref = rest
    else:
        (o_ref,) = rest
    ar, ai = a_ref[0, 0, 0], a_ref[0, 1, 0]
    twr, twi = twr_ref[0], twi_ref[0]
    br = ar * twr - ai * twi
    bi = ar * twi + ai * twr
    x = _dot3(m_ref[...], jnp.concatenate([br, bi], axis=0))
    n2 = ar.shape[0]
    xr, xi = x[:n2], x[n2:]
    if not conv:
        o_ref[0, 0, 0] = xr
        o_ref[0, 1, 0] = xi
        return
    kr, ki = kf_ref[0, 0, 0], kf_ref[0, 1, 0]
    yr = xr * kr - xi * ki
    yi = xr * ki + xi * kr
    pq = _dot3(mi_ref[...], jnp.concatenate([yr, yi], axis=0))
    pr, pi = pq[:n2], pq[n2:]
    o_ref[0, 0, 0] = pr * twr + pi * twi
    o_ref[0, 1, 0] = pi * twr - pr * twi


def _stage2(a, kf, ch):
    bsz = a.shape[0]
    n1, n2 = DFT_N1, DFT_N2
    n = n1 * n2
    tw_ang = 2.0 * np.pi * np.outer(np.arange(n1), np.arange(n2)) / n
    twr = jnp.asarray(np.cos(tw_ang), F32).reshape(n1, n2, 1)
    twi = jnp.asarray(-np.sin(tw_ang), F32).reshape(n1, n2, 1)
    c2, s2 = _cs(n2, n2, n2)
    m_fwd = jnp.asarray(np.block([[c2, s2], [-s2, c2]]), F32)
    m_inv = jnp.asarray(np.block([[c2, -s2], [s2, c2]]), F32)
    conv = kf is not None
    blk = pl.BlockSpec((1, 2, 1, n2, ch), lambda s, k: (s, 0, k, 0, 0))
    specs = [blk, pl.BlockSpec((1, n2, 1), lambda s, k: (k, 0, 0)), pl.BlockSpec((1, n2, 1), lambda s, k: (k, 0, 0)),
             pl.BlockSpec((2 * n2, 2 * n2), lambda s, k: (0, 0))]
    args = [a, twr, twi, m_fwd]
    if conv:
        specs += [pl.BlockSpec((1, 2, 1, n2, ch), lambda s, k: (0, 0, k, 0, 0)),
                  pl.BlockSpec((2 * n2, 2 * n2), lambda s, k: (0, 0))]
        args += [kf, m_inv]
    return pl.pallas_call(
        functools.partial(_stage2_kernel, conv=conv),
        out_shape=jax.ShapeDtypeStruct(a.shape, F32),
        grid=(bsz, n1),
        in_specs=specs,
        out_specs=blk,
        compiler_params=_cp(("arbitrary", "arbitrary")),
        name="hyena_stage2",
    )(*args)


def _hy_post_kernel(g_ref, q_ref, x0_ref, p_ref, bias_ref, o_ref):
    y = _dot3(g_ref[...], q_ref[0])
    o_ref[0] = x0_ref[0] * (y + p_ref[0] * bias_ref[...])


def hyena_lat(x0, p, filt, bias, *, n_seq, seq_len):
    n1, n2 = DFT_N1, DFT_N2
    n = n1 * n2
    assert n == 2 * seq_len
    ch = p.shape[1]
    h1 = n1 // 2
    wide = n2 * ch
    c1, s1 = _cs(n1, n1, n1)
    f1_full = jnp.asarray(np.concatenate([c1, -s1], axis=0), F32)
    f1_half = f1_full[:, :h1]
    g1 = jnp.asarray(np.concatenate([c1[:h1], -s1[:h1]], axis=1) / n, F32)
    tn = 8192
    kcirc = _circular_filter(filt).reshape(1, n1, wide)
    kf = _stage2(const_matmul(f1_full, kcirc, tn).reshape(1, 2, n1, n2, ch), None, ch)
    a = const_matmul(f1_half, p.reshape(n_seq, h1, wide), tn).reshape(n_seq, 2, n1, n2, ch)
    q = _stage2(a, kf, ch).reshape(n_seq, 2 * n1, wide)
    bias_w = jnp.tile(bias, n2).reshape(1, wide)
    out = pl.pallas_call(
        _hy_post_kernel,
        out_shape=jax.ShapeDtypeStruct((n_seq, h1, wide), F32),
        grid=(n_seq, wide // tn),
        in_specs=[pl.BlockSpec((h1, 2 * n1), lambda s, j: (0, 0)),
                  pl.BlockSpec((1, 2 * n1, tn), lambda s, j: (s, 0, j)),
                  pl.BlockSpec((1, h1, tn), lambda s, j: (s, 0, j)),
                  pl.BlockSpec((1, h1, tn), lambda s, j: (s, 0, j)),
                  pl.BlockSpec((1, tn), lambda s, j: (0, j))],
        out_specs=pl.BlockSpec((1, h1, tn), lambda s, j: (s, 0, j)),
        compiler_params=_cp(("arbitrary", "arbitrary")),
        name="hyena_post",
    )(g1, q, x0.reshape(n_seq, h1, wide), p.reshape(n_seq, h1, wide), bias_w)
    return out.reshape(n_seq * seq_len, ch)


def _dot_01(x, m, *, x_is_lhs):
    x1 = x.astype(BF16)
    r1 = x - x1.astype(F32)
    x2 = r1.astype(BF16)
    x3 = (r1 - x2.astype(F32)).astype(BF16)
    mb = m.astype(BF16)
    if x_is_lhs:
        dot = lambda u: jnp.dot(u, mb, preferred_element_type=F32)
    else:
        dot = lambda u: jnp.dot(mb, u, preferred_element_type=F32)
    return dot(x1) + (dot(x2) + dot(x3))


def _ssd_dir(xbc, dtraw, st_ref, d, consts, y_ref):
    tri, expand, dtb, acont, dskip = consts
    cl = xbc.shape[0]
    xs = xbc[:, :SSD_INNER]
    dt = _softplus(dtraw + dtb)
    a = dt * acont
    tri_d = tri if d == 0 else tri.T
    cs = _dot_01(a, tri_d, x_is_lhs=False)
    cs_t = _dot_01(a.T, tri_d.T, x_is_lhs=True)
    dt_x = _dot_01(dt, expand[d], x_is_lhs=True)
    xdt = xs * dt_x
    row = lax.broadcasted_iota(jnp.int32, (cl, cl), 0)
    col = lax.broadcasted_iota(jnp.int32, (cl, cl), 1)
    keep = (col <= row) if d == 0 else (col >= row)
    lane = lax.broadcasted_iota(jnp.int32, (cl, LANES), 1)
    low = lane < SSD_HEADDIM
    edge = cl - 1 if d == 0 else 0
    for g in range(SSD_GROUPS):
        bg = xbc[:, SSD_INNER + g * SSD_STATE: SSD_INNER + (g + 1) * SSD_STATE]
        cg = xbc[:, SSD_INNER + (SSD_GROUPS + g) * SSD_STATE: SSD_INNER + (SSD_GROUPS + g + 1) * SSD_STATE]
        bg16, cg16 = bg.astype(BF16), cg.astype(BF16)
        cb = lax.dot_general(cg16, bg16, (((1,), (1,)), ((), ())), preferred_element_type=F32)
        for pr in range(2):
            pair = g * 2 + pr
            h0 = 2 * pair
            ms = []
            for h in (h0, h0 + 1):
                ln = d * SSD_HEADS + h
                diff = cs[:, ln:ln + 1] - cs_t[ln:ln + 1, :]
                ms.append(jnp.where(keep, cb * jnp.exp(jnp.minimum(diff, 0.0)), 0.0).astype(BF16))
            xp = xdt[:, pair * LANES:(pair + 1) * LANES]
            xs_p = xs[:, pair * LANES:(pair + 1) * LANES]
            xlo = jnp.where(low, xp, 0.0).astype(BF16)
            xhi = jnp.where(low, 0.0, xp).astype(BF16)
            y_diag = (jnp.dot(ms[0], xlo, preferred_element_type=F32)
                      + jnp.dot(ms[1], xhi, preferred_element_type=F32))
            l0, l1 = d * SSD_HEADS + h0, d * SSD_HEADS + h0 + 1
            e_cs = jnp.where(low, jnp.exp(cs[:, l0:l0 + 1]), jnp.exp(cs[:, l1:l1 + 1]))
            st = st_ref[pair]
            y_off = jnp.dot(cg16, st.astype(BF16), preferred_element_type=F32) * e_cs
            y_ref[:, pair * LANES:(pair + 1) * LANES] = (
                y_diag + y_off + xs_p * dskip[d:d + 1, pair * LANES:(pair + 1) * LANES])
            tot0, tot1 = cs[edge:edge + 1, l0:l0 + 1], cs[edge:edge + 1, l1:l1 + 1]
            dec = jnp.where(low, jnp.exp(tot0 - cs[:, l0:l0 + 1]), jnp.exp(tot1 - cs[:, l1:l1 + 1]))
            upd = lax.dot_general(bg16, (xp * dec).astype(BF16), (((0,), (0,)), ((), ())),
                                  preferred_element_type=F32)
            st_ref[pair] = st * jnp.where(low[0:1], jnp.exp(tot0), jnp.exp(tot1)) + upd


def _ssd_kernel(xf_ref, xfp_ref, xfn_ref, dtf_ref, xb_ref, xbp_ref, xbn_ref, dtb_ref, init_ref, cw_ref, cb_ref,
                tri_ref, exp_ref, dtbias_ref, acont_ref, dskip_ref, yf_ref, yb_ref, fin_ref, st_ref, *, has_init):
    s = pl.program_id(1)
    at_start, at_end = s == 0, s == pl.num_programs(1) - 1

    @pl.when(at_start)
    def _():
        if has_init:
            st_ref[...] = init_ref[0]
        else:
            st_ref[...] = jnp.zeros_like(st_ref)

    consts = (tri_ref[...], (exp_ref[0], exp_ref[1]), dtbias_ref[...], acont_ref[...], dskip_ref[...])
    xbc_f = _silu(_conv3(xf_ref[...], xfp_ref[...], xfn_ref[...], cw_ref, cb_ref, at_start, at_end))
    _ssd_dir(xbc_f, dtf_ref[...], st_ref.at[0], 0, consts, yf_ref)
    xbc_b = _silu(_conv3(xb_ref[...], xbp_ref[...], xbn_ref[...], cw_ref, cb_ref, at_end, at_start))
    _ssd_dir(xbc_b, dtb_ref[...], st_ref.at[1], 1, consts, yb_ref)

    @pl.when(s == pl.num_programs(1) - 1)
    def _():
        fin_ref[0] = st_ref[...]


def _pair_states(s):
    b = s.shape[0]
    s = s.reshape(b, 2, SSD_HEADS // 2, 2, SSD_HEADDIM, SSD_STATE)
    return s.transpose(0, 1, 2, 5, 3, 4).reshape(b, 2, SSD_HEADS // 2, SSD_STATE, 2 * SSD_HEADDIM)


def _unpair_states(s):
    b = s.shape[0]
    s = s.reshape(b, 2, SSD_HEADS // 2, SSD_STATE, 2, SSD_HEADDIM)
    return s.transpose(0, 1, 2, 4, 5, 3).reshape(b, 2, SSD_HEADS, SSD_HEADDIM, SSD_STATE)


def ssd_scan(proj, init, conv_w, conv_b, dt_bias, a_log, d_skip, *, row0, n_seq, seq_len):
    cl = SSD_CHUNK
    nc = seq_len // cl
    base = row0 // cl
    has_init = init is not None
    hp = SSD_HEADS // 2
    init_p = _pair_states(init) if has_init else jnp.zeros((1, 2, hp, SSD_STATE, LANES), F32)
    tri = jnp.asarray(np.tril(np.ones((cl, cl))), F32)
    expand = np.zeros((2, LANES, SSD_INNER), np.float32)
    for d in range(2):
        for h in range(SSD_HEADS):
            expand[d, d * SSD_HEADS + h, h * SSD_HEADDIM:(h + 1) * SSD_HEADDIM] = 1.0
    pad16 = lambda v: jnp.pad(v.reshape(1, 2 * SSD_HEADS), ((0, 0), (0, LANES - 2 * SSD_HEADS)))
    acont = pad16(-jnp.exp(a_log))
    dtb = pad16(dt_bias)
    dskip = jnp.repeat(d_skip, SSD_HEADDIM, axis=1)
    cxbc = conv_w.shape[1]
    xblk = P_XBC // cxbc
    dtblk = P_DT // LANES
    r8 = cl // 8
    last8 = proj.shape[0] // 8 - 1
    full2 = lambda shape: pl.BlockSpec(shape, lambda b, s: (0,) * len(shape))
    fwd = lambda b, s: base + b * nc + s
    bwd = lambda b, s: base + b * nc + (nc - 1 - s)

    def stream(chunk):
        return [pl.BlockSpec((cl, cxbc), lambda b, s: (chunk(b, s), xblk)),
                pl.BlockSpec((8, cxbc), lambda b, s: (jnp.maximum(chunk(b, s) * r8 - 1, 0), xblk)),
                pl.BlockSpec((8, cxbc), lambda b, s: (jnp.minimum((chunk(b, s) + 1) * r8, last8), xblk)),
                pl.BlockSpec((cl, LANES), lambda b, s: (chunk(b, s), dtblk))]

    st_spec = pl.BlockSpec((1, 2, hp, SSD_STATE, LANES), lambda b, s: (b if has_init else 0, 0, 0, 0, 0))
    yf, yb, fin = pl.pallas_call(
        functools.partial(_ssd_kernel, has_init=has_init),
        out_shape=(jax.ShapeDtypeStruct((n_seq * seq_len, SSD_INNER), F32),
                   jax.ShapeDtypeStruct((n_seq * seq_len, SSD_INNER), F32),
                   jax.ShapeDtypeStruct((n_seq, 2, hp, SSD_STATE, LANES), F32)),
        grid=(n_seq, nc),
        in_specs=stream(fwd) + stream(bwd) + [
            st_spec, full2((3, cxbc)), full2((1, cxbc)), full2((cl, cl)), full2((2, LANES, SSD_INNER)),
            full2((1, LANES)), full2((1, LANES)), full2((2, SSD_INNER))],
        out_specs=(pl.BlockSpec((cl, SSD_INNER), lambda b, s: (b * nc + s, 0)),
                   pl.BlockSpec((cl, SSD_INNER), lambda b, s: (b * nc + (nc - 1 - s), 0)),
                   pl.BlockSpec((1, 2, hp, SSD_STATE, LANES), lambda b, s: (b, 0, 0, 0, 0))),
        scratch_shapes=[pltpu.VMEM((2, hp, SSD_STATE, LANES), F32)],
        compiler_params=_cp(("arbitrary", "arbitrary")),
        name="ssd_scan",
    )(*([proj] * 8), init_p, conv_w, conv_b.reshape(1, cxbc), tri, jnp.asarray(expand), dtb, acont, dskip)
    return yf, yb, _unpair_states(fin)


def _headnorm(x, g_ref, w_ref):
    ms = _dot3(x * x, g_ref[...])
    return x * lax.rsqrt(ms + EPS) * w_ref[...]


def _rope(x, cos, sin_signed):
    lane = lax.broadcasted_iota(jnp.int32, x.shape, 1)
    w = x.shape[1]
    swapped = jnp.where(lane % 2 == 0, pltpu.roll(x, w - 1, axis=1), pltpu.roll(x, 1, axis=1))
    return x * cos + swapped * sin_signed


Q_SCALE = HEAD_DIM ** -0.5 * math.log2(math.e)


def _store_padded_heads(qo_ref, q):
    rep = N_HEADS // N_KV_HEADS
    lane = lax.broadcasted_iota(jnp.int32, (q.shape[0], LANES), 1)
    for h in range(N_HEADS):
        g = h // rep
        chunk = q[:, (h // 2) * LANES:(h // 2 + 1) * LANES]
        if h % 2 != g:
            chunk = pltpu.roll(chunk, HEAD_DIM, axis=1)
        keep = (lane >= g * HEAD_DIM) & (lane < (g + 1) * HEAD_DIM)
        qo_ref[:, h * LANES:(h + 1) * LANES] = jnp.where(keep, chunk, 0.0).astype(qo_ref.dtype)


def _qk_kernel(q_ref, k_ref, cos_ref, sin_ref, gq_ref, gk_ref, qw_ref, kw_ref, qo_ref, ko_ref, *, n_ctx_tiles):
    q = _headnorm(q_ref[...], gq_ref, qw_ref)
    k = _headnorm(k_ref[...], gk_ref, kw_ref)
    is_lat = pl.program_id(0) >= n_ctx_tiles

    @pl.when(is_lat)
    def _():
        cos, sin = cos_ref[...], sin_ref[...]
        _store_padded_heads(qo_ref, _rope(q, cos, sin) * Q_SCALE)
        ko_ref[...] = _rope(k, cos[:, :k.shape[1]], sin[:, :k.shape[1]])

    @pl.when(jnp.logical_not(is_lat))
    def _():
        _store_padded_heads(qo_ref, q * Q_SCALE)
        ko_ref[...] = k


def _rope_tables(seq_len):
    n_rows = seq_len // GRID_W
    row = jnp.repeat(jnp.arange(n_rows), GRID_W).astype(F32)
    col = jnp.tile(jnp.arange(GRID_W), n_rows).astype(F32)
    n_freq = HEAD_DIM // 4
    inv = ROPE_THETA ** (-jnp.arange(n_freq, dtype=F32) / n_freq)
    ang = jnp.concatenate([row[:, None] * inv, col[:, None] * inv], axis=-1)
    cos = jnp.repeat(jnp.cos(ang), 2, axis=1)
    sin = jnp.repeat(jnp.sin(ang), 2, axis=1) * jnp.tile(jnp.asarray([-1.0, 1.0], F32), HEAD_DIM // 2)
    return jnp.tile(cos, (1, N_HEADS)), jnp.tile(sin, (1, N_HEADS))


def qk_prep(proj, q_norm_w, k_norm_w, *, t_ctx, l_lat):
    t = proj.shape[0]
    tr = ROW_TILE
    qc, kc = N_HEADS * HEAD_DIM, N_KV_HEADS * HEAD_DIM
    cos, sin = _rope_tables(l_lat)
    group = lambda c: jnp.asarray(np.kron(np.eye(c // HEAD_DIM), np.ones((HEAD_DIM, HEAD_DIM))) / HEAD_DIM, F32)
    n_ctx_tiles = t_ctx // tr
    per = l_lat // tr
    tab = pl.BlockSpec((tr, qc), lambda i: (jnp.maximum(i - n_ctx_tiles, 0) % per, 0))
    full = lambda shape: pl.BlockSpec(shape, lambda i: (0, 0))
    return pl.pallas_call(
        functools.partial(_qk_kernel, n_ctx_tiles=n_ctx_tiles),
        out_shape=(jax.ShapeDtypeStruct((t, N_HEADS * LANES), BF16), jax.ShapeDtypeStruct((t, kc), F32)),
        grid=(t // tr,),
        in_specs=[pl.BlockSpec((tr, qc), lambda i: (i, P_Q // qc)), pl.BlockSpec((tr, kc), lambda i: (i, P_K // kc)),
                  tab, tab, full((qc, qc)), full((kc, kc)), full((1, qc)), full((1, kc))],
        out_specs=(pl.BlockSpec((tr, N_HEADS * LANES), lambda i: (i, 0)), pl.BlockSpec((tr, kc), lambda i: (i, 0))),
        compiler_params=_cp(("arbitrary",)),
        name="qk_prep",
    )(proj, proj, cos, sin, group(qc), group(kc), jnp.tile(q_norm_w, N_HEADS).reshape(1, qc),
      jnp.tile(k_norm_w, N_KV_HEADS).reshape(1, kc))


def _attention_kernel(q_ref, k_ref, vt_ref, o_ref, *, tq):
    rep = N_HEADS // N_KV_HEADS
    k = k_ref[0]
    vt = vt_ref[0]
    for g in range(N_KV_HEADS):
        qs = jnp.concatenate([q_ref[:, (g * rep + r) * LANES:(g * rep + r + 1) * LANES] for r in range(rep)],
                             axis=0)
        s = lax.dot_general(k, qs, (((1,), (1,)), ((), ())), preferred_element_type=F32)
        p = jnp.exp2(s - jnp.max(s, axis=0, keepdims=True))
        l = jnp.sum(p, axis=0, keepdims=True)
        out = (jnp.dot(vt, p.astype(BF16), preferred_element_type=F32) / l).T
        for r in range(rep):
            h = g * rep + r
            o_ref[:, h * HEAD_DIM:(h + 1) * HEAD_DIM] = (
                out[r * tq:(r + 1) * tq, g * HEAD_DIM:(g + 1) * HEAD_DIM].astype(o_ref.dtype))


def attention(q, k, vt, *, row0, seq_len, tq):
    qc = q.shape[1]
    b, lk, kc = k.shape
    per = seq_len // tq
    base = row0 // tq
    return pl.pallas_call(
        functools.partial(_attention_kernel, tq=tq),
        out_shape=jax.ShapeDtypeStruct((b * seq_len, N_HEADS * HEAD_DIM), BF16),
        grid=(b, per),
        in_specs=[pl.BlockSpec((tq, qc), lambda s, i: (base + s * per + i, 0)),
                  pl.BlockSpec((1, lk, kc), lambda s, i: (s, 0, 0)),
                  pl.BlockSpec((1, kc, lk), lambda s, i: (s, 0, 0))],
        out_specs=pl.BlockSpec((tq, N_HEADS * HEAD_DIM), lambda s, i: (s * per + i, 0)),
        compiler_params=_cp(("arbitrary", "arbitrary")),
        name="attention",
    )(q, k, vt)


def _merge_kernel(x_ref, yhy_c, yhy_l, yf_c, yf_l, yb_c, yb_l, z_ref, yatt_c, yatt_l, gate_ref, snw_ref, whb_ref,
                  wsb_ref, wab_ref, wout_ref, g1_ref, n2w_ref, sh2_ref, sc2_ref, wr_ref, br_ref, ltri_ref,
                  xo_ref, h2_ref, ti_ref, tw_ref, hist_ref, *, n_ctx_tiles):
    d = x_ref.shape[1]
    is_ctx = pl.program_id(0) < n_ctx_tiles
    pick = lambda c_ref, l_ref: jnp.where(is_ctx, c_ref[...], l_ref[...])
    ys = (pick(yf_c, yf_l) + pick(yb_c, yb_l)) * _silu(z_ref[...])
    ys = ys * lax.rsqrt(jnp.mean(ys * ys, axis=-1, keepdims=True) + EPS) * snw_ref[...]
    gate = gate_ref[...].astype(F32)
    merged = (gate[:, :d] * jnp.dot(pick(yhy_c, yhy_l).astype(BF16), whb_ref[...], preferred_element_type=F32)
              + gate[:, d:2 * d] * jnp.dot(ys.astype(BF16), wsb_ref[...], preferred_element_type=F32)
              + gate[:, 2 * d:] * jnp.dot(pick(yatt_c, yatt_l), wab_ref[...], preferred_element_type=F32))
    mix = jnp.dot(merged.astype(BF16), wout_ref[...], preferred_element_type=F32)
    x = x_ref[...] + g1_ref[0] * mix
    xo_ref[...] = x
    h = x * lax.rsqrt(jnp.mean(x * x, axis=-1, keepdims=True) + EPS) * n2w_ref[...]
    h = h * (1.0 + sc2_ref[0]) + sh2_ref[0]
    _store_tiled(h2_ref, h)
    logits = _dot3(h, wr_ref[...]) + br_ref[...]
    lane = lax.broadcasted_iota(jnp.int32, logits.shape, 1)
    work = jnp.where(lane < N_EXPERTS, logits, -jnp.inf)
    idx_out = jnp.zeros(logits.shape, jnp.int32)
    val_out = jnp.full(logits.shape, -jnp.inf, F32)
    picks = []
    for j in range(TOP_K):
        mx = jnp.max(work, axis=-1, keepdims=True)
        am = jnp.min(jnp.where(work == mx, lane, LANES), axis=-1, keepdims=True)
        idx_out = jnp.where(lane == j, am, idx_out)
        val_out = jnp.where(lane == j, mx, val_out)
        picks.append(lane == am)
        work = jnp.where(picks[-1], -jnp.inf, work)
    e = jnp.exp(val_out - jnp.max(val_out, axis=-1, keepdims=True))
    tw_ref[...] = e / jnp.sum(e, axis=-1, keepdims=True)
    chosen = jnp.where(picks[0] | picks[1] | picks[2] | picks[3], 1.0, 0.0)
    before = jnp.dot(ltri_ref[...], chosen.astype(BF16), preferred_element_type=F32)
    rank_out = jnp.zeros(logits.shape, F32)
    for j in range(TOP_K):
        rank_j = jnp.sum(jnp.where(picks[j], before, 0.0), axis=-1, keepdims=True)
        rank_out = jnp.where(lane == TOP_K + j, rank_j, rank_out)
    ti_ref[...] = idx_out + rank_out.astype(jnp.int32)
    hist_ref[...] = jnp.broadcast_to(jnp.sum(chosen, axis=0, keepdims=True), hist_ref.shape)


def merge_router(x, y_hy, yf, yb, proj, y_att, gate, ssd_norm_w, whb, wsb, wab, wout, mods, n2w, wr, br,
                 *, t_ctx, l_lat, tm=512):
    t, d = x.shape
    nct = t_ctx // tm
    nlt = (t - t_ctx) // tm
    row = lambda tc, c0=0: pl.BlockSpec((tm, tc), lambda i: (i, c0))
    full = lambda shape: pl.BlockSpec(shape, lambda i: (0, 0))
    pair = lambda tc: [pl.BlockSpec((tm, tc), lambda i: (jnp.minimum(i, nct - 1), 0)),
                       pl.BlockSpec((tm, tc), lambda i: (jnp.clip(i - nct, 0, nlt - 1), 0))]
    ltri = jnp.asarray(np.tril(np.ones((tm, tm)), -1), BF16)
    return pl.pallas_call(
        functools.partial(_merge_kernel, n_ctx_tiles=nct),
        out_shape=(jax.ShapeDtypeStruct((t, d), F32), jax.ShapeDtypeStruct((t * ROW_SUB, LANES), F32),
                   jax.ShapeDtypeStruct((t, LANES), jnp.int32), jax.ShapeDtypeStruct((t, LANES), F32),
                   jax.ShapeDtypeStruct((t // tm * 8, LANES), F32)),
        grid=(t // tm,),
        in_specs=[row(d)] + pair(HY_DIM) + pair(SSD_INNER) + pair(SSD_INNER) + [row(SSD_INNER, P_Z // SSD_INNER)]
                 + pair(N_HEADS * HEAD_DIM) + [row(3 * d), full((1, SSD_INNER)),
                  full(whb.shape), full(wsb.shape), full(wab.shape), full(wout.shape),
                  _mod_spec(2, tm, t_ctx, l_lat), full((1, d)), _mod_spec(3, tm, t_ctx, l_lat),
                  _mod_spec(4, tm, t_ctx, l_lat), full((d, LANES)), full((1, LANES)), full((tm, tm))],
        out_specs=(row(d), pl.BlockSpec((tm * ROW_SUB, LANES), lambda i: (i, 0)), row(LANES), row(LANES),
                   pl.BlockSpec((8, LANES), lambda i: (i, 0))),
        compiler_params=_cp(("arbitrary",)),
        name="merge_router",
    )(x, *y_hy, *yf, *yb, proj, *y_att, gate, ssd_norm_w.reshape(1, -1), whb, wsb, wab, wout, mods, n2w, mods, mods,
      wr, br, ltri)


DEINT_BLOCK = 256


def _deinterleave_kernel(w_ref, perm_ref, g_ref, u_ref):
    half = DEINT_BLOCK // 2
    for blk in range(w_ref.shape[2] // DEINT_BLOCK):
        wb = w_ref[0, :, blk * DEINT_BLOCK:(blk + 1) * DEINT_BLOCK].astype(BF16)
        r = jnp.dot(wb, perm_ref[...], preferred_element_type=F32)
        g_ref[0, :, blk * half:(blk + 1) * half] = r[:, :half].astype(BF16)
        u_ref[0, :, blk * half:(blk + 1) * half] = r[:, half:].astype(BF16)


def deinterleave_experts(w1):
    e, d, f2 = w1.shape
    perm = np.zeros((DEINT_BLOCK, DEINT_BLOCK), np.float32)
    half = DEINT_BLOCK // 2
    perm[2 * np.arange(half), np.arange(half)] = 1.0
    perm[2 * np.arange(half) + 1, half + np.arange(half)] = 1.0
    out = jax.ShapeDtypeStruct((e, d, f2 // 2), BF16)
    return pl.pallas_call(
        _deinterleave_kernel,
        out_shape=(out, out),
        grid=(e,),
        in_specs=[pl.BlockSpec((1, d, f2), lambda i: (i, 0, 0)),
                  pl.BlockSpec((DEINT_BLOCK, DEINT_BLOCK), lambda i: (0, 0))],
        out_specs=(pl.BlockSpec((1, d, f2 // 2), lambda i: (i, 0, 0)),
                   pl.BlockSpec((1, d, f2 // 2), lambda i: (i, 0, 0))),
        compiler_params=_cp(("arbitrary",)),
        name="deinterleave_experts",
    )(w1, jnp.asarray(perm, BF16))


DMA_CHUNK = 512
DMA_UNROLL = 8
ROW_SUB = 8


def _tok_rows(t):
    return pl.ds(pl.multiple_of(t * ROW_SUB, ROW_SUB), ROW_SUB)


def _chunk_wait(ref, sem):
    n = DMA_CHUNK * ROW_SUB
    pltpu.make_async_copy(ref.at[pl.ds(0, n)], ref.at[pl.ds(0, n)], sem).wait()


def _issue_rows(copy_of):
    def body(it, carry):
        slot = it & (TOP_K - 1)
        tok0 = pl.multiple_of(lax.shift_right_logical(it, TOP_K_SHIFT) * DMA_UNROLL, DMA_UNROLL)
        for u in range(DMA_UNROLL):
            copy_of(slot, tok0, u).start(priority=u % 2)
        return carry
    lax.fori_loop(0, DMA_CHUNK // DMA_UNROLL, body, 0)


def _dispatch_kernel(idx_ref, x_ref, zero_ref, dst_ref, sem):
    del zero_ref

    def copy_of(slot, tok0, u):
        j = idx_ref[0, 0, (tok0 + u) * TOP_K + slot]
        return pltpu.make_async_copy(x_ref.at[_tok_rows(tok0 + u)], dst_ref.at[_tok_rows(j)], sem)

    _issue_rows(copy_of)
    _chunk_wait(dst_ref, sem)


def moe_dispatch(dest, h2, n_dst):
    n = dest.shape[0]
    nc = n // DMA_CHUNK
    rows = DMA_CHUNK // TOP_K * ROW_SUB
    shape = (n_dst * ROW_SUB, LANES)
    return pl.pallas_call(
        _dispatch_kernel,
        out_shape=jax.ShapeDtypeStruct(shape, h2.dtype),
        grid=(nc,),
        in_specs=[pl.BlockSpec((1, 1, DMA_CHUNK), lambda c: (c, 0, 0), memory_space=pltpu.SMEM),
                  pl.BlockSpec((rows, LANES), lambda c: (c, 0)),
                  pl.BlockSpec(memory_space=pl.ANY)],
        out_specs=pl.BlockSpec(memory_space=pl.ANY),
        scratch_shapes=[pltpu.SemaphoreType.DMA(())],
        input_output_aliases={2: 0},
        compiler_params=_cp(("arbitrary",)),
        name="moe_dispatch",
    )(dest.reshape(nc, 1, DMA_CHUNK), h2, jnp.zeros(shape, h2.dtype))


def _load_tiled(ref, n_tok, first_tok=0):
    return jnp.concatenate([ref[pl.ds(first_tok * ROW_SUB + s, n_tok, stride=ROW_SUB), :] for s in range(ROW_SUB)],
                           axis=1)


def _store_tiled(ref, x):
    for s in range(ROW_SUB):
        ref[pl.ds(s, x.shape[0], stride=ROW_SUB), :] = x[:, s * LANES:(s + 1) * LANES]


def _moe_kernel(te_ref, nv_ref, x_ref, w1g_ref, w1u_ref, b1g_ref, b1u_ref, w2_ref, b2_ref, o_ref):
    @pl.when(pl.program_id(0) < nv_ref[0])
    def _():
        x = _load_tiled(x_ref, MOE_TILE).astype(BF16)
        gate = jnp.dot(x, w1g_ref[0], preferred_element_type=F32) + b1g_ref[0]
        up = jnp.dot(x, w1u_ref[0], preferred_element_type=F32) + b1u_ref[0]
        gate = jnp.minimum(gate, SWIGLU_LIMIT)
        up = jnp.clip(up, -SWIGLU_LIMIT, SWIGLU_LIMIT)
        act = (up + 1.0) * (gate * _sigmoid(SWIGLU_ALPHA * gate))
        _store_tiled(o_ref, jnp.dot(act.astype(BF16), w2_ref[0], preferred_element_type=F32) + b2_ref[0])

    @pl.when(pl.program_id(0) >= nv_ref[0])
    def _():
        o_ref[...] = jnp.zeros_like(o_ref)


def moe_experts(xs, tile_expert, n_valid, w1g, w1u, b1g, b1u, w2, b2):
    tm = MOE_TILE * ROW_SUB
    d, ff = w1g.shape[1:]
    n_tiles = xs.shape[0] // tm
    rows = lambda i, te, nv: (jnp.minimum(i, nv[0] - 1), 0)
    wsel = lambda i, te, nv: (te[i], 0, 0)
    return pl.pallas_call(
        _moe_kernel,
        out_shape=jax.ShapeDtypeStruct(xs.shape, F32),
        grid_spec=pltpu.PrefetchScalarGridSpec(
            num_scalar_prefetch=2,
            grid=(n_tiles,),
            in_specs=[pl.BlockSpec((tm, LANES), rows),
                      pl.BlockSpec((1, d, ff), wsel), pl.BlockSpec((1, d, ff), wsel),
                      pl.BlockSpec((1, 1, ff), wsel), pl.BlockSpec((1, 1, ff), wsel),
                      pl.BlockSpec((1, ff, d), wsel), pl.BlockSpec((1, 1, d), wsel)],
            out_specs=pl.BlockSpec((tm, LANES), lambda i, te, nv: (i, 0))),
        compiler_params=_cp(("arbitrary",)),
        name="moe_experts",
    )(tile_expert, n_valid, xs, w1g, w1u, b1g, b1u, w2, b2)


def _combine_kernel(idx_ref, nxt_ref, x_ref, rows_ref, tw_ref, g2_ref, y_ref, buf_ref, sems):
    i = pl.program_id(0)
    slot = i % 2
    tm = x_ref.shape[0]

    def gather(ids_ref, s):
        def copy_of(slot, tok0, u):
            j = ids_ref[0, 0, (tok0 + u) * TOP_K + slot]
            return pltpu.make_async_copy(rows_ref.at[_tok_rows(j)], buf_ref.at[s, _tok_rows(slot * tm + tok0 + u)],
                                         sems.at[s])
        _issue_rows(copy_of)

    @pl.when(i == 0)
    def _():
        gather(idx_ref, 0)

    @pl.when(i + 1 < pl.num_programs(0))
    def _():
        gather(nxt_ref, 1 - slot)

    _chunk_wait(rows_ref, sems.at[slot])
    tw = tw_ref[...]
    cur = buf_ref.at[slot]
    acc = tw[:, 0:1] * _load_tiled(cur, tm, 0)
    for s in range(1, TOP_K):
        acc = acc + tw[:, s:s + 1] * _load_tiled(cur, tm, s * tm)
    y_ref[...] = x_ref[...] + g2_ref[0] * acc


def moe_combine(x, dest, rows, top_w, mods, *, t_ctx, l_lat):
    t, d = x.shape
    tm = DMA_CHUNK // TOP_K
    nt = t // tm
    ids = dest.reshape(nt, 1, DMA_CHUNK)
    smem = lambda fn: pl.BlockSpec((1, 1, DMA_CHUNK), fn, memory_space=pltpu.SMEM)
    return pl.pallas_call(
        _combine_kernel,
        out_shape=jax.ShapeDtypeStruct((t, d), F32),
        grid=(nt,),
        in_specs=[smem(lambda i: (i, 0, 0)), smem(lambda i: (jnp.minimum(i + 1, nt - 1), 0, 0)),
                  pl.BlockSpec((tm, d), lambda i: (i, 0)), pl.BlockSpec(memory_space=pl.ANY),
                  pl.BlockSpec((tm, LANES), lambda i: (i, 0)), _mod_spec(5, tm, t_ctx, l_lat)],
        out_specs=pl.BlockSpec((tm, d), lambda i: (i, 0)),
        scratch_shapes=[pltpu.VMEM((2, DMA_CHUNK * ROW_SUB, LANES), F32), pltpu.SemaphoreType.DMA((2,))],
        compiler_params=_cp(("arbitrary",)),
        name="moe_combine",
    )(ids, ids, x, rows, top_w, mods)


def _dispatch_plan(top_ir, hist, tm):
    t = top_ir.shape[0]
    n_rt = hist.shape[0]
    hist = hist.astype(jnp.int32)
    counts = jnp.sum(hist, axis=0)
    tiles = (counts + tm - 1) // tm
    tile_end = jnp.cumsum(tiles)
    base = (tile_end - tiles)[None, :] * tm + jnp.cumsum(hist, axis=0) - hist
    e = top_ir[:, :TOP_K].reshape(n_rt, -1)
    rank = top_ir[:, TOP_K:].reshape(n_rt, -1)
    pick = e[:, :, None] == jnp.arange(N_EXPERTS, dtype=jnp.int32)[None, None, :]
    dest = (jnp.sum(jnp.where(pick, base[:, None, :], 0), axis=2) + rank).reshape(-1).astype(jnp.int32)
    n_tiles = (t * TOP_K) // tm + N_EXPERTS
    tile_ids = jnp.arange(n_tiles, dtype=jnp.int32)
    tile_expert = jnp.minimum(jnp.sum((tile_ids[:, None] >= tile_end[None, :]).astype(jnp.int32), axis=1),
                              N_EXPERTS - 1).astype(jnp.int32)
    return dest, n_tiles, tile_expert, tile_end[-1:].astype(jnp.int32)


def kernel(x_prompt, x_sample, c, cache_k, cache_v, state_ssd, c_ctx, norm1_w, norm2_w, w_mod, b_mod, w_in, w_gate,
           b_gate, hy_conv_w, hy_conv_b, hy_w1, hy_b1, hy_w2, hy_b2, hy_w3, hy_freq, hy_decay, hy_bias, ssd_conv_w,
           ssd_conv_b, ssd_a_log, ssd_dt_bias, ssd_d, ssd_norm_w, q_norm_w, k_norm_w, w_br_hy, w_br_ssd, w_br_att,
           w_out, w_router, b_router, w_e1, b_e1, w_e2, b_e2):
    n_ctx, l_ctx, d = x_prompt.shape
    n_lat, l_lat, _ = x_sample.shape
    depth = w_in.shape[0]
    t_ctx, t_lat = n_ctx * l_ctx, n_lat * l_lat
    t = t_ctx + t_lat
    kc = N_KV_HEADS * HEAD_DIM
    geo = dict(t_ctx=t_ctx, l_lat=l_lat)

    x = jnp.concatenate([x_prompt.reshape(t_ctx, d), x_sample.reshape(t_lat, d)], axis=0)
    cvec = jnp.zeros((8, d), F32).at[0].set(c_ctx).at[1:1 + n_lat].set(c)
    mods_all = modulation_all(cvec, w_mod, b_mod)
    n_exp, _, ff2 = w_e1.shape[1:]
    w1g_all, w1u_all = deinterleave_experts(w_e1.reshape(depth * n_exp, d, ff2))
    w1g_all = w1g_all.reshape(depth, n_exp, d, ff2 // 2)
    w1u_all = w1u_all.reshape(depth, n_exp, d, ff2 // 2)

    new_k, new_v, new_s = [], [], []
    for l in range(depth):
        mods = mods_all[l].reshape(8 * 6, 1, d)
        wi = w_in[l]
        w_proj = jnp.concatenate([wi[:, 0:3072], wi[:, 3088:3856], wi[:, 3072:3088],
                                  jnp.zeros((d, P_COLS - 3856), F32)], axis=1).astype(BF16)
        nw1 = norm1_w[l].reshape(1, d)
        proj = norm_mod_matmul(x, nw1, mods, w_proj, jnp.zeros((1, P_COLS), F32), sigmoid=False, out_dtype=F32,
                               **geo)
        gate = norm_mod_matmul(x, nw1, mods, w_gate[l].astype(BF16), b_gate[l].reshape(1, -1), sigmoid=True,
                               out_dtype=BF16, **geo)

        x0, p = hyena_pre(proj, hy_conv_w[l], hy_conv_b[l], t_ctx=t_ctx, l_ctx=l_ctx, l_lat=l_lat)
        hy_args = (hy_w1[l], hy_b1[l], hy_w2[l], hy_b2[l], hy_w3[l], hy_freq[l], hy_decay[l])
        y_hy_ctx = hyena_ctx(x0, p, hyena_filter(l_ctx, *hy_args), hy_bias[l], n_seq=n_ctx, seq_len=l_ctx)
        y_hy_lat = hyena_lat(x0[t_ctx:], p[t_ctx:], hyena_filter(l_lat, *hy_args), hy_bias[l], n_seq=n_lat,
                             seq_len=l_lat)

        ssd_args = (ssd_conv_w[l], ssd_conv_b[l], ssd_dt_bias[l], ssd_a_log[l], ssd_d[l])
        yf_c, yb_c, fin_c = ssd_scan(proj, None, *ssd_args, row0=0, n_seq=n_ctx, seq_len=l_ctx)
        yf_l, yb_l, _ = ssd_scan(proj, state_ssd[:, l], *ssd_args, row0=t_ctx, n_seq=n_lat, seq_len=l_lat)

        qn, kn = qk_prep(proj, q_norm_w[l], k_norm_w[l], **geo)
        v_all = proj[:, P_V:P_V + kc]
        k_ctx = kn[:t_ctx].reshape(n_ctx, l_ctx, kc)
        v_ctx = v_all[:t_ctx].reshape(n_ctx, l_ctx, kc)
        vt = lambda v: jnp.swapaxes(v, 1, 2).astype(BF16)
        att_ctx = attention(qn, k_ctx.astype(BF16), vt(v_ctx), row0=0, seq_len=l_ctx, tq=l_ctx)
        k_lat = jnp.concatenate([kn[t_ctx:].reshape(n_lat, l_lat, kc), cache_k[:, l].reshape(n_lat, -1, kc)], axis=1)
        v_lat = jnp.concatenate([v_all[t_ctx:].reshape(n_lat, l_lat, kc), cache_v[:, l].reshape(n_lat, -1, kc)],
                                axis=1)
        att_lat = attention(qn, k_lat.astype(BF16), vt(v_lat), row0=t_ctx, seq_len=l_lat, tq=128)

        wr = jnp.pad(w_router[l], ((0, 0), (0, LANES - N_EXPERTS)))
        br = jnp.pad(b_router[l], (0, LANES - N_EXPERTS)).reshape(1, LANES)
        x, h2, top_ir, top_w, hist = merge_router(
            x, (y_hy_ctx, y_hy_lat), (yf_c, yf_l), (yb_c, yb_l), proj, (att_ctx, att_lat), gate, ssd_norm_w[l],
            w_br_hy[l].astype(BF16), w_br_ssd[l].astype(BF16),
            w_br_att[l].astype(BF16), w_out[l].astype(BF16), mods, norm2_w[l].reshape(1, d), wr, br, **geo)

        dest, n_tiles, tile_expert, n_valid = _dispatch_plan(top_ir[:, :2 * TOP_K], hist[::8, :N_EXPERTS], MOE_TILE)
        xs = moe_dispatch(dest, h2, n_tiles * MOE_TILE)
        out_sorted = moe_experts(
            xs, tile_expert, n_valid, w1g_all[l], w1u_all[l],
            b_e1[l][:, None, 0::2], b_e1[l][:, None, 1::2], w_e2[l].astype(BF16), b_e2[l][:, None, :])
        x = moe_combine(x, dest, out_sorted, top_w, mods, **geo)

        new_k.append(k_ctx.reshape(n_ctx, l_ctx, N_KV_HEADS, HEAD_DIM))
        new_v.append(v_ctx.reshape(n_ctx, l_ctx, N_KV_HEADS, HEAD_DIM))
        new_s.append(fin_c)

    y_prompt = x[:t_ctx].reshape(n_ctx, l_ctx, d)
    y_sample = x[t_ctx:].reshape(n_lat, l_lat, d)
    return (y_prompt, y_sample, jnp.stack(new_k, axis=1), jnp.stack(new_v, axis=1), jnp.stack(new_s, axis=1))
```

```python
import functools
import math

import numpy as np
import jax
import jax.numpy as jnp
from jax import lax
from jax.experimental import pallas as pl
from jax.experimental.pallas import tpu as pltpu

F32 = jnp.float32
BF16 = jnp.bfloat16
HI = lax.Precision.HIGHEST

EPS = 1e-6
GRID_W = 64
HY_DIM = 512
SSD_INNER = 512
SSD_HEADDIM = 64
SSD_HEADS = 8
SSD_GROUPS = 2
SSD_STATE = 128
SSD_CHUNK = 128
N_HEADS = 8
N_KV_HEADS = 2
HEAD_DIM = 64
ROPE_THETA = 10000.0
N_EXPERTS = 32
TOP_K = 4
TOP_K_SHIFT = 2
SWIGLU_ALPHA = 1.702
SWIGLU_LIMIT = 7.0

P_HY, P_Z, P_XBC, P_Q, P_K, P_V, P_DT, P_COLS = 0, 1536, 2048, 3072, 3584, 3712, 3840, 3968

VMEM_LIMIT = 56 * 1024 * 1024
LANES = 128
ROW_TILE = 256
MOE_TILE = 512
DFT_N1, DFT_N2 = 64, 128
STAGE2_K1 = 4


def _cp(sem, vmem=VMEM_LIMIT):
    return pltpu.CompilerParams(dimension_semantics=sem, vmem_limit_bytes=vmem)


def _sigmoid(x):
    return 1.0 / (1.0 + jnp.exp(-x))


def _silu(x):
    return x * _sigmoid(x)


def _softplus(x):
    return jnp.maximum(x, 0.0) + jnp.log(1.0 + jnp.exp(-jnp.abs(x)))


def _dot3(a, b):
    ah = a.astype(BF16)
    al = (a - ah.astype(F32)).astype(BF16)
    bh = b.astype(BF16)
    bl = (b - bh.astype(F32)).astype(BF16)
    dot = lambda u, v: jnp.dot(u, v, preferred_element_type=F32)
    return dot(ah, bh) + (dot(ah, bl) + dot(al, bh))


def _mod_kernel(c_ref, w_ref, b_ref, o_ref):
    s = _silu(c_ref[...])
    o_ref[0] = jnp.dot(s, w_ref[0], precision=HI, preferred_element_type=F32) + b_ref[0]


def modulation_all(cvec, w_mod, b_mod):
    depth, d, n = w_mod.shape
    tn = 1536
    return pl.pallas_call(
        _mod_kernel,
        out_shape=jax.ShapeDtypeStruct((depth, 8, n), F32),
        grid=(depth, n // tn),
        in_specs=[pl.BlockSpec((8, d), lambda l, j: (0, 0)),
                  pl.BlockSpec((1, d, tn), lambda l, j: (l, 0, j)),
                  pl.BlockSpec((1, 1, tn), lambda l, j: (l, 0, j))],
        out_specs=pl.BlockSpec((1, 8, tn), lambda l, j: (l, 0, j)),
        compiler_params=_cp(("arbitrary", "arbitrary")),
        name="modulation",
    )(cvec, w_mod, b_mod.reshape(depth, 1, n))


def _mod_row(i, tm, t_ctx, l_lat):
    n_ctx = t_ctx // tm
    per = l_lat // tm
    return jnp.where(i < n_ctx, 0, 1 + (i - n_ctx) // per)


def _mod_spec(k, tm, t_ctx, l_lat):
    return pl.BlockSpec((1, 1, 1024), lambda i: (_mod_row(i, tm, t_ctx, l_lat) * 6 + k, 0, 0))


def _nmm_kernel(x_ref, nw_ref, sh_ref, sc_ref, w_ref, b_ref, o_ref, *, sigmoid):
    x = x_ref[...]
    ms = jnp.mean(x * x, axis=-1, keepdims=True)
    h = x * lax.rsqrt(ms + EPS) * nw_ref[...]
    h = h * (1.0 + sc_ref[0]) + sh_ref[0]
    acc = jnp.dot(h.astype(BF16), w_ref[...], preferred_element_type=F32) + b_ref[...]
    if sigmoid:
        acc = _sigmoid(acc)
    o_ref[...] = acc.astype(o_ref.dtype)


def norm_mod_matmul(x, nw, mods, w, b, *, t_ctx, l_lat, sigmoid, out_dtype, tm=512):
    t, d = x.shape
    n = w.shape[1]
    return pl.pallas_call(
        functools.partial(_nmm_kernel, sigmoid=sigmoid),
        out_shape=jax.ShapeDtypeStruct((t, n), out_dtype),
        grid=(t // tm,),
        in_specs=[pl.BlockSpec((tm, d), lambda i: (i, 0)),
                  pl.BlockSpec((1, d), lambda i: (0, 0)),
                  _mod_spec(0, tm, t_ctx, l_lat),
                  _mod_spec(1, tm, t_ctx, l_lat),
                  pl.BlockSpec((d, n), lambda i: (0, 0)),
                  pl.BlockSpec((1, n), lambda i: (0, 0))],
        out_specs=pl.BlockSpec((tm, n), lambda i: (i, 0)),
        compiler_params=_cp(("arbitrary",)),
        name="norm_mod_matmul",
    )(x, nw, mods, mods, w, b)


def _seq_edges(i, tr, t_ctx, l_ctx, l_lat):
    tok = i * tr
    pos = jnp.where(tok < t_ctx, tok % l_ctx, (tok - t_ctx) % l_lat)
    length = jnp.where(tok < t_ctx, l_ctx, l_lat)
    return pos == 0, pos + tr == length


def _conv3(x, prev8, next8, w_ref, b_ref, first, last):
    tr = x.shape[0]
    row = lax.broadcasted_iota(jnp.int32, x.shape, 0)
    pm = jnp.where(first, 0.0, 1.0)
    nm = jnp.where(last, 0.0, 1.0)
    xm1 = jnp.where(row == 0, prev8[7:8, :] * pm, pltpu.roll(x, 1, axis=0))
    xp1 = jnp.where(row == tr - 1, next8[0:1, :] * nm, pltpu.roll(x, tr - 1, axis=0))
    return b_ref[...] + xm1 * w_ref[0:1, :] + x * w_ref[1:2, :] + xp1 * w_ref[2:3, :]


def _conv_specs(tr, tc, col_blk, n_rows):
    r8 = tr // 8
    last8 = n_rows // 8 - 1
    return [pl.BlockSpec((tr, tc), lambda i, j: (i, col_blk(j))),
            pl.BlockSpec((8, tc), lambda i, j: (jnp.maximum(i * r8 - 1, 0), col_blk(j))),
            pl.BlockSpec((8, tc), lambda i, j: (jnp.minimum((i + 1) * r8, last8), col_blk(j)))]


def _hy_pre_kernel(*refs, tr, t_ctx, l_ctx, l_lat):
    (x0, x0p, x0n, x1, x1p, x1n, xv, xvp, xvn, w0, w1, wv, b0, b1, bv, o0_ref, op_ref) = refs
    first, last = _seq_edges(pl.program_id(0), tr, t_ctx, l_ctx, l_lat)
    o0_ref[...] = _conv3(x0[...], x0p[...], x0n[...], w0, b0, first, last)
    u1 = _conv3(x1[...], x1p[...], x1n[...], w1, b1, first, last)
    uv = _conv3(xv[...], xvp[...], xvn[...], wv, bv, first, last)
    op_ref[...] = u1 * uv


def hyena_pre(proj, w, b, *, t_ctx, l_ctx, l_lat):
    t = proj.shape[0]
    tr, tc = ROW_TILE, HY_DIM
    nb = HY_DIM // tc
    b2 = b.reshape(1, 3 * HY_DIM)
    specs = []
    for s in range(3):
        specs += _conv_specs(tr, tc, lambda j, s=s: P_HY // tc + s * nb + j, t)
    specs += [pl.BlockSpec((3, tc), lambda i, j, s=s: (0, s * nb + j)) for s in range(3)]
    specs += [pl.BlockSpec((1, tc), lambda i, j, s=s: (0, s * nb + j)) for s in range(3)]
    return pl.pallas_call(
        functools.partial(_hy_pre_kernel, tr=tr, t_ctx=t_ctx, l_ctx=l_ctx, l_lat=l_lat),
        out_shape=(jax.ShapeDtypeStruct((t, HY_DIM), F32), jax.ShapeDtypeStruct((t, HY_DIM), F32)),
        grid=(t // tr, nb),
        in_specs=specs,
        out_specs=(pl.BlockSpec((tr, tc), lambda i, j: (i, j)), pl.BlockSpec((tr, tc), lambda i, j: (i, j))),
        compiler_params=_cp(("arbitrary", "arbitrary")),
        name="hyena_pre",
    )(*([proj] * 9), w, w, w, b2, b2, b2)


def _filter_kernel(z_ref, w1_ref, b1_ref, w2_ref, b2_ref, w3_ref, fr_ref, dec_ref, o_ref, *, zero_row):
    z = z_ref[...]
    fr = fr_ref[...]
    h = jnp.sin(fr * (jnp.dot(z, w1_ref[...], precision=HI, preferred_element_type=F32) + b1_ref[...]))
    h = jnp.sin(fr * (jnp.dot(h, w2_ref[...], precision=HI, preferred_element_type=F32) + b2_ref[...]))
    f = jnp.dot(h, w3_ref[...], precision=HI, preferred_element_type=F32)
    f = f * jnp.exp(-z[:, 0:1] * jnp.abs(dec_ref[...]))
    row = pl.program_id(0) * f.shape[0] + lax.broadcasted_iota(jnp.int32, f.shape, 0)
    o_ref[...] = jnp.where(row == zero_row, 0.0, f)


def _filter_embedding(seq_len, emb):
    bands_n = (emb - 1) // 2
    t = jnp.linspace(0.0, 1.0, seq_len, dtype=F32)[:, None]
    bands = jnp.linspace(1e-4, bands_n - 1, bands_n, dtype=F32)[None, :]
    ang = (2.0 * math.pi / seq_len) * jnp.arange(seq_len, dtype=F32)[:, None] * bands
    z = jnp.concatenate([t, jnp.cos(ang), -jnp.sin(ang)], axis=-1)
    return jnp.pad(z, ((0, 0), (0, LANES - emb)))


def hyena_filter(seq_len, w1, b1, w2, b2, w3, freq, decay):
    emb, ff = w1.shape
    ch = w3.shape[1] // 2
    lags = np.concatenate([np.arange(seq_len), [0], np.arange(seq_len - 1, 0, -1)])
    z = _filter_embedding(seq_len, emb)[lags]
    padc = LANES - ff
    w1p = jnp.pad(w1, ((0, LANES - emb), (0, padc)))
    w2p = jnp.pad(w2, ((0, padc), (0, padc)))
    w3p = jnp.pad(w3, ((0, padc), (0, 0)))
    row = lambda v: jnp.pad(v, (0, padc)).reshape(1, LANES)
    tr = 256
    fwd_tiles = seq_len // tr
    full = lambda shape: pl.BlockSpec(shape, lambda i: (0, 0))
    half = lambda shape: pl.BlockSpec(shape, lambda i: (0, jnp.where(i < fwd_tiles, 0, 1)))
    return pl.pallas_call(
        functools.partial(_filter_kernel, zero_row=seq_len),
        out_shape=jax.ShapeDtypeStruct((2 * seq_len, ch), F32),
        grid=(2 * seq_len // tr,),
        in_specs=[pl.BlockSpec((tr, LANES), lambda i: (i, 0)), full((LANES, LANES)), full((1, LANES)),
                  full((LANES, LANES)), full((1, LANES)), half((LANES, ch)), full((1, LANES)), half((1, ch))],
        out_specs=pl.BlockSpec((tr, ch), lambda i: (i, 0)),
        compiler_params=_cp(("arbitrary",)),
        name="hyena_filter",
    )(z, w1p, row(b1), w2p, row(b2), w3p, row(freq), decay.reshape(1, 2 * ch))


def _cs(n_rows, n_cols, period):
    ang = 2.0 * np.pi * (np.outer(np.arange(n_rows), np.arange(n_cols)) % period) / period
    return np.cos(ang), np.sin(ang)


def _mm_kernel(a_ref, b_ref, o_ref):
    o_ref[0] = _dot3(a_ref[...], b_ref[0])


def const_matmul(a, b, tn):
    m, k = a.shape
    bsz, _, n = b.shape
    return pl.pallas_call(
        _mm_kernel,
        out_shape=jax.ShapeDtypeStruct((bsz, m, n), F32),
        grid=(bsz, n // tn),
        in_specs=[pl.BlockSpec((m, k), lambda s, j: (0, 0)), pl.BlockSpec((1, k, tn), lambda s, j: (s, 0, j))],
        out_specs=pl.BlockSpec((1, m, tn), lambda s, j: (s, 0, j)),
        compiler_params=_cp(("arbitrary", "arbitrary")),
        name="const_matmul",
    )(a, b)


def _hy_ctx_kernel(x0_ref, p_ref, kr_ref, ki_ref, fw_ref, iv_ref, bias_ref, o_ref, *, n):
    p = p_ref[...]
    xf = _dot3(fw_ref[...], p)
    xr, xi = xf[:n], xf[n:]
    kr, ki = kr_ref[...], ki_ref[...]
    yr = xr * kr - xi * ki
    yi = xr * ki + xi * kr
    y = _dot3(iv_ref[...], jnp.concatenate([yr, yi], axis=0))
    o_ref[...] = x0_ref[...] * (y + p * bias_ref[...])


def hyena_ctx(x0, p, filt, bias, *, n_seq, seq_len):
    n = 2 * seq_len
    ch = p.shape[1]
    c_full, s_full = _cs(n, n, n)
    fw_full = jnp.asarray(np.concatenate([c_full, -s_full], axis=0), F32)
    fw_half = fw_full[:, :seq_len]
    iv = jnp.asarray(np.concatenate([c_full[:seq_len], -s_full[:seq_len]], axis=1) / n, F32)
    kf = const_matmul(fw_full, filt[None], ch)[0]
    kr, ki = kf[:n], kf[n:]
    full = lambda shape: pl.BlockSpec(shape, lambda s: (0, 0))
    return pl.pallas_call(
        functools.partial(_hy_ctx_kernel, n=n),
        out_shape=jax.ShapeDtypeStruct((n_seq * seq_len, ch), F32),
        grid=(n_seq,),
        in_specs=[pl.BlockSpec((seq_len, ch), lambda s: (s, 0)), pl.BlockSpec((seq_len, ch), lambda s: (s, 0)),
                  full((n, ch)), full((n, ch)), full((2 * n, seq_len)), full((seq_len, 2 * n)), full((1, ch))],
        out_specs=pl.BlockSpec((seq_len, ch), lambda s: (s, 0)),
        compiler_params=_cp(("arbitrary",)),
        name="hyena_ctx",
    )(x0, p, kr, ki, fw_half, iv, bias.reshape(1, ch))


def _stage2_kernel(a_ref, twr_ref, twi_ref, m_ref, *rest, conv):
    if conv:
        kf_ref, mi_ref, o_ref = rest
    else:
        (o_ref,) = rest
    for j in range(STAGE2_K1):
        ar, ai = a_ref[0, 0, j], a_ref[0, 1, j]
        twr, twi = twr_ref[j], twi_ref[j]
        br = ar * twr - ai * twi
        bi = ar * twi + ai * twr
        x = _dot3(m_ref[...], jnp.concatenate([br, bi], axis=0))
        n2 = ar.shape[0]
        xr, xi = x[:n2], x[n2:]
        if not conv:
            o_ref[0, 0, j] = xr
            o_ref[0, 1, j] = xi
            continue
        kr, ki = kf_ref[0, 0, j], kf_ref[0, 1, j]
        yr = xr * kr - xi * ki
        yi = xr * ki + xi * kr
        pq = _dot3(mi_ref[...], jnp.concatenate([yr, yi], axis=0))
        pr, pi = pq[:n2], pq[n2:]
        o_ref[0, 0, j] = pr * twr + pi * twi
        o_ref[0, 1, j] = pi * twr - pr * twi


def _stage2(a, kf, ch):
    bsz = a.shape[0]
    n1, n2 = DFT_N1, DFT_N2
    n = n1 * n2
    tw_ang = 2.0 * np.pi * np.outer(np.arange(n1), np.arange(n2)) / n
    twr = jnp.asarray(np.cos(tw_ang), F32).reshape(n1, n2, 1)
    twi = jnp.asarray(-np.sin(tw_ang), F32).reshape(n1, n2, 1)
    c2, s2 = _cs(n2, n2, n2)
    m_fwd = jnp.asarray(np.block([[c2, s2], [-s2, c2]]), F32)
    m_inv = jnp.asarray(np.block([[c2, -s2], [s2, c2]]), F32)
    conv = kf is not None
    kb = STAGE2_K1
    blk = pl.BlockSpec((1, 2, kb, n2, ch), lambda s, k: (s, 0, k, 0, 0))
    tw_spec = pl.BlockSpec((kb, n2, 1), lambda s, k: (k, 0, 0))
    specs = [blk, tw_spec, tw_spec, pl.BlockSpec((2 * n2, 2 * n2), lambda s, k: (0, 0))]
    args = [a, twr, twi, m_fwd]
    if conv:
        specs += [pl.BlockSpec((1, 2, kb, n2, ch), lambda s, k: (0, 0, k, 0, 0)),
                  pl.BlockSpec((2 * n2, 2 * n2), lambda s, k: (0, 0))]
        args += [kf, m_inv]
    return pl.pallas_call(
        functools.partial(_stage2_kernel, conv=conv),
        out_shape=jax.ShapeDtypeStruct(a.shape, F32),
        grid=(bsz, n1 // kb),
        in_specs=specs,
        out_specs=blk,
        compiler_params=_cp(("arbitrary", "arbitrary")),
        name="hyena_stage2",
    )(*args)


def _hy_post_kernel(g_ref, q_ref, x0_ref, p_ref, bias_ref, o_ref):
    y = _dot3(g_ref[...], q_ref[0])
    o_ref[0] = x0_ref[0] * (y + p_ref[0] * bias_ref[...])


def hyena_lat(x0, p, filt, bias, *, n_seq, seq_len):
    n1, n2 = DFT_N1, DFT_N2
    n = n1 * n2
    assert n == 2 * seq_len
    ch = p.shape[1]
    h1 = n1 // 2
    wide = n2 * ch
    c1, s1 = _cs(n1, n1, n1)
    f1_full = jnp.asarray(np.concatenate([c1, -s1], axis=0), F32)
    f1_half = f1_full[:, :h1]
    g1 = jnp.asarray(np.concatenate([c1[:h1], -s1[:h1]], axis=1) / n, F32)
    tn = 8192
    kcirc = filt.reshape(1, n1, wide)
    kf = _stage2(const_matmul(f1_full, kcirc, tn).reshape(1, 2, n1, n2, ch), None, ch)
    a = const_matmul(f1_half, p.reshape(n_seq, h1, wide), tn).reshape(n_seq, 2, n1, n2, ch)
    q = _stage2(a, kf, ch).reshape(n_seq, 2 * n1, wide)
    bias_w = jnp.tile(bias, n2).reshape(1, wide)
    out = pl.pallas_call(
        _hy_post_kernel,
        out_shape=jax.ShapeDtypeStruct((n_seq, h1, wide), F32),
        grid=(n_seq, wide // tn),
        in_specs=[pl.BlockSpec((h1, 2 * n1), lambda s, j: (0, 0)),
                  pl.BlockSpec((1, 2 * n1, tn), lambda s, j: (s, 0, j)),
                  pl.BlockSpec((1, h1, tn), lambda s, j: (s, 0, j)),
                  pl.BlockSpec((1, h1, tn), lambda s, j: (s, 0, j)),
                  pl.BlockSpec((1, tn), lambda s, j: (0, j))],
        out_specs=pl.BlockSpec((1, h1, tn), lambda s, j: (s, 0, j)),
        compiler_params=_cp(("arbitrary", "arbitrary")),
        name="hyena_post",
    )(g1, q, x0.reshape(n_seq, h1, wide), p.reshape(n_seq, h1, wide), bias_w)
    return out.reshape(n_seq * seq_len, ch)


def _dot_01(x, m, *, x_is_lhs):
    x1 = x.astype(BF16)
    r1 = x - x1.astype(F32)
    x2 = r1.astype(BF16)
    x3 = (r1 - x2.astype(F32)).astype(BF16)
    mb = m.astype(BF16)
    if x_is_lhs:
        dot = lambda u: jnp.dot(u, mb, preferred_element_type=F32)
    else:
        dot = lambda u: jnp.dot(mb, u, preferred_element_type=F32)
    return dot(x1) + (dot(x2) + dot(x3))


def _ssd_dir(xbc, dtraw, st_ref, d, consts, y_ref):
    tri, expand, dtb, acont, dskip = consts
    cl = xbc.shape[0]
    xs = xbc[:, :SSD_INNER]
    dt = _softplus(dtraw + dtb)
    a = dt * acont
    tri_d = tri if d == 0 else tri.T
    cs = _dot_01(a, tri_d, x_is_lhs=False)
    cs_t = _dot_01(a.T, tri_d.T, x_is_lhs=True)
    dt_x = _dot_01(dt, expand[d], x_is_lhs=True)
    xdt = xs * dt_x
    row = lax.broadcasted_iota(jnp.int32, (cl, cl), 0)
    col = lax.broadcasted_iota(jnp.int32, (cl, cl), 1)
    keep = (col <= row) if d == 0 else (col >= row)
    lane = lax.broadcasted_iota(jnp.int32, (cl, LANES), 1)
    low = lane < SSD_HEADDIM
    edge = cl - 1 if d == 0 else 0
    for g in range(SSD_GROUPS):
        bg = xbc[:, SSD_INNER + g * SSD_STATE: SSD_INNER + (g + 1) * SSD_STATE]
        cg = xbc[:, SSD_INNER + (SSD_GROUPS + g) * SSD_STATE: SSD_INNER + (SSD_GROUPS + g + 1) * SSD_STATE]
        bg16, cg16 = bg.astype(BF16), cg.astype(BF16)
        cb = lax.dot_general(cg16, bg16, (((1,), (1,)), ((), ())), preferred_element_type=F32)
        for pr in range(2):
            pair = g * 2 + pr
            h0 = 2 * pair
            ms = []
            for h in (h0, h0 + 1):
                ln = d * SSD_HEADS + h
                diff = cs[:, ln:ln + 1] - cs_t[ln:ln + 1, :]
                ms.append(jnp.where(keep, cb * jnp.exp(jnp.minimum(diff, 0.0)), 0.0).astype(BF16))
            xp = xdt[:, pair * LANES:(pair + 1) * LANES]
            xs_p = xs[:, pair * LANES:(pair + 1) * LANES]
            xlo = jnp.where(low, xp, 0.0).astype(BF16)
            xhi = jnp.where(low, 0.0, xp).astype(BF16)
            y_diag = (jnp.dot(ms[0], xlo, preferred_element_type=F32)
                      + jnp.dot(ms[1], xhi, preferred_element_type=F32))
            l0, l1 = d * SSD_HEADS + h0, d * SSD_HEADS + h0 + 1
            e_cs = jnp.where(low, jnp.exp(cs[:, l0:l0 + 1]), jnp.exp(cs[:, l1:l1 + 1]))
            st = st_ref[pair]
            y_off = jnp.dot(cg16, st.astype(BF16), preferred_element_type=F32) * e_cs
            y_ref[:, pair * LANES:(pair + 1) * LANES] = (
                y_diag + y_off + xs_p * dskip[d:d + 1, pair * LANES:(pair + 1) * LANES])
            tot0, tot1 = cs[edge:edge + 1, l0:l0 + 1], cs[edge:edge + 1, l1:l1 + 1]
            dec = jnp.where(low, jnp.exp(tot0 - cs[:, l0:l0 + 1]), jnp.exp(tot1 - cs[:, l1:l1 + 1]))
            upd = lax.dot_general(bg16, (xp * dec).astype(BF16), (((0,), (0,)), ((), ())),
                                  preferred_element_type=F32)
            st_ref[pair] = st * jnp.where(low[0:1], jnp.exp(tot0), jnp.exp(tot1)) + upd


def _ssd_kernel(xf_ref, xfp_ref, xfn_ref, dtf_ref, xb_ref, xbp_ref, xbn_ref, dtb_ref, init_ref, cw_ref, cb_ref,
                tri_ref, exp_ref, dtbias_ref, acont_ref, dskip_ref, yf_ref, yb_ref, fin_ref, st_ref, *, has_init):
    s = pl.program_id(1)
    at_start, at_end = s == 0, s == pl.num_programs(1) - 1

    @pl.when(at_start)
    def _():
        if has_init:
            st_ref[...] = init_ref[0]
        else:
            st_ref[...] = jnp.zeros_like(st_ref)

    consts = (tri_ref[...], (exp_ref[0], exp_ref[1]), dtbias_ref[...], acont_ref[...], dskip_ref[...])
    xbc_f = _silu(_conv3(xf_ref[...], xfp_ref[...], xfn_ref[...], cw_ref, cb_ref, at_start, at_end))
    _ssd_dir(xbc_f, dtf_ref[...], st_ref.at[0], 0, consts, yf_ref)
    xbc_b = _silu(_conv3(xb_ref[...], xbp_ref[...], xbn_ref[...], cw_ref, cb_ref, at_end, at_start))
    _ssd_dir(xbc_b, dtb_ref[...], st_ref.at[1], 1, consts, yb_ref)

    @pl.when(s == pl.num_programs(1) - 1)
    def _():
        fin_ref[0] = st_ref[...]


def _pair_states(s):
    b = s.shape[0]
    s = s.reshape(b, 2, SSD_HEADS // 2, 2, SSD_HEADDIM, SSD_STATE)
    return s.transpose(0, 1, 2, 5, 3, 4).reshape(b, 2, SSD_HEADS // 2, SSD_STATE, 2 * SSD_HEADDIM)


def _unpair_states(s):
    b = s.shape[0]
    s = s.reshape(b, 2, SSD_HEADS // 2, SSD_STATE, 2, SSD_HEADDIM)
    return s.transpose(0, 1, 2, 4, 5, 3).reshape(b, 2, SSD_HEADS, SSD_HEADDIM, SSD_STATE)


def ssd_scan(proj, init, conv_w, conv_b, dt_bias, a_log, d_skip, *, row0, n_seq, seq_len):
    cl = SSD_CHUNK
    nc = seq_len // cl
    base = row0 // cl
    has_init = init is not None
    hp = SSD_HEADS // 2
    init_p = _pair_states(init) if has_init else jnp.zeros((1, 2, hp, SSD_STATE, LANES), F32)
    tri = jnp.asarray(np.tril(np.ones((cl, cl))), F32)
    expand = np.zeros((2, LANES, SSD_INNER), np.float32)
    for d in range(2):
        for h in range(SSD_HEADS):
            expand[d, d * SSD_HEADS + h, h * SSD_HEADDIM:(h + 1) * SSD_HEADDIM] = 1.0
    pad16 = lambda v: jnp.pad(v.reshape(1, 2 * SSD_HEADS), ((0, 0), (0, LANES - 2 * SSD_HEADS)))
    acont = pad16(-jnp.exp(a_log))
    dtb = pad16(dt_bias)
    dskip = jnp.repeat(d_skip, SSD_HEADDIM, axis=1)
    cxbc = conv_w.shape[1]
    xblk = P_XBC // cxbc
    dtblk = P_DT // LANES
    r8 = cl // 8
    last8 = proj.shape[0] // 8 - 1
    full2 = lambda shape: pl.BlockSpec(shape, lambda b, s: (0,) * len(shape))
    fwd = lambda b, s: base + b * nc + s
    bwd = lambda b, s: base + b * nc + (nc - 1 - s)

    def stream(chunk):
        return [pl.BlockSpec((cl, cxbc), lambda b, s: (chunk(b, s), xblk)),
                pl.BlockSpec((8, cxbc), lambda b, s: (jnp.maximum(chunk(b, s) * r8 - 1, 0), xblk)),
                pl.BlockSpec((8, cxbc), lambda b, s: (jnp.minimum((chunk(b, s) + 1) * r8, last8), xblk)),
                pl.BlockSpec((cl, LANES), lambda b, s: (chunk(b, s), dtblk))]

    st_spec = pl.BlockSpec((1, 2, hp, SSD_STATE, LANES), lambda b, s: (b if has_init else 0, 0, 0, 0, 0))
    yf, yb, fin = pl.pallas_call(
        functools.partial(_ssd_kernel, has_init=has_init),
        out_shape=(jax.ShapeDtypeStruct((n_seq * seq_len, SSD_INNER), F32),
                   jax.ShapeDtypeStruct((n_seq * seq_len, SSD_INNER), F32),
                   jax.ShapeDtypeStruct((n_seq, 2, hp, SSD_STATE, LANES), F32)),
        grid=(n_seq, nc),
        in_specs=stream(fwd) + stream(bwd) + [
            st_spec, full2((3, cxbc)), full2((1, cxbc)), full2((cl, cl)), full2((2, LANES, SSD_INNER)),
            full2((1, LANES)), full2((1, LANES)), full2((2, SSD_INNER))],
        out_specs=(pl.BlockSpec((cl, SSD_INNER), lambda b, s: (b * nc + s, 0)),
                   pl.BlockSpec((cl, SSD_INNER), lambda b, s: (b * nc + (nc - 1 - s), 0)),
                   pl.BlockSpec((1, 2, hp, SSD_STATE, LANES), lambda b, s: (b, 0, 0, 0, 0))),
        scratch_shapes=[pltpu.VMEM((2, hp, SSD_STATE, LANES), F32)],
        compiler_params=_cp(("arbitrary", "arbitrary")),
        name="ssd_scan",
    )(*([proj] * 8), init_p, conv_w, conv_b.reshape(1, cxbc), tri, jnp.asarray(expand), dtb, acont, dskip)
    return yf, yb, _unpair_states(fin)


def _headnorm(x, g_ref, w_ref):
    ms = _dot3(x * x, g_ref[...])
    return x * lax.rsqrt(ms + EPS) * w_ref[...]


def _rope(x, cos, sin_signed):
    lane = lax.broadcasted_iota(jnp.int32, x.shape, 1)
    w = x.shape[1]
    swapped = jnp.where(lane % 2 == 0, pltpu.roll(x, w - 1, axis=1), pltpu.roll(x, 1, axis=1))
    return x * cos + swapped * sin_signed


Q_SCALE = HEAD_DIM ** -0.5 * math.log2(math.e)


def _store_padded_heads(qo_ref, q):
    rep = N_HEADS // N_KV_HEADS
    lane = lax.broadcasted_iota(jnp.int32, (q.shape[0], LANES), 1)
    for h in range(N_HEADS):
        g = h // rep
        chunk = q[:, (h // 2) * LANES:(h // 2 + 1) * LANES]
        if h % 2 != g:
            chunk = pltpu.roll(chunk, HEAD_DIM, axis=1)
        keep = (lane >= g * HEAD_DIM) & (lane < (g + 1) * HEAD_DIM)
        qo_ref[:, h * LANES:(h + 1) * LANES] = jnp.where(keep, chunk, 0.0).astype(qo_ref.dtype)


def _qk_kernel(q_ref, k_ref, cos_ref, sin_ref, gq_ref, gk_ref, qw_ref, kw_ref, qo_ref, ko_ref, *, n_ctx_tiles):
    q = _headnorm(q_ref[...], gq_ref, qw_ref)
    k = _headnorm(k_ref[...], gk_ref, kw_ref)
    is_lat = pl.program_id(0) >= n_ctx_tiles

    @pl.when(is_lat)
    def _():
        cos, sin = cos_ref[...], sin_ref[...]
        _store_padded_heads(qo_ref, _rope(q, cos, sin) * Q_SCALE)
        ko_ref[...] = _rope(k, cos[:, :k.shape[1]], sin[:, :k.shape[1]])

    @pl.when(jnp.logical_not(is_lat))
    def _():
        _store_padded_heads(qo_ref, q * Q_SCALE)
        ko_ref[...] = k


def _rope_tables(seq_len):
    n_rows = seq_len // GRID_W
    row = jnp.repeat(jnp.arange(n_rows), GRID_W).astype(F32)
    col = jnp.tile(jnp.arange(GRID_W), n_rows).astype(F32)
    n_freq = HEAD_DIM // 4
    inv = ROPE_THETA ** (-jnp.arange(n_freq, dtype=F32) / n_freq)
    ang = jnp.concatenate([row[:, None] * inv, col[:, None] * inv], axis=-1)
    cos = jnp.repeat(jnp.cos(ang), 2, axis=1)
    sin = jnp.repeat(jnp.sin(ang), 2, axis=1) * jnp.tile(jnp.asarray([-1.0, 1.0], F32), HEAD_DIM // 2)
    return jnp.tile(cos, (1, N_HEADS)), jnp.tile(sin, (1, N_HEADS))


def qk_prep(proj, q_norm_w, k_norm_w, *, t_ctx, l_lat):
    t = proj.shape[0]
    tr = ROW_TILE
    qc, kc = N_HEADS * HEAD_DIM, N_KV_HEADS * HEAD_DIM
    cos, sin = _rope_tables(l_lat)
    group = lambda c: jnp.asarray(np.kron(np.eye(c // HEAD_DIM), np.ones((HEAD_DIM, HEAD_DIM))) / HEAD_DIM, F32)
    n_ctx_tiles = t_ctx // tr
    per = l_lat // tr
    tab = pl.BlockSpec((tr, qc), lambda i: (jnp.maximum(i - n_ctx_tiles, 0) % per, 0))
    full = lambda shape: pl.BlockSpec(shape, lambda i: (0, 0))
    return pl.pallas_call(
        functools.partial(_qk_kernel, n_ctx_tiles=n_ctx_tiles),
        out_shape=(jax.ShapeDtypeStruct((t, N_HEADS * LANES), BF16), jax.ShapeDtypeStruct((t, kc), F32)),
        grid=(t // tr,),
        in_specs=[pl.BlockSpec((tr, qc), lambda i: (i, P_Q // qc)), pl.BlockSpec((tr, kc), lambda i: (i, P_K // kc)),
                  tab, tab, full((qc, qc)), full((kc, kc)), full((1, qc)), full((1, kc))],
        out_specs=(pl.BlockSpec((tr, N_HEADS * LANES), lambda i: (i, 0)), pl.BlockSpec((tr, kc), lambda i: (i, 0))),
        compiler_params=_cp(("arbitrary",)),
        name="qk_prep",
    )(proj, proj, cos, sin, group(qc), group(kc), jnp.tile(q_norm_w, N_HEADS).reshape(1, qc),
      jnp.tile(k_norm_w, N_KV_HEADS).reshape(1, kc))


def _attention_kernel(q_ref, k_ref, vt_ref, o_ref, *, tq):
    rep = N_HEADS // N_KV_HEADS
    k = k_ref[0]
    vt = vt_ref[0]
    for g in range(N_KV_HEADS):
        qs = jnp.concatenate([q_ref[:, (g * rep + r) * LANES:(g * rep + r + 1) * LANES] for r in range(rep)],
                             axis=0)
        s = lax.dot_general(k, qs, (((1,), (1,)), ((), ())), preferred_element_type=F32)
        p = jnp.exp2(s - jnp.max(s, axis=0, keepdims=True))
        l = jnp.sum(p, axis=0, keepdims=True)
        out = (jnp.dot(vt, p.astype(BF16), preferred_element_type=F32) / l).T
        for r in range(rep):
            h = g * rep + r
            o_ref[:, h * HEAD_DIM:(h + 1) * HEAD_DIM] = (
                out[r * tq:(r + 1) * tq, g * HEAD_DIM:(g + 1) * HEAD_DIM].astype(o_ref.dtype))


def attention(q, k, vt, *, row0, seq_len, tq):
    qc = q.shape[1]
    b, lk, kc = k.shape
    per = seq_len // tq
    base = row0 // tq
    return pl.pallas_call(
        functools.partial(_attention_kernel, tq=tq),
        out_shape=jax.ShapeDtypeStruct((b * seq_len, N_HEADS * HEAD_DIM), BF16),
        grid=(b, per),
        in_specs=[pl.BlockSpec((tq, qc), lambda s, i: (base + s * per + i, 0)),
                  pl.BlockSpec((1, lk, kc), lambda s, i: (s, 0, 0)),
                  pl.BlockSpec((1, kc, lk), lambda s, i: (s, 0, 0))],
        out_specs=pl.BlockSpec((tq, N_HEADS * HEAD_DIM), lambda s, i: (s * per + i, 0)),
        compiler_params=_cp(("arbitrary", "arbitrary")),
        name="attention",
    )(q, k, vt)


def _merge_kernel(x_ref, yhy_c, yhy_l, yf_c, yf_l, yb_c, yb_l, z_ref, yatt_c, yatt_l, gate_ref, snw_ref, whb_ref,
                  wsb_ref, wab_ref, wout_ref, g1_ref, n2w_ref, sh2_ref, sc2_ref, wr_ref, br_ref, ltri_ref,
                  xo_ref, h2_ref, ti_ref, tw_ref, hist_ref, *, n_ctx_tiles):
    d = x_ref.shape[1]
    is_ctx = pl.program_id(0) < n_ctx_tiles
    pick = lambda c_ref, l_ref: jnp.where(is_ctx, c_ref[...], l_ref[...])
    ys = (pick(yf_c, yf_l) + pick(yb_c, yb_l)) * _silu(z_ref[...])
    ys = ys * lax.rsqrt(jnp.mean(ys * ys, axis=-1, keepdims=True) + EPS) * snw_ref[...]
    gate = gate_ref[...].astype(F32)
    merged = (gate[:, :d] * jnp.dot(pick(yhy_c, yhy_l).astype(BF16), whb_ref[...], preferred_element_type=F32)
              + gate[:, d:2 * d] * jnp.dot(ys.astype(BF16), wsb_ref[...], preferred_element_type=F32)
              + gate[:, 2 * d:] * jnp.dot(pick(yatt_c, yatt_l), wab_ref[...], preferred_element_type=F32))
    mix = jnp.dot(merged.astype(BF16), wout_ref[...], preferred_element_type=F32)
    x = x_ref[...] + g1_ref[0] * mix
    xo_ref[...] = x
    h = x * lax.rsqrt(jnp.mean(x * x, axis=-1, keepdims=True) + EPS) * n2w_ref[...]
    h = h * (1.0 + sc2_ref[0]) + sh2_ref[0]
    _store_tiled(h2_ref, h)
    logits = _dot3(h, wr_ref[...]) + br_ref[...]
    lane = lax.broadcasted_iota(jnp.int32, logits.shape, 1)
    work = jnp.where(lane < N_EXPERTS, logits, -jnp.inf)
    idx_out = jnp.zeros(logits.shape, jnp.int32)
    val_out = jnp.full(logits.shape, -jnp.inf, F32)
    picks = []
    for j in range(TOP_K):
        mx = jnp.max(work, axis=-1, keepdims=True)
        am = jnp.min(jnp.where(work == mx, lane, LANES), axis=-1, keepdims=True)
        idx_out = jnp.where(lane == j, am, idx_out)
        val_out = jnp.where(lane == j, mx, val_out)
        picks.append(lane == am)
        work = jnp.where(picks[-1], -jnp.inf, work)
    e = jnp.exp(val_out - jnp.max(val_out, axis=-1, keepdims=True))
    tw_ref[...] = e / jnp.sum(e, axis=-1, keepdims=True)
    chosen = jnp.where(picks[0] | picks[1] | picks[2] | picks[3], 1.0, 0.0)
    before = jnp.dot(ltri_ref[...], chosen.astype(BF16), preferred_element_type=F32)
    rank_out = jnp.zeros(logits.shape, F32)
    for j in range(TOP_K):
        rank_j = jnp.sum(jnp.where(picks[j], before, 0.0), axis=-1, keepdims=True)
        rank_out = jnp.where(lane == TOP_K + j, rank_j, rank_out)
    ti_ref[...] = idx_out + rank_out.astype(jnp.int32)
    hist_ref[...] = jnp.broadcast_to(jnp.sum(chosen, axis=0, keepdims=True), hist_ref.shape)


def merge_router(x, y_hy, yf, yb, proj, y_att, gate, ssd_norm_w, whb, wsb, wab, wout, mods, n2w, wr, br,
                 *, t_ctx, l_lat, tm=512):
    t, d = x.shape
    nct = t_ctx // tm
    nlt = (t - t_ctx) // tm
    row = lambda tc, c0=0: pl.BlockSpec((tm, tc), lambda i: (i, c0))
    full = lambda shape: pl.BlockSpec(shape, lambda i: (0, 0))
    pair = lambda tc: [pl.BlockSpec((tm, tc), lambda i: (jnp.minimum(i, nct - 1), 0)),
                       pl.BlockSpec((tm, tc), lambda i: (jnp.clip(i - nct, 0, nlt - 1), 0))]
    ltri = jnp.asarray(np.tril(np.ones((tm, tm)), -1), BF16)
    return pl.pallas_call(
        functools.partial(_merge_kernel, n_ctx_tiles=nct),
        out_shape=(jax.ShapeDtypeStruct((t, d), F32), jax.ShapeDtypeStruct((t * ROW_SUB, LANES), F32),
                   jax.ShapeDtypeStruct((t, LANES), jnp.int32), jax.ShapeDtypeStruct((t, LANES), F32),
                   jax.ShapeDtypeStruct((t // tm * 8, LANES), F32)),
        grid=(t // tm,),
        in_specs=[row(d)] + pair(HY_DIM) + pair(SSD_INNER) + pair(SSD_INNER) + [row(SSD_INNER, P_Z // SSD_INNER)]
                 + pair(N_HEADS * HEAD_DIM) + [row(3 * d), full((1, SSD_INNER)),
                  full(whb.shape), full(wsb.shape), full(wab.shape), full(wout.shape),
                  _mod_spec(2, tm, t_ctx, l_lat), full((1, d)), _mod_spec(3, tm, t_ctx, l_lat),
                  _mod_spec(4, tm, t_ctx, l_lat), full((d, LANES)), full((1, LANES)), full((tm, tm))],
        out_specs=(row(d), pl.BlockSpec((tm * ROW_SUB, LANES), lambda i: (i, 0)), row(LANES), row(LANES),
                   pl.BlockSpec((8, LANES), lambda i: (i, 0))),
        compiler_params=_cp(("arbitrary",)),
        name="merge_router",
    )(x, *y_hy, *yf, *yb, proj, *y_att, gate, ssd_norm_w.reshape(1, -1), whb, wsb, wab, wout, mods, n2w, mods, mods,
      wr, br, ltri)


DEINT_BLOCK = 256


def _deinterleave(w_ref, perm_ref, g_ref, u_ref):
    half = DEINT_BLOCK // 2
    for blk in range(w_ref.shape[2] // DEINT_BLOCK):
        wb = w_ref[0, :, blk * DEINT_BLOCK:(blk + 1) * DEINT_BLOCK].astype(BF16)
        r = jnp.dot(wb, perm_ref[...], preferred_element_type=F32)
        g_ref[:, blk * half:(blk + 1) * half] = r[:, :half].astype(BF16)
        u_ref[:, blk * half:(blk + 1) * half] = r[:, half:].astype(BF16)


def _deinterleave_perm():
    perm = np.zeros((DEINT_BLOCK, DEINT_BLOCK), np.float32)
    half = DEINT_BLOCK // 2
    perm[2 * np.arange(half), np.arange(half)] = 1.0
    perm[2 * np.arange(half) + 1, half + np.arange(half)] = 1.0
    return jnp.asarray(perm, BF16)


DMA_CHUNK = 512
DMA_UNROLL = 8
ROW_SUB = 8


def _tok_rows(t):
    return pl.ds(pl.multiple_of(t * ROW_SUB, ROW_SUB), ROW_SUB)


def _chunk_wait(ref, sem):
    n = DMA_CHUNK * ROW_SUB
    pltpu.make_async_copy(ref.at[pl.ds(0, n)], ref.at[pl.ds(0, n)], sem).wait()


def _issue_rows(copy_of):
    def body(it, carry):
        slot = it & (TOP_K - 1)
        tok0 = pl.multiple_of(lax.shift_right_logical(it, TOP_K_SHIFT) * DMA_UNROLL, DMA_UNROLL)
        for u in range(DMA_UNROLL):
            copy_of(slot, tok0, u).start(priority=u % 2)
        return carry
    lax.fori_loop(0, DMA_CHUNK // DMA_UNROLL, body, 0)


def _dispatch_kernel(idx_ref, x_ref, zero_ref, dst_ref, sem):
    del zero_ref

    def copy_of(slot, tok0, u):
        j = idx_ref[0, 0, (tok0 + u) * TOP_K + slot]
        return pltpu.make_async_copy(x_ref.at[_tok_rows(tok0 + u)], dst_ref.at[_tok_rows(j)], sem)

    _issue_rows(copy_of)
    _chunk_wait(dst_ref, sem)


def moe_dispatch(dest, h2, n_dst):
    n = dest.shape[0]
    nc = n // DMA_CHUNK
    rows = DMA_CHUNK // TOP_K * ROW_SUB
    shape = (n_dst * ROW_SUB, LANES)
    return pl.pallas_call(
        _dispatch_kernel,
        out_shape=jax.ShapeDtypeStruct(shape, h2.dtype),
        grid=(nc,),
        in_specs=[pl.BlockSpec((1, 1, DMA_CHUNK), lambda c: (c, 0, 0), memory_space=pltpu.SMEM),
                  pl.BlockSpec((rows, LANES), lambda c: (c, 0)),
                  pl.BlockSpec(memory_space=pl.ANY)],
        out_specs=pl.BlockSpec(memory_space=pl.ANY),
        scratch_shapes=[pltpu.SemaphoreType.DMA(())],
        input_output_aliases={2: 0},
        compiler_params=_cp(("arbitrary",)),
        name="moe_dispatch",
    )(dest.reshape(nc, 1, DMA_CHUNK), h2, jnp.zeros(shape, h2.dtype))


def _load_tiled(ref, n_tok, first_tok=0):
    return jnp.concatenate([ref[pl.ds(first_tok * ROW_SUB + s, n_tok, stride=ROW_SUB), :] for s in range(ROW_SUB)],
                           axis=1)


def _store_tiled(ref, x):
    for s in range(ROW_SUB):
        ref[pl.ds(s, x.shape[0], stride=ROW_SUB), :] = x[:, s * LANES:(s + 1) * LANES]


def _moe_kernel(te_ref, nv_ref, x_ref, w1_ref, perm_ref, b1g_ref, b1u_ref, w2_ref, b2_ref, o_ref,
                w1g_s, w1u_s, w2_s):
    i = pl.program_id(0)
    valid = i < nv_ref[0]
    new_expert = jnp.logical_or(i == 0, te_ref[i] != te_ref[jnp.maximum(i - 1, 0)])

    @pl.when(jnp.logical_and(valid, new_expert))
    def _():
        _deinterleave(w1_ref, perm_ref, w1g_s, w1u_s)
        w2_s[...] = w2_ref[0].astype(BF16)

    @pl.when(valid)
    def _():
        x = _load_tiled(x_ref, MOE_TILE).astype(BF16)
        gate = jnp.dot(x, w1g_s[...], preferred_element_type=F32) + b1g_ref[0]
        up = jnp.dot(x, w1u_s[...], preferred_element_type=F32) + b1u_ref[0]
        gate = jnp.minimum(gate, SWIGLU_LIMIT)
        up = jnp.clip(up, -SWIGLU_LIMIT, SWIGLU_LIMIT)
        act = (up + 1.0) * (gate * _sigmoid(SWIGLU_ALPHA * gate))
        _store_tiled(o_ref, jnp.dot(act.astype(BF16), w2_s[...], preferred_element_type=F32) + b2_ref[0])

    @pl.when(jnp.logical_not(valid))
    def _():
        o_ref[...] = jnp.zeros_like(o_ref)


def moe_experts(xs, tile_expert, n_valid, w1, b1g, b1u, w2, b2, *, layer):
    tm = MOE_TILE * ROW_SUB
    d, ff2 = w1.shape[1:]
    ff = ff2 // 2
    n_tiles = xs.shape[0] // tm
    rows = lambda i, te, nv: (jnp.minimum(i, nv[0] - 1), 0)
    wsel = lambda i, te, nv: (layer * N_EXPERTS + te[i], 0, 0)
    return pl.pallas_call(
        _moe_kernel,
        out_shape=jax.ShapeDtypeStruct(xs.shape, F32),
        grid_spec=pltpu.PrefetchScalarGridSpec(
            num_scalar_prefetch=2,
            grid=(n_tiles,),
            in_specs=[pl.BlockSpec((tm, LANES), rows),
                      pl.BlockSpec((1, d, ff2), wsel),
                      pl.BlockSpec((DEINT_BLOCK, DEINT_BLOCK), lambda i, te, nv: (0, 0)),
                      pl.BlockSpec((1, 1, ff), wsel), pl.BlockSpec((1, 1, ff), wsel),
                      pl.BlockSpec((1, ff, d), wsel), pl.BlockSpec((1, 1, d), wsel)],
            out_specs=pl.BlockSpec((tm, LANES), lambda i, te, nv: (i, 0)),
            scratch_shapes=[pltpu.VMEM((d, ff), BF16), pltpu.VMEM((d, ff), BF16), pltpu.VMEM((ff, d), BF16)]),
        compiler_params=_cp(("arbitrary",)),
        name="moe_experts",
    )(tile_expert, n_valid, xs, w1, _deinterleave_perm(), b1g, b1u, w2, b2)


def _combine_kernel(idx_ref, nxt_ref, x_ref, rows_ref, tw_ref, g2_ref, y_ref, buf_ref, sems):
    i = pl.program_id(0)
    slot = i % 2
    tm = x_ref.shape[0]

    def gather(ids_ref, s):
        def copy_of(slot, tok0, u):
            j = ids_ref[0, 0, (tok0 + u) * TOP_K + slot]
            return pltpu.make_async_copy(rows_ref.at[_tok_rows(j)], buf_ref.at[s, _tok_rows(slot * tm + tok0 + u)],
                                         sems.at[s])
        _issue_rows(copy_of)

    @pl.when(i == 0)
    def _():
        gather(idx_ref, 0)

    @pl.when(i + 1 < pl.num_programs(0))
    def _():
        gather(nxt_ref, 1 - slot)

    _chunk_wait(rows_ref, sems.at[slot])
    tw = tw_ref[...]
    cur = buf_ref.at[slot]
    acc = tw[:, 0:1] * _load_tiled(cur, tm, 0)
    for s in range(1, TOP_K):
        acc = acc + tw[:, s:s + 1] * _load_tiled(cur, tm, s * tm)
    y_ref[...] = x_ref[...] + g2_ref[0] * acc


def moe_combine(x, dest, rows, top_w, mods, *, t_ctx, l_lat):
    t, d = x.shape
    tm = DMA_CHUNK // TOP_K
    nt = t // tm
    ids = dest.reshape(nt, 1, DMA_CHUNK)
    smem = lambda fn: pl.BlockSpec((1, 1, DMA_CHUNK), fn, memory_space=pltpu.SMEM)
    return pl.pallas_call(
        _combine_kernel,
        out_shape=jax.ShapeDtypeStruct((t, d), F32),
        grid=(nt,),
        in_specs=[smem(lambda i: (i, 0, 0)), smem(lambda i: (jnp.minimum(i + 1, nt - 1), 0, 0)),
                  pl.BlockSpec((tm, d), lambda i: (i, 0)), pl.BlockSpec(memory_space=pl.ANY),
                  pl.BlockSpec((tm, LANES), lambda i: (i, 0)), _mod_spec(5, tm, t_ctx, l_lat)],
        out_specs=pl.BlockSpec((tm, d), lambda i: (i, 0)),
        scratch_shapes=[pltpu.VMEM((2, DMA_CHUNK * ROW_SUB, LANES), F32), pltpu.SemaphoreType.DMA((2,))],
        compiler_params=_cp(("arbitrary",)),
        name="moe_combine",
    )(ids, ids, x, rows, top_w, mods)


def _dispatch_plan(top_ir, hist, tm):
    t = top_ir.shape[0]
    n_rt = hist.shape[0]
    hist = hist.astype(jnp.int32)
    counts = jnp.sum(hist, axis=0)
    tiles = (counts + tm - 1) // tm
    tile_end = jnp.cumsum(tiles)
    base = (tile_end - tiles)[None, :] * tm + jnp.cumsum(hist, axis=0) - hist
    e = top_ir[:, :TOP_K].reshape(n_rt, -1)
    rank = top_ir[:, TOP_K:].reshape(n_rt, -1)
    pick = e[:, :, None] == jnp.arange(N_EXPERTS, dtype=jnp.int32)[None, None, :]
    dest = (jnp.sum(jnp.where(pick, base[:, None, :], 0), axis=2) + rank).reshape(-1).astype(jnp.int32)
    n_tiles = (t * TOP_K) // tm + N_EXPERTS
    tile_ids = jnp.arange(n_tiles, dtype=jnp.int32)
    tile_expert = jnp.minimum(jnp.sum((tile_ids[:, None] >= tile_end[None, :]).astype(jnp.int32), axis=1),
                              N_EXPERTS - 1).astype(jnp.int32)
    return dest, n_tiles, tile_expert, tile_end[-1:].astype(jnp.int32)


def kernel(x_prompt, x_sample, c, cache_k, cache_v, state_ssd, c_ctx, norm1_w, norm2_w, w_mod, b_mod, w_in, w_gate,
           b_gate, hy_conv_w, hy_conv_b, hy_w1, hy_b1, hy_w2, hy_b2, hy_w3, hy_freq, hy_decay, hy_bias, ssd_conv_w,
           ssd_conv_b, ssd_a_log, ssd_dt_bias, ssd_d, ssd_norm_w, q_norm_w, k_norm_w, w_br_hy, w_br_ssd, w_br_att,
           w_out, w_router, b_router, w_e1, b_e1, w_e2, b_e2):
    n_ctx, l_ctx, d = x_prompt.shape
    n_lat, l_lat, _ = x_sample.shape
    depth = w_in.shape[0]
    t_ctx, t_lat = n_ctx * l_ctx, n_lat * l_lat
    t = t_ctx + t_lat
    kc = N_KV_HEADS * HEAD_DIM
    geo = dict(t_ctx=t_ctx, l_lat=l_lat)

    x = jnp.concatenate([x_prompt.reshape(t_ctx, d), x_sample.reshape(t_lat, d)], axis=0)
    cvec = jnp.zeros((8, d), F32).at[0].set(c_ctx).at[1:1 + n_lat].set(c)
    mods_all = modulation_all(cvec, w_mod, b_mod)
    b1_flat = b_e1.reshape(-1, 1, b_e1.shape[-1])
    b1g_all, b1u_all = b1_flat[:, :, 0::2], b1_flat[:, :, 1::2]
    new_k, new_v, new_s = [], [], []
    for l in range(depth):
        mods = mods_all[l].reshape(8 * 6, 1, d)
        wi = w_in[l]
        w_proj = jnp.concatenate([wi[:, 0:3072], wi[:, 3088:3856], wi[:, 3072:3088],
                                  jnp.zeros((d, P_COLS - 3856), F32)], axis=1).astype(BF16)
        nw1 = norm1_w[l].reshape(1, d)
        proj = norm_mod_matmul(x, nw1, mods, w_proj, jnp.zeros((1, P_COLS), F32), sigmoid=False, out_dtype=F32,
                               **geo)
        gate = norm_mod_matmul(x, nw1, mods, w_gate[l].astype(BF16), b_gate[l].reshape(1, -1), sigmoid=True,
                               out_dtype=BF16, **geo)

        x0, p = hyena_pre(proj, hy_conv_w[l], hy_conv_b[l], t_ctx=t_ctx, l_ctx=l_ctx, l_lat=l_lat)
        hy_args = (hy_w1[l], hy_b1[l], hy_w2[l], hy_b2[l], hy_w3[l], hy_freq[l], hy_decay[l])
        y_hy_ctx = hyena_ctx(x0, p, hyena_filter(l_ctx, *hy_args), hy_bias[l], n_seq=n_ctx, seq_len=l_ctx)
        y_hy_lat = hyena_lat(x0[t_ctx:], p[t_ctx:], hyena_filter(l_lat, *hy_args), hy_bias[l], n_seq=n_lat,
                             seq_len=l_lat)

        ssd_args = (ssd_conv_w[l], ssd_conv_b[l], ssd_dt_bias[l], ssd_a_log[l], ssd_d[l])
        yf_c, yb_c, fin_c = ssd_scan(proj, None, *ssd_args, row0=0, n_seq=n_ctx, seq_len=l_ctx)
        yf_l, yb_l, _ = ssd_scan(proj, state_ssd[:, l], *ssd_args, row0=t_ctx, n_seq=n_lat, seq_len=l_lat)

        qn, kn = qk_prep(proj, q_norm_w[l], k_norm_w[l], **geo)
        v_all = proj[:, P_V:P_V + kc]
        k_ctx = kn[:t_ctx].reshape(n_ctx, l_ctx, kc)
        v_ctx = v_all[:t_ctx].reshape(n_ctx, l_ctx, kc)
        vt = lambda v: jnp.swapaxes(v, 1, 2).astype(BF16)
        att_ctx = attention(qn, k_ctx.astype(BF16), vt(v_ctx), row0=0, seq_len=l_ctx, tq=l_ctx)
        k_lat = jnp.concatenate([kn[t_ctx:].reshape(n_lat, l_lat, kc), cache_k[:, l].reshape(n_lat, -1, kc)], axis=1)
        v_lat = jnp.concatenate([v_all[t_ctx:].reshape(n_lat, l_lat, kc), cache_v[:, l].reshape(n_lat, -1, kc)],
                                axis=1)
        att_lat = attention(qn, k_lat.astype(BF16), vt(v_lat), row0=t_ctx, seq_len=l_lat, tq=128)

        wr = jnp.pad(w_router[l], ((0, 0), (0, LANES - N_EXPERTS)))
        br = jnp.pad(b_router[l], (0, LANES - N_EXPERTS)).reshape(1, LANES)
        x, h2, top_ir, top_w, hist = merge_router(
            x, (y_hy_ctx, y_hy_lat), (yf_c, yf_l), (yb_c, yb_l), proj, (att_ctx, att_lat), gate, ssd_norm_w[l],
            w_br_hy[l].astype(BF16), w_br_ssd[l].astype(BF16),
            w_br_att[l].astype(BF16), w_out[l].astype(BF16), mods, norm2_w[l].reshape(1, d), wr, br, **geo)

        dest, n_tiles, tile_expert, n_valid = _dispatch_plan(top_ir[:, :2 * TOP_K], hist[::8, :N_EXPERTS], MOE_TILE)
        xs = moe_dispatch(dest, h2, n_tiles * MOE_TILE)
        out_sorted = moe_experts(
            xs, tile_expert, n_valid, w_e1.reshape((-1,) + w_e1.shape[2:]), b1g_all, b1u_all,
            w_e2.reshape((-1,) + w_e2.shape[2:]), b_e2.reshape(-1, 1, d), layer=l)
        x = moe_combine(x, dest, out_sorted, top_w, mods, **geo)

        new_k.append(k_ctx.reshape(n_ctx, l_ctx, N_KV_HEADS, HEAD_DIM))
        new_v.append(v_ctx.reshape(n_ctx, l_ctx, N_KV_HEADS, HEAD_DIM))
        new_s.append(fin_c)

    y_prompt = x[:t_ctx].reshape(n_ctx, l_ctx, d)
    y_sample = x[t_ctx:].reshape(n_lat, l_lat, d)
    return (y_prompt, y_sample, jnp.stack(new_k, axis=1), jnp.stack(new_v, axis=1), jnp.stack(new_s, axis=1))
```

```python
import functools
import math

import numpy as np
import jax
import jax.numpy as jnp
from jax import lax
from jax.experimental import pallas as pl
from jax.experimental.pallas import tpu as pltpu

F32 = jnp.float32
BF16 = jnp.bfloat16

EPS = 1e-6
GRID_W = 64
HY_DIM = 512
SSD_INNER = 512
SSD_HEADDIM = 64
SSD_HEADS = 8
SSD_GROUPS = 2
SSD_STATE = 128
SSD_CHUNK = 128
N_HEADS = 8
N_KV_HEADS = 2
HEAD_DIM = 64
ROPE_THETA = 10000.0
N_EXPERTS = 32
TOP_K = 4
TOP_K_SHIFT = 2
SWIGLU_ALPHA = 1.702
SWIGLU_LIMIT = 7.0

P_HY, P_Z, P_XBC, P_Q, P_K, P_V, P_DT, P_COLS = 0, 1536, 2048, 3072, 3584, 3712, 3840, 3968

VMEM_LIMIT = 56 * 1024 * 1024
LANES = 128
ROW_TILE = 256
MOE_TILE = 512
DFT_N1, DFT_N2 = 64, 128
STAGE2_K1 = 4


def _cp(sem, vmem=VMEM_LIMIT):
    return pltpu.CompilerParams(dimension_semantics=sem, vmem_limit_bytes=vmem)


def _sigmoid(x):
    return 1.0 / (1.0 + jnp.exp(-x))


def _silu(x):
    return x * _sigmoid(x)


def _softplus(x):
    return jnp.maximum(x, 0.0) + jnp.log(1.0 + jnp.exp(-jnp.abs(x)))


def _dot3(a, b):
    ah = a.astype(BF16)
    al = (a - ah.astype(F32)).astype(BF16)
    bh = b.astype(BF16)
    bl = (b - bh.astype(F32)).astype(BF16)
    dot = lambda u, v: jnp.dot(u, v, preferred_element_type=F32)
    return dot(ah, bh) + (dot(ah, bl) + dot(al, bh))


def _mod_kernel(c_ref, w_ref, b_ref, o_ref):
    s = _silu(c_ref[...])
    o_ref[0] = _dot3(s, w_ref[0]) + b_ref[0]


def modulation_all(cvec, w_mod, b_mod):
    depth, d, n = w_mod.shape
    tn = 1536
    return pl.pallas_call(
        _mod_kernel,
        out_shape=jax.ShapeDtypeStruct((depth, 8, n), F32),
        grid=(depth, n // tn),
        in_specs=[pl.BlockSpec((8, d), lambda l, j: (0, 0)),
                  pl.BlockSpec((1, d, tn), lambda l, j: (l, 0, j)),
                  pl.BlockSpec((1, 1, tn), lambda l, j: (l, 0, j))],
        out_specs=pl.BlockSpec((1, 8, tn), lambda l, j: (l, 0, j)),
        compiler_params=_cp(("arbitrary", "arbitrary")),
        name="modulation",
    )(cvec, w_mod, b_mod.reshape(depth, 1, n))


def _mod_row(i, tm, t_ctx, l_lat):
    n_ctx = t_ctx // tm
    per = l_lat // tm
    return jnp.where(i < n_ctx, 0, 1 + (i - n_ctx) // per)


def _mod_spec(k, tm, t_ctx, l_lat):
    return pl.BlockSpec((1, 1, 1024), lambda i: (_mod_row(i, tm, t_ctx, l_lat) * 6 + k, 0, 0))


def _nmm_kernel(x_ref, nw_ref, sh_ref, sc_ref, w_ref, b_ref, o_ref, *, sigmoid):
    x = x_ref[...]
    ms = jnp.mean(x * x, axis=-1, keepdims=True)
    h = x * lax.rsqrt(ms + EPS) * nw_ref[...]
    h = h * (1.0 + sc_ref[0]) + sh_ref[0]
    acc = jnp.dot(h.astype(BF16), w_ref[...], preferred_element_type=F32) + b_ref[...]
    if sigmoid:
        acc = _sigmoid(acc)
    o_ref[...] = acc.astype(o_ref.dtype)


def norm_mod_matmul(x, nw, mods, w, b, *, t_ctx, l_lat, sigmoid, out_dtype, tm=512):
    t, d = x.shape
    n = w.shape[1]
    return pl.pallas_call(
        functools.partial(_nmm_kernel, sigmoid=sigmoid),
        out_shape=jax.ShapeDtypeStruct((t, n), out_dtype),
        grid=(t // tm,),
        in_specs=[pl.BlockSpec((tm, d), lambda i: (i, 0)),
                  pl.BlockSpec((1, d), lambda i: (0, 0)),
                  _mod_spec(0, tm, t_ctx, l_lat),
                  _mod_spec(1, tm, t_ctx, l_lat),
                  pl.BlockSpec((d, n), lambda i: (0, 0)),
                  pl.BlockSpec((1, n), lambda i: (0, 0))],
        out_specs=pl.BlockSpec((tm, n), lambda i: (i, 0)),
        compiler_params=_cp(("arbitrary",)),
        name="norm_mod_matmul",
    )(x, nw, mods, mods, w, b)


def _seq_edges(i, tr, t_ctx, l_ctx, l_lat):
    tok = i * tr
    pos = jnp.where(tok < t_ctx, tok % l_ctx, (tok - t_ctx) % l_lat)
    length = jnp.where(tok < t_ctx, l_ctx, l_lat)
    return pos == 0, pos + tr == length


def _conv3(x, prev8, next8, w_ref, b_ref, first, last):
    tr = x.shape[0]
    row = lax.broadcasted_iota(jnp.int32, x.shape, 0)
    pm = jnp.where(first, 0.0, 1.0)
    nm = jnp.where(last, 0.0, 1.0)
    xm1 = jnp.where(row == 0, prev8[7:8, :] * pm, pltpu.roll(x, 1, axis=0))
    xp1 = jnp.where(row == tr - 1, next8[0:1, :] * nm, pltpu.roll(x, tr - 1, axis=0))
    return b_ref[...] + xm1 * w_ref[0:1, :] + x * w_ref[1:2, :] + xp1 * w_ref[2:3, :]


def _conv_specs(tr, tc, col_blk, n_rows):
    r8 = tr // 8
    last8 = n_rows // 8 - 1
    return [pl.BlockSpec((tr, tc), lambda i, j: (i, col_blk(j))),
            pl.BlockSpec((8, tc), lambda i, j: (jnp.maximum(i * r8 - 1, 0), col_blk(j))),
            pl.BlockSpec((8, tc), lambda i, j: (jnp.minimum((i + 1) * r8, last8), col_blk(j)))]


def _hy_pre_kernel(*refs, tr, t_ctx, l_ctx, l_lat):
    (x0, x0p, x0n, x1, x1p, x1n, xv, xvp, xvn, w0, w1, wv, b0, b1, bv, o0_ref, op_ref) = refs
    first, last = _seq_edges(pl.program_id(0), tr, t_ctx, l_ctx, l_lat)
    o0_ref[...] = _conv3(x0[...], x0p[...], x0n[...], w0, b0, first, last)
    u1 = _conv3(x1[...], x1p[...], x1n[...], w1, b1, first, last)
    uv = _conv3(xv[...], xvp[...], xvn[...], wv, bv, first, last)
    op_ref[...] = u1 * uv


def hyena_pre(proj, w, b, *, t_ctx, l_ctx, l_lat):
    t = proj.shape[0]
    tr, tc = ROW_TILE, HY_DIM
    nb = HY_DIM // tc
    b2 = b.reshape(1, 3 * HY_DIM)
    specs = []
    for s in range(3):
        specs += _conv_specs(tr, tc, lambda j, s=s: P_HY // tc + s * nb + j, t)
    specs += [pl.BlockSpec((3, tc), lambda i, j, s=s: (0, s * nb + j)) for s in range(3)]
    specs += [pl.BlockSpec((1, tc), lambda i, j, s=s: (0, s * nb + j)) for s in range(3)]
    return pl.pallas_call(
        functools.partial(_hy_pre_kernel, tr=tr, t_ctx=t_ctx, l_ctx=l_ctx, l_lat=l_lat),
        out_shape=(jax.ShapeDtypeStruct((t, HY_DIM), F32), jax.ShapeDtypeStruct((t, HY_DIM), F32)),
        grid=(t // tr, nb),
        in_specs=specs,
        out_specs=(pl.BlockSpec((tr, tc), lambda i, j: (i, j)), pl.BlockSpec((tr, tc), lambda i, j: (i, j))),
        compiler_params=_cp(("arbitrary", "arbitrary")),
        name="hyena_pre",
    )(*([proj] * 9), w, w, w, b2, b2, b2)


def _filter_kernel(z_ref, w1_ref, b1_ref, w2_ref, b2_ref, w3_ref, fr_ref, dec_ref, o_ref, *, zero_row):
    z = z_ref[...]
    fr = fr_ref[...]
    h = jnp.sin(fr * (_dot3(z, w1_ref[...]) + b1_ref[...]))
    h = jnp.sin(fr * (_dot3(h, w2_ref[...]) + b2_ref[...]))
    f = _dot3(h, w3_ref[...])
    f = f * jnp.exp(-z[:, 0:1] * jnp.abs(dec_ref[...]))
    row = pl.program_id(0) * f.shape[0] + lax.broadcasted_iota(jnp.int32, f.shape, 0)
    o_ref[...] = jnp.where(row == zero_row, 0.0, f)


def _filter_embedding(seq_len, emb):
    bands_n = (emb - 1) // 2
    t = jnp.linspace(0.0, 1.0, seq_len, dtype=F32)[:, None]
    bands = jnp.linspace(1e-4, bands_n - 1, bands_n, dtype=F32)[None, :]
    ang = (2.0 * math.pi / seq_len) * jnp.arange(seq_len, dtype=F32)[:, None] * bands
    z = jnp.concatenate([t, jnp.cos(ang), -jnp.sin(ang)], axis=-1)
    return jnp.pad(z, ((0, 0), (0, LANES - emb)))


def hyena_filter(seq_len, w1, b1, w2, b2, w3, freq, decay):
    emb, ff = w1.shape
    ch = w3.shape[1] // 2
    lags = np.concatenate([np.arange(seq_len), [0], np.arange(seq_len - 1, 0, -1)])
    z = _filter_embedding(seq_len, emb)[lags]
    padc = LANES - ff
    w1p = jnp.pad(w1, ((0, LANES - emb), (0, padc)))
    w2p = jnp.pad(w2, ((0, padc), (0, padc)))
    w3p = jnp.pad(w3, ((0, padc), (0, 0)))
    row = lambda v: jnp.pad(v, (0, padc)).reshape(1, LANES)
    tr = 256
    fwd_tiles = seq_len // tr
    full = lambda shape: pl.BlockSpec(shape, lambda i: (0, 0))
    half = lambda shape: pl.BlockSpec(shape, lambda i: (0, jnp.where(i < fwd_tiles, 0, 1)))
    return pl.pallas_call(
        functools.partial(_filter_kernel, zero_row=seq_len),
        out_shape=jax.ShapeDtypeStruct((2 * seq_len, ch), F32),
        grid=(2 * seq_len // tr,),
        in_specs=[pl.BlockSpec((tr, LANES), lambda i: (i, 0)), full((LANES, LANES)), full((1, LANES)),
                  full((LANES, LANES)), full((1, LANES)), half((LANES, ch)), full((1, LANES)), half((1, ch))],
        out_specs=pl.BlockSpec((tr, ch), lambda i: (i, 0)),
        compiler_params=_cp(("arbitrary",)),
        name="hyena_filter",
    )(z, w1p, row(b1), w2p, row(b2), w3p, row(freq), decay.reshape(1, 2 * ch))


def _cs(n_rows, n_cols, period):
    ang = 2.0 * np.pi * (np.outer(np.arange(n_rows), np.arange(n_cols)) % period) / period
    return np.cos(ang), np.sin(ang)


def _mm_kernel(a_ref, b_ref, o_ref):
    o_ref[0] = _dot3(a_ref[...], b_ref[0])


def const_matmul(a, b, tn):
    m, k = a.shape
    bsz, _, n = b.shape
    return pl.pallas_call(
        _mm_kernel,
        out_shape=jax.ShapeDtypeStruct((bsz, m, n), F32),
        grid=(bsz, n // tn),
        in_specs=[pl.BlockSpec((m, k), lambda s, j: (0, 0)), pl.BlockSpec((1, k, tn), lambda s, j: (s, 0, j))],
        out_specs=pl.BlockSpec((1, m, tn), lambda s, j: (s, 0, j)),
        compiler_params=_cp(("arbitrary", "arbitrary")),
        name="const_matmul",
    )(a, b)


def _hy_ctx_kernel(x0_ref, p_ref, kr_ref, ki_ref, fw_ref, iv_ref, bias_ref, o_ref, *, n):
    p = p_ref[...]
    xf = _dot3(fw_ref[...], p)
    xr, xi = xf[:n], xf[n:]
    kr, ki = kr_ref[...], ki_ref[...]
    yr = xr * kr - xi * ki
    yi = xr * ki + xi * kr
    y = _dot3(iv_ref[...], jnp.concatenate([yr, yi], axis=0))
    o_ref[...] = x0_ref[...] * (y + p * bias_ref[...])


def hyena_ctx(x0, p, filt, bias, *, n_seq, seq_len):
    n = 2 * seq_len
    ch = p.shape[1]
    c_full, s_full = _cs(n, n, n)
    fw_full = jnp.asarray(np.concatenate([c_full, -s_full], axis=0), F32)
    fw_half = fw_full[:, :seq_len]
    iv = jnp.asarray(np.concatenate([c_full[:seq_len], -s_full[:seq_len]], axis=1) / n, F32)
    kf = const_matmul(fw_full, filt[None], ch)[0]
    kr, ki = kf[:n], kf[n:]
    full = lambda shape: pl.BlockSpec(shape, lambda s: (0, 0))
    return pl.pallas_call(
        functools.partial(_hy_ctx_kernel, n=n),
        out_shape=jax.ShapeDtypeStruct((n_seq * seq_len, ch), F32),
        grid=(n_seq,),
        in_specs=[pl.BlockSpec((seq_len, ch), lambda s: (s, 0)), pl.BlockSpec((seq_len, ch), lambda s: (s, 0)),
                  full((n, ch)), full((n, ch)), full((2 * n, seq_len)), full((seq_len, 2 * n)), full((1, ch))],
        out_specs=pl.BlockSpec((seq_len, ch), lambda s: (s, 0)),
        compiler_params=_cp(("arbitrary",)),
        name="hyena_ctx",
    )(x0, p, kr, ki, fw_half, iv, bias.reshape(1, ch))


def _stage2_kernel(a_ref, twr_ref, twi_ref, m_ref, *rest, conv):
    if conv:
        kf_ref, mi_ref, o_ref = rest
    else:
        (o_ref,) = rest
    for j in range(STAGE2_K1):
        ar, ai = a_ref[0, 0, j], a_ref[0, 1, j]
        twr, twi = twr_ref[j], twi_ref[j]
        br = ar * twr - ai * twi
        bi = ar * twi + ai * twr
        x = _dot3(m_ref[...], jnp.concatenate([br, bi], axis=0))
        n2 = ar.shape[0]
        xr, xi = x[:n2], x[n2:]
        if not conv:
            o_ref[0, 0, j] = xr
            o_ref[0, 1, j] = xi
            continue
        kr, ki = kf_ref[0, 0, j], kf_ref[0, 1, j]
        yr = xr * kr - xi * ki
        yi = xr * ki + xi * kr
        pq = _dot3(mi_ref[...], jnp.concatenate([yr, yi], axis=0))
        pr, pi = pq[:n2], pq[n2:]
        o_ref[0, 0, j] = pr * twr + pi * twi
        o_ref[0, 1, j] = pi * twr - pr * twi


def _stage2(a, kf, ch):
    bsz = a.shape[0]
    n1, n2 = DFT_N1, DFT_N2
    n = n1 * n2
    tw_ang = 2.0 * np.pi * np.outer(np.arange(n1), np.arange(n2)) / n
    twr = jnp.asarray(np.cos(tw_ang), F32).reshape(n1, n2, 1)
    twi = jnp.asarray(-np.sin(tw_ang), F32).reshape(n1, n2, 1)
    c2, s2 = _cs(n2, n2, n2)
    m_fwd = jnp.asarray(np.block([[c2, s2], [-s2, c2]]), F32)
    m_inv = jnp.asarray(np.block([[c2, -s2], [s2, c2]]), F32)
    conv = kf is not None
    kb = STAGE2_K1
    blk = pl.BlockSpec((1, 2, kb, n2, ch), lambda s, k: (s, 0, k, 0, 0))
    tw_spec = pl.BlockSpec((kb, n2, 1), lambda s, k: (k, 0, 0))
    specs = [blk, tw_spec, tw_spec, pl.BlockSpec((2 * n2, 2 * n2), lambda s, k: (0, 0))]
    args = [a, twr, twi, m_fwd]
    if conv:
        specs += [pl.BlockSpec((1, 2, kb, n2, ch), lambda s, k: (0, 0, k, 0, 0)),
                  pl.BlockSpec((2 * n2, 2 * n2), lambda s, k: (0, 0))]
        args += [kf, m_inv]
    return pl.pallas_call(
        functools.partial(_stage2_kernel, conv=conv),
        out_shape=jax.ShapeDtypeStruct(a.shape, F32),
        grid=(bsz, n1 // kb),
        in_specs=specs,
        out_specs=blk,
        compiler_params=_cp(("arbitrary", "arbitrary")),
        name="hyena_stage2",
    )(*args)


def _hy_post_kernel(g_ref, q_ref, x0_ref, p_ref, bias_ref, o_ref):
    y = _dot3(g_ref[...], q_ref[0])
    o_ref[0] = x0_ref[0] * (y + p_ref[0] * bias_ref[...])


def hyena_lat(x0, p, filt, bias, *, n_seq, seq_len):
    n1, n2 = DFT_N1, DFT_N2
    n = n1 * n2
    assert n == 2 * seq_len
    ch = p.shape[1]
    h1 = n1 // 2
    wide = n2 * ch
    c1, s1 = _cs(n1, n1, n1)
    f1_full = jnp.asarray(np.concatenate([c1, -s1], axis=0), F32)
    f1_half = f1_full[:, :h1]
    g1 = jnp.asarray(np.concatenate([c1[:h1], -s1[:h1]], axis=1) / n, F32)
    tn = 8192
    kcirc = filt.reshape(1, n1, wide)
    kf = _stage2(const_matmul(f1_full, kcirc, tn).reshape(1, 2, n1, n2, ch), None, ch)
    a = const_matmul(f1_half, p.reshape(n_seq, h1, wide), tn).reshape(n_seq, 2, n1, n2, ch)
    q = _stage2(a, kf, ch).reshape(n_seq, 2 * n1, wide)
    bias_w = jnp.tile(bias, n2).reshape(1, wide)
    out = pl.pallas_call(
        _hy_post_kernel,
        out_shape=jax.ShapeDtypeStruct((n_seq, h1, wide), F32),
        grid=(n_seq, wide // tn),
        in_specs=[pl.BlockSpec((h1, 2 * n1), lambda s, j: (0, 0)),
                  pl.BlockSpec((1, 2 * n1, tn), lambda s, j: (s, 0, j)),
                  pl.BlockSpec((1, h1, tn), lambda s, j: (s, 0, j)),
                  pl.BlockSpec((1, h1, tn), lambda s, j: (s, 0, j)),
                  pl.BlockSpec((1, tn), lambda s, j: (0, j))],
        out_specs=pl.BlockSpec((1, h1, tn), lambda s, j: (s, 0, j)),
        compiler_params=_cp(("arbitrary", "arbitrary")),
        name="hyena_post",
    )(g1, q, x0.reshape(n_seq, h1, wide), p.reshape(n_seq, h1, wide), bias_w)
    return out.reshape(n_seq * seq_len, ch)


def _dot_01(x, m, *, x_is_lhs):
    x1 = x.astype(BF16)
    r1 = x - x1.astype(F32)
    x2 = r1.astype(BF16)
    x3 = (r1 - x2.astype(F32)).astype(BF16)
    mb = m.astype(BF16)
    if x_is_lhs:
        dot = lambda u: jnp.dot(u, mb, preferred_element_type=F32)
    else:
        dot = lambda u: jnp.dot(mb, u, preferred_element_type=F32)
    return dot(x1) + (dot(x2) + dot(x3))


def _ssd_dir(xbc, dtraw, st_ref, d, consts, y_ref):
    tri, expand, dtb, acont, dskip = consts
    cl = xbc.shape[0]
    xs = xbc[:, :SSD_INNER]
    dt = _softplus(dtraw + dtb)
    a = dt * acont
    tri_d = tri if d == 0 else tri.T
    cs = _dot_01(a, tri_d, x_is_lhs=False)
    cs_t = _dot_01(a.T, tri_d.T, x_is_lhs=True)
    dt_x = _dot_01(dt, expand[d], x_is_lhs=True)
    xdt = xs * dt_x
    row = lax.broadcasted_iota(jnp.int32, (cl, cl), 0)
    col = lax.broadcasted_iota(jnp.int32, (cl, cl), 1)
    keep = (col <= row) if d == 0 else (col >= row)
    lane = lax.broadcasted_iota(jnp.int32, (cl, LANES), 1)
    low = lane < SSD_HEADDIM
    edge = cl - 1 if d == 0 else 0
    for g in range(SSD_GROUPS):
        bg = xbc[:, SSD_INNER + g * SSD_STATE: SSD_INNER + (g + 1) * SSD_STATE]
        cg = xbc[:, SSD_INNER + (SSD_GROUPS + g) * SSD_STATE: SSD_INNER + (SSD_GROUPS + g + 1) * SSD_STATE]
        bg16, cg16 = bg.astype(BF16), cg.astype(BF16)
        cb = lax.dot_general(cg16, bg16, (((1,), (1,)), ((), ())), preferred_element_type=F32)
        for pr in range(2):
            pair = g * 2 + pr
            h0 = 2 * pair
            ms = []
            for h in (h0, h0 + 1):
                ln = d * SSD_HEADS + h
                diff = cs[:, ln:ln + 1] - cs_t[ln:ln + 1, :]
                ms.append(jnp.where(keep, cb * jnp.exp(jnp.minimum(diff, 0.0)), 0.0).astype(BF16))
            xp = xdt[:, pair * LANES:(pair + 1) * LANES]
            xs_p = xs[:, pair * LANES:(pair + 1) * LANES]
            xlo = jnp.where(low, xp, 0.0).astype(BF16)
            xhi = jnp.where(low, 0.0, xp).astype(BF16)
            y_diag = (jnp.dot(ms[0], xlo, preferred_element_type=F32)
                      + jnp.dot(ms[1], xhi, preferred_element_type=F32))
            l0, l1 = d * SSD_HEADS + h0, d * SSD_HEADS + h0 + 1
            e_cs = jnp.where(low, jnp.exp(cs[:, l0:l0 + 1]), jnp.exp(cs[:, l1:l1 + 1]))
            st = st_ref[pair]
            y_off = jnp.dot(cg16, st.astype(BF16), preferred_element_type=F32) * e_cs
            y_ref[:, pair * LANES:(pair + 1) * LANES] = (
                y_diag + y_off + xs_p * dskip[d:d + 1, pair * LANES:(pair + 1) * LANES])
            tot0, tot1 = cs[edge:edge + 1, l0:l0 + 1], cs[edge:edge + 1, l1:l1 + 1]
            dec = jnp.where(low, jnp.exp(tot0 - cs[:, l0:l0 + 1]), jnp.exp(tot1 - cs[:, l1:l1 + 1]))
            upd = lax.dot_general(bg16, (xp * dec).astype(BF16), (((0,), (0,)), ((), ())),
                                  preferred_element_type=F32)
            st_ref[pair] = st * jnp.where(low[0:1], jnp.exp(tot0), jnp.exp(tot1)) + upd


def _ssd_kernel(xf_ref, xfp_ref, xfn_ref, dtf_ref, xb_ref, xbp_ref, xbn_ref, dtb_ref, init_ref, cw_ref, cb_ref,
                tri_ref, exp_ref, dtbias_ref, acont_ref, dskip_ref, yf_ref, yb_ref, fin_ref, st_ref, *, has_init):
    s = pl.program_id(1)
    at_start, at_end = s == 0, s == pl.num_programs(1) - 1

    @pl.when(at_start)
    def _():
        if has_init:
            st_ref[...] = init_ref[0]
        else:
            st_ref[...] = jnp.zeros_like(st_ref)

    consts = (tri_ref[...], (exp_ref[0], exp_ref[1]), dtbias_ref[...], acont_ref[...], dskip_ref[...])
    xbc_f = _silu(_conv3(xf_ref[...], xfp_ref[...], xfn_ref[...], cw_ref, cb_ref, at_start, at_end))
    _ssd_dir(xbc_f, dtf_ref[...], st_ref.at[0], 0, consts, yf_ref)
    xbc_b = _silu(_conv3(xb_ref[...], xbp_ref[...], xbn_ref[...], cw_ref, cb_ref, at_end, at_start))
    _ssd_dir(xbc_b, dtb_ref[...], st_ref.at[1], 1, consts, yb_ref)

    @pl.when(s == pl.num_programs(1) - 1)
    def _():
        fin_ref[0] = st_ref[...]


def _pair_states(s):
    b = s.shape[0]
    s = s.reshape(b, 2, SSD_HEADS // 2, 2, SSD_HEADDIM, SSD_STATE)
    return s.transpose(0, 1, 2, 5, 3, 4).reshape(b, 2, SSD_HEADS // 2, SSD_STATE, 2 * SSD_HEADDIM)


def _unpair_states(s):
    b = s.shape[0]
    s = s.reshape(b, 2, SSD_HEADS // 2, SSD_STATE, 2, SSD_HEADDIM)
    return s.transpose(0, 1, 2, 4, 5, 3).reshape(b, 2, SSD_HEADS, SSD_HEADDIM, SSD_STATE)


def ssd_scan(proj, init, conv_w, conv_b, dt_bias, a_log, d_skip, *, row0, n_seq, seq_len):
    cl = SSD_CHUNK
    nc = seq_len // cl
    base = row0 // cl
    has_init = init is not None
    hp = SSD_HEADS // 2
    init_p = _pair_states(init) if has_init else jnp.zeros((1, 2, hp, SSD_STATE, LANES), F32)
    tri = jnp.asarray(np.tril(np.ones((cl, cl))), F32)
    expand = np.zeros((2, LANES, SSD_INNER), np.float32)
    for d in range(2):
        for h in range(SSD_HEADS):
            expand[d, d * SSD_HEADS + h, h * SSD_HEADDIM:(h + 1) * SSD_HEADDIM] = 1.0
    pad16 = lambda v: jnp.pad(v.reshape(1, 2 * SSD_HEADS), ((0, 0), (0, LANES - 2 * SSD_HEADS)))
    acont = pad16(-jnp.exp(a_log))
    dtb = pad16(dt_bias)
    dskip = jnp.repeat(d_skip, SSD_HEADDIM, axis=1)
    cxbc = conv_w.shape[1]
    xblk = P_XBC // cxbc
    dtblk = P_DT // LANES
    r8 = cl // 8
    last8 = proj.shape[0] // 8 - 1
    full2 = lambda shape: pl.BlockSpec(shape, lambda b, s: (0,) * len(shape))
    fwd = lambda b, s: base + b * nc + s
    bwd = lambda b, s: base + b * nc + (nc - 1 - s)

    def stream(chunk):
        return [pl.BlockSpec((cl, cxbc), lambda b, s: (chunk(b, s), xblk)),
                pl.BlockSpec((8, cxbc), lambda b, s: (jnp.maximum(chunk(b, s) * r8 - 1, 0), xblk)),
                pl.BlockSpec((8, cxbc), lambda b, s: (jnp.minimum((chunk(b, s) + 1) * r8, last8), xblk)),
                pl.BlockSpec((cl, LANES), lambda b, s: (chunk(b, s), dtblk))]

    st_spec = pl.BlockSpec((1, 2, hp, SSD_STATE, LANES), lambda b, s: (b if has_init else 0, 0, 0, 0, 0))
    yf, yb, fin = pl.pallas_call(
        functools.partial(_ssd_kernel, has_init=has_init),
        out_shape=(jax.ShapeDtypeStruct((n_seq * seq_len, SSD_INNER), F32),
                   jax.ShapeDtypeStruct((n_seq * seq_len, SSD_INNER), F32),
                   jax.ShapeDtypeStruct((n_seq, 2, hp, SSD_STATE, LANES), F32)),
        grid=(n_seq, nc),
        in_specs=stream(fwd) + stream(bwd) + [
            st_spec, full2((3, cxbc)), full2((1, cxbc)), full2((cl, cl)), full2((2, LANES, SSD_INNER)),
            full2((1, LANES)), full2((1, LANES)), full2((2, SSD_INNER))],
        out_specs=(pl.BlockSpec((cl, SSD_INNER), lambda b, s: (b * nc + s, 0)),
                   pl.BlockSpec((cl, SSD_INNER), lambda b, s: (b * nc + (nc - 1 - s), 0)),
                   pl.BlockSpec((1, 2, hp, SSD_STATE, LANES), lambda b, s: (b, 0, 0, 0, 0))),
        scratch_shapes=[pltpu.VMEM((2, hp, SSD_STATE, LANES), F32)],
        compiler_params=_cp(("arbitrary", "arbitrary")),
        name="ssd_scan",
    )(*([proj] * 8), init_p, conv_w, conv_b.reshape(1, cxbc), tri, jnp.asarray(expand), dtb, acont, dskip)
    return yf, yb, _unpair_states(fin)


def _headnorm(x, g_ref, w_ref):
    ms = _dot3(x * x, g_ref[...])
    return x * lax.rsqrt(ms + EPS) * w_ref[...]


def _rope(x, cos, sin_signed):
    lane = lax.broadcasted_iota(jnp.int32, x.shape, 1)
    w = x.shape[1]
    swapped = jnp.where(lane % 2 == 0, pltpu.roll(x, w - 1, axis=1), pltpu.roll(x, 1, axis=1))
    return x * cos + swapped * sin_signed


Q_SCALE = HEAD_DIM ** -0.5 * math.log2(math.e)


def _store_padded_heads(qo_ref, q):
    rep = N_HEADS // N_KV_HEADS
    lane = lax.broadcasted_iota(jnp.int32, (q.shape[0], LANES), 1)
    for h in range(N_HEADS):
        g = h // rep
        chunk = q[:, (h // 2) * LANES:(h // 2 + 1) * LANES]
        if h % 2 != g:
            chunk = pltpu.roll(chunk, HEAD_DIM, axis=1)
        keep = (lane >= g * HEAD_DIM) & (lane < (g + 1) * HEAD_DIM)
        qo_ref[:, h * LANES:(h + 1) * LANES] = jnp.where(keep, chunk, 0.0).astype(qo_ref.dtype)


def _qk_kernel(q_ref, k_ref, cos_ref, sin_ref, gq_ref, gk_ref, qw_ref, kw_ref, qo_ref, ko_ref, *, n_ctx_tiles):
    q = _headnorm(q_ref[...], gq_ref, qw_ref)
    k = _headnorm(k_ref[...], gk_ref, kw_ref)
    is_lat = pl.program_id(0) >= n_ctx_tiles

    @pl.when(is_lat)
    def _():
        cos, sin = cos_ref[...], sin_ref[...]
        _store_padded_heads(qo_ref, _rope(q, cos, sin) * Q_SCALE)
        ko_ref[...] = _rope(k, cos[:, :k.shape[1]], sin[:, :k.shape[1]])

    @pl.when(jnp.logical_not(is_lat))
    def _():
        _store_padded_heads(qo_ref, q * Q_SCALE)
        ko_ref[...] = k


def _rope_tables(seq_len):
    n_rows = seq_len // GRID_W
    row = jnp.repeat(jnp.arange(n_rows), GRID_W).astype(F32)
    col = jnp.tile(jnp.arange(GRID_W), n_rows).astype(F32)
    n_freq = HEAD_DIM // 4
    inv = ROPE_THETA ** (-jnp.arange(n_freq, dtype=F32) / n_freq)
    ang = jnp.concatenate([row[:, None] * inv, col[:, None] * inv], axis=-1)
    cos = jnp.repeat(jnp.cos(ang), 2, axis=1)
    sin = jnp.repeat(jnp.sin(ang), 2, axis=1) * jnp.tile(jnp.asarray([-1.0, 1.0], F32), HEAD_DIM // 2)
    return jnp.tile(cos, (1, N_HEADS)), jnp.tile(sin, (1, N_HEADS))


def qk_prep(proj, q_norm_w, k_norm_w, *, t_ctx, l_lat):
    t = proj.shape[0]
    tr = ROW_TILE
    qc, kc = N_HEADS * HEAD_DIM, N_KV_HEADS * HEAD_DIM
    cos, sin = _rope_tables(l_lat)
    group = lambda c: jnp.asarray(np.kron(np.eye(c // HEAD_DIM), np.ones((HEAD_DIM, HEAD_DIM))) / HEAD_DIM, F32)
    n_ctx_tiles = t_ctx // tr
    per = l_lat // tr
    tab = pl.BlockSpec((tr, qc), lambda i: (jnp.maximum(i - n_ctx_tiles, 0) % per, 0))
    full = lambda shape: pl.BlockSpec(shape, lambda i: (0, 0))
    return pl.pallas_call(
        functools.partial(_qk_kernel, n_ctx_tiles=n_ctx_tiles),
        out_shape=(jax.ShapeDtypeStruct((t, N_HEADS * LANES), BF16), jax.ShapeDtypeStruct((t, kc), F32)),
        grid=(t // tr,),
        in_specs=[pl.BlockSpec((tr, qc), lambda i: (i, P_Q // qc)), pl.BlockSpec((tr, kc), lambda i: (i, P_K // kc)),
                  tab, tab, full((qc, qc)), full((kc, kc)), full((1, qc)), full((1, kc))],
        out_specs=(pl.BlockSpec((tr, N_HEADS * LANES), lambda i: (i, 0)), pl.BlockSpec((tr, kc), lambda i: (i, 0))),
        compiler_params=_cp(("arbitrary",)),
        name="qk_prep",
    )(proj, proj, cos, sin, group(qc), group(kc), jnp.tile(q_norm_w, N_HEADS).reshape(1, qc),
      jnp.tile(k_norm_w, N_KV_HEADS).reshape(1, kc))


def _attention_kernel(q_ref, k_ref, vt_ref, o_ref, *, tq):
    rep = N_HEADS // N_KV_HEADS
    k = k_ref[0]
    vt = vt_ref[0]
    for g in range(N_KV_HEADS):
        qs = jnp.concatenate([q_ref[:, (g * rep + r) * LANES:(g * rep + r + 1) * LANES] for r in range(rep)],
                             axis=0)
        s = lax.dot_general(k, qs, (((1,), (1,)), ((), ())), preferred_element_type=F32)
        p = jnp.exp2(s - jnp.max(s, axis=0, keepdims=True))
        l = jnp.sum(p, axis=0, keepdims=True)
        out = (jnp.dot(vt, p.astype(BF16), preferred_element_type=F32) / l).T
        for r in range(rep):
            h = g * rep + r
            o_ref[:, h * HEAD_DIM:(h + 1) * HEAD_DIM] = (
                out[r * tq:(r + 1) * tq, g * HEAD_DIM:(g + 1) * HEAD_DIM].astype(o_ref.dtype))


def _values_t(v):
    return jnp.swapaxes(v, 1, 2).astype(BF16)


def attention(q, k, vt, *, row0, seq_len, tq):
    qc = q.shape[1]
    b, lk, kc = k.shape
    per = seq_len // tq
    base = row0 // tq
    return pl.pallas_call(
        functools.partial(_attention_kernel, tq=tq),
        out_shape=jax.ShapeDtypeStruct((b * seq_len, N_HEADS * HEAD_DIM), BF16),
        grid=(b, per),
        in_specs=[pl.BlockSpec((tq, qc), lambda s, i: (base + s * per + i, 0)),
                  pl.BlockSpec((1, lk, kc), lambda s, i: (s, 0, 0)),
                  pl.BlockSpec((1, kc, lk), lambda s, i: (s, 0, 0))],
        out_specs=pl.BlockSpec((tq, N_HEADS * HEAD_DIM), lambda s, i: (s * per + i, 0)),
        compiler_params=_cp(("arbitrary", "arbitrary")),
        name="attention",
    )(q, k, vt)


def _merge_kernel(x_ref, yhy_c, yhy_l, yf_c, yf_l, yb_c, yb_l, z_ref, yatt_c, yatt_l, gate_ref, snw_ref, whb_ref,
                  wsb_ref, wab_ref, wout_ref, g1_ref, n2w_ref, sh2_ref, sc2_ref, wr_ref, br_ref, ltri_ref,
                  xo_ref, h2_ref, ti_ref, tw_ref, hist_ref, *, n_ctx_tiles):
    d = x_ref.shape[1]
    is_ctx = pl.program_id(0) < n_ctx_tiles
    pick = lambda c_ref, l_ref: jnp.where(is_ctx, c_ref[...], l_ref[...])
    ys = (pick(yf_c, yf_l) + pick(yb_c, yb_l)) * _silu(z_ref[...])
    ys = ys * lax.rsqrt(jnp.mean(ys * ys, axis=-1, keepdims=True) + EPS) * snw_ref[...]
    gate = gate_ref[...].astype(F32)
    merged = (gate[:, :d] * jnp.dot(pick(yhy_c, yhy_l).astype(BF16), whb_ref[...], preferred_element_type=F32)
              + gate[:, d:2 * d] * jnp.dot(ys.astype(BF16), wsb_ref[...], preferred_element_type=F32)
              + gate[:, 2 * d:] * jnp.dot(pick(yatt_c, yatt_l), wab_ref[...], preferred_element_type=F32))
    mix = jnp.dot(merged.astype(BF16), wout_ref[...], preferred_element_type=F32)
    x = x_ref[...] + g1_ref[0] * mix
    xo_ref[...] = x
    h = x * lax.rsqrt(jnp.mean(x * x, axis=-1, keepdims=True) + EPS) * n2w_ref[...]
    h = h * (1.0 + sc2_ref[0]) + sh2_ref[0]
    _store_tiled(h2_ref, h)
    logits = _dot3(h, wr_ref[...]) + br_ref[...]
    lane = lax.broadcasted_iota(jnp.int32, logits.shape, 1)
    work = jnp.where(lane < N_EXPERTS, logits, -jnp.inf)
    idx_out = jnp.zeros(logits.shape, jnp.int32)
    val_out = jnp.full(logits.shape, -jnp.inf, F32)
    picks = []
    for j in range(TOP_K):
        mx = jnp.max(work, axis=-1, keepdims=True)
        am = jnp.min(jnp.where(work == mx, lane, LANES), axis=-1, keepdims=True)
        idx_out = jnp.where(lane == j, am, idx_out)
        val_out = jnp.where(lane == j, mx, val_out)
        picks.append(lane == am)
        work = jnp.where(picks[-1], -jnp.inf, work)
    e = jnp.exp(val_out - jnp.max(val_out, axis=-1, keepdims=True))
    tw_ref[...] = e / jnp.sum(e, axis=-1, keepdims=True)
    chosen = jnp.where(picks[0] | picks[1] | picks[2] | picks[3], 1.0, 0.0)
    before = jnp.dot(ltri_ref[...], chosen.astype(BF16), preferred_element_type=F32)
    rank_out = jnp.zeros(logits.shape, F32)
    for j in range(TOP_K):
        rank_j = jnp.sum(jnp.where(picks[j], before, 0.0), axis=-1, keepdims=True)
        rank_out = jnp.where(lane == TOP_K + j, rank_j, rank_out)
    ti_ref[...] = idx_out + rank_out.astype(jnp.int32)
    hist_ref[...] = jnp.broadcast_to(jnp.sum(chosen, axis=0, keepdims=True), hist_ref.shape)


def merge_router(x, y_hy, yf, yb, proj, y_att, gate, ssd_norm_w, whb, wsb, wab, wout, mods, n2w, wr, br,
                 *, t_ctx, l_lat, tm=512):
    t, d = x.shape
    nct = t_ctx // tm
    nlt = (t - t_ctx) // tm
    row = lambda tc, c0=0: pl.BlockSpec((tm, tc), lambda i: (i, c0))
    full = lambda shape: pl.BlockSpec(shape, lambda i: (0, 0))
    pair = lambda tc: [pl.BlockSpec((tm, tc), lambda i: (jnp.minimum(i, nct - 1), 0)),
                       pl.BlockSpec((tm, tc), lambda i: (jnp.clip(i - nct, 0, nlt - 1), 0))]
    ltri = jnp.asarray(np.tril(np.ones((tm, tm)), -1), BF16)
    return pl.pallas_call(
        functools.partial(_merge_kernel, n_ctx_tiles=nct),
        out_shape=(jax.ShapeDtypeStruct((t, d), F32), jax.ShapeDtypeStruct((t * ROW_SUB, LANES), F32),
                   jax.ShapeDtypeStruct((t, LANES), jnp.int32), jax.ShapeDtypeStruct((t, LANES), F32),
                   jax.ShapeDtypeStruct((t // tm * 8, LANES), F32)),
        grid=(t // tm,),
        in_specs=[row(d)] + pair(HY_DIM) + pair(SSD_INNER) + pair(SSD_INNER) + [row(SSD_INNER, P_Z // SSD_INNER)]
                 + pair(N_HEADS * HEAD_DIM) + [row(3 * d), full((1, SSD_INNER)),
                  full(whb.shape), full(wsb.shape), full(wab.shape), full(wout.shape),
                  _mod_spec(2, tm, t_ctx, l_lat), full((1, d)), _mod_spec(3, tm, t_ctx, l_lat),
                  _mod_spec(4, tm, t_ctx, l_lat), full((d, LANES)), full((1, LANES)), full((tm, tm))],
        out_specs=(row(d), pl.BlockSpec((tm * ROW_SUB, LANES), lambda i: (i, 0)), row(LANES), row(LANES),
                   pl.BlockSpec((8, LANES), lambda i: (i, 0))),
        compiler_params=_cp(("arbitrary",)),
        name="merge_router",
    )(x, *y_hy, *yf, *yb, proj, *y_att, gate, ssd_norm_w.reshape(1, -1), whb, wsb, wab, wout, mods, n2w, mods, mods,
      wr, br, ltri)


DEINT_BLOCK = 256


def _deinterleave(w_ref, perm_ref, g_ref, u_ref):
    half = DEINT_BLOCK // 2
    for blk in range(w_ref.shape[2] // DEINT_BLOCK):
        wb = w_ref[0, :, blk * DEINT_BLOCK:(blk + 1) * DEINT_BLOCK].astype(BF16)
        r = jnp.dot(wb, perm_ref[...], preferred_element_type=F32)
        g_ref[:, blk * half:(blk + 1) * half] = r[:, :half].astype(BF16)
        u_ref[:, blk * half:(blk + 1) * half] = r[:, half:].astype(BF16)


def _deinterleave_perm():
    perm = np.zeros((DEINT_BLOCK, DEINT_BLOCK), np.float32)
    half = DEINT_BLOCK // 2
    perm[2 * np.arange(half), np.arange(half)] = 1.0
    perm[2 * np.arange(half) + 1, half + np.arange(half)] = 1.0
    return jnp.asarray(perm, BF16)


DMA_CHUNK = 512
DMA_UNROLL = 8
ROW_SUB = 8


def _tok_rows(t):
    return pl.ds(pl.multiple_of(t * ROW_SUB, ROW_SUB), ROW_SUB)


def _chunk_wait(ref, sem):
    n = DMA_CHUNK * ROW_SUB
    pltpu.make_async_copy(ref.at[pl.ds(0, n)], ref.at[pl.ds(0, n)], sem).wait()


def _issue_rows(copy_of):
    def body(it, carry):
        slot = it & (TOP_K - 1)
        tok0 = pl.multiple_of(lax.shift_right_logical(it, TOP_K_SHIFT) * DMA_UNROLL, DMA_UNROLL)
        for u in range(DMA_UNROLL):
            copy_of(slot, tok0, u).start(priority=u % 2)
        return carry
    lax.fori_loop(0, DMA_CHUNK // DMA_UNROLL, body, 0)


def _dispatch_kernel(zf_ref, idx_ref, x_ref, dst_ref, zbuf_ref, sem, zsem):
    tile_rows = zbuf_ref.shape[0]

    @pl.when(pl.program_id(0) == 0)
    def _():
        zbuf_ref[...] = jnp.zeros_like(zbuf_ref)
        fill = lambda i: pltpu.make_async_copy(
            zbuf_ref, dst_ref.at[pl.ds(pl.multiple_of(i * tile_rows, tile_rows), tile_rows)], zsem)

        def start(i, carry):
            @pl.when(zf_ref[i] == 1)
            def _():
                fill(i).start()
            return carry

        def wait(i, carry):
            @pl.when(zf_ref[i] == 1)
            def _():
                fill(i).wait()
            return carry

        lax.fori_loop(0, zf_ref.shape[0], start, 0)
        lax.fori_loop(0, zf_ref.shape[0], wait, 0)

    def copy_of(slot, tok0, u):
        j = idx_ref[0, 0, (tok0 + u) * TOP_K + slot]
        return pltpu.make_async_copy(x_ref.at[_tok_rows(tok0 + u)], dst_ref.at[_tok_rows(j)], sem)

    _issue_rows(copy_of)
    _chunk_wait(dst_ref, sem)


def moe_dispatch(dest, zero_tile, h2, n_dst):
    n = dest.shape[0]
    nc = n // DMA_CHUNK
    rows = DMA_CHUNK // TOP_K * ROW_SUB
    shape = (n_dst * ROW_SUB, LANES)
    return pl.pallas_call(
        _dispatch_kernel,
        out_shape=jax.ShapeDtypeStruct(shape, h2.dtype),
        grid_spec=pltpu.PrefetchScalarGridSpec(
            num_scalar_prefetch=1,
            grid=(nc,),
            in_specs=[pl.BlockSpec((1, 1, DMA_CHUNK), lambda c, zf: (c, 0, 0), memory_space=pltpu.SMEM),
                      pl.BlockSpec((rows, LANES), lambda c, zf: (c, 0))],
            out_specs=pl.BlockSpec(memory_space=pl.ANY),
            scratch_shapes=[pltpu.VMEM((MOE_TILE * ROW_SUB, LANES), h2.dtype), pltpu.SemaphoreType.DMA(()),
                            pltpu.SemaphoreType.DMA(())]),
        compiler_params=_cp(("arbitrary",)),
        name="moe_dispatch",
    )(zero_tile, dest.reshape(nc, 1, DMA_CHUNK), h2)


def _load_tiled(ref, n_tok, first_tok=0):
    return jnp.concatenate([ref[pl.ds(first_tok * ROW_SUB + s, n_tok, stride=ROW_SUB), :] for s in range(ROW_SUB)],
                           axis=1)


def _store_tiled(ref, x):
    for s in range(ROW_SUB):
        ref[pl.ds(s, x.shape[0], stride=ROW_SUB), :] = x[:, s * LANES:(s + 1) * LANES]


def _moe_kernel(te_ref, nv_ref, x_ref, w1_ref, perm_ref, b1g_ref, b1u_ref, w2_ref, b2_ref, o_ref,
                w1g_s, w1u_s, w2_s):
    i = pl.program_id(0)
    valid = i < nv_ref[0]
    new_expert = jnp.logical_or(i == 0, te_ref[i] != te_ref[jnp.maximum(i - 1, 0)])

    @pl.when(jnp.logical_and(valid, new_expert))
    def _():
        _deinterleave(w1_ref, perm_ref, w1g_s, w1u_s)
        w2_s[...] = w2_ref[0].astype(BF16)

    @pl.when(valid)
    def _():
        x = _load_tiled(x_ref, MOE_TILE).astype(BF16)
        gate = jnp.dot(x, w1g_s[...], preferred_element_type=F32) + b1g_ref[0]
        up = jnp.dot(x, w1u_s[...], preferred_element_type=F32) + b1u_ref[0]
        gate = jnp.minimum(gate, SWIGLU_LIMIT)
        up = jnp.clip(up, -SWIGLU_LIMIT, SWIGLU_LIMIT)
        act = (up + 1.0) * (gate * _sigmoid(SWIGLU_ALPHA * gate))
        _store_tiled(o_ref, jnp.dot(act.astype(BF16), w2_s[...], preferred_element_type=F32) + b2_ref[0])

    @pl.when(jnp.logical_not(valid))
    def _():
        o_ref[...] = jnp.zeros_like(o_ref)


def moe_experts(xs, tile_expert, n_valid, w1, b1g, b1u, w2, b2, *, layer):
    tm = MOE_TILE * ROW_SUB
    d, ff2 = w1.shape[1:]
    ff = ff2 // 2
    n_tiles = xs.shape[0] // tm
    rows = lambda i, te, nv: (jnp.minimum(i, nv[0] - 1), 0)
    wsel = lambda i, te, nv: (layer * N_EXPERTS + te[i], 0, 0)
    return pl.pallas_call(
        _moe_kernel,
        out_shape=jax.ShapeDtypeStruct(xs.shape, F32),
        grid_spec=pltpu.PrefetchScalarGridSpec(
            num_scalar_prefetch=2,
            grid=(n_tiles,),
            in_specs=[pl.BlockSpec((tm, LANES), rows),
                      pl.BlockSpec((1, d, ff2), wsel),
                      pl.BlockSpec((DEINT_BLOCK, DEINT_BLOCK), lambda i, te, nv: (0, 0)),
                      pl.BlockSpec((1, 1, ff), wsel), pl.BlockSpec((1, 1, ff), wsel),
                      pl.BlockSpec((1, ff, d), wsel), pl.BlockSpec((1, 1, d), wsel)],
            out_specs=pl.BlockSpec((tm, LANES), lambda i, te, nv: (i, 0)),
            scratch_shapes=[pltpu.VMEM((d, ff), BF16), pltpu.VMEM((d, ff), BF16), pltpu.VMEM((ff, d), BF16)]),
        compiler_params=_cp(("arbitrary",)),
        name="moe_experts",
    )(tile_expert, n_valid, xs, w1, _deinterleave_perm(), b1g, b1u, w2, b2)


def _combine_kernel(idx_ref, nxt_ref, x_ref, rows_ref, tw_ref, g2_ref, y_ref, buf_ref, sems):
    i = pl.program_id(0)
    slot = i % 2
    tm = x_ref.shape[0]

    def gather(ids_ref, s):
        def copy_of(slot, tok0, u):
            j = ids_ref[0, 0, (tok0 + u) * TOP_K + slot]
            return pltpu.make_async_copy(rows_ref.at[_tok_rows(j)], buf_ref.at[s, _tok_rows(slot * tm + tok0 + u)],
                                         sems.at[s])
        _issue_rows(copy_of)

    @pl.when(i == 0)
    def _():
        gather(idx_ref, 0)

    @pl.when(i + 1 < pl.num_programs(0))
    def _():
        gather(nxt_ref, 1 - slot)

    _chunk_wait(rows_ref, sems.at[slot])
    tw = tw_ref[...]
    cur = buf_ref.at[slot]
    acc = tw[:, 0:1] * _load_tiled(cur, tm, 0)
    for s in range(1, TOP_K):
        acc = acc + tw[:, s:s + 1] * _load_tiled(cur, tm, s * tm)
    y_ref[...] = x_ref[...] + g2_ref[0] * acc


def moe_combine(x, dest, rows, top_w, mods, *, t_ctx, l_lat):
    t, d = x.shape
    tm = DMA_CHUNK // TOP_K
    nt = t // tm
    ids = dest.reshape(nt, 1, DMA_CHUNK)
    smem = lambda fn: pl.BlockSpec((1, 1, DMA_CHUNK), fn, memory_space=pltpu.SMEM)
    return pl.pallas_call(
        _combine_kernel,
        out_shape=jax.ShapeDtypeStruct((t, d), F32),
        grid=(nt,),
        in_specs=[smem(lambda i: (i, 0, 0)), smem(lambda i: (jnp.minimum(i + 1, nt - 1), 0, 0)),
                  pl.BlockSpec((tm, d), lambda i: (i, 0)), pl.BlockSpec(memory_space=pl.ANY),
                  pl.BlockSpec((tm, LANES), lambda i: (i, 0)), _mod_spec(5, tm, t_ctx, l_lat)],
        out_specs=pl.BlockSpec((tm, d), lambda i: (i, 0)),
        scratch_shapes=[pltpu.VMEM((2, DMA_CHUNK * ROW_SUB, LANES), F32), pltpu.SemaphoreType.DMA((2,))],
        compiler_params=_cp(("arbitrary",)),
        name="moe_combine",
    )(ids, ids, x, rows, top_w, mods)


def _dispatch_plan(top_ir, hist, tm):
    t = top_ir.shape[0]
    n_rt = hist.shape[0]
    hist = hist.astype(jnp.int32)
    counts = jnp.sum(hist, axis=0)
    tiles = (counts + tm - 1) // tm
    tile_end = jnp.cumsum(tiles)
    base = (tile_end - tiles)[None, :] * tm + jnp.cumsum(hist, axis=0) - hist
    e = top_ir[:, :TOP_K].reshape(n_rt, -1)
    rank = top_ir[:, TOP_K:].reshape(n_rt, -1)
    pick = e[:, :, None] == jnp.arange(N_EXPERTS, dtype=jnp.int32)[None, None, :]
    dest = (jnp.sum(jnp.where(pick, base[:, None, :], 0), axis=2) + rank).reshape(-1).astype(jnp.int32)
    n_tiles = (t * TOP_K) // tm + N_EXPERTS
    tile_ids = jnp.arange(n_tiles, dtype=jnp.int32)
    tile_expert = jnp.minimum(jnp.sum((tile_ids[:, None] >= tile_end[None, :]).astype(jnp.int32), axis=1),
                              N_EXPERTS - 1).astype(jnp.int32)
    n_valid = tile_end[-1]
    partial = jnp.logical_or(tile_ids + 1 == tile_end[tile_expert], tile_ids >= n_valid)
    return dest, n_tiles, tile_expert, n_valid.reshape(1).astype(jnp.int32), partial.astype(jnp.int32)


def kernel(x_prompt, x_sample, c, cache_k, cache_v, state_ssd, c_ctx, norm1_w, norm2_w, w_mod, b_mod, w_in, w_gate,
           b_gate, hy_conv_w, hy_conv_b, hy_w1, hy_b1, hy_w2, hy_b2, hy_w3, hy_freq, hy_decay, hy_bias, ssd_conv_w,
           ssd_conv_b, ssd_a_log, ssd_dt_bias, ssd_d, ssd_norm_w, q_norm_w, k_norm_w, w_br_hy, w_br_ssd, w_br_att,
           w_out, w_router, b_router, w_e1, b_e1, w_e2, b_e2):
    n_ctx, l_ctx, d = x_prompt.shape
    n_lat, l_lat, _ = x_sample.shape
    depth = w_in.shape[0]
    t_ctx, t_lat = n_ctx * l_ctx, n_lat * l_lat
    t = t_ctx + t_lat
    kc = N_KV_HEADS * HEAD_DIM
    geo = dict(t_ctx=t_ctx, l_lat=l_lat)

    x = jnp.concatenate([x_prompt.reshape(t_ctx, d), x_sample.reshape(t_lat, d)], axis=0)
    cvec = jnp.zeros((8, d), F32).at[0].set(c_ctx).at[1:1 + n_lat].set(c)
    mods_all = modulation_all(cvec, w_mod, b_mod)
    b1_flat = b_e1.reshape(-1, 1, b_e1.shape[-1])
    b1g_all, b1u_all = b1_flat[:, :, 0::2], b1_flat[:, :, 1::2]
    new_k, new_v, new_s = [], [], []
    for l in range(depth):
        mods = mods_all[l].reshape(8 * 6, 1, d)
        wi = w_in[l]
        w_proj = jnp.concatenate([wi[:, 0:3072], wi[:, 3088:3856], wi[:, 3072:3088],
                                  jnp.zeros((d, P_COLS - 3856), F32)], axis=1).astype(BF16)
        nw1 = norm1_w[l].reshape(1, d)
        proj = norm_mod_matmul(x, nw1, mods, w_proj, jnp.zeros((1, P_COLS), F32), sigmoid=False, out_dtype=F32,
                               **geo)
        gate = norm_mod_matmul(x, nw1, mods, w_gate[l].astype(BF16), b_gate[l].reshape(1, -1), sigmoid=True,
                               out_dtype=BF16, **geo)

        x0, p = hyena_pre(proj, hy_conv_w[l], hy_conv_b[l], t_ctx=t_ctx, l_ctx=l_ctx, l_lat=l_lat)
        hy_args = (hy_w1[l], hy_b1[l], hy_w2[l], hy_b2[l], hy_w3[l], hy_freq[l], hy_decay[l])
        y_hy_ctx = hyena_ctx(x0, p, hyena_filter(l_ctx, *hy_args), hy_bias[l], n_seq=n_ctx, seq_len=l_ctx)
        y_hy_lat = hyena_lat(x0[t_ctx:], p[t_ctx:], hyena_filter(l_lat, *hy_args), hy_bias[l], n_seq=n_lat,
                             seq_len=l_lat)

        ssd_args = (ssd_conv_w[l], ssd_conv_b[l], ssd_dt_bias[l], ssd_a_log[l], ssd_d[l])
        yf_c, yb_c, fin_c = ssd_scan(proj, None, *ssd_args, row0=0, n_seq=n_ctx, seq_len=l_ctx)
        yf_l, yb_l, _ = ssd_scan(proj, state_ssd[:, l], *ssd_args, row0=t_ctx, n_seq=n_lat, seq_len=l_lat)

        qn, kn = qk_prep(proj, q_norm_w[l], k_norm_w[l], **geo)
        v_all = proj[:, P_V:P_V + kc]
        k_ctx = kn[:t_ctx].reshape(n_ctx, l_ctx, kc)
        v_ctx = v_all[:t_ctx].reshape(n_ctx, l_ctx, kc)
        att_ctx = attention(qn, k_ctx.astype(BF16), _values_t(v_ctx), row0=0, seq_len=l_ctx, tq=l_ctx)
        k_lat = jnp.concatenate([kn[t_ctx:].reshape(n_lat, l_lat, kc), cache_k[:, l].reshape(n_lat, -1, kc)], axis=1)
        v_lat = jnp.concatenate([v_all[t_ctx:].reshape(n_lat, l_lat, kc), cache_v[:, l].reshape(n_lat, -1, kc)],
                                axis=1)
        att_lat = attention(qn, k_lat.astype(BF16), _values_t(v_lat), row0=t_ctx, seq_len=l_lat, tq=128)

        wr = jnp.pad(w_router[l], ((0, 0), (0, LANES - N_EXPERTS)))
        br = jnp.pad(b_router[l], (0, LANES - N_EXPERTS)).reshape(1, LANES)
        x, h2, top_ir, top_w, hist = merge_router(
            x, (y_hy_ctx, y_hy_lat), (yf_c, yf_l), (yb_c, yb_l), proj, (att_ctx, att_lat), gate, ssd_norm_w[l],
            w_br_hy[l].astype(BF16), w_br_ssd[l].astype(BF16),
            w_br_att[l].astype(BF16), w_out[l].astype(BF16), mods, norm2_w[l].reshape(1, d), wr, br, **geo)

        dest, n_tiles, tile_expert, n_valid, partial = _dispatch_plan(
            top_ir[:, :2 * TOP_K], hist[::8, :N_EXPERTS], MOE_TILE)
        xs = moe_dispatch(dest, partial, h2, n_tiles * MOE_TILE)
        out_sorted = moe_experts(
            xs, tile_expert, n_valid, w_e1.reshape((-1,) + w_e1.shape[2:]), b1g_all, b1u_all,
            w_e2.reshape((-1,) + w_e2.shape[2:]), b_e2.reshape(-1, 1, d), layer=l)
        x = moe_combine(x, dest, out_sorted, top_w, mods, **geo)

        new_k.append(k_ctx.reshape(n_ctx, l_ctx, N_KV_HEADS, HEAD_DIM))
        new_v.append(v_ctx.reshape(n_ctx, l_ctx, N_KV_HEADS, HEAD_DIM))
        new_s.append(fin_c)

    y_prompt = x[:t_ctx].reshape(n_ctx, l_ctx, d)
    y_sample = x[t_ctx:].reshape(n_lat, l_lat, d)
    return (y_prompt, y_sample, jnp.stack(new_k, axis=1), jnp.stack(new_v, axis=1), jnp.stack(new_s, axis=1))
```

```python
import functools
import math

import numpy as np
import jax
import jax.numpy as jnp
from jax import lax
from jax.experimental import pallas as pl
from jax.experimental.pallas import tpu as pltpu

F32 = jnp.float32
BF16 = jnp.bfloat16

EPS = 1e-6
GRID_W = 64
HY_DIM = 512
SSD_INNER = 512
SSD_HEADDIM = 64
SSD_HEADS = 8
SSD_GROUPS = 2
SSD_STATE = 128
SSD_CHUNK = 128
SSD_STEP_CHUNKS = 2
N_HEADS = 8
N_KV_HEADS = 2
HEAD_DIM = 64
ROPE_THETA = 10000.0
N_EXPERTS = 32
TOP_K = 4
TOP_K_SHIFT = 2
SWIGLU_ALPHA = 1.702
SWIGLU_LIMIT = 7.0

P_HY, P_Z, P_XBC, P_Q, P_K, P_V, P_DT, P_COLS = 0, 1536, 2048, 3072, 3584, 3712, 3840, 3968

VMEM_LIMIT = 56 * 1024 * 1024
LANES = 128
ROW_TILE = 256
MOE_TILE = 512
DFT_N1, DFT_N2 = 64, 128
STAGE2_K1 = 4


def _cp(sem, vmem=VMEM_LIMIT):
    return pltpu.CompilerParams(dimension_semantics=sem, vmem_limit_bytes=vmem)


def _sigmoid(x):
    return 1.0 / (1.0 + jnp.exp(-x))


def _silu(x):
    return x * _sigmoid(x)


def _softplus(x):
    return jnp.maximum(x, 0.0) + jnp.log(1.0 + jnp.exp(-jnp.abs(x)))


def _dot3(a, b):
    ah = a.astype(BF16)
    al = (a - ah.astype(F32)).astype(BF16)
    bh = b.astype(BF16)
    bl = (b - bh.astype(F32)).astype(BF16)
    dot = lambda u, v: jnp.dot(u, v, preferred_element_type=F32)
    return dot(ah, bh) + (dot(ah, bl) + dot(al, bh))


def _mod_kernel(c_ref, w_ref, b_ref, o_ref):
    s = _silu(c_ref[...])
    o_ref[0] = _dot3(s, w_ref[0]) + b_ref[0]


def modulation_all(cvec, w_mod, b_mod):
    depth, d, n = w_mod.shape
    tn = 1536
    return pl.pallas_call(
        _mod_kernel,
        out_shape=jax.ShapeDtypeStruct((depth, 8, n), F32),
        grid=(depth, n // tn),
        in_specs=[pl.BlockSpec((8, d), lambda l, j: (0, 0)),
                  pl.BlockSpec((1, d, tn), lambda l, j: (l, 0, j)),
                  pl.BlockSpec((1, 1, tn), lambda l, j: (l, 0, j))],
        out_specs=pl.BlockSpec((1, 8, tn), lambda l, j: (l, 0, j)),
        compiler_params=_cp(("arbitrary", "arbitrary")),
        name="modulation",
    )(cvec, w_mod, b_mod.reshape(depth, 1, n))


def _mod_row(i, tm, t_ctx, l_lat):
    n_ctx = t_ctx // tm
    per = l_lat // tm
    return jnp.where(i < n_ctx, 0, 1 + (i - n_ctx) // per)


def _mod_spec(k, tm, t_ctx, l_lat):
    return pl.BlockSpec((1, 1, 1024), lambda i: (_mod_row(i, tm, t_ctx, l_lat) * 6 + k, 0, 0))


def _nmm_kernel(x_ref, nw_ref, sh_ref, sc_ref, w_ref, b_ref, o_ref, *, sigmoid):
    x = x_ref[...]
    ms = jnp.mean(x * x, axis=-1, keepdims=True)
    h = x * lax.rsqrt(ms + EPS) * nw_ref[...]
    h = h * (1.0 + sc_ref[0]) + sh_ref[0]
    acc = jnp.dot(h.astype(BF16), w_ref[...], preferred_element_type=F32) + b_ref[...]
    if sigmoid:
        acc = _sigmoid(acc)
    o_ref[...] = acc.astype(o_ref.dtype)


def norm_mod_matmul(x, nw, mods, w, b, *, t_ctx, l_lat, sigmoid, out_dtype, tm=512):
    t, d = x.shape
    n = w.shape[1]
    return pl.pallas_call(
        functools.partial(_nmm_kernel, sigmoid=sigmoid),
        out_shape=jax.ShapeDtypeStruct((t, n), out_dtype),
        grid=(t // tm,),
        in_specs=[pl.BlockSpec((tm, d), lambda i: (i, 0)),
                  pl.BlockSpec((1, d), lambda i: (0, 0)),
                  _mod_spec(0, tm, t_ctx, l_lat),
                  _mod_spec(1, tm, t_ctx, l_lat),
                  pl.BlockSpec((d, n), lambda i: (0, 0)),
                  pl.BlockSpec((1, n), lambda i: (0, 0))],
        out_specs=pl.BlockSpec((tm, n), lambda i: (i, 0)),
        compiler_params=_cp(("arbitrary",)),
        name="norm_mod_matmul",
    )(x, nw, mods, mods, w, b)


def _seq_edges(i, tr, t_ctx, l_ctx, l_lat):
    tok = i * tr
    pos = jnp.where(tok < t_ctx, tok % l_ctx, (tok - t_ctx) % l_lat)
    length = jnp.where(tok < t_ctx, l_ctx, l_lat)
    return pos == 0, pos + tr == length


def _conv3(x, prev8, next8, w_ref, b_ref, first, last):
    tr = x.shape[0]
    row = lax.broadcasted_iota(jnp.int32, x.shape, 0)
    pm = jnp.where(first, 0.0, 1.0)
    nm = jnp.where(last, 0.0, 1.0)
    xm1 = jnp.where(row == 0, prev8[7:8, :] * pm, pltpu.roll(x, 1, axis=0))
    xp1 = jnp.where(row == tr - 1, next8[0:1, :] * nm, pltpu.roll(x, tr - 1, axis=0))
    return b_ref[...] + xm1 * w_ref[0:1, :] + x * w_ref[1:2, :] + xp1 * w_ref[2:3, :]


def _conv_specs(tr, tc, col_blk, n_rows):
    r8 = tr // 8
    last8 = n_rows // 8 - 1
    return [pl.BlockSpec((tr, tc), lambda i, j: (i, col_blk(j))),
            pl.BlockSpec((8, tc), lambda i, j: (jnp.maximum(i * r8 - 1, 0), col_blk(j))),
            pl.BlockSpec((8, tc), lambda i, j: (jnp.minimum((i + 1) * r8, last8), col_blk(j)))]


def _hy_pre_kernel(*refs, tr, t_ctx, l_ctx, l_lat):
    (x0, x0p, x0n, x1, x1p, x1n, xv, xvp, xvn, w0, w1, wv, b0, b1, bv, o0_ref, op_ref) = refs
    first, last = _seq_edges(pl.program_id(0), tr, t_ctx, l_ctx, l_lat)
    o0_ref[...] = _conv3(x0[...], x0p[...], x0n[...], w0, b0, first, last)
    u1 = _conv3(x1[...], x1p[...], x1n[...], w1, b1, first, last)
    uv = _conv3(xv[...], xvp[...], xvn[...], wv, bv, first, last)
    op_ref[...] = u1 * uv


def hyena_pre(proj, w, b, *, t_ctx, l_ctx, l_lat):
    t = proj.shape[0]
    tr, tc = ROW_TILE, HY_DIM
    nb = HY_DIM // tc
    b2 = b.reshape(1, 3 * HY_DIM)
    specs = []
    for s in range(3):
        specs += _conv_specs(tr, tc, lambda j, s=s: P_HY // tc + s * nb + j, t)
    specs += [pl.BlockSpec((3, tc), lambda i, j, s=s: (0, s * nb + j)) for s in range(3)]
    specs += [pl.BlockSpec((1, tc), lambda i, j, s=s: (0, s * nb + j)) for s in range(3)]
    return pl.pallas_call(
        functools.partial(_hy_pre_kernel, tr=tr, t_ctx=t_ctx, l_ctx=l_ctx, l_lat=l_lat),
        out_shape=(jax.ShapeDtypeStruct((t, HY_DIM), F32), jax.ShapeDtypeStruct((t, HY_DIM), F32)),
        grid=(t // tr, nb),
        in_specs=specs,
        out_specs=(pl.BlockSpec((tr, tc), lambda i, j: (i, j)), pl.BlockSpec((tr, tc), lambda i, j: (i, j))),
        compiler_params=_cp(("arbitrary", "arbitrary")),
        name="hyena_pre",
    )(*([proj] * 9), w, w, w, b2, b2, b2)


def _filter_kernel(z_ref, w1_ref, b1_ref, w2_ref, b2_ref, w3_ref, fr_ref, dec_ref, o_ref, *, zero_row):
    z = z_ref[...]
    fr = fr_ref[...]
    h = jnp.sin(fr * (_dot3(z, w1_ref[...]) + b1_ref[...]))
    h = jnp.sin(fr * (_dot3(h, w2_ref[...]) + b2_ref[...]))
    f = _dot3(h, w3_ref[...])
    f = f * jnp.exp(-z[:, 0:1] * jnp.abs(dec_ref[...]))
    row = pl.program_id(0) * f.shape[0] + lax.broadcasted_iota(jnp.int32, f.shape, 0)
    o_ref[...] = jnp.where(row == zero_row, 0.0, f)


def _filter_embedding(seq_len, emb):
    bands_n = (emb - 1) // 2
    t = jnp.linspace(0.0, 1.0, seq_len, dtype=F32)[:, None]
    bands = jnp.linspace(1e-4, bands_n - 1, bands_n, dtype=F32)[None, :]
    ang = (2.0 * math.pi / seq_len) * jnp.arange(seq_len, dtype=F32)[:, None] * bands
    z = jnp.concatenate([t, jnp.cos(ang), -jnp.sin(ang)], axis=-1)
    return jnp.pad(z, ((0, 0), (0, LANES - emb)))


def hyena_filter(seq_len, w1, b1, w2, b2, w3, freq, decay):
    emb, ff = w1.shape
    ch = w3.shape[1] // 2
    lags = np.concatenate([np.arange(seq_len), [0], np.arange(seq_len - 1, 0, -1)])
    z = _filter_embedding(seq_len, emb)[lags]
    padc = LANES - ff
    w1p = jnp.pad(w1, ((0, LANES - emb), (0, padc)))
    w2p = jnp.pad(w2, ((0, padc), (0, padc)))
    w3p = jnp.pad(w3, ((0, padc), (0, 0)))
    row = lambda v: jnp.pad(v, (0, padc)).reshape(1, LANES)
    tr = 256
    fwd_tiles = seq_len // tr
    full = lambda shape: pl.BlockSpec(shape, lambda i: (0, 0))
    half = lambda shape: pl.BlockSpec(shape, lambda i: (0, jnp.where(i < fwd_tiles, 0, 1)))
    return pl.pallas_call(
        functools.partial(_filter_kernel, zero_row=seq_len),
        out_shape=jax.ShapeDtypeStruct((2 * seq_len, ch), F32),
        grid=(2 * seq_len // tr,),
        in_specs=[pl.BlockSpec((tr, LANES), lambda i: (i, 0)), full((LANES, LANES)), full((1, LANES)),
                  full((LANES, LANES)), full((1, LANES)), half((LANES, ch)), full((1, LANES)), half((1, ch))],
        out_specs=pl.BlockSpec((tr, ch), lambda i: (i, 0)),
        compiler_params=_cp(("arbitrary",)),
        name="hyena_filter",
    )(z, w1p, row(b1), w2p, row(b2), w3p, row(freq), decay.reshape(1, 2 * ch))


def _cs(n_rows, n_cols, period):
    ang = 2.0 * np.pi * (np.outer(np.arange(n_rows), np.arange(n_cols)) % period) / period
    return np.cos(ang), np.sin(ang)


def _mm_kernel(a_ref, b_ref, o_ref):
    o_ref[0] = _dot3(a_ref[...], b_ref[0])


def const_matmul(a, b, tn):
    m, k = a.shape
    bsz, _, n = b.shape
    return pl.pallas_call(
        _mm_kernel,
        out_shape=jax.ShapeDtypeStruct((bsz, m, n), F32),
        grid=(bsz, n // tn),
        in_specs=[pl.BlockSpec((m, k), lambda s, j: (0, 0)), pl.BlockSpec((1, k, tn), lambda s, j: (s, 0, j))],
        out_specs=pl.BlockSpec((1, m, tn), lambda s, j: (s, 0, j)),
        compiler_params=_cp(("arbitrary", "arbitrary")),
        name="const_matmul",
    )(a, b)


def _hy_ctx_kernel(x0_ref, p_ref, kr_ref, ki_ref, fw_ref, iv_ref, bias_ref, o_ref, *, n):
    p = p_ref[...]
    xf = _dot3(fw_ref[...], p)
    xr, xi = xf[:n], xf[n:]
    kr, ki = kr_ref[...], ki_ref[...]
    yr = xr * kr - xi * ki
    yi = xr * ki + xi * kr
    y = _dot3(iv_ref[...], jnp.concatenate([yr, yi], axis=0))
    o_ref[...] = x0_ref[...] * (y + p * bias_ref[...])


def hyena_ctx(x0, p, filt, bias, *, n_seq, seq_len):
    n = 2 * seq_len
    ch = p.shape[1]
    c_full, s_full = _cs(n, n, n)
    fw_full = jnp.asarray(np.concatenate([c_full, -s_full], axis=0), F32)
    fw_half = fw_full[:, :seq_len]
    iv = jnp.asarray(np.concatenate([c_full[:seq_len], -s_full[:seq_len]], axis=1) / n, F32)
    kf = const_matmul(fw_full, filt[None], ch)[0]
    kr, ki = kf[:n], kf[n:]
    full = lambda shape: pl.BlockSpec(shape, lambda s: (0, 0))
    return pl.pallas_call(
        functools.partial(_hy_ctx_kernel, n=n),
        out_shape=jax.ShapeDtypeStruct((n_seq * seq_len, ch), F32),
        grid=(n_seq,),
        in_specs=[pl.BlockSpec((seq_len, ch), lambda s: (s, 0)), pl.BlockSpec((seq_len, ch), lambda s: (s, 0)),
                  full((n, ch)), full((n, ch)), full((2 * n, seq_len)), full((seq_len, 2 * n)), full((1, ch))],
        out_specs=pl.BlockSpec((seq_len, ch), lambda s: (s, 0)),
        compiler_params=_cp(("arbitrary",)),
        name="hyena_ctx",
    )(x0, p, kr, ki, fw_half, iv, bias.reshape(1, ch))


def _stage2_kernel(a_ref, twr_ref, twi_ref, m_ref, *rest, conv):
    if conv:
        kf_ref, mi_ref, o_ref = rest
    else:
        (o_ref,) = rest
    for j in range(STAGE2_K1):
        ar, ai = a_ref[0, 0, j], a_ref[0, 1, j]
        twr, twi = twr_ref[j], twi_ref[j]
        br = ar * twr - ai * twi
        bi = ar * twi + ai * twr
        x = _dot3(m_ref[...], jnp.concatenate([br, bi], axis=0))
        n2 = ar.shape[0]
        xr, xi = x[:n2], x[n2:]
        if not conv:
            o_ref[0, 0, j] = xr
            o_ref[0, 1, j] = xi
            continue
        kr, ki = kf_ref[0, 0, j], kf_ref[0, 1, j]
        yr = xr * kr - xi * ki
        yi = xr * ki + xi * kr
        pq = _dot3(mi_ref[...], jnp.concatenate([yr, yi], axis=0))
        pr, pi = pq[:n2], pq[n2:]
        o_ref[0, 0, j] = pr * twr + pi * twi
        o_ref[0, 1, j] = pi * twr - pr * twi


def _stage2(a, kf, ch):
    bsz = a.shape[0]
    n1, n2 = DFT_N1, DFT_N2
    n = n1 * n2
    tw_ang = 2.0 * np.pi * np.outer(np.arange(n1), np.arange(n2)) / n
    twr = jnp.asarray(np.cos(tw_ang), F32).reshape(n1, n2, 1)
    twi = jnp.asarray(-np.sin(tw_ang), F32).reshape(n1, n2, 1)
    c2, s2 = _cs(n2, n2, n2)
    m_fwd = jnp.asarray(np.block([[c2, s2], [-s2, c2]]), F32)
    m_inv = jnp.asarray(np.block([[c2, -s2], [s2, c2]]), F32)
    conv = kf is not None
    kb = STAGE2_K1
    blk = pl.BlockSpec((1, 2, kb, n2, ch), lambda s, k: (s, 0, k, 0, 0))
    tw_spec = pl.BlockSpec((kb, n2, 1), lambda s, k: (k, 0, 0))
    specs = [blk, tw_spec, tw_spec, pl.BlockSpec((2 * n2, 2 * n2), lambda s, k: (0, 0))]
    args = [a, twr, twi, m_fwd]
    if conv:
        specs += [pl.BlockSpec((1, 2, kb, n2, ch), lambda s, k: (0, 0, k, 0, 0)),
                  pl.BlockSpec((2 * n2, 2 * n2), lambda s, k: (0, 0))]
        args += [kf, m_inv]
    return pl.pallas_call(
        functools.partial(_stage2_kernel, conv=conv),
        out_shape=jax.ShapeDtypeStruct(a.shape, F32),
        grid=(bsz, n1 // kb),
        in_specs=specs,
        out_specs=blk,
        compiler_params=_cp(("arbitrary", "arbitrary")),
        name="hyena_stage2",
    )(*args)


def _hy_post_kernel(g_ref, q_ref, x0_ref, p_ref, bias_ref, o_ref):
    y = _dot3(g_ref[...], q_ref[0])
    o_ref[0] = x0_ref[0] * (y + p_ref[0] * bias_ref[...])


def hyena_lat(x0, p, filt, bias, *, n_seq, seq_len):
    n1, n2 = DFT_N1, DFT_N2
    n = n1 * n2
    assert n == 2 * seq_len
    ch = p.shape[1]
    h1 = n1 // 2
    wide = n2 * ch
    c1, s1 = _cs(n1, n1, n1)
    f1_full = jnp.asarray(np.concatenate([c1, -s1], axis=0), F32)
    f1_half = f1_full[:, :h1]
    g1 = jnp.asarray(np.concatenate([c1[:h1], -s1[:h1]], axis=1) / n, F32)
    tn = 8192
    kcirc = filt.reshape(1, n1, wide)
    kf = _stage2(const_matmul(f1_full, kcirc, tn).reshape(1, 2, n1, n2, ch), None, ch)
    a = const_matmul(f1_half, p.reshape(n_seq, h1, wide), tn).reshape(n_seq, 2, n1, n2, ch)
    q = _stage2(a, kf, ch).reshape(n_seq, 2 * n1, wide)
    bias_w = jnp.tile(bias, n2).reshape(1, wide)
    out = pl.pallas_call(
        _hy_post_kernel,
        out_shape=jax.ShapeDtypeStruct((n_seq, h1, wide), F32),
        grid=(n_seq, wide // tn),
        in_specs=[pl.BlockSpec((h1, 2 * n1), lambda s, j: (0, 0)),
                  pl.BlockSpec((1, 2 * n1, tn), lambda s, j: (s, 0, j)),
                  pl.BlockSpec((1, h1, tn), lambda s, j: (s, 0, j)),
                  pl.BlockSpec((1, h1, tn), lambda s, j: (s, 0, j)),
                  pl.BlockSpec((1, tn), lambda s, j: (0, j))],
        out_specs=pl.BlockSpec((1, h1, tn), lambda s, j: (s, 0, j)),
        compiler_params=_cp(("arbitrary", "arbitrary")),
        name="hyena_post",
    )(g1, q, x0.reshape(n_seq, h1, wide), p.reshape(n_seq, h1, wide), bias_w)
    return out.reshape(n_seq * seq_len, ch)


def _dot_01(x, m, *, x_is_lhs):
    x1 = x.astype(BF16)
    r1 = x - x1.astype(F32)
    x2 = r1.astype(BF16)
    x3 = (r1 - x2.astype(F32)).astype(BF16)
    mb = m.astype(BF16)
    if x_is_lhs:
        dot = lambda u: jnp.dot(u, mb, preferred_element_type=F32)
    else:
        dot = lambda u: jnp.dot(mb, u, preferred_element_type=F32)
    return dot(x1) + (dot(x2) + dot(x3))


def _ssd_dir(xbc, dtraw, st_ref, d, consts, y_ref):
    tri, expand, dtb, acont, dskip = consts
    cl = xbc.shape[0]
    xs = xbc[:, :SSD_INNER]
    dt = _softplus(dtraw + dtb)
    a = dt * acont
    tri_d = tri if d == 0 else tri.T
    cs = _dot_01(a, tri_d, x_is_lhs=False)
    cs_t = _dot_01(a.T, tri_d.T, x_is_lhs=True)
    dt_x = _dot_01(dt, expand[d], x_is_lhs=True)
    xdt = xs * dt_x
    row = lax.broadcasted_iota(jnp.int32, (cl, cl), 0)
    col = lax.broadcasted_iota(jnp.int32, (cl, cl), 1)
    keep = (col <= row) if d == 0 else (col >= row)
    lane = lax.broadcasted_iota(jnp.int32, (cl, LANES), 1)
    low = lane < SSD_HEADDIM
    edge = cl - 1 if d == 0 else 0
    for g in range(SSD_GROUPS):
        bg = xbc[:, SSD_INNER + g * SSD_STATE: SSD_INNER + (g + 1) * SSD_STATE]
        cg = xbc[:, SSD_INNER + (SSD_GROUPS + g) * SSD_STATE: SSD_INNER + (SSD_GROUPS + g + 1) * SSD_STATE]
        bg16, cg16 = bg.astype(BF16), cg.astype(BF16)
        cb = lax.dot_general(cg16, bg16, (((1,), (1,)), ((), ())), preferred_element_type=F32)
        for pr in range(2):
            pair = g * 2 + pr
            h0 = 2 * pair
            ms = []
            for h in (h0, h0 + 1):
                ln = d * SSD_HEADS + h
                diff = cs[:, ln:ln + 1] - cs_t[ln:ln + 1, :]
                ms.append(jnp.where(keep, cb * jnp.exp(jnp.minimum(diff, 0.0)), 0.0).astype(BF16))
            xp = xdt[:, pair * LANES:(pair + 1) * LANES]
            xs_p = xs[:, pair * LANES:(pair + 1) * LANES]
            xlo = jnp.where(low, xp, 0.0).astype(BF16)
            xhi = jnp.where(low, 0.0, xp).astype(BF16)
            y_diag = (jnp.dot(ms[0], xlo, preferred_element_type=F32)
                      + jnp.dot(ms[1], xhi, preferred_element_type=F32))
            l0, l1 = d * SSD_HEADS + h0, d * SSD_HEADS + h0 + 1
            e_cs = jnp.where(low, jnp.exp(cs[:, l0:l0 + 1]), jnp.exp(cs[:, l1:l1 + 1]))
            st = st_ref[pair]
            y_off = jnp.dot(cg16, st.astype(BF16), preferred_element_type=F32) * e_cs
            y_ref[:, pair * LANES:(pair + 1) * LANES] = (
                y_diag + y_off + xs_p * dskip[d:d + 1, pair * LANES:(pair + 1) * LANES])
            tot0, tot1 = cs[edge:edge + 1, l0:l0 + 1], cs[edge:edge + 1, l1:l1 + 1]
            dec = jnp.where(low, jnp.exp(tot0 - cs[:, l0:l0 + 1]), jnp.exp(tot1 - cs[:, l1:l1 + 1]))
            upd = lax.dot_general(bg16, (xp * dec).astype(BF16), (((0,), (0,)), ((), ())),
                                  preferred_element_type=F32)
            st_ref[pair] = st * jnp.where(low[0:1], jnp.exp(tot0), jnp.exp(tot1)) + upd


def _ssd_kernel(xf_ref, xfp_ref, xfn_ref, dtf_ref, xb_ref, xbp_ref, xbn_ref, dtb_ref, init_ref, cw_ref, cb_ref,
                tri_ref, exp_ref, dtbias_ref, acont_ref, dskip_ref, yf_ref, yb_ref, fin_ref, st_ref, *, has_init):
    s = pl.program_id(1)
    at_start, at_end = s == 0, s == pl.num_programs(1) - 1

    @pl.when(at_start)
    def _():
        if has_init:
            st_ref[...] = init_ref[0]
        else:
            st_ref[...] = jnp.zeros_like(st_ref)

    consts = (tri_ref[...], (exp_ref[0], exp_ref[1]), dtbias_ref[...], acont_ref[...], dskip_ref[...])
    xbc_f = _silu(_conv3(xf_ref[...], xfp_ref[...], xfn_ref[...], cw_ref, cb_ref, at_start, at_end))
    xbc_b = _silu(_conv3(xb_ref[...], xbp_ref[...], xbn_ref[...], cw_ref, cb_ref, at_end, at_start))
    cl = SSD_CHUNK
    for c in range(SSD_STEP_CHUNKS):
        rf = slice(c * cl, (c + 1) * cl)
        _ssd_dir(xbc_f[rf], dtf_ref[rf, :], st_ref.at[0], 0, consts, yf_ref.at[rf])
        rb = slice((SSD_STEP_CHUNKS - 1 - c) * cl, (SSD_STEP_CHUNKS - c) * cl)
        _ssd_dir(xbc_b[rb], dtb_ref[rb, :], st_ref.at[1], 1, consts, yb_ref.at[rb])

    @pl.when(s == pl.num_programs(1) - 1)
    def _():
        fin_ref[0] = st_ref[...]


def _pair_states(s):
    b = s.shape[0]
    s = s.reshape(b, 2, SSD_HEADS // 2, 2, SSD_HEADDIM, SSD_STATE)
    return s.transpose(0, 1, 2, 5, 3, 4).reshape(b, 2, SSD_HEADS // 2, SSD_STATE, 2 * SSD_HEADDIM)


def _unpair_states(s):
    b = s.shape[0]
    s = s.reshape(b, 2, SSD_HEADS // 2, SSD_STATE, 2, SSD_HEADDIM)
    return s.transpose(0, 1, 2, 4, 5, 3).reshape(b, 2, SSD_HEADS, SSD_HEADDIM, SSD_STATE)


def ssd_scan(proj, init, conv_w, conv_b, dt_bias, a_log, d_skip, *, row0, n_seq, seq_len):
    cl = SSD_CHUNK
    bl = SSD_STEP_CHUNKS * cl
    nc = seq_len // bl
    base = row0 // bl
    has_init = init is not None
    hp = SSD_HEADS // 2
    init_p = _pair_states(init) if has_init else jnp.zeros((1, 2, hp, SSD_STATE, LANES), F32)
    tri = jnp.asarray(np.tril(np.ones((cl, cl))), F32)
    expand = np.zeros((2, LANES, SSD_INNER), np.float32)
    for d in range(2):
        for h in range(SSD_HEADS):
            expand[d, d * SSD_HEADS + h, h * SSD_HEADDIM:(h + 1) * SSD_HEADDIM] = 1.0
    pad16 = lambda v: jnp.pad(v.reshape(1, 2 * SSD_HEADS), ((0, 0), (0, LANES - 2 * SSD_HEADS)))
    acont = pad16(-jnp.exp(a_log))
    dtb = pad16(dt_bias)
    dskip = jnp.repeat(d_skip, SSD_HEADDIM, axis=1)
    cxbc = conv_w.shape[1]
    xblk = P_XBC // cxbc
    dtblk = P_DT // LANES
    r8 = bl // 8
    last8 = proj.shape[0] // 8 - 1
    full2 = lambda shape: pl.BlockSpec(shape, lambda b, s: (0,) * len(shape))
    fwd = lambda b, s: base + b * nc + s
    bwd = lambda b, s: base + b * nc + (nc - 1 - s)

    def stream(chunk):
        return [pl.BlockSpec((bl, cxbc), lambda b, s: (chunk(b, s), xblk)),
                pl.BlockSpec((8, cxbc), lambda b, s: (jnp.maximum(chunk(b, s) * r8 - 1, 0), xblk)),
                pl.BlockSpec((8, cxbc), lambda b, s: (jnp.minimum((chunk(b, s) + 1) * r8, last8), xblk)),
                pl.BlockSpec((bl, LANES), lambda b, s: (chunk(b, s), dtblk))]

    st_spec = pl.BlockSpec((1, 2, hp, SSD_STATE, LANES), lambda b, s: (b if has_init else 0, 0, 0, 0, 0))
    yf, yb, fin = pl.pallas_call(
        functools.partial(_ssd_kernel, has_init=has_init),
        out_shape=(jax.ShapeDtypeStruct((n_seq * seq_len, SSD_INNER), F32),
                   jax.ShapeDtypeStruct((n_seq * seq_len, SSD_INNER), F32),
                   jax.ShapeDtypeStruct((n_seq, 2, hp, SSD_STATE, LANES), F32)),
        grid=(n_seq, nc),
        in_specs=stream(fwd) + stream(bwd) + [
            st_spec, full2((3, cxbc)), full2((1, cxbc)), full2((cl, cl)), full2((2, LANES, SSD_INNER)),
            full2((1, LANES)), full2((1, LANES)), full2((2, SSD_INNER))],
        out_specs=(pl.BlockSpec((bl, SSD_INNER), lambda b, s: (b * nc + s, 0)),
                   pl.BlockSpec((bl, SSD_INNER), lambda b, s: (b * nc + (nc - 1 - s), 0)),
                   pl.BlockSpec((1, 2, hp, SSD_STATE, LANES), lambda b, s: (b, 0, 0, 0, 0))),
        scratch_shapes=[pltpu.VMEM((2, hp, SSD_STATE, LANES), F32)],
        compiler_params=_cp(("arbitrary", "arbitrary")),
        name="ssd_scan",
    )(*([proj] * 8), init_p, conv_w, conv_b.reshape(1, cxbc), tri, jnp.asarray(expand), dtb, acont, dskip)
    return yf, yb, _unpair_states(fin)


def _headnorm(x, g_ref, w_ref):
    ms = _dot3(x * x, g_ref[...])
    return x * lax.rsqrt(ms + EPS) * w_ref[...]


def _rope(x, cos, sin_signed):
    lane = lax.broadcasted_iota(jnp.int32, x.shape, 1)
    w = x.shape[1]
    swapped = jnp.where(lane % 2 == 0, pltpu.roll(x, w - 1, axis=1), pltpu.roll(x, 1, axis=1))
    return x * cos + swapped * sin_signed


Q_SCALE = HEAD_DIM ** -0.5 * math.log2(math.e)


def _store_padded_heads(qo_ref, q):
    rep = N_HEADS // N_KV_HEADS
    lane = lax.broadcasted_iota(jnp.int32, (q.shape[0], LANES), 1)
    for h in range(N_HEADS):
        g = h // rep
        chunk = q[:, (h // 2) * LANES:(h // 2 + 1) * LANES]
        if h % 2 != g:
            chunk = pltpu.roll(chunk, HEAD_DIM, axis=1)
        keep = (lane >= g * HEAD_DIM) & (lane < (g + 1) * HEAD_DIM)
        qo_ref[:, h * LANES:(h + 1) * LANES] = jnp.where(keep, chunk, 0.0).astype(qo_ref.dtype)


def _qk_kernel(q_ref, k_ref, cos_ref, sin_ref, gq_ref, gk_ref, qw_ref, kw_ref, qo_ref, ko_ref, *, n_ctx_tiles):
    q = _headnorm(q_ref[...], gq_ref, qw_ref)
    k = _headnorm(k_ref[...], gk_ref, kw_ref)
    is_lat = pl.program_id(0) >= n_ctx_tiles

    @pl.when(is_lat)
    def _():
        cos, sin = cos_ref[...], sin_ref[...]
        _store_padded_heads(qo_ref, _rope(q, cos, sin) * Q_SCALE)
        ko_ref[...] = _rope(k, cos[:, :k.shape[1]], sin[:, :k.shape[1]])

    @pl.when(jnp.logical_not(is_lat))
    def _():
        _store_padded_heads(qo_ref, q * Q_SCALE)
        ko_ref[...] = k


def _rope_tables(seq_len):
    n_rows = seq_len // GRID_W
    row = jnp.repeat(jnp.arange(n_rows), GRID_W).astype(F32)
    col = jnp.tile(jnp.arange(GRID_W), n_rows).astype(F32)
    n_freq = HEAD_DIM // 4
    inv = ROPE_THETA ** (-jnp.arange(n_freq, dtype=F32) / n_freq)
    ang = jnp.concatenate([row[:, None] * inv, col[:, None] * inv], axis=-1)
    cos = jnp.repeat(jnp.cos(ang), 2, axis=1)
    sin = jnp.repeat(jnp.sin(ang), 2, axis=1) * jnp.tile(jnp.asarray([-1.0, 1.0], F32), HEAD_DIM // 2)
    return jnp.tile(cos, (1, N_HEADS)), jnp.tile(sin, (1, N_HEADS))


def qk_prep(proj, q_norm_w, k_norm_w, *, t_ctx, l_lat):
    t = proj.shape[0]
    tr = 2 * ROW_TILE
    qc, kc = N_HEADS * HEAD_DIM, N_KV_HEADS * HEAD_DIM
    cos, sin = _rope_tables(l_lat)
    group = lambda c: jnp.asarray(np.kron(np.eye(c // HEAD_DIM), np.ones((HEAD_DIM, HEAD_DIM))) / HEAD_DIM, F32)
    n_ctx_tiles = t_ctx // tr
    per = l_lat // tr
    tab = pl.BlockSpec((tr, qc), lambda i: (jnp.maximum(i - n_ctx_tiles, 0) % per, 0))
    full = lambda shape: pl.BlockSpec(shape, lambda i: (0, 0))
    return pl.pallas_call(
        functools.partial(_qk_kernel, n_ctx_tiles=n_ctx_tiles),
        out_shape=(jax.ShapeDtypeStruct((t, N_HEADS * LANES), BF16), jax.ShapeDtypeStruct((t, kc), F32)),
        grid=(t // tr,),
        in_specs=[pl.BlockSpec((tr, qc), lambda i: (i, P_Q // qc)), pl.BlockSpec((tr, kc), lambda i: (i, P_K // kc)),
                  tab, tab, full((qc, qc)), full((kc, kc)), full((1, qc)), full((1, kc))],
        out_specs=(pl.BlockSpec((tr, N_HEADS * LANES), lambda i: (i, 0)), pl.BlockSpec((tr, kc), lambda i: (i, 0))),
        compiler_params=_cp(("arbitrary",)),
        name="qk_prep",
    )(proj, proj, cos, sin, group(qc), group(kc), jnp.tile(q_norm_w, N_HEADS).reshape(1, qc),
      jnp.tile(k_norm_w, N_KV_HEADS).reshape(1, kc))


def _attention_kernel(q_ref, k_ref, vt_ref, o_ref, *, tq):
    rep = N_HEADS // N_KV_HEADS
    k = k_ref[0]
    vt = vt_ref[0]
    for g in range(N_KV_HEADS):
        qs = jnp.concatenate([q_ref[:, (g * rep + r) * LANES:(g * rep + r + 1) * LANES] for r in range(rep)],
                             axis=0)
        s = lax.dot_general(k, qs, (((1,), (1,)), ((), ())), preferred_element_type=F32)
        p = jnp.exp2(s - jnp.max(s, axis=0, keepdims=True))
        l = jnp.sum(p, axis=0, keepdims=True)
        out = (jnp.dot(vt, p.astype(BF16), preferred_element_type=F32) / l).T
        for r in range(rep):
            h = g * rep + r
            o_ref[:, h * HEAD_DIM:(h + 1) * HEAD_DIM] = (
                out[r * tq:(r + 1) * tq, g * HEAD_DIM:(g + 1) * HEAD_DIM].astype(o_ref.dtype))


def _values_t(v):
    return jnp.swapaxes(v, 1, 2).astype(BF16)


def attention(q, k, vt, *, row0, seq_len, tq):
    qc = q.shape[1]
    b, lk, kc = k.shape
    per = seq_len // tq
    base = row0 // tq
    return pl.pallas_call(
        functools.partial(_attention_kernel, tq=tq),
        out_shape=jax.ShapeDtypeStruct((b * seq_len, N_HEADS * HEAD_DIM), BF16),
        grid=(b, per),
        in_specs=[pl.BlockSpec((tq, qc), lambda s, i: (base + s * per + i, 0)),
                  pl.BlockSpec((1, lk, kc), lambda s, i: (s, 0, 0)),
                  pl.BlockSpec((1, kc, lk), lambda s, i: (s, 0, 0))],
        out_specs=pl.BlockSpec((tq, N_HEADS * HEAD_DIM), lambda s, i: (s * per + i, 0)),
        compiler_params=_cp(("arbitrary", "arbitrary")),
        name="attention",
    )(q, k, vt)


def _merge_kernel(x_ref, yhy_c, yhy_l, yf_c, yf_l, yb_c, yb_l, z_ref, yatt_c, yatt_l, gate_ref, snw_ref, whb_ref,
                  wsb_ref, wab_ref, wout_ref, g1_ref, n2w_ref, sh2_ref, sc2_ref, wr_ref, br_ref, ltri_ref,
                  xo_ref, h2_ref, ti_ref, tw_ref, hist_ref, *, n_ctx_tiles):
    d = x_ref.shape[1]
    is_ctx = pl.program_id(0) < n_ctx_tiles
    pick = lambda c_ref, l_ref: jnp.where(is_ctx, c_ref[...], l_ref[...])
    ys = (pick(yf_c, yf_l) + pick(yb_c, yb_l)) * _silu(z_ref[...])
    ys = ys * lax.rsqrt(jnp.mean(ys * ys, axis=-1, keepdims=True) + EPS) * snw_ref[...]
    gate = gate_ref[...].astype(F32)
    merged = (gate[:, :d] * jnp.dot(pick(yhy_c, yhy_l).astype(BF16), whb_ref[...], preferred_element_type=F32)
              + gate[:, d:2 * d] * jnp.dot(ys.astype(BF16), wsb_ref[...], preferred_element_type=F32)
              + gate[:, 2 * d:] * jnp.dot(pick(yatt_c, yatt_l), wab_ref[...], preferred_element_type=F32))
    mix = jnp.dot(merged.astype(BF16), wout_ref[...], preferred_element_type=F32)
    x = x_ref[...] + g1_ref[0] * mix
    xo_ref[...] = x
    h = x * lax.rsqrt(jnp.mean(x * x, axis=-1, keepdims=True) + EPS) * n2w_ref[...]
    h = h * (1.0 + sc2_ref[0]) + sh2_ref[0]
    _store_tiled(h2_ref, h)
    logits = _dot3(h, wr_ref[...]) + br_ref[...]
    lane = lax.broadcasted_iota(jnp.int32, logits.shape, 1)
    work = jnp.where(lane < N_EXPERTS, logits, -jnp.inf)
    idx_out = jnp.zeros(logits.shape, jnp.int32)
    val_out = jnp.full(logits.shape, -jnp.inf, F32)
    picks = []
    for j in range(TOP_K):
        mx = jnp.max(work, axis=-1, keepdims=True)
        am = jnp.min(jnp.where(work == mx, lane, LANES), axis=-1, keepdims=True)
        idx_out = jnp.where(lane == j, am, idx_out)
        val_out = jnp.where(lane == j, mx, val_out)
        picks.append(lane == am)
        work = jnp.where(picks[-1], -jnp.inf, work)
    e = jnp.exp(val_out - jnp.max(val_out, axis=-1, keepdims=True))
    tw_ref[...] = e / jnp.sum(e, axis=-1, keepdims=True)
    chosen = jnp.where(picks[0] | picks[1] | picks[2] | picks[3], 1.0, 0.0)
    before = jnp.dot(ltri_ref[...], chosen.astype(BF16), preferred_element_type=F32)
    rank_out = jnp.zeros(logits.shape, F32)
    for j in range(TOP_K):
        rank_j = jnp.sum(jnp.where(picks[j], before, 0.0), axis=-1, keepdims=True)
        rank_out = jnp.where(lane == TOP_K + j, rank_j, rank_out)
    ti_ref[...] = idx_out + rank_out.astype(jnp.int32)
    hist_ref[...] = jnp.broadcast_to(jnp.sum(chosen, axis=0, keepdims=True), hist_ref.shape)


def merge_router(x, y_hy, yf, yb, proj, y_att, gate, ssd_norm_w, whb, wsb, wab, wout, mods, n2w, wr, br,
                 *, t_ctx, l_lat, tm=512):
    t, d = x.shape
    nct = t_ctx // tm
    nlt = (t - t_ctx) // tm
    row = lambda tc, c0=0: pl.BlockSpec((tm, tc), lambda i: (i, c0))
    full = lambda shape: pl.BlockSpec(shape, lambda i: (0, 0))
    pair = lambda tc: [pl.BlockSpec((tm, tc), lambda i: (jnp.minimum(i, nct - 1), 0)),
                       pl.BlockSpec((tm, tc), lambda i: (jnp.clip(i - nct, 0, nlt - 1), 0))]
    ltri = jnp.asarray(np.tril(np.ones((tm, tm)), -1), BF16)
    return pl.pallas_call(
        functools.partial(_merge_kernel, n_ctx_tiles=nct),
        out_shape=(jax.ShapeDtypeStruct((t, d), F32), jax.ShapeDtypeStruct((t * ROW_SUB, LANES), F32),
                   jax.ShapeDtypeStruct((t, LANES), jnp.int32), jax.ShapeDtypeStruct((t, LANES), F32),
                   jax.ShapeDtypeStruct((t // tm * 8, LANES), F32)),
        grid=(t // tm,),
        in_specs=[row(d)] + pair(HY_DIM) + pair(SSD_INNER) + pair(SSD_INNER) + [row(SSD_INNER, P_Z // SSD_INNER)]
                 + pair(N_HEADS * HEAD_DIM) + [row(3 * d), full((1, SSD_INNER)),
                  full(whb.shape), full(wsb.shape), full(wab.shape), full(wout.shape),
                  _mod_spec(2, tm, t_ctx, l_lat), full((1, d)), _mod_spec(3, tm, t_ctx, l_lat),
                  _mod_spec(4, tm, t_ctx, l_lat), full((d, LANES)), full((1, LANES)), full((tm, tm))],
        out_specs=(row(d), pl.BlockSpec((tm * ROW_SUB, LANES), lambda i: (i, 0)), row(LANES), row(LANES),
                   pl.BlockSpec((8, LANES), lambda i: (i, 0))),
        compiler_params=_cp(("arbitrary",)),
        name="merge_router",
    )(x, *y_hy, *yf, *yb, proj, *y_att, gate, ssd_norm_w.reshape(1, -1), whb, wsb, wab, wout, mods, n2w, mods, mods,
      wr, br, ltri)


DEINT_BLOCK = 256


def _deinterleave(w_ref, perm_ref, g_ref, u_ref):
    half = DEINT_BLOCK // 2
    for blk in range(w_ref.shape[2] // DEINT_BLOCK):
        wb = w_ref[0, :, blk * DEINT_BLOCK:(blk + 1) * DEINT_BLOCK].astype(BF16)
        r = jnp.dot(wb, perm_ref[...], preferred_element_type=F32)
        g_ref[:, blk * half:(blk + 1) * half] = r[:, :half].astype(BF16)
        u_ref[:, blk * half:(blk + 1) * half] = r[:, half:].astype(BF16)


def _deinterleave_perm():
    perm = np.zeros((DEINT_BLOCK, DEINT_BLOCK), np.float32)
    half = DEINT_BLOCK // 2
    perm[2 * np.arange(half), np.arange(half)] = 1.0
    perm[2 * np.arange(half) + 1, half + np.arange(half)] = 1.0
    return jnp.asarray(perm, BF16)


DMA_CHUNK = 512
DMA_UNROLL = 8
ROW_SUB = 8


def _tok_rows(t):
    return pl.ds(pl.multiple_of(t * ROW_SUB, ROW_SUB), ROW_SUB)


def _chunk_wait(ref, sem):
    n = DMA_CHUNK * ROW_SUB
    pltpu.make_async_copy(ref.at[pl.ds(0, n)], ref.at[pl.ds(0, n)], sem).wait()


def _issue_rows(copy_of):
    def body(it, carry):
        slot = it & (TOP_K - 1)
        tok0 = pl.multiple_of(lax.shift_right_logical(it, TOP_K_SHIFT) * DMA_UNROLL, DMA_UNROLL)
        for u in range(DMA_UNROLL):
            copy_of(slot, tok0, u).start(priority=u % 2)
        return carry
    lax.fori_loop(0, DMA_CHUNK // DMA_UNROLL, body, 0)


def _dispatch_kernel(zf_ref, idx_ref, x_ref, dst_ref, zbuf_ref, sem, zsem):
    tile_rows = zbuf_ref.shape[0]

    @pl.when(pl.program_id(0) == 0)
    def _():
        zbuf_ref[...] = jnp.zeros_like(zbuf_ref)
        fill = lambda i: pltpu.make_async_copy(
            zbuf_ref, dst_ref.at[pl.ds(pl.multiple_of(i * tile_rows, tile_rows), tile_rows)], zsem)

        def start(i, carry):
            @pl.when(zf_ref[i] == 1)
            def _():
                fill(i).start()
            return carry

        def wait(i, carry):
            @pl.when(zf_ref[i] == 1)
            def _():
                fill(i).wait()
            return carry

        lax.fori_loop(0, zf_ref.shape[0], start, 0)
        lax.fori_loop(0, zf_ref.shape[0], wait, 0)

    def copy_of(slot, tok0, u):
        j = idx_ref[0, 0, (tok0 + u) * TOP_K + slot]
        return pltpu.make_async_copy(x_ref.at[_tok_rows(tok0 + u)], dst_ref.at[_tok_rows(j)], sem)

    _issue_rows(copy_of)
    _chunk_wait(dst_ref, sem)


def moe_dispatch(dest, zero_tile, h2, n_dst):
    n = dest.shape[0]
    nc = n // DMA_CHUNK
    rows = DMA_CHUNK // TOP_K * ROW_SUB
    shape = (n_dst * ROW_SUB, LANES)
    return pl.pallas_call(
        _dispatch_kernel,
        out_shape=jax.ShapeDtypeStruct(shape, h2.dtype),
        grid_spec=pltpu.PrefetchScalarGridSpec(
            num_scalar_prefetch=1,
            grid=(nc,),
            in_specs=[pl.BlockSpec((1, 1, DMA_CHUNK), lambda c, zf: (c, 0, 0), memory_space=pltpu.SMEM),
                      pl.BlockSpec((rows, LANES), lambda c, zf: (c, 0))],
            out_specs=pl.BlockSpec(memory_space=pl.ANY),
            scratch_shapes=[pltpu.VMEM((MOE_TILE * ROW_SUB, LANES), h2.dtype), pltpu.SemaphoreType.DMA(()),
                            pltpu.SemaphoreType.DMA(())]),
        compiler_params=_cp(("arbitrary",)),
        name="moe_dispatch",
    )(zero_tile, dest.reshape(nc, 1, DMA_CHUNK), h2)


def _load_tiled(ref, n_tok, first_tok=0):
    return jnp.concatenate([ref[pl.ds(first_tok * ROW_SUB + s, n_tok, stride=ROW_SUB), :] for s in range(ROW_SUB)],
                           axis=1)


def _store_tiled(ref, x):
    for s in range(ROW_SUB):
        ref[pl.ds(s, x.shape[0], stride=ROW_SUB), :] = x[:, s * LANES:(s + 1) * LANES]


def _moe_kernel(te_ref, nv_ref, x_ref, w1_ref, perm_ref, b1g_ref, b1u_ref, w2_ref, b2_ref, o_ref,
                w1g_s, w1u_s, w2_s):
    i = pl.program_id(0)
    valid = i < nv_ref[0]
    new_expert = jnp.logical_or(i == 0, te_ref[i] != te_ref[jnp.maximum(i - 1, 0)])

    @pl.when(jnp.logical_and(valid, new_expert))
    def _():
        _deinterleave(w1_ref, perm_ref, w1g_s, w1u_s)
        w2_s[...] = w2_ref[0].astype(BF16)

    @pl.when(valid)
    def _():
        x = _load_tiled(x_ref, MOE_TILE).astype(BF16)
        gate = jnp.dot(x, w1g_s[...], preferred_element_type=F32) + b1g_ref[0]
        up = jnp.dot(x, w1u_s[...], preferred_element_type=F32) + b1u_ref[0]
        gate = jnp.minimum(gate, SWIGLU_LIMIT)
        up = jnp.clip(up, -SWIGLU_LIMIT, SWIGLU_LIMIT)
        act = (up + 1.0) * (gate * _sigmoid(SWIGLU_ALPHA * gate))
        _store_tiled(o_ref, jnp.dot(act.astype(BF16), w2_s[...], preferred_element_type=F32) + b2_ref[0])

    @pl.when(jnp.logical_not(valid))
    def _():
        o_ref[...] = jnp.zeros_like(o_ref)


def moe_experts(xs, tile_expert, n_valid, w1, b1g, b1u, w2, b2, *, layer):
    tm = MOE_TILE * ROW_SUB
    d, ff2 = w1.shape[1:]
    ff = ff2 // 2
    n_tiles = xs.shape[0] // tm
    rows = lambda i, te, nv: (jnp.minimum(i, nv[0] - 1), 0)
    wsel = lambda i, te, nv: (layer * N_EXPERTS + te[i], 0, 0)
    return pl.pallas_call(
        _moe_kernel,
        out_shape=jax.ShapeDtypeStruct(xs.shape, F32),
        grid_spec=pltpu.PrefetchScalarGridSpec(
            num_scalar_prefetch=2,
            grid=(n_tiles,),
            in_specs=[pl.BlockSpec((tm, LANES), rows),
                      pl.BlockSpec((1, d, ff2), wsel),
                      pl.BlockSpec((DEINT_BLOCK, DEINT_BLOCK), lambda i, te, nv: (0, 0)),
                      pl.BlockSpec((1, 1, ff), wsel), pl.BlockSpec((1, 1, ff), wsel),
                      pl.BlockSpec((1, ff, d), wsel), pl.BlockSpec((1, 1, d), wsel)],
            out_specs=pl.BlockSpec((tm, LANES), lambda i, te, nv: (i, 0)),
            scratch_shapes=[pltpu.VMEM((d, ff), BF16), pltpu.VMEM((d, ff), BF16), pltpu.VMEM((ff, d), BF16)]),
        compiler_params=_cp(("arbitrary",)),
        name="moe_experts",
    )(tile_expert, n_valid, xs, w1, _deinterleave_perm(), b1g, b1u, w2, b2)


def _combine_kernel(idx_ref, nxt_ref, x_ref, rows_ref, tw_ref, g2_ref, y_ref, buf_ref, sems):
    i = pl.program_id(0)
    slot = i % 2
    tm = x_ref.shape[0]

    def gather(ids_ref, s):
        def copy_of(slot, tok0, u):
            j = ids_ref[0, 0, (tok0 + u) * TOP_K + slot]
            return pltpu.make_async_copy(rows_ref.at[_tok_rows(j)], buf_ref.at[s, _tok_rows(slot * tm + tok0 + u)],
                                         sems.at[s])
        _issue_rows(copy_of)

    @pl.when(i == 0)
    def _():
        gather(idx_ref, 0)

    @pl.when(i + 1 < pl.num_programs(0))
    def _():
        gather(nxt_ref, 1 - slot)

    _chunk_wait(rows_ref, sems.at[slot])
    tw = tw_ref[...]
    cur = buf_ref.at[slot]
    acc = tw[:, 0:1] * _load_tiled(cur, tm, 0)
    for s in range(1, TOP_K):
        acc = acc + tw[:, s:s + 1] * _load_tiled(cur, tm, s * tm)
    y_ref[...] = x_ref[...] + g2_ref[0] * acc


def moe_combine(x, dest, rows, top_w, mods, *, t_ctx, l_lat):
    t, d = x.shape
    tm = DMA_CHUNK // TOP_K
    nt = t // tm
    ids = dest.reshape(nt, 1, DMA_CHUNK)
    smem = lambda fn: pl.BlockSpec((1, 1, DMA_CHUNK), fn, memory_space=pltpu.SMEM)
    return pl.pallas_call(
        _combine_kernel,
        out_shape=jax.ShapeDtypeStruct((t, d), F32),
        grid=(nt,),
        in_specs=[smem(lambda i: (i, 0, 0)), smem(lambda i: (jnp.minimum(i + 1, nt - 1), 0, 0)),
                  pl.BlockSpec((tm, d), lambda i: (i, 0)), pl.BlockSpec(memory_space=pl.ANY),
                  pl.BlockSpec((tm, LANES), lambda i: (i, 0)), _mod_spec(5, tm, t_ctx, l_lat)],
        out_specs=pl.BlockSpec((tm, d), lambda i: (i, 0)),
        scratch_shapes=[pltpu.VMEM((2, DMA_CHUNK * ROW_SUB, LANES), F32), pltpu.SemaphoreType.DMA((2,))],
        compiler_params=_cp(("arbitrary",)),
        name="moe_combine",
    )(ids, ids, x, rows, top_w, mods)


def _dispatch_plan(top_ir, hist, tm):
    t = top_ir.shape[0]
    n_rt = hist.shape[0]
    hist = hist.astype(jnp.int32)
    counts = jnp.sum(hist, axis=0)
    tiles = (counts + tm - 1) // tm
    tile_end = jnp.cumsum(tiles)
    base = (tile_end - tiles)[None, :] * tm + jnp.cumsum(hist, axis=0) - hist
    e = top_ir[:, :TOP_K].reshape(n_rt, -1)
    rank = top_ir[:, TOP_K:].reshape(n_rt, -1)
    pick = e[:, :, None] == jnp.arange(N_EXPERTS, dtype=jnp.int32)[None, None, :]
    dest = (jnp.sum(jnp.where(pick, base[:, None, :], 0), axis=2) + rank).reshape(-1).astype(jnp.int32)
    n_tiles = (t * TOP_K) // tm + N_EXPERTS
    tile_ids = jnp.arange(n_tiles, dtype=jnp.int32)
    tile_expert = jnp.minimum(jnp.sum((tile_ids[:, None] >= tile_end[None, :]).astype(jnp.int32), axis=1),
                              N_EXPERTS - 1).astype(jnp.int32)
    n_valid = tile_end[-1]
    partial = jnp.logical_or(tile_ids + 1 == tile_end[tile_expert], tile_ids >= n_valid)
    return dest, n_tiles, tile_expert, n_valid.reshape(1).astype(jnp.int32), partial.astype(jnp.int32)


def kernel(x_prompt, x_sample, c, cache_k, cache_v, state_ssd, c_ctx, norm1_w, norm2_w, w_mod, b_mod, w_in, w_gate,
           b_gate, hy_conv_w, hy_conv_b, hy_w1, hy_b1, hy_w2, hy_b2, hy_w3, hy_freq, hy_decay, hy_bias, ssd_conv_w,
           ssd_conv_b, ssd_a_log, ssd_dt_bias, ssd_d, ssd_norm_w, q_norm_w, k_norm_w, w_br_hy, w_br_ssd, w_br_att,
           w_out, w_router, b_router, w_e1, b_e1, w_e2, b_e2):
    n_ctx, l_ctx, d = x_prompt.shape
    n_lat, l_lat, _ = x_sample.shape
    depth = w_in.shape[0]
    t_ctx, t_lat = n_ctx * l_ctx, n_lat * l_lat
    t = t_ctx + t_lat
    kc = N_KV_HEADS * HEAD_DIM
    geo = dict(t_ctx=t_ctx, l_lat=l_lat)

    x = jnp.concatenate([x_prompt.reshape(t_ctx, d), x_sample.reshape(t_lat, d)], axis=0)
    cvec = jnp.zeros((8, d), F32).at[0].set(c_ctx).at[1:1 + n_lat].set(c)
    mods_all = modulation_all(cvec, w_mod, b_mod)
    b1_flat = b_e1.reshape(-1, 1, b_e1.shape[-1])
    b1g_all, b1u_all = b1_flat[:, :, 0::2], b1_flat[:, :, 1::2]
    new_k, new_v, new_s = [], [], []
    for l in range(depth):
        mods = mods_all[l].reshape(8 * 6, 1, d)
        wi = w_in[l]
        w_proj = jnp.concatenate([wi[:, 0:3072], wi[:, 3088:3856], wi[:, 3072:3088],
                                  jnp.zeros((d, P_COLS - 3856), F32)], axis=1).astype(BF16)
        nw1 = norm1_w[l].reshape(1, d)
        proj = norm_mod_matmul(x, nw1, mods, w_proj, jnp.zeros((1, P_COLS), F32), sigmoid=False, out_dtype=F32,
                               **geo)
        gate = norm_mod_matmul(x, nw1, mods, w_gate[l].astype(BF16), b_gate[l].reshape(1, -1), sigmoid=True,
                               out_dtype=BF16, **geo)

        x0, p = hyena_pre(proj, hy_conv_w[l], hy_conv_b[l], t_ctx=t_ctx, l_ctx=l_ctx, l_lat=l_lat)
        hy_args = (hy_w1[l], hy_b1[l], hy_w2[l], hy_b2[l], hy_w3[l], hy_freq[l], hy_decay[l])
        y_hy_ctx = hyena_ctx(x0, p, hyena_filter(l_ctx, *hy_args), hy_bias[l], n_seq=n_ctx, seq_len=l_ctx)
        y_hy_lat = hyena_lat(x0[t_ctx:], p[t_ctx:], hyena_filter(l_lat, *hy_args), hy_bias[l], n_seq=n_lat,
                             seq_len=l_lat)

        ssd_args = (ssd_conv_w[l], ssd_conv_b[l], ssd_dt_bias[l], ssd_a_log[l], ssd_d[l])
        yf_c, yb_c, fin_c = ssd_scan(proj, None, *ssd_args, row0=0, n_seq=n_ctx, seq_len=l_ctx)
        yf_l, yb_l, _ = ssd_scan(proj, state_ssd[:, l], *ssd_args, row0=t_ctx, n_seq=n_lat, seq_len=l_lat)

        qn, kn = qk_prep(proj, q_norm_w[l], k_norm_w[l], **geo)
        v_all = proj[:, P_V:P_V + kc]
        k_ctx = kn[:t_ctx].reshape(n_ctx, l_ctx, kc)
        v_ctx = v_all[:t_ctx].reshape(n_ctx, l_ctx, kc)
        att_ctx = attention(qn, k_ctx.astype(BF16), _values_t(v_ctx), row0=0, seq_len=l_ctx, tq=l_ctx)
        k_lat = jnp.concatenate([kn[t_ctx:].reshape(n_lat, l_lat, kc), cache_k[:, l].reshape(n_lat, -1, kc)], axis=1)
        v_lat = jnp.concatenate([v_all[t_ctx:].reshape(n_lat, l_lat, kc), cache_v[:, l].reshape(n_lat, -1, kc)],
                                axis=1)
        att_lat = attention(qn, k_lat.astype(BF16), _values_t(v_lat), row0=t_ctx, seq_len=l_lat, tq=128)

        wr = jnp.pad(w_router[l], ((0, 0), (0, LANES - N_EXPERTS)))
        br = jnp.pad(b_router[l], (0, LANES - N_EXPERTS)).reshape(1, LANES)
        x, h2, top_ir, top_w, hist = merge_router(
            x, (y_hy_ctx, y_hy_lat), (yf_c, yf_l), (yb_c, yb_l), proj, (att_ctx, att_lat), gate, ssd_norm_w[l],
            w_br_hy[l].astype(BF16), w_br_ssd[l].astype(BF16),
            w_br_att[l].astype(BF16), w_out[l].astype(BF16), mods, norm2_w[l].reshape(1, d), wr, br, **geo)

        dest, n_tiles, tile_expert, n_valid, partial = _dispatch_plan(
            top_ir[:, :2 * TOP_K], hist[::8, :N_EXPERTS], MOE_TILE)
        xs = moe_dispatch(dest, partial, h2, n_tiles * MOE_TILE)
        out_sorted = moe_experts(
            xs, tile_expert, n_valid, w_e1.reshape((-1,) + w_e1.shape[2:]), b1g_all, b1u_all,
            w_e2.reshape((-1,) + w_e2.shape[2:]), b_e2.reshape(-1, 1, d), layer=l)
        x = moe_combine(x, dest, out_sorted, top_w, mods, **geo)

        new_k.append(k_ctx.reshape(n_ctx, l_ctx, N_KV_HEADS, HEAD_DIM))
        new_v.append(v_ctx.reshape(n_ctx, l_ctx, N_KV_HEADS, HEAD_DIM))
        new_s.append(fin_c)

    y_prompt = x[:t_ctx].reshape(n_ctx, l_ctx, d)
    y_sample = x[t_ctx:].reshape(n_lat, l_lat, d)
    return (y_prompt, y_sample, jnp.stack(new_k, axis=1), jnp.stack(new_v, axis=1), jnp.stack(new_s, axis=1))
```

```python
import functools
import math

import numpy as np
import jax
import jax.numpy as jnp
from jax import lax
from jax.experimental import pallas as pl
from jax.experimental.pallas import tpu as pltpu

F32 = jnp.float32
BF16 = jnp.bfloat16

EPS = 1e-6
GRID_W = 64
HY_DIM = 512
SSD_INNER = 512
SSD_HEADDIM = 64
SSD_HEADS = 8
SSD_GROUPS = 2
SSD_STATE = 128
SSD_CHUNK = 128
SSD_STEP_CHUNKS = 2
N_HEADS = 8
N_KV_HEADS = 2
HEAD_DIM = 64
ROPE_THETA = 10000.0
N_EXPERTS = 32
TOP_K = 4
TOP_K_SHIFT = 2
SWIGLU_ALPHA = 1.702
SWIGLU_LIMIT = 7.0

P_HY, P_Z, P_XBC, P_Q, P_K, P_V, P_DT, P_COLS = 0, 1536, 2048, 3072, 3584, 3712, 3840, 3968

VMEM_LIMIT = 56 * 1024 * 1024
LANES = 128
ROW_TILE = 256
MOE_TILE = 512
DFT_N1, DFT_N2 = 64, 128
STAGE2_K1 = 8


def _cp(sem, vmem=VMEM_LIMIT):
    return pltpu.CompilerParams(dimension_semantics=sem, vmem_limit_bytes=vmem)


def _sigmoid(x):
    return 1.0 / (1.0 + jnp.exp(-x))


def _silu(x):
    return x * _sigmoid(x)


def _softplus(x):
    return jnp.maximum(x, 0.0) + jnp.log(1.0 + jnp.exp(-jnp.abs(x)))


def _dot3(a, b):
    ah = a.astype(BF16)
    al = (a - ah.astype(F32)).astype(BF16)
    bh = b.astype(BF16)
    bl = (b - bh.astype(F32)).astype(BF16)
    dot = lambda u, v: jnp.dot(u, v, preferred_element_type=F32)
    return dot(ah, bh) + (dot(ah, bl) + dot(al, bh))


def _mod_kernel(c_ref, w_ref, b_ref, o_ref):
    s = _silu(c_ref[...])
    o_ref[0] = _dot3(s, w_ref[0]) + b_ref[0]


def modulation_all(cvec, w_mod, b_mod):
    depth, d, n = w_mod.shape
    tn = 1536
    return pl.pallas_call(
        _mod_kernel,
        out_shape=jax.ShapeDtypeStruct((depth, 8, n), F32),
        grid=(depth, n // tn),
        in_specs=[pl.BlockSpec((8, d), lambda l, j: (0, 0)),
                  pl.BlockSpec((1, d, tn), lambda l, j: (l, 0, j)),
                  pl.BlockSpec((1, 1, tn), lambda l, j: (l, 0, j))],
        out_specs=pl.BlockSpec((1, 8, tn), lambda l, j: (l, 0, j)),
        compiler_params=_cp(("arbitrary", "arbitrary")),
        name="modulation",
    )(cvec, w_mod, b_mod.reshape(depth, 1, n))


def _mod_row(i, tm, t_ctx, l_lat):
    n_ctx = t_ctx // tm
    per = l_lat // tm
    return jnp.where(i < n_ctx, 0, 1 + (i - n_ctx) // per)


def _mod_spec(k, tm, t_ctx, l_lat):
    return pl.BlockSpec((1, 1, 1024), lambda i: (_mod_row(i, tm, t_ctx, l_lat) * 6 + k, 0, 0))


def _nmm_kernel(x_ref, nw_ref, sh_ref, sc_ref, w_ref, b_ref, o_ref, *, sigmoid):
    x = x_ref[...]
    ms = jnp.mean(x * x, axis=-1, keepdims=True)
    h = x * lax.rsqrt(ms + EPS) * nw_ref[...]
    h = h * (1.0 + sc_ref[0]) + sh_ref[0]
    acc = jnp.dot(h.astype(BF16), w_ref[...], preferred_element_type=F32) + b_ref[...]
    if sigmoid:
        acc = _sigmoid(acc)
    o_ref[...] = acc.astype(o_ref.dtype)


def norm_mod_matmul(x, nw, mods, w, b, *, t_ctx, l_lat, sigmoid, out_dtype, tm=512):
    t, d = x.shape
    n = w.shape[1]
    return pl.pallas_call(
        functools.partial(_nmm_kernel, sigmoid=sigmoid),
        out_shape=jax.ShapeDtypeStruct((t, n), out_dtype),
        grid=(t // tm,),
        in_specs=[pl.BlockSpec((tm, d), lambda i: (i, 0)),
                  pl.BlockSpec((1, d), lambda i: (0, 0)),
                  _mod_spec(0, tm, t_ctx, l_lat),
                  _mod_spec(1, tm, t_ctx, l_lat),
                  pl.BlockSpec((d, n), lambda i: (0, 0)),
                  pl.BlockSpec((1, n), lambda i: (0, 0))],
        out_specs=pl.BlockSpec((tm, n), lambda i: (i, 0)),
        compiler_params=_cp(("arbitrary",)),
        name="norm_mod_matmul",
    )(x, nw, mods, mods, w, b)


def _seq_edges(i, tr, t_ctx, l_ctx, l_lat):
    tok = i * tr
    pos = jnp.where(tok < t_ctx, tok % l_ctx, (tok - t_ctx) % l_lat)
    length = jnp.where(tok < t_ctx, l_ctx, l_lat)
    return pos == 0, pos + tr == length


def _conv3(x, prev8, next8, w_ref, b_ref, first, last):
    tr = x.shape[0]
    row = lax.broadcasted_iota(jnp.int32, x.shape, 0)
    pm = jnp.where(first, 0.0, 1.0)
    nm = jnp.where(last, 0.0, 1.0)
    xm1 = jnp.where(row == 0, prev8[7:8, :] * pm, pltpu.roll(x, 1, axis=0))
    xp1 = jnp.where(row == tr - 1, next8[0:1, :] * nm, pltpu.roll(x, tr - 1, axis=0))
    return b_ref[...] + xm1 * w_ref[0:1, :] + x * w_ref[1:2, :] + xp1 * w_ref[2:3, :]


def _conv_specs(tr, tc, col_blk, n_rows):
    r8 = tr // 8
    last8 = n_rows // 8 - 1
    return [pl.BlockSpec((tr, tc), lambda i, j: (i, col_blk(j))),
            pl.BlockSpec((8, tc), lambda i, j: (jnp.maximum(i * r8 - 1, 0), col_blk(j))),
            pl.BlockSpec((8, tc), lambda i, j: (jnp.minimum((i + 1) * r8, last8), col_blk(j)))]


def _hy_pre_kernel(*refs, tr, t_ctx, l_ctx, l_lat):
    (x0, x0p, x0n, x1, x1p, x1n, xv, xvp, xvn, w0, w1, wv, b0, b1, bv, o0_ref, op_ref) = refs
    first, last = _seq_edges(pl.program_id(0), tr, t_ctx, l_ctx, l_lat)
    o0_ref[...] = _conv3(x0[...], x0p[...], x0n[...], w0, b0, first, last)
    u1 = _conv3(x1[...], x1p[...], x1n[...], w1, b1, first, last)
    uv = _conv3(xv[...], xvp[...], xvn[...], wv, bv, first, last)
    op_ref[...] = u1 * uv


def hyena_pre(proj, w, b, *, t_ctx, l_ctx, l_lat):
    t = proj.shape[0]
    tr, tc = ROW_TILE, HY_DIM
    nb = HY_DIM // tc
    b2 = b.reshape(1, 3 * HY_DIM)
    specs = []
    for s in range(3):
        specs += _conv_specs(tr, tc, lambda j, s=s: P_HY // tc + s * nb + j, t)
    specs += [pl.BlockSpec((3, tc), lambda i, j, s=s: (0, s * nb + j)) for s in range(3)]
    specs += [pl.BlockSpec((1, tc), lambda i, j, s=s: (0, s * nb + j)) for s in range(3)]
    return pl.pallas_call(
        functools.partial(_hy_pre_kernel, tr=tr, t_ctx=t_ctx, l_ctx=l_ctx, l_lat=l_lat),
        out_shape=(jax.ShapeDtypeStruct((t, HY_DIM), F32), jax.ShapeDtypeStruct((t, HY_DIM), F32)),
        grid=(t // tr, nb),
        in_specs=specs,
        out_specs=(pl.BlockSpec((tr, tc), lambda i, j: (i, j)), pl.BlockSpec((tr, tc), lambda i, j: (i, j))),
        compiler_params=_cp(("arbitrary", "arbitrary")),
        name="hyena_pre",
    )(*([proj] * 9), w, w, w, b2, b2, b2)


def _filter_kernel(z_ref, w1_ref, b1_ref, w2_ref, b2_ref, w3_ref, fr_ref, dec_ref, o_ref, *, zero_row):
    z = z_ref[...]
    fr = fr_ref[...]
    h = jnp.sin(fr * (_dot3(z, w1_ref[...]) + b1_ref[...]))
    h = jnp.sin(fr * (_dot3(h, w2_ref[...]) + b2_ref[...]))
    f = _dot3(h, w3_ref[...])
    f = f * jnp.exp(-z[:, 0:1] * jnp.abs(dec_ref[...]))
    row = pl.program_id(0) * f.shape[0] + lax.broadcasted_iota(jnp.int32, f.shape, 0)
    o_ref[...] = jnp.where(row == zero_row, 0.0, f)


def _filter_embedding(seq_len, emb):
    bands_n = (emb - 1) // 2
    t = jnp.linspace(0.0, 1.0, seq_len, dtype=F32)[:, None]
    bands = jnp.linspace(1e-4, bands_n - 1, bands_n, dtype=F32)[None, :]
    ang = (2.0 * math.pi / seq_len) * jnp.arange(seq_len, dtype=F32)[:, None] * bands
    z = jnp.concatenate([t, jnp.cos(ang), -jnp.sin(ang)], axis=-1)
    return jnp.pad(z, ((0, 0), (0, LANES - emb)))


def hyena_filter(seq_len, w1, b1, w2, b2, w3, freq, decay):
    emb, ff = w1.shape
    ch = w3.shape[1] // 2
    lags = np.concatenate([np.arange(seq_len), [0], np.arange(seq_len - 1, 0, -1)])
    z = _filter_embedding(seq_len, emb)[lags]
    padc = LANES - ff
    w1p = jnp.pad(w1, ((0, LANES - emb), (0, padc)))
    w2p = jnp.pad(w2, ((0, padc), (0, padc)))
    w3p = jnp.pad(w3, ((0, padc), (0, 0)))
    row = lambda v: jnp.pad(v, (0, padc)).reshape(1, LANES)
    tr = 256
    fwd_tiles = seq_len // tr
    full = lambda shape: pl.BlockSpec(shape, lambda i: (0, 0))
    half = lambda shape: pl.BlockSpec(shape, lambda i: (0, jnp.where(i < fwd_tiles, 0, 1)))
    return pl.pallas_call(
        functools.partial(_filter_kernel, zero_row=seq_len),
        out_shape=jax.ShapeDtypeStruct((2 * seq_len, ch), F32),
        grid=(2 * seq_len // tr,),
        in_specs=[pl.BlockSpec((tr, LANES), lambda i: (i, 0)), full((LANES, LANES)), full((1, LANES)),
                  full((LANES, LANES)), full((1, LANES)), half((LANES, ch)), full((1, LANES)), half((1, ch))],
        out_specs=pl.BlockSpec((tr, ch), lambda i: (i, 0)),
        compiler_params=_cp(("arbitrary",)),
        name="hyena_filter",
    )(z, w1p, row(b1), w2p, row(b2), w3p, row(freq), decay.reshape(1, 2 * ch))


def _cs(n_rows, n_cols, period):
    ang = 2.0 * np.pi * (np.outer(np.arange(n_rows), np.arange(n_cols)) % period) / period
    return np.cos(ang), np.sin(ang)


def _mm_kernel(a_ref, b_ref, o_ref):
    o_ref[0] = _dot3(a_ref[...], b_ref[0])


def const_matmul(a, b, tn):
    m, k = a.shape
    bsz, _, n = b.shape
    return pl.pallas_call(
        _mm_kernel,
        out_shape=jax.ShapeDtypeStruct((bsz, m, n), F32),
        grid=(bsz, n // tn),
        in_specs=[pl.BlockSpec((m, k), lambda s, j: (0, 0)), pl.BlockSpec((1, k, tn), lambda s, j: (s, 0, j))],
        out_specs=pl.BlockSpec((1, m, tn), lambda s, j: (s, 0, j)),
        compiler_params=_cp(("arbitrary", "arbitrary")),
        name="const_matmul",
    )(a, b)


def _hy_ctx_kernel(x0_ref, p_ref, kr_ref, ki_ref, fw_ref, iv_ref, bias_ref, o_ref, *, n):
    p = p_ref[...]
    xf = _dot3(fw_ref[...], p)
    xr, xi = xf[:n], xf[n:]
    kr, ki = kr_ref[...], ki_ref[...]
    yr = xr * kr - xi * ki
    yi = xr * ki + xi * kr
    y = _dot3(iv_ref[...], jnp.concatenate([yr, yi], axis=0))
    o_ref[...] = x0_ref[...] * (y + p * bias_ref[...])


def hyena_ctx(x0, p, filt, bias, *, n_seq, seq_len):
    n = 2 * seq_len
    ch = p.shape[1]
    c_full, s_full = _cs(n, n, n)
    fw_full = jnp.asarray(np.concatenate([c_full, -s_full], axis=0), F32)
    fw_half = fw_full[:, :seq_len]
    iv = jnp.asarray(np.concatenate([c_full[:seq_len], -s_full[:seq_len]], axis=1) / n, F32)
    kf = const_matmul(fw_full, filt[None], ch)[0]
    kr, ki = kf[:n], kf[n:]
    full = lambda shape: pl.BlockSpec(shape, lambda s: (0, 0))
    return pl.pallas_call(
        functools.partial(_hy_ctx_kernel, n=n),
        out_shape=jax.ShapeDtypeStruct((n_seq * seq_len, ch), F32),
        grid=(n_seq,),
        in_specs=[pl.BlockSpec((seq_len, ch), lambda s: (s, 0)), pl.BlockSpec((seq_len, ch), lambda s: (s, 0)),
                  full((n, ch)), full((n, ch)), full((2 * n, seq_len)), full((seq_len, 2 * n)), full((1, ch))],
        out_specs=pl.BlockSpec((seq_len, ch), lambda s: (s, 0)),
        compiler_params=_cp(("arbitrary",)),
        name="hyena_ctx",
    )(x0, p, kr, ki, fw_half, iv, bias.reshape(1, ch))


def _stage2_kernel(a_ref, twr_ref, twi_ref, m_ref, *rest, conv):
    if conv:
        kf_ref, mi_ref, o_ref = rest
    else:
        (o_ref,) = rest
    for j in range(STAGE2_K1):
        ar, ai = a_ref[0, 0, j], a_ref[0, 1, j]
        twr, twi = twr_ref[j], twi_ref[j]
        br = ar * twr - ai * twi
        bi = ar * twi + ai * twr
        x = _dot3(m_ref[...], jnp.concatenate([br, bi], axis=0))
        n2 = ar.shape[0]
        xr, xi = x[:n2], x[n2:]
        if not conv:
            o_ref[0, 0, j] = xr
            o_ref[0, 1, j] = xi
            continue
        kr, ki = kf_ref[0, 0, j], kf_ref[0, 1, j]
        yr = xr * kr - xi * ki
        yi = xr * ki + xi * kr
        pq = _dot3(mi_ref[...], jnp.concatenate([yr, yi], axis=0))
        pr, pi = pq[:n2], pq[n2:]
        o_ref[0, 0, j] = pr * twr + pi * twi
        o_ref[0, 1, j] = pi * twr - pr * twi


def _stage2(a, kf, ch):
    bsz = a.shape[0]
    n1, n2 = DFT_N1, DFT_N2
    n = n1 * n2
    tw_ang = 2.0 * np.pi * np.outer(np.arange(n1), np.arange(n2)) / n
    twr = jnp.asarray(np.cos(tw_ang), F32).reshape(n1, n2, 1)
    twi = jnp.asarray(-np.sin(tw_ang), F32).reshape(n1, n2, 1)
    c2, s2 = _cs(n2, n2, n2)
    m_fwd = jnp.asarray(np.block([[c2, s2], [-s2, c2]]), F32)
    m_inv = jnp.asarray(np.block([[c2, -s2], [s2, c2]]), F32)
    conv = kf is not None
    kb = STAGE2_K1
    blk = pl.BlockSpec((1, 2, kb, n2, ch), lambda s, k: (s, 0, k, 0, 0))
    tw_spec = pl.BlockSpec((kb, n2, 1), lambda s, k: (k, 0, 0))
    specs = [blk, tw_spec, tw_spec, pl.BlockSpec((2 * n2, 2 * n2), lambda s, k: (0, 0))]
    args = [a, twr, twi, m_fwd]
    if conv:
        specs += [pl.BlockSpec((1, 2, kb, n2, ch), lambda s, k: (0, 0, k, 0, 0)),
                  pl.BlockSpec((2 * n2, 2 * n2), lambda s, k: (0, 0))]
        args += [kf, m_inv]
    return pl.pallas_call(
        functools.partial(_stage2_kernel, conv=conv),
        out_shape=jax.ShapeDtypeStruct(a.shape, F32),
        grid=(bsz, n1 // kb),
        in_specs=specs,
        out_specs=blk,
        compiler_params=_cp(("arbitrary", "arbitrary")),
        name="hyena_stage2",
    )(*args)


def _hy_post_kernel(g_ref, q_ref, x0_ref, p_ref, bias_ref, o_ref):
    y = _dot3(g_ref[...], q_ref[0])
    o_ref[0] = x0_ref[0] * (y + p_ref[0] * bias_ref[...])


def hyena_lat(x0, p, filt, bias, *, n_seq, seq_len):
    n1, n2 = DFT_N1, DFT_N2
    n = n1 * n2
    assert n == 2 * seq_len
    ch = p.shape[1]
    h1 = n1 // 2
    wide = n2 * ch
    c1, s1 = _cs(n1, n1, n1)
    f1_full = jnp.asarray(np.concatenate([c1, -s1], axis=0), F32)
    f1_half = f1_full[:, :h1]
    g1 = jnp.asarray(np.concatenate([c1[:h1], -s1[:h1]], axis=1) / n, F32)
    tn = 8192
    kcirc = filt.reshape(1, n1, wide)
    kf = _stage2(const_matmul(f1_full, kcirc, tn).reshape(1, 2, n1, n2, ch), None, ch)
    a = const_matmul(f1_half, p.reshape(n_seq, h1, wide), tn).reshape(n_seq, 2, n1, n2, ch)
    q = _stage2(a, kf, ch).reshape(n_seq, 2 * n1, wide)
    bias_w = jnp.tile(bias, n2).reshape(1, wide)
    out = pl.pallas_call(
        _hy_post_kernel,
        out_shape=jax.ShapeDtypeStruct((n_seq, h1, wide), F32),
        grid=(n_seq, wide // tn),
        in_specs=[pl.BlockSpec((h1, 2 * n1), lambda s, j: (0, 0)),
                  pl.BlockSpec((1, 2 * n1, tn), lambda s, j: (s, 0, j)),
                  pl.BlockSpec((1, h1, tn), lambda s, j: (s, 0, j)),
                  pl.BlockSpec((1, h1, tn), lambda s, j: (s, 0, j)),
                  pl.BlockSpec((1, tn), lambda s, j: (0, j))],
        out_specs=pl.BlockSpec((1, h1, tn), lambda s, j: (s, 0, j)),
        compiler_params=_cp(("arbitrary", "arbitrary")),
        name="hyena_post",
    )(g1, q, x0.reshape(n_seq, h1, wide), p.reshape(n_seq, h1, wide), bias_w)
    return out.reshape(n_seq * seq_len, ch)


def _dot_01(x, m, *, x_is_lhs):
    x1 = x.astype(BF16)
    r1 = x - x1.astype(F32)
    x2 = r1.astype(BF16)
    x3 = (r1 - x2.astype(F32)).astype(BF16)
    mb = m.astype(BF16)
    if x_is_lhs:
        dot = lambda u: jnp.dot(u, mb, preferred_element_type=F32)
    else:
        dot = lambda u: jnp.dot(mb, u, preferred_element_type=F32)
    return dot(x1) + (dot(x2) + dot(x3))


def _ssd_dir(xbc, dtraw, st_ref, d, consts, y_ref):
    tri, expand, dtb, acont, dskip = consts
    cl = xbc.shape[0]
    xs = xbc[:, :SSD_INNER]
    dt = _softplus(dtraw + dtb)
    a = dt * acont
    tri_d = tri if d == 0 else tri.T
    cs = _dot_01(a, tri_d, x_is_lhs=False)
    cs_t = _dot_01(a.T, tri_d.T, x_is_lhs=True)
    dt_x = _dot_01(dt, expand[d], x_is_lhs=True)
    xdt = xs * dt_x
    row = lax.broadcasted_iota(jnp.int32, (cl, cl), 0)
    col = lax.broadcasted_iota(jnp.int32, (cl, cl), 1)
    keep = (col <= row) if d == 0 else (col >= row)
    lane = lax.broadcasted_iota(jnp.int32, (cl, LANES), 1)
    low = lane < SSD_HEADDIM
    edge = cl - 1 if d == 0 else 0
    for g in range(SSD_GROUPS):
        bg = xbc[:, SSD_INNER + g * SSD_STATE: SSD_INNER + (g + 1) * SSD_STATE]
        cg = xbc[:, SSD_INNER + (SSD_GROUPS + g) * SSD_STATE: SSD_INNER + (SSD_GROUPS + g + 1) * SSD_STATE]
        bg16, cg16 = bg.astype(BF16), cg.astype(BF16)
        cb = lax.dot_general(cg16, bg16, (((1,), (1,)), ((), ())), preferred_element_type=F32)
        for pr in range(2):
            pair = g * 2 + pr
            h0 = 2 * pair
            ms = []
            for h in (h0, h0 + 1):
                ln = d * SSD_HEADS + h
                diff = cs[:, ln:ln + 1] - cs_t[ln:ln + 1, :]
                ms.append(jnp.where(keep, cb * jnp.exp(jnp.minimum(diff, 0.0)), 0.0).astype(BF16))
            xp = xdt[:, pair * LANES:(pair + 1) * LANES]
            xs_p = xs[:, pair * LANES:(pair + 1) * LANES]
            xlo = jnp.where(low, xp, 0.0).astype(BF16)
            xhi = jnp.where(low, 0.0, xp).astype(BF16)
            y_diag = (jnp.dot(ms[0], xlo, preferred_element_type=F32)
                      + jnp.dot(ms[1], xhi, preferred_element_type=F32))
            l0, l1 = d * SSD_HEADS + h0, d * SSD_HEADS + h0 + 1
            e_cs = jnp.where(low, jnp.exp(cs[:, l0:l0 + 1]), jnp.exp(cs[:, l1:l1 + 1]))
            st = st_ref[pair]
            y_off = jnp.dot(cg16, st.astype(BF16), preferred_element_type=F32) * e_cs
            y_ref[:, pair * LANES:(pair + 1) * LANES] = (
                y_diag + y_off + xs_p * dskip[d:d + 1, pair * LANES:(pair + 1) * LANES])
            tot0, tot1 = cs[edge:edge + 1, l0:l0 + 1], cs[edge:edge + 1, l1:l1 + 1]
            dec = jnp.where(low, jnp.exp(tot0 - cs[:, l0:l0 + 1]), jnp.exp(tot1 - cs[:, l1:l1 + 1]))
            upd = lax.dot_general(bg16, (xp * dec).astype(BF16), (((0,), (0,)), ((), ())),
                                  preferred_element_type=F32)
            st_ref[pair] = st * jnp.where(low[0:1], jnp.exp(tot0), jnp.exp(tot1)) + upd


def _ssd_kernel(xf_ref, xfp_ref, xfn_ref, dtf_ref, xb_ref, xbp_ref, xbn_ref, dtb_ref, init_ref, cw_ref, cb_ref,
                tri_ref, exp_ref, dtbias_ref, acont_ref, dskip_ref, yf_ref, yb_ref, fin_ref, st_ref, *, has_init):
    s = pl.program_id(1)
    at_start, at_end = s == 0, s == pl.num_programs(1) - 1

    @pl.when(at_start)
    def _():
        if has_init:
            st_ref[...] = init_ref[0]
        else:
            st_ref[...] = jnp.zeros_like(st_ref)

    consts = (tri_ref[...], (exp_ref[0], exp_ref[1]), dtbias_ref[...], acont_ref[...], dskip_ref[...])
    xbc_f = _silu(_conv3(xf_ref[...], xfp_ref[...], xfn_ref[...], cw_ref, cb_ref, at_start, at_end))
    xbc_b = _silu(_conv3(xb_ref[...], xbp_ref[...], xbn_ref[...], cw_ref, cb_ref, at_end, at_start))
    cl = SSD_CHUNK
    for c in range(SSD_STEP_CHUNKS):
        rf = slice(c * cl, (c + 1) * cl)
        _ssd_dir(xbc_f[rf], dtf_ref[rf, :], st_ref.at[0], 0, consts, yf_ref.at[rf])
        rb = slice((SSD_STEP_CHUNKS - 1 - c) * cl, (SSD_STEP_CHUNKS - c) * cl)
        _ssd_dir(xbc_b[rb], dtb_ref[rb, :], st_ref.at[1], 1, consts, yb_ref.at[rb])

    @pl.when(s == pl.num_programs(1) - 1)
    def _():
        fin_ref[0] = st_ref[...]


def _pair_states(s):
    b = s.shape[0]
    s = s.reshape(b, 2, SSD_HEADS // 2, 2, SSD_HEADDIM, SSD_STATE)
    return s.transpose(0, 1, 2, 5, 3, 4).reshape(b, 2, SSD_HEADS // 2, SSD_STATE, 2 * SSD_HEADDIM)


def _unpair_states(s):
    b = s.shape[0]
    s = s.reshape(b, 2, SSD_HEADS // 2, SSD_STATE, 2, SSD_HEADDIM)
    return s.transpose(0, 1, 2, 4, 5, 3).reshape(b, 2, SSD_HEADS, SSD_HEADDIM, SSD_STATE)


def ssd_scan(proj, init, conv_w, conv_b, dt_bias, a_log, d_skip, *, row0, n_seq, seq_len):
    cl = SSD_CHUNK
    bl = SSD_STEP_CHUNKS * cl
    nc = seq_len // bl
    base = row0 // bl
    has_init = init is not None
    hp = SSD_HEADS // 2
    init_p = _pair_states(init) if has_init else jnp.zeros((1, 2, hp, SSD_STATE, LANES), F32)
    tri = jnp.asarray(np.tril(np.ones((cl, cl))), F32)
    expand = np.zeros((2, LANES, SSD_INNER), np.float32)
    for d in range(2):
        for h in range(SSD_HEADS):
            expand[d, d * SSD_HEADS + h, h * SSD_HEADDIM:(h + 1) * SSD_HEADDIM] = 1.0
    pad16 = lambda v: jnp.pad(v.reshape(1, 2 * SSD_HEADS), ((0, 0), (0, LANES - 2 * SSD_HEADS)))
    acont = pad16(-jnp.exp(a_log))
    dtb = pad16(dt_bias)
    dskip = jnp.repeat(d_skip, SSD_HEADDIM, axis=1)
    cxbc = conv_w.shape[1]
    xblk = P_XBC // cxbc
    dtblk = P_DT // LANES
    r8 = bl // 8
    last8 = proj.shape[0] // 8 - 1
    full2 = lambda shape: pl.BlockSpec(shape, lambda b, s: (0,) * len(shape))
    fwd = lambda b, s: base + b * nc + s
    bwd = lambda b, s: base + b * nc + (nc - 1 - s)

    def stream(chunk):
        return [pl.BlockSpec((bl, cxbc), lambda b, s: (chunk(b, s), xblk)),
                pl.BlockSpec((8, cxbc), lambda b, s: (jnp.maximum(chunk(b, s) * r8 - 1, 0), xblk)),
                pl.BlockSpec((8, cxbc), lambda b, s: (jnp.minimum((chunk(b, s) + 1) * r8, last8), xblk)),
                pl.BlockSpec((bl, LANES), lambda b, s: (chunk(b, s), dtblk))]

    st_spec = pl.BlockSpec((1, 2, hp, SSD_STATE, LANES), lambda b, s: (b if has_init else 0, 0, 0, 0, 0))
    yf, yb, fin = pl.pallas_call(
        functools.partial(_ssd_kernel, has_init=has_init),
        out_shape=(jax.ShapeDtypeStruct((n_seq * seq_len, SSD_INNER), F32),
                   jax.ShapeDtypeStruct((n_seq * seq_len, SSD_INNER), F32),
                   jax.ShapeDtypeStruct((n_seq, 2, hp, SSD_STATE, LANES), F32)),
        grid=(n_seq, nc),
        in_specs=stream(fwd) + stream(bwd) + [
            st_spec, full2((3, cxbc)), full2((1, cxbc)), full2((cl, cl)), full2((2, LANES, SSD_INNER)),
            full2((1, LANES)), full2((1, LANES)), full2((2, SSD_INNER))],
        out_specs=(pl.BlockSpec((bl, SSD_INNER), lambda b, s: (b * nc + s, 0)),
                   pl.BlockSpec((bl, SSD_INNER), lambda b, s: (b * nc + (nc - 1 - s), 0)),
                   pl.BlockSpec((1, 2, hp, SSD_STATE, LANES), lambda b, s: (b, 0, 0, 0, 0))),
        scratch_shapes=[pltpu.VMEM((2, hp, SSD_STATE, LANES), F32)],
        compiler_params=_cp(("arbitrary", "arbitrary")),
        name="ssd_scan",
    )(*([proj] * 8), init_p, conv_w, conv_b.reshape(1, cxbc), tri, jnp.asarray(expand), dtb, acont, dskip)
    return yf, yb, _unpair_states(fin)


def _headnorm(x, g_ref, w_ref):
    ms = _dot3(x * x, g_ref[...])
    return x * lax.rsqrt(ms + EPS) * w_ref[...]


def _rope(x, cos, sin_signed):
    lane = lax.broadcasted_iota(jnp.int32, x.shape, 1)
    w = x.shape[1]
    swapped = jnp.where(lane % 2 == 0, pltpu.roll(x, w - 1, axis=1), pltpu.roll(x, 1, axis=1))
    return x * cos + swapped * sin_signed


Q_SCALE = HEAD_DIM ** -0.5 * math.log2(math.e)


def _store_padded_heads(qo_ref, q):
    rep = N_HEADS // N_KV_HEADS
    lane = lax.broadcasted_iota(jnp.int32, (q.shape[0], LANES), 1)
    for h in range(N_HEADS):
        g = h // rep
        chunk = q[:, (h // 2) * LANES:(h // 2 + 1) * LANES]
        if h % 2 != g:
            chunk = pltpu.roll(chunk, HEAD_DIM, axis=1)
        keep = (lane >= g * HEAD_DIM) & (lane < (g + 1) * HEAD_DIM)
        qo_ref[:, h * LANES:(h + 1) * LANES] = jnp.where(keep, chunk, 0.0).astype(qo_ref.dtype)


def _qk_kernel(q_ref, k_ref, cos_ref, sin_ref, gq_ref, gk_ref, qw_ref, kw_ref, qo_ref, ko_ref, *, n_ctx_tiles):
    q = _headnorm(q_ref[...], gq_ref, qw_ref)
    k = _headnorm(k_ref[...], gk_ref, kw_ref)
    is_lat = pl.program_id(0) >= n_ctx_tiles

    @pl.when(is_lat)
    def _():
        cos, sin = cos_ref[...], sin_ref[...]
        _store_padded_heads(qo_ref, _rope(q, cos, sin) * Q_SCALE)
        ko_ref[...] = _rope(k, cos[:, :k.shape[1]], sin[:, :k.shape[1]])

    @pl.when(jnp.logical_not(is_lat))
    def _():
        _store_padded_heads(qo_ref, q * Q_SCALE)
        ko_ref[...] = k


def _rope_tables(seq_len):
    n_rows = seq_len // GRID_W
    row = jnp.repeat(jnp.arange(n_rows), GRID_W).astype(F32)
    col = jnp.tile(jnp.arange(GRID_W), n_rows).astype(F32)
    n_freq = HEAD_DIM // 4
    inv = ROPE_THETA ** (-jnp.arange(n_freq, dtype=F32) / n_freq)
    ang = jnp.concatenate([row[:, None] * inv, col[:, None] * inv], axis=-1)
    cos = jnp.repeat(jnp.cos(ang), 2, axis=1)
    sin = jnp.repeat(jnp.sin(ang), 2, axis=1) * jnp.tile(jnp.asarray([-1.0, 1.0], F32), HEAD_DIM // 2)
    return jnp.tile(cos, (1, N_HEADS)), jnp.tile(sin, (1, N_HEADS))


def qk_prep(proj, q_norm_w, k_norm_w, *, t_ctx, l_lat):
    t = proj.shape[0]
    tr = 2 * ROW_TILE
    qc, kc = N_HEADS * HEAD_DIM, N_KV_HEADS * HEAD_DIM
    cos, sin = _rope_tables(l_lat)
    group = lambda c: jnp.asarray(np.kron(np.eye(c // HEAD_DIM), np.ones((HEAD_DIM, HEAD_DIM))) / HEAD_DIM, F32)
    n_ctx_tiles = t_ctx // tr
    per = l_lat // tr
    tab = pl.BlockSpec((tr, qc), lambda i: (jnp.maximum(i - n_ctx_tiles, 0) % per, 0))
    full = lambda shape: pl.BlockSpec(shape, lambda i: (0, 0))
    return pl.pallas_call(
        functools.partial(_qk_kernel, n_ctx_tiles=n_ctx_tiles),
        out_shape=(jax.ShapeDtypeStruct((t, N_HEADS * LANES), BF16), jax.ShapeDtypeStruct((t, kc), F32)),
        grid=(t // tr,),
        in_specs=[pl.BlockSpec((tr, qc), lambda i: (i, P_Q // qc)), pl.BlockSpec((tr, kc), lambda i: (i, P_K // kc)),
                  tab, tab, full((qc, qc)), full((kc, kc)), full((1, qc)), full((1, kc))],
        out_specs=(pl.BlockSpec((tr, N_HEADS * LANES), lambda i: (i, 0)), pl.BlockSpec((tr, kc), lambda i: (i, 0))),
        compiler_params=_cp(("arbitrary",)),
        name="qk_prep",
    )(proj, proj, cos, sin, group(qc), group(kc), jnp.tile(q_norm_w, N_HEADS).reshape(1, qc),
      jnp.tile(k_norm_w, N_KV_HEADS).reshape(1, kc))


def _attention_kernel(q_ref, k_ref, vt_ref, o_ref, *, tq):
    rep = N_HEADS // N_KV_HEADS
    k = k_ref[0]
    vt = vt_ref[0]
    for g in range(N_KV_HEADS):
        qs = jnp.concatenate([q_ref[:, (g * rep + r) * LANES:(g * rep + r + 1) * LANES] for r in range(rep)],
                             axis=0)
        s = lax.dot_general(k, qs, (((1,), (1,)), ((), ())), preferred_element_type=F32)
        p = jnp.exp2(s - jnp.max(s, axis=0, keepdims=True))
        l = jnp.sum(p, axis=0, keepdims=True)
        out = (jnp.dot(vt, p.astype(BF16), preferred_element_type=F32) / l).T
        for r in range(rep):
            h = g * rep + r
            o_ref[:, h * HEAD_DIM:(h + 1) * HEAD_DIM] = (
                out[r * tq:(r + 1) * tq, g * HEAD_DIM:(g + 1) * HEAD_DIM].astype(o_ref.dtype))


def _values_t(v):
    return jnp.swapaxes(v, 1, 2).astype(BF16)


def attention(q, k, vt, *, row0, seq_len, tq):
    qc = q.shape[1]
    b, lk, kc = k.shape
    per = seq_len // tq
    base = row0 // tq
    return pl.pallas_call(
        functools.partial(_attention_kernel, tq=tq),
        out_shape=jax.ShapeDtypeStruct((b * seq_len, N_HEADS * HEAD_DIM), BF16),
        grid=(b, per),
        in_specs=[pl.BlockSpec((tq, qc), lambda s, i: (base + s * per + i, 0)),
                  pl.BlockSpec((1, lk, kc), lambda s, i: (s, 0, 0)),
                  pl.BlockSpec((1, kc, lk), lambda s, i: (s, 0, 0))],
        out_specs=pl.BlockSpec((tq, N_HEADS * HEAD_DIM), lambda s, i: (s * per + i, 0)),
        compiler_params=_cp(("arbitrary", "arbitrary")),
        name="attention",
    )(q, k, vt)


def _merge_kernel(x_ref, yhy_c, yhy_l, yf_c, yf_l, yb_c, yb_l, z_ref, yatt_c, yatt_l, gate_ref, snw_ref, whb_ref,
                  wsb_ref, wab_ref, wout_ref, g1_ref, n2w_ref, sh2_ref, sc2_ref, wr_ref, br_ref, ltri_ref,
                  xo_ref, h2_ref, ti_ref, tw_ref, hist_ref, *, n_ctx_tiles):
    d = x_ref.shape[1]
    is_ctx = pl.program_id(0) < n_ctx_tiles
    pick = lambda c_ref, l_ref: jnp.where(is_ctx, c_ref[...], l_ref[...])
    ys = (pick(yf_c, yf_l) + pick(yb_c, yb_l)) * _silu(z_ref[...])
    ys = ys * lax.rsqrt(jnp.mean(ys * ys, axis=-1, keepdims=True) + EPS) * snw_ref[...]
    gate = gate_ref[...].astype(F32)
    merged = (gate[:, :d] * jnp.dot(pick(yhy_c, yhy_l).astype(BF16), whb_ref[...], preferred_element_type=F32)
              + gate[:, d:2 * d] * jnp.dot(ys.astype(BF16), wsb_ref[...], preferred_element_type=F32)
              + gate[:, 2 * d:] * jnp.dot(pick(yatt_c, yatt_l), wab_ref[...], preferred_element_type=F32))
    mix = jnp.dot(merged.astype(BF16), wout_ref[...], preferred_element_type=F32)
    x = x_ref[...] + g1_ref[0] * mix
    xo_ref[...] = x
    h = x * lax.rsqrt(jnp.mean(x * x, axis=-1, keepdims=True) + EPS) * n2w_ref[...]
    h = h * (1.0 + sc2_ref[0]) + sh2_ref[0]
    _store_tiled(h2_ref, h)
    logits = _dot3(h, wr_ref[...]) + br_ref[...]
    lane = lax.broadcasted_iota(jnp.int32, logits.shape, 1)
    work = jnp.where(lane < N_EXPERTS, logits, -jnp.inf)
    idx_out = jnp.zeros(logits.shape, jnp.int32)
    val_out = jnp.full(logits.shape, -jnp.inf, F32)
    picks = []
    for j in range(TOP_K):
        mx = jnp.max(work, axis=-1, keepdims=True)
        am = jnp.min(jnp.where(work == mx, lane, LANES), axis=-1, keepdims=True)
        idx_out = jnp.where(lane == j, am, idx_out)
        val_out = jnp.where(lane == j, mx, val_out)
        picks.append(lane == am)
        work = jnp.where(picks[-1], -jnp.inf, work)
    e = jnp.exp(val_out - jnp.max(val_out, axis=-1, keepdims=True))
    tw_ref[...] = e / jnp.sum(e, axis=-1, keepdims=True)
    chosen = jnp.where(picks[0] | picks[1] | picks[2] | picks[3], 1.0, 0.0)
    before = jnp.dot(ltri_ref[...], chosen.astype(BF16), preferred_element_type=F32)
    rank_out = jnp.zeros(logits.shape, F32)
    for j in range(TOP_K):
        rank_j = jnp.sum(jnp.where(picks[j], before, 0.0), axis=-1, keepdims=True)
        rank_out = jnp.where(lane == TOP_K + j, rank_j, rank_out)
    ti_ref[...] = idx_out + rank_out.astype(jnp.int32)
    hist_ref[...] = jnp.broadcast_to(jnp.sum(chosen, axis=0, keepdims=True), hist_ref.shape)


def merge_router(x, y_hy, yf, yb, proj, y_att, gate, ssd_norm_w, whb, wsb, wab, wout, mods, n2w, wr, br,
                 *, t_ctx, l_lat, tm=512):
    t, d = x.shape
    nct = t_ctx // tm
    nlt = (t - t_ctx) // tm
    row = lambda tc, c0=0: pl.BlockSpec((tm, tc), lambda i: (i, c0))
    full = lambda shape: pl.BlockSpec(shape, lambda i: (0, 0))
    pair = lambda tc: [pl.BlockSpec((tm, tc), lambda i: (jnp.minimum(i, nct - 1), 0)),
                       pl.BlockSpec((tm, tc), lambda i: (jnp.clip(i - nct, 0, nlt - 1), 0))]
    ltri = jnp.asarray(np.tril(np.ones((tm, tm)), -1), BF16)
    return pl.pallas_call(
        functools.partial(_merge_kernel, n_ctx_tiles=nct),
        out_shape=(jax.ShapeDtypeStruct((t, d), F32), jax.ShapeDtypeStruct((t * ROW_SUB, LANES), F32),
                   jax.ShapeDtypeStruct((t, LANES), jnp.int32), jax.ShapeDtypeStruct((t, LANES), F32),
                   jax.ShapeDtypeStruct((t // tm * 8, LANES), F32)),
        grid=(t // tm,),
        in_specs=[row(d)] + pair(HY_DIM) + pair(SSD_INNER) + pair(SSD_INNER) + [row(SSD_INNER, P_Z // SSD_INNER)]
                 + pair(N_HEADS * HEAD_DIM) + [row(3 * d), full((1, SSD_INNER)),
                  full(whb.shape), full(wsb.shape), full(wab.shape), full(wout.shape),
                  _mod_spec(2, tm, t_ctx, l_lat), full((1, d)), _mod_spec(3, tm, t_ctx, l_lat),
                  _mod_spec(4, tm, t_ctx, l_lat), full((d, LANES)), full((1, LANES)), full((tm, tm))],
        out_specs=(row(d), pl.BlockSpec((tm * ROW_SUB, LANES), lambda i: (i, 0)), row(LANES), row(LANES),
                   pl.BlockSpec((8, LANES), lambda i: (i, 0))),
        compiler_params=_cp(("arbitrary",)),
        name="merge_router",
    )(x, *y_hy, *yf, *yb, proj, *y_att, gate, ssd_norm_w.reshape(1, -1), whb, wsb, wab, wout, mods, n2w, mods, mods,
      wr, br, ltri)


DEINT_BLOCK = 256


def _deinterleave(w_ref, perm_ref, g_ref, u_ref):
    half = DEINT_BLOCK // 2
    for blk in range(w_ref.shape[2] // DEINT_BLOCK):
        wb = w_ref[0, :, blk * DEINT_BLOCK:(blk + 1) * DEINT_BLOCK].astype(BF16)
        r = jnp.dot(wb, perm_ref[...], preferred_element_type=F32)
        g_ref[:, blk * half:(blk + 1) * half] = r[:, :half].astype(BF16)
        u_ref[:, blk * half:(blk + 1) * half] = r[:, half:].astype(BF16)


def _deinterleave_perm():
    perm = np.zeros((DEINT_BLOCK, DEINT_BLOCK), np.float32)
    half = DEINT_BLOCK // 2
    perm[2 * np.arange(half), np.arange(half)] = 1.0
    perm[2 * np.arange(half) + 1, half + np.arange(half)] = 1.0
    return jnp.asarray(perm, BF16)


DMA_CHUNK = 1024
DMA_UNROLL = 8
ROW_SUB = 8


def _tok_rows(t):
    return pl.ds(pl.multiple_of(t * ROW_SUB, ROW_SUB), ROW_SUB)


def _chunk_wait(ref, sem):
    n = DMA_CHUNK * ROW_SUB
    pltpu.make_async_copy(ref.at[pl.ds(0, n)], ref.at[pl.ds(0, n)], sem).wait()


def _issue_rows(copy_of):
    def body(it, carry):
        slot = it & (TOP_K - 1)
        tok0 = pl.multiple_of(lax.shift_right_logical(it, TOP_K_SHIFT) * DMA_UNROLL, DMA_UNROLL)
        for u in range(DMA_UNROLL):
            copy_of(slot, tok0, u).start(priority=u % 2)
        return carry
    lax.fori_loop(0, DMA_CHUNK // DMA_UNROLL, body, 0)


def _dispatch_kernel(zf_ref, idx_ref, x_ref, dst_ref, zbuf_ref, sem, zsem):
    tile_rows = zbuf_ref.shape[0]

    @pl.when(pl.program_id(0) == 0)
    def _():
        zbuf_ref[...] = jnp.zeros_like(zbuf_ref)
        fill = lambda i: pltpu.make_async_copy(
            zbuf_ref, dst_ref.at[pl.ds(pl.multiple_of(i * tile_rows, tile_rows), tile_rows)], zsem)

        def start(i, carry):
            @pl.when(zf_ref[i] == 1)
            def _():
                fill(i).start()
            return carry

        def wait(i, carry):
            @pl.when(zf_ref[i] == 1)
            def _():
                fill(i).wait()
            return carry

        lax.fori_loop(0, zf_ref.shape[0], start, 0)
        lax.fori_loop(0, zf_ref.shape[0], wait, 0)

    def copy_of(slot, tok0, u):
        j = idx_ref[0, 0, (tok0 + u) * TOP_K + slot]
        return pltpu.make_async_copy(x_ref.at[_tok_rows(tok0 + u)], dst_ref.at[_tok_rows(j)], sem)

    _issue_rows(copy_of)
    _chunk_wait(dst_ref, sem)


def moe_dispatch(dest, zero_tile, h2, n_dst):
    n = dest.shape[0]
    nc = n // DMA_CHUNK
    rows = DMA_CHUNK // TOP_K * ROW_SUB
    shape = (n_dst * ROW_SUB, LANES)
    return pl.pallas_call(
        _dispatch_kernel,
        out_shape=jax.ShapeDtypeStruct(shape, h2.dtype),
        grid_spec=pltpu.PrefetchScalarGridSpec(
            num_scalar_prefetch=1,
            grid=(nc,),
            in_specs=[pl.BlockSpec((1, 1, DMA_CHUNK), lambda c, zf: (c, 0, 0), memory_space=pltpu.SMEM),
                      pl.BlockSpec((rows, LANES), lambda c, zf: (c, 0))],
            out_specs=pl.BlockSpec(memory_space=pl.ANY),
            scratch_shapes=[pltpu.VMEM((MOE_TILE * ROW_SUB, LANES), h2.dtype), pltpu.SemaphoreType.DMA(()),
                            pltpu.SemaphoreType.DMA(())]),
        compiler_params=_cp(("arbitrary",)),
        name="moe_dispatch",
    )(zero_tile, dest.reshape(nc, 1, DMA_CHUNK), h2)


def _load_tiled(ref, n_tok, first_tok=0):
    return jnp.concatenate([ref[pl.ds(first_tok * ROW_SUB + s, n_tok, stride=ROW_SUB), :] for s in range(ROW_SUB)],
                           axis=1)


def _store_tiled(ref, x):
    for s in range(ROW_SUB):
        ref[pl.ds(s, x.shape[0], stride=ROW_SUB), :] = x[:, s * LANES:(s + 1) * LANES]


def _moe_kernel(te_ref, nv_ref, x_ref, w1_ref, perm_ref, b1g_ref, b1u_ref, w2_ref, b2_ref, o_ref,
                w1g_s, w1u_s, w2_s):
    i = pl.program_id(0)
    valid = i < nv_ref[0]
    new_expert = jnp.logical_or(i == 0, te_ref[i] != te_ref[jnp.maximum(i - 1, 0)])

    @pl.when(jnp.logical_and(valid, new_expert))
    def _():
        _deinterleave(w1_ref, perm_ref, w1g_s, w1u_s)
        w2_s[...] = w2_ref[0].astype(BF16)

    @pl.when(valid)
    def _():
        x = _load_tiled(x_ref, MOE_TILE).astype(BF16)
        gate = jnp.dot(x, w1g_s[...], preferred_element_type=F32) + b1g_ref[0]
        up = jnp.dot(x, w1u_s[...], preferred_element_type=F32) + b1u_ref[0]
        gate = jnp.minimum(gate, SWIGLU_LIMIT)
        up = jnp.clip(up, -SWIGLU_LIMIT, SWIGLU_LIMIT)
        act = (up + 1.0) * (gate * _sigmoid(SWIGLU_ALPHA * gate))
        _store_tiled(o_ref, jnp.dot(act.astype(BF16), w2_s[...], preferred_element_type=F32) + b2_ref[0])

    @pl.when(jnp.logical_not(valid))
    def _():
        o_ref[...] = jnp.zeros_like(o_ref)


def moe_experts(xs, tile_expert, n_valid, w1, b1g, b1u, w2, b2, *, layer):
    tm = MOE_TILE * ROW_SUB
    d, ff2 = w1.shape[1:]
    ff = ff2 // 2
    n_tiles = xs.shape[0] // tm
    rows = lambda i, te, nv: (jnp.minimum(i, nv[0] - 1), 0)
    wsel = lambda i, te, nv: (layer * N_EXPERTS + te[i], 0, 0)
    return pl.pallas_call(
        _moe_kernel,
        out_shape=jax.ShapeDtypeStruct(xs.shape, F32),
        grid_spec=pltpu.PrefetchScalarGridSpec(
            num_scalar_prefetch=2,
            grid=(n_tiles,),
            in_specs=[pl.BlockSpec((tm, LANES), rows),
                      pl.BlockSpec((1, d, ff2), wsel),
                      pl.BlockSpec((DEINT_BLOCK, DEINT_BLOCK), lambda i, te, nv: (0, 0)),
                      pl.BlockSpec((1, 1, ff), wsel), pl.BlockSpec((1, 1, ff), wsel),
                      pl.BlockSpec((1, ff, d), wsel), pl.BlockSpec((1, 1, d), wsel)],
            out_specs=pl.BlockSpec((tm, LANES), lambda i, te, nv: (i, 0)),
            scratch_shapes=[pltpu.VMEM((d, ff), BF16), pltpu.VMEM((d, ff), BF16), pltpu.VMEM((ff, d), BF16)]),
        compiler_params=_cp(("arbitrary",)),
        name="moe_experts",
    )(tile_expert, n_valid, xs, w1, _deinterleave_perm(), b1g, b1u, w2, b2)


def _combine_kernel(idx_ref, nxt_ref, x_ref, rows_ref, tw_ref, g2_ref, y_ref, buf_ref, sems):
    i = pl.program_id(0)
    slot = i % 2
    tm = x_ref.shape[0]

    def gather(ids_ref, s):
        def copy_of(slot, tok0, u):
            j = ids_ref[0, 0, (tok0 + u) * TOP_K + slot]
            return pltpu.make_async_copy(rows_ref.at[_tok_rows(j)], buf_ref.at[s, _tok_rows(slot * tm + tok0 + u)],
                                         sems.at[s])
        _issue_rows(copy_of)

    @pl.when(i == 0)
    def _():
        gather(idx_ref, 0)

    @pl.when(i + 1 < pl.num_programs(0))
    def _():
        gather(nxt_ref, 1 - slot)

    _chunk_wait(rows_ref, sems.at[slot])
    tw = tw_ref[...]
    cur = buf_ref.at[slot]
    acc = tw[:, 0:1] * _load_tiled(cur, tm, 0)
    for s in range(1, TOP_K):
        acc = acc + tw[:, s:s + 1] * _load_tiled(cur, tm, s * tm)
    y_ref[...] = x_ref[...] + g2_ref[0] * acc


def moe_combine(x, dest, rows, top_w, mods, *, t_ctx, l_lat):
    t, d = x.shape
    tm = DMA_CHUNK // TOP_K
    nt = t // tm
    ids = dest.reshape(nt, 1, DMA_CHUNK)
    smem = lambda fn: pl.BlockSpec((1, 1, DMA_CHUNK), fn, memory_space=pltpu.SMEM)
    return pl.pallas_call(
        _combine_kernel,
        out_shape=jax.ShapeDtypeStruct((t, d), F32),
        grid=(nt,),
        in_specs=[smem(lambda i: (i, 0, 0)), smem(lambda i: (jnp.minimum(i + 1, nt - 1), 0, 0)),
                  pl.BlockSpec((tm, d), lambda i: (i, 0)), pl.BlockSpec(memory_space=pl.ANY),
                  pl.BlockSpec((tm, LANES), lambda i: (i, 0)), _mod_spec(5, tm, t_ctx, l_lat)],
        out_specs=pl.BlockSpec((tm, d), lambda i: (i, 0)),
        scratch_shapes=[pltpu.VMEM((2, DMA_CHUNK * ROW_SUB, LANES), F32), pltpu.SemaphoreType.DMA((2,))],
        compiler_params=_cp(("arbitrary",)),
        name="moe_combine",
    )(ids, ids, x, rows, top_w, mods)


def _dispatch_plan(top_ir, hist, tm):
    t = top_ir.shape[0]
    n_rt = hist.shape[0]
    hist = hist.astype(jnp.int32)
    counts = jnp.sum(hist, axis=0)
    tiles = (counts + tm - 1) // tm
    tile_end = jnp.cumsum(tiles)
    base = (tile_end - tiles)[None, :] * tm + jnp.cumsum(hist, axis=0) - hist
    e = top_ir[:, :TOP_K].reshape(n_rt, -1)
    rank = top_ir[:, TOP_K:].reshape(n_rt, -1)
    pick = e[:, :, None] == jnp.arange(N_EXPERTS, dtype=jnp.int32)[None, None, :]
    dest = (jnp.sum(jnp.where(pick, base[:, None, :], 0), axis=2) + rank).reshape(-1).astype(jnp.int32)
    n_tiles = (t * TOP_K) // tm + N_EXPERTS
    tile_ids = jnp.arange(n_tiles, dtype=jnp.int32)
    tile_expert = jnp.minimum(jnp.sum((tile_ids[:, None] >= tile_end[None, :]).astype(jnp.int32), axis=1),
                              N_EXPERTS - 1).astype(jnp.int32)
    n_valid = tile_end[-1]
    partial = jnp.logical_or(tile_ids + 1 == tile_end[tile_expert], tile_ids >= n_valid)
    return dest, n_tiles, tile_expert, n_valid.reshape(1).astype(jnp.int32), partial.astype(jnp.int32)


def kernel(x_prompt, x_sample, c, cache_k, cache_v, state_ssd, c_ctx, norm1_w, norm2_w, w_mod, b_mod, w_in, w_gate,
           b_gate, hy_conv_w, hy_conv_b, hy_w1, hy_b1, hy_w2, hy_b2, hy_w3, hy_freq, hy_decay, hy_bias, ssd_conv_w,
           ssd_conv_b, ssd_a_log, ssd_dt_bias, ssd_d, ssd_norm_w, q_norm_w, k_norm_w, w_br_hy, w_br_ssd, w_br_att,
           w_out, w_router, b_router, w_e1, b_e1, w_e2, b_e2):
    n_ctx, l_ctx, d = x_prompt.shape
    n_lat, l_lat, _ = x_sample.shape
    depth = w_in.shape[0]
    t_ctx, t_lat = n_ctx * l_ctx, n_lat * l_lat
    t = t_ctx + t_lat
    kc = N_KV_HEADS * HEAD_DIM
    geo = dict(t_ctx=t_ctx, l_lat=l_lat)

    x = jnp.concatenate([x_prompt.reshape(t_ctx, d), x_sample.reshape(t_lat, d)], axis=0)
    cvec = jnp.zeros((8, d), F32).at[0].set(c_ctx).at[1:1 + n_lat].set(c)
    mods_all = modulation_all(cvec, w_mod, b_mod)
    b1_flat = b_e1.reshape(-1, 1, b_e1.shape[-1])
    b1g_all, b1u_all = b1_flat[:, :, 0::2], b1_flat[:, :, 1::2]
    new_k, new_v, new_s = [], [], []
    for l in range(depth):
        mods = mods_all[l].reshape(8 * 6, 1, d)
        wi = w_in[l]
        w_proj = jnp.concatenate([wi[:, 0:3072], wi[:, 3088:3856], wi[:, 3072:3088],
                                  jnp.zeros((d, P_COLS - 3856), F32)], axis=1).astype(BF16)
        nw1 = norm1_w[l].reshape(1, d)
        proj = norm_mod_matmul(x, nw1, mods, w_proj, jnp.zeros((1, P_COLS), F32), sigmoid=False, out_dtype=F32,
                               **geo)
        gate = norm_mod_matmul(x, nw1, mods, w_gate[l].astype(BF16), b_gate[l].reshape(1, -1), sigmoid=True,
                               out_dtype=BF16, **geo)

        x0, p = hyena_pre(proj, hy_conv_w[l], hy_conv_b[l], t_ctx=t_ctx, l_ctx=l_ctx, l_lat=l_lat)
        hy_args = (hy_w1[l], hy_b1[l], hy_w2[l], hy_b2[l], hy_w3[l], hy_freq[l], hy_decay[l])
        y_hy_ctx = hyena_ctx(x0, p, hyena_filter(l_ctx, *hy_args), hy_bias[l], n_seq=n_ctx, seq_len=l_ctx)
        y_hy_lat = hyena_lat(x0[t_ctx:], p[t_ctx:], hyena_filter(l_lat, *hy_args), hy_bias[l], n_seq=n_lat,
                             seq_len=l_lat)

        ssd_args = (ssd_conv_w[l], ssd_conv_b[l], ssd_dt_bias[l], ssd_a_log[l], ssd_d[l])
        yf_c, yb_c, fin_c = ssd_scan(proj, None, *ssd_args, row0=0, n_seq=n_ctx, seq_len=l_ctx)
        yf_l, yb_l, _ = ssd_scan(proj, state_ssd[:, l], *ssd_args, row0=t_ctx, n_seq=n_lat, seq_len=l_lat)

        qn, kn = qk_prep(proj, q_norm_w[l], k_norm_w[l], **geo)
        v_all = proj[:, P_V:P_V + kc]
        k_ctx = kn[:t_ctx].reshape(n_ctx, l_ctx, kc)
        v_ctx = v_all[:t_ctx].reshape(n_ctx, l_ctx, kc)
        att_ctx = attention(qn, k_ctx.astype(BF16), _values_t(v_ctx), row0=0, seq_len=l_ctx, tq=l_ctx)
        k_lat = jnp.concatenate([kn[t_ctx:].reshape(n_lat, l_lat, kc), cache_k[:, l].reshape(n_lat, -1, kc)], axis=1)
        v_lat = jnp.concatenate([v_all[t_ctx:].reshape(n_lat, l_lat, kc), cache_v[:, l].reshape(n_lat, -1, kc)],
                                axis=1)
        att_lat = attention(qn, k_lat.astype(BF16), _values_t(v_lat), row0=t_ctx, seq_len=l_lat, tq=128)

        wr = jnp.pad(w_router[l], ((0, 0), (0, LANES - N_EXPERTS)))
        br = jnp.pad(b_router[l], (0, LANES - N_EXPERTS)).reshape(1, LANES)
        x, h2, top_ir, top_w, hist = merge_router(
            x, (y_hy_ctx, y_hy_lat), (yf_c, yf_l), (yb_c, yb_l), proj, (att_ctx, att_lat), gate, ssd_norm_w[l],
            w_br_hy[l].astype(BF16), w_br_ssd[l].astype(BF16),
            w_br_att[l].astype(BF16), w_out[l].astype(BF16), mods, norm2_w[l].reshape(1, d), wr, br, **geo)

        dest, n_tiles, tile_expert, n_valid, partial = _dispatch_plan(
            top_ir[:, :2 * TOP_K], hist[::8, :N_EXPERTS], MOE_TILE)
        xs = moe_dispatch(dest, partial, h2, n_tiles * MOE_TILE)
        out_sorted = moe_experts(
            xs, tile_expert, n_valid, w_e1.reshape((-1,) + w_e1.shape[2:]), b1g_all, b1u_all,
            w_e2.reshape((-1,) + w_e2.shape[2:]), b_e2.reshape(-1, 1, d), layer=l)
        x = moe_combine(x, dest, out_sorted, top_w, mods, **geo)

        new_k.append(k_ctx.reshape(n_ctx, l_ctx, N_KV_HEADS, HEAD_DIM))
        new_v.append(v_ctx.reshape(n_ctx, l_ctx, N_KV_HEADS, HEAD_DIM))
        new_s.append(fin_c)

    y_prompt = x[:t_ctx].reshape(n_ctx, l_ctx, d)
    y_sample = x[t_ctx:].reshape(n_lat, l_lat, d)
    return (y_prompt, y_sample, jnp.stack(new_k, axis=1), jnp.stack(new_v, axis=1), jnp.stack(new_s, axis=1))
```

```python
import functools
import math

import numpy as np
import jax
import jax.numpy as jnp
from jax import lax
from jax.experimental import pallas as pl
from jax.experimental.pallas import tpu as pltpu

F32 = jnp.float32
BF16 = jnp.bfloat16

EPS = 1e-6
GRID_W = 64
HY_DIM = 512
SSD_INNER = 512
SSD_HEADDIM = 64
SSD_HEADS = 8
SSD_GROUPS = 2
SSD_STATE = 128
SSD_CHUNK = 128
SSD_STEP_CHUNKS = 4
N_HEADS = 8
N_KV_HEADS = 2
HEAD_DIM = 64
ROPE_THETA = 10000.0
N_EXPERTS = 32
TOP_K = 4
TOP_K_SHIFT = 2
SWIGLU_ALPHA = 1.702
SWIGLU_LIMIT = 7.0

P_HY, P_Z, P_XBC, P_Q, P_K, P_V, P_DT, P_COLS = 0, 1536, 2048, 3072, 3584, 3712, 3840, 3968

VMEM_LIMIT = 56 * 1024 * 1024
LANES = 128
ROW_TILE = 256
MOE_TILE = 512
DFT_N1, DFT_N2 = 64, 128
STAGE2_K1 = 8
DFT_COL_TILE = 8192
MOD_COL_TILE = 1536


def _cp(sem, vmem=VMEM_LIMIT):
    return pltpu.CompilerParams(dimension_semantics=sem, vmem_limit_bytes=vmem)


def _sigmoid(x):
    return 1.0 / (1.0 + jnp.exp(-x))


def _silu(x):
    return x * _sigmoid(x)


def _softplus(x):
    return jnp.maximum(x, 0.0) + jnp.log(1.0 + jnp.exp(-jnp.abs(x)))


def _dot3(a, b):
    ah = a.astype(BF16)
    al = (a - ah.astype(F32)).astype(BF16)
    bh = b.astype(BF16)
    bl = (b - bh.astype(F32)).astype(BF16)
    dot = lambda u, v: jnp.dot(u, v, preferred_element_type=F32)
    return dot(ah, bh) + (dot(ah, bl) + dot(al, bh))


def _mod_kernel(c_ref, w_ref, b_ref, o_ref):
    s = _silu(c_ref[...])
    o_ref[0] = _dot3(s, w_ref[0]) + b_ref[0]


def modulation_all(cvec, w_mod, b_mod):
    depth, d, n = w_mod.shape
    tn = MOD_COL_TILE
    return pl.pallas_call(
        _mod_kernel,
        out_shape=jax.ShapeDtypeStruct((depth, 8, n), F32),
        grid=(depth, n // tn),
        in_specs=[pl.BlockSpec((8, d), lambda l, j: (0, 0)),
                  pl.BlockSpec((1, d, tn), lambda l, j: (l, 0, j)),
                  pl.BlockSpec((1, 1, tn), lambda l, j: (l, 0, j))],
        out_specs=pl.BlockSpec((1, 8, tn), lambda l, j: (l, 0, j)),
        compiler_params=_cp(("arbitrary", "arbitrary")),
        name="modulation",
    )(cvec, w_mod, b_mod.reshape(depth, 1, n))


def _mod_row(i, tm, t_ctx, l_lat):
    n_ctx = t_ctx // tm
    per = l_lat // tm
    return jnp.where(i < n_ctx, 0, 1 + (i - n_ctx) // per)


def _mod_spec(k, tm, t_ctx, l_lat):
    return pl.BlockSpec((1, 1, 1024), lambda i: (_mod_row(i, tm, t_ctx, l_lat) * 6 + k, 0, 0))


def _nmm_kernel(x_ref, nw_ref, sh_ref, sc_ref, w_ref, b_ref, o_ref, *, sigmoid):
    x = x_ref[...]
    ms = jnp.mean(x * x, axis=-1, keepdims=True)
    h = x * lax.rsqrt(ms + EPS) * nw_ref[...]
    h = h * (1.0 + sc_ref[0]) + sh_ref[0]
    acc = jnp.dot(h.astype(BF16), w_ref[...], preferred_element_type=F32) + b_ref[...]
    if sigmoid:
        acc = _sigmoid(acc)
    o_ref[...] = acc.astype(o_ref.dtype)


def norm_mod_matmul(x, nw, mods, w, b, *, t_ctx, l_lat, sigmoid, out_dtype, tm=512):
    t, d = x.shape
    n = w.shape[1]
    return pl.pallas_call(
        functools.partial(_nmm_kernel, sigmoid=sigmoid),
        out_shape=jax.ShapeDtypeStruct((t, n), out_dtype),
        grid=(t // tm,),
        in_specs=[pl.BlockSpec((tm, d), lambda i: (i, 0)),
                  pl.BlockSpec((1, d), lambda i: (0, 0)),
                  _mod_spec(0, tm, t_ctx, l_lat),
                  _mod_spec(1, tm, t_ctx, l_lat),
                  pl.BlockSpec((d, n), lambda i: (0, 0)),
                  pl.BlockSpec((1, n), lambda i: (0, 0))],
        out_specs=pl.BlockSpec((tm, n), lambda i: (i, 0)),
        compiler_params=_cp(("arbitrary",)),
        name="norm_mod_matmul",
    )(x, nw, mods, mods, w, b)


def _seq_edges(i, tr, t_ctx, l_ctx, l_lat):
    tok = i * tr
    pos = jnp.where(tok < t_ctx, tok % l_ctx, (tok - t_ctx) % l_lat)
    length = jnp.where(tok < t_ctx, l_ctx, l_lat)
    return pos == 0, pos + tr == length


def _conv3(x, prev8, next8, w_ref, b_ref, first, last):
    tr = x.shape[0]
    row = lax.broadcasted_iota(jnp.int32, x.shape, 0)
    pm = jnp.where(first, 0.0, 1.0)
    nm = jnp.where(last, 0.0, 1.0)
    xm1 = jnp.where(row == 0, prev8[7:8, :] * pm, pltpu.roll(x, 1, axis=0))
    xp1 = jnp.where(row == tr - 1, next8[0:1, :] * nm, pltpu.roll(x, tr - 1, axis=0))
    return b_ref[...] + xm1 * w_ref[0:1, :] + x * w_ref[1:2, :] + xp1 * w_ref[2:3, :]


def _conv_specs(tr, tc, col_blk, n_rows):
    r8 = tr // 8
    last8 = n_rows // 8 - 1
    return [pl.BlockSpec((tr, tc), lambda i, j: (i, col_blk(j))),
            pl.BlockSpec((8, tc), lambda i, j: (jnp.maximum(i * r8 - 1, 0), col_blk(j))),
            pl.BlockSpec((8, tc), lambda i, j: (jnp.minimum((i + 1) * r8, last8), col_blk(j)))]


def _hy_pre_kernel(*refs, tr, t_ctx, l_ctx, l_lat):
    (x0, x0p, x0n, x1, x1p, x1n, xv, xvp, xvn, w0, w1, wv, b0, b1, bv, o0_ref, op_ref) = refs
    first, last = _seq_edges(pl.program_id(0), tr, t_ctx, l_ctx, l_lat)
    o0_ref[...] = _conv3(x0[...], x0p[...], x0n[...], w0, b0, first, last)
    u1 = _conv3(x1[...], x1p[...], x1n[...], w1, b1, first, last)
    uv = _conv3(xv[...], xvp[...], xvn[...], wv, bv, first, last)
    op_ref[...] = u1 * uv


def hyena_pre(proj, w, b, *, t_ctx, l_ctx, l_lat):
    t = proj.shape[0]
    tr, tc = ROW_TILE, HY_DIM
    nb = HY_DIM // tc
    b2 = b.reshape(1, 3 * HY_DIM)
    specs = []
    for s in range(3):
        specs += _conv_specs(tr, tc, lambda j, s=s: P_HY // tc + s * nb + j, t)
    specs += [pl.BlockSpec((3, tc), lambda i, j, s=s: (0, s * nb + j)) for s in range(3)]
    specs += [pl.BlockSpec((1, tc), lambda i, j, s=s: (0, s * nb + j)) for s in range(3)]
    return pl.pallas_call(
        functools.partial(_hy_pre_kernel, tr=tr, t_ctx=t_ctx, l_ctx=l_ctx, l_lat=l_lat),
        out_shape=(jax.ShapeDtypeStruct((t, HY_DIM), F32), jax.ShapeDtypeStruct((t, HY_DIM), F32)),
        grid=(t // tr, nb),
        in_specs=specs,
        out_specs=(pl.BlockSpec((tr, tc), lambda i, j: (i, j)), pl.BlockSpec((tr, tc), lambda i, j: (i, j))),
        compiler_params=_cp(("arbitrary", "arbitrary")),
        name="hyena_pre",
    )(*([proj] * 9), w, w, w, b2, b2, b2)


def _filter_kernel(z_ref, w1_ref, b1_ref, w2_ref, b2_ref, w3_ref, fr_ref, dec_ref, o_ref, *, zero_row):
    z = z_ref[...]
    fr = fr_ref[...]
    h = jnp.sin(fr * (_dot3(z, w1_ref[...]) + b1_ref[...]))
    h = jnp.sin(fr * (_dot3(h, w2_ref[...]) + b2_ref[...]))
    f = _dot3(h, w3_ref[...])
    f = f * jnp.exp(-z[:, 0:1] * jnp.abs(dec_ref[...]))
    row = pl.program_id(0) * f.shape[0] + lax.broadcasted_iota(jnp.int32, f.shape, 0)
    o_ref[...] = jnp.where(row == zero_row, 0.0, f)


def _filter_embedding(seq_len, emb):
    bands_n = (emb - 1) // 2
    t = jnp.linspace(0.0, 1.0, seq_len, dtype=F32)[:, None]
    bands = jnp.linspace(1e-4, bands_n - 1, bands_n, dtype=F32)[None, :]
    ang = (2.0 * math.pi / seq_len) * jnp.arange(seq_len, dtype=F32)[:, None] * bands
    z = jnp.concatenate([t, jnp.cos(ang), -jnp.sin(ang)], axis=-1)
    return jnp.pad(z, ((0, 0), (0, LANES - emb)))


def hyena_filter(seq_len, w1, b1, w2, b2, w3, freq, decay):
    emb, ff = w1.shape
    ch = w3.shape[1] // 2
    lags = np.concatenate([np.arange(seq_len), [0], np.arange(seq_len - 1, 0, -1)])
    z = _filter_embedding(seq_len, emb)[lags]
    padc = LANES - ff
    w1p = jnp.pad(w1, ((0, LANES - emb), (0, padc)))
    w2p = jnp.pad(w2, ((0, padc), (0, padc)))
    w3p = jnp.pad(w3, ((0, padc), (0, 0)))
    row = lambda v: jnp.pad(v, (0, padc)).reshape(1, LANES)
    tr = 256
    fwd_tiles = seq_len // tr
    full = lambda shape: pl.BlockSpec(shape, lambda i: (0, 0))
    half = lambda shape: pl.BlockSpec(shape, lambda i: (0, jnp.where(i < fwd_tiles, 0, 1)))
    return pl.pallas_call(
        functools.partial(_filter_kernel, zero_row=seq_len),
        out_shape=jax.ShapeDtypeStruct((2 * seq_len, ch), F32),
        grid=(2 * seq_len // tr,),
        in_specs=[pl.BlockSpec((tr, LANES), lambda i: (i, 0)), full((LANES, LANES)), full((1, LANES)),
                  full((LANES, LANES)), full((1, LANES)), half((LANES, ch)), full((1, LANES)), half((1, ch))],
        out_specs=pl.BlockSpec((tr, ch), lambda i: (i, 0)),
        compiler_params=_cp(("arbitrary",)),
        name="hyena_filter",
    )(z, w1p, row(b1), w2p, row(b2), w3p, row(freq), decay.reshape(1, 2 * ch))


def _cs(n_rows, n_cols, period):
    ang = 2.0 * np.pi * (np.outer(np.arange(n_rows), np.arange(n_cols)) % period) / period
    return np.cos(ang), np.sin(ang)


def _mm_kernel(a_ref, b_ref, o_ref):
    o_ref[0] = _dot3(a_ref[...], b_ref[0])


def const_matmul(a, b, tn):
    m, k = a.shape
    bsz, _, n = b.shape
    return pl.pallas_call(
        _mm_kernel,
        out_shape=jax.ShapeDtypeStruct((bsz, m, n), F32),
        grid=(bsz, n // tn),
        in_specs=[pl.BlockSpec((m, k), lambda s, j: (0, 0)), pl.BlockSpec((1, k, tn), lambda s, j: (s, 0, j))],
        out_specs=pl.BlockSpec((1, m, tn), lambda s, j: (s, 0, j)),
        compiler_params=_cp(("arbitrary", "arbitrary")),
        name="const_matmul",
    )(a, b)


def _hy_ctx_kernel(x0_ref, p_ref, kr_ref, ki_ref, fw_ref, iv_ref, bias_ref, o_ref, *, n):
    p = p_ref[...]
    xf = _dot3(fw_ref[...], p)
    xr, xi = xf[:n], xf[n:]
    kr, ki = kr_ref[...], ki_ref[...]
    yr = xr * kr - xi * ki
    yi = xr * ki + xi * kr
    y = _dot3(iv_ref[...], jnp.concatenate([yr, yi], axis=0))
    o_ref[...] = x0_ref[...] * (y + p * bias_ref[...])


def hyena_ctx(x0, p, filt, bias, *, n_seq, seq_len):
    n = 2 * seq_len
    ch = p.shape[1]
    c_full, s_full = _cs(n, n, n)
    fw_full = jnp.asarray(np.concatenate([c_full, -s_full], axis=0), F32)
    fw_half = fw_full[:, :seq_len]
    iv = jnp.asarray(np.concatenate([c_full[:seq_len], -s_full[:seq_len]], axis=1) / n, F32)
    kf = const_matmul(fw_full, filt[None], ch)[0]
    kr, ki = kf[:n], kf[n:]
    full = lambda shape: pl.BlockSpec(shape, lambda s: (0, 0))
    return pl.pallas_call(
        functools.partial(_hy_ctx_kernel, n=n),
        out_shape=jax.ShapeDtypeStruct((n_seq * seq_len, ch), F32),
        grid=(n_seq,),
        in_specs=[pl.BlockSpec((seq_len, ch), lambda s: (s, 0)), pl.BlockSpec((seq_len, ch), lambda s: (s, 0)),
                  full((n, ch)), full((n, ch)), full((2 * n, seq_len)), full((seq_len, 2 * n)), full((1, ch))],
        out_specs=pl.BlockSpec((seq_len, ch), lambda s: (s, 0)),
        compiler_params=_cp(("arbitrary",)),
        name="hyena_ctx",
    )(x0, p, kr, ki, fw_half, iv, bias.reshape(1, ch))


def _stage2_kernel(a_ref, twr_ref, twi_ref, m_ref, *rest, conv):
    if conv:
        kf_ref, mi_ref, o_ref = rest
    else:
        (o_ref,) = rest
    for j in range(STAGE2_K1):
        ar, ai = a_ref[0, 0, j], a_ref[0, 1, j]
        twr, twi = twr_ref[j], twi_ref[j]
        br = ar * twr - ai * twi
        bi = ar * twi + ai * twr
        x = _dot3(m_ref[...], jnp.concatenate([br, bi], axis=0))
        n2 = ar.shape[0]
        xr, xi = x[:n2], x[n2:]
        if not conv:
            o_ref[0, 0, j] = xr
            o_ref[0, 1, j] = xi
            continue
        kr, ki = kf_ref[0, 0, j], kf_ref[0, 1, j]
        yr = xr * kr - xi * ki
        yi = xr * ki + xi * kr
        pq = _dot3(mi_ref[...], jnp.concatenate([yr, yi], axis=0))
        pr, pi = pq[:n2], pq[n2:]
        o_ref[0, 0, j] = pr * twr + pi * twi
        o_ref[0, 1, j] = pi * twr - pr * twi


def _stage2(a, kf, ch):
    bsz = a.shape[0]
    n1, n2 = DFT_N1, DFT_N2
    n = n1 * n2
    tw_ang = 2.0 * np.pi * np.outer(np.arange(n1), np.arange(n2)) / n
    twr = jnp.asarray(np.cos(tw_ang), F32).reshape(n1, n2, 1)
    twi = jnp.asarray(-np.sin(tw_ang), F32).reshape(n1, n2, 1)
    c2, s2 = _cs(n2, n2, n2)
    m_fwd = jnp.asarray(np.block([[c2, s2], [-s2, c2]]), F32)
    m_inv = jnp.asarray(np.block([[c2, -s2], [s2, c2]]), F32)
    conv = kf is not None
    kb = STAGE2_K1
    blk = pl.BlockSpec((1, 2, kb, n2, ch), lambda s, k: (s, 0, k, 0, 0))
    tw_spec = pl.BlockSpec((kb, n2, 1), lambda s, k: (k, 0, 0))
    specs = [blk, tw_spec, tw_spec, pl.BlockSpec((2 * n2, 2 * n2), lambda s, k: (0, 0))]
    args = [a, twr, twi, m_fwd]
    if conv:
        specs += [pl.BlockSpec((1, 2, kb, n2, ch), lambda s, k: (0, 0, k, 0, 0)),
                  pl.BlockSpec((2 * n2, 2 * n2), lambda s, k: (0, 0))]
        args += [kf, m_inv]
    return pl.pallas_call(
        functools.partial(_stage2_kernel, conv=conv),
        out_shape=jax.ShapeDtypeStruct(a.shape, F32),
        grid=(bsz, n1 // kb),
        in_specs=specs,
        out_specs=blk,
        compiler_params=_cp(("arbitrary", "arbitrary")),
        name="hyena_stage2",
    )(*args)


def _hy_post_kernel(g_ref, q_ref, x0_ref, p_ref, bias_ref, o_ref):
    y = _dot3(g_ref[...], q_ref[0])
    o_ref[0] = x0_ref[0] * (y + p_ref[0] * bias_ref[...])


def hyena_lat(x0, p, filt, bias, *, n_seq, seq_len):
    n1, n2 = DFT_N1, DFT_N2
    n = n1 * n2
    assert n == 2 * seq_len
    ch = p.shape[1]
    h1 = n1 // 2
    wide = n2 * ch
    c1, s1 = _cs(n1, n1, n1)
    f1_full = jnp.asarray(np.concatenate([c1, -s1], axis=0), F32)
    f1_half = f1_full[:, :h1]
    g1 = jnp.asarray(np.concatenate([c1[:h1], -s1[:h1]], axis=1) / n, F32)
    tn = DFT_COL_TILE
    kcirc = filt.reshape(1, n1, wide)
    kf = _stage2(const_matmul(f1_full, kcirc, tn).reshape(1, 2, n1, n2, ch), None, ch)
    a = const_matmul(f1_half, p.reshape(n_seq, h1, wide), tn).reshape(n_seq, 2, n1, n2, ch)
    q = _stage2(a, kf, ch).reshape(n_seq, 2 * n1, wide)
    bias_w = jnp.tile(bias, n2).reshape(1, wide)
    out = pl.pallas_call(
        _hy_post_kernel,
        out_shape=jax.ShapeDtypeStruct((n_seq, h1, wide), F32),
        grid=(n_seq, wide // tn),
        in_specs=[pl.BlockSpec((h1, 2 * n1), lambda s, j: (0, 0)),
                  pl.BlockSpec((1, 2 * n1, tn), lambda s, j: (s, 0, j)),
                  pl.BlockSpec((1, h1, tn), lambda s, j: (s, 0, j)),
                  pl.BlockSpec((1, h1, tn), lambda s, j: (s, 0, j)),
                  pl.BlockSpec((1, tn), lambda s, j: (0, j))],
        out_specs=pl.BlockSpec((1, h1, tn), lambda s, j: (s, 0, j)),
        compiler_params=_cp(("arbitrary", "arbitrary")),
        name="hyena_post",
    )(g1, q, x0.reshape(n_seq, h1, wide), p.reshape(n_seq, h1, wide), bias_w)
    return out.reshape(n_seq * seq_len, ch)


def _dot_01(x, m, *, x_is_lhs):
    x1 = x.astype(BF16)
    r1 = x - x1.astype(F32)
    x2 = r1.astype(BF16)
    x3 = (r1 - x2.astype(F32)).astype(BF16)
    mb = m.astype(BF16)
    if x_is_lhs:
        dot = lambda u: jnp.dot(u, mb, preferred_element_type=F32)
    else:
        dot = lambda u: jnp.dot(mb, u, preferred_element_type=F32)
    return dot(x1) + (dot(x2) + dot(x3))


def _ssd_dir(xbc, dtraw, st_ref, d, consts, y_ref):
    tri, expand, dtb, acont, dskip = consts
    cl = xbc.shape[0]
    xs = xbc[:, :SSD_INNER]
    dt = _softplus(dtraw + dtb)
    a = dt * acont
    tri_d = tri if d == 0 else tri.T
    cs = _dot_01(a, tri_d, x_is_lhs=False)
    cs_t = _dot_01(a.T, tri_d.T, x_is_lhs=True)
    dt_x = _dot_01(dt, expand[d], x_is_lhs=True)
    xdt = xs * dt_x
    row = lax.broadcasted_iota(jnp.int32, (cl, cl), 0)
    col = lax.broadcasted_iota(jnp.int32, (cl, cl), 1)
    keep = (col <= row) if d == 0 else (col >= row)
    lane = lax.broadcasted_iota(jnp.int32, (cl, LANES), 1)
    low = lane < SSD_HEADDIM
    edge = cl - 1 if d == 0 else 0
    for g in range(SSD_GROUPS):
        bg = xbc[:, SSD_INNER + g * SSD_STATE: SSD_INNER + (g + 1) * SSD_STATE]
        cg = xbc[:, SSD_INNER + (SSD_GROUPS + g) * SSD_STATE: SSD_INNER + (SSD_GROUPS + g + 1) * SSD_STATE]
        bg16, cg16 = bg.astype(BF16), cg.astype(BF16)
        cb = lax.dot_general(cg16, bg16, (((1,), (1,)), ((), ())), preferred_element_type=F32)
        for pr in range(2):
            pair = g * 2 + pr
            h0 = 2 * pair
            ms = []
            for h in (h0, h0 + 1):
                ln = d * SSD_HEADS + h
                diff = cs[:, ln:ln + 1] - cs_t[ln:ln + 1, :]
                ms.append(jnp.where(keep, cb * jnp.exp(jnp.minimum(diff, 0.0)), 0.0).astype(BF16))
            xp = xdt[:, pair * LANES:(pair + 1) * LANES]
            xs_p = xs[:, pair * LANES:(pair + 1) * LANES]
            xlo = jnp.where(low, xp, 0.0).astype(BF16)
            xhi = jnp.where(low, 0.0, xp).astype(BF16)
            y_diag = (jnp.dot(ms[0], xlo, preferred_element_type=F32)
                      + jnp.dot(ms[1], xhi, preferred_element_type=F32))
            l0, l1 = d * SSD_HEADS + h0, d * SSD_HEADS + h0 + 1
            e_cs = jnp.where(low, jnp.exp(cs[:, l0:l0 + 1]), jnp.exp(cs[:, l1:l1 + 1]))
            st = st_ref[pair]
            y_off = jnp.dot(cg16, st.astype(BF16), preferred_element_type=F32) * e_cs
            y_ref[:, pair * LANES:(pair + 1) * LANES] = (
                y_diag + y_off + xs_p * dskip[d:d + 1, pair * LANES:(pair + 1) * LANES])
            tot0, tot1 = cs[edge:edge + 1, l0:l0 + 1], cs[edge:edge + 1, l1:l1 + 1]
            dec = jnp.where(low, jnp.exp(tot0 - cs[:, l0:l0 + 1]), jnp.exp(tot1 - cs[:, l1:l1 + 1]))
            upd = lax.dot_general(bg16, (xp * dec).astype(BF16), (((0,), (0,)), ((), ())),
                                  preferred_element_type=F32)
            st_ref[pair] = st * jnp.where(low[0:1], jnp.exp(tot0), jnp.exp(tot1)) + upd


def _ssd_kernel(xf_ref, xfp_ref, xfn_ref, dtf_ref, xb_ref, xbp_ref, xbn_ref, dtb_ref, init_ref, cw_ref, cb_ref,
                tri_ref, exp_ref, dtbias_ref, acont_ref, dskip_ref, yf_ref, yb_ref, fin_ref, st_ref, *, has_init,
                step_chunks):
    s = pl.program_id(1)
    at_start, at_end = s == 0, s == pl.num_programs(1) - 1

    @pl.when(at_start)
    def _():
        if has_init:
            st_ref[...] = init_ref[0]
        else:
            st_ref[...] = jnp.zeros_like(st_ref)

    consts = (tri_ref[...], (exp_ref[0], exp_ref[1]), dtbias_ref[...], acont_ref[...], dskip_ref[...])
    xbc_f = _silu(_conv3(xf_ref[...], xfp_ref[...], xfn_ref[...], cw_ref, cb_ref, at_start, at_end))
    xbc_b = _silu(_conv3(xb_ref[...], xbp_ref[...], xbn_ref[...], cw_ref, cb_ref, at_end, at_start))
    cl = SSD_CHUNK
    for c in range(step_chunks):
        rf = slice(c * cl, (c + 1) * cl)
        _ssd_dir(xbc_f[rf], dtf_ref[rf, :], st_ref.at[0], 0, consts, yf_ref.at[rf])
        rb = slice((step_chunks - 1 - c) * cl, (step_chunks - c) * cl)
        _ssd_dir(xbc_b[rb], dtb_ref[rb, :], st_ref.at[1], 1, consts, yb_ref.at[rb])

    @pl.when(s == pl.num_programs(1) - 1)
    def _():
        fin_ref[0] = st_ref[...]


def _pair_states(s):
    b = s.shape[0]
    s = s.reshape(b, 2, SSD_HEADS // 2, 2, SSD_HEADDIM, SSD_STATE)
    return s.transpose(0, 1, 2, 5, 3, 4).reshape(b, 2, SSD_HEADS // 2, SSD_STATE, 2 * SSD_HEADDIM)


def _unpair_states(s):
    b = s.shape[0]
    s = s.reshape(b, 2, SSD_HEADS // 2, SSD_STATE, 2, SSD_HEADDIM)
    return s.transpose(0, 1, 2, 4, 5, 3).reshape(b, 2, SSD_HEADS, SSD_HEADDIM, SSD_STATE)


def ssd_scan(proj, init, conv_w, conv_b, dt_bias, a_log, d_skip, *, row0, n_seq, seq_len):
    cl = SSD_CHUNK
    step_chunks = min(SSD_STEP_CHUNKS, seq_len // cl)
    bl = step_chunks * cl
    nc = seq_len // bl
    base = row0 // bl
    has_init = init is not None
    hp = SSD_HEADS // 2
    init_p = _pair_states(init) if has_init else jnp.zeros((1, 2, hp, SSD_STATE, LANES), F32)
    tri = jnp.asarray(np.tril(np.ones((cl, cl))), F32)
    expand = np.zeros((2, LANES, SSD_INNER), np.float32)
    for d in range(2):
        for h in range(SSD_HEADS):
            expand[d, d * SSD_HEADS + h, h * SSD_HEADDIM:(h + 1) * SSD_HEADDIM] = 1.0
    pad16 = lambda v: jnp.pad(v.reshape(1, 2 * SSD_HEADS), ((0, 0), (0, LANES - 2 * SSD_HEADS)))
    acont = pad16(-jnp.exp(a_log))
    dtb = pad16(dt_bias)
    dskip = jnp.repeat(d_skip, SSD_HEADDIM, axis=1)
    cxbc = conv_w.shape[1]
    xblk = P_XBC // cxbc
    dtblk = P_DT // LANES
    r8 = bl // 8
    last8 = proj.shape[0] // 8 - 1
    full2 = lambda shape: pl.BlockSpec(shape, lambda b, s: (0,) * len(shape))
    fwd = lambda b, s: base + b * nc + s
    bwd = lambda b, s: base + b * nc + (nc - 1 - s)

    def stream(chunk):
        return [pl.BlockSpec((bl, cxbc), lambda b, s: (chunk(b, s), xblk)),
                pl.BlockSpec((8, cxbc), lambda b, s: (jnp.maximum(chunk(b, s) * r8 - 1, 0), xblk)),
                pl.BlockSpec((8, cxbc), lambda b, s: (jnp.minimum((chunk(b, s) + 1) * r8, last8), xblk)),
                pl.BlockSpec((bl, LANES), lambda b, s: (chunk(b, s), dtblk))]

    st_spec = pl.BlockSpec((1, 2, hp, SSD_STATE, LANES), lambda b, s: (b if has_init else 0, 0, 0, 0, 0))
    yf, yb, fin = pl.pallas_call(
        functools.partial(_ssd_kernel, has_init=has_init, step_chunks=step_chunks),
        out_shape=(jax.ShapeDtypeStruct((n_seq * seq_len, SSD_INNER), F32),
                   jax.ShapeDtypeStruct((n_seq * seq_len, SSD_INNER), F32),
                   jax.ShapeDtypeStruct((n_seq, 2, hp, SSD_STATE, LANES), F32)),
        grid=(n_seq, nc),
        in_specs=stream(fwd) + stream(bwd) + [
            st_spec, full2((3, cxbc)), full2((1, cxbc)), full2((cl, cl)), full2((2, LANES, SSD_INNER)),
            full2((1, LANES)), full2((1, LANES)), full2((2, SSD_INNER))],
        out_specs=(pl.BlockSpec((bl, SSD_INNER), lambda b, s: (b * nc + s, 0)),
                   pl.BlockSpec((bl, SSD_INNER), lambda b, s: (b * nc + (nc - 1 - s), 0)),
                   pl.BlockSpec((1, 2, hp, SSD_STATE, LANES), lambda b, s: (b, 0, 0, 0, 0))),
        scratch_shapes=[pltpu.VMEM((2, hp, SSD_STATE, LANES), F32)],
        compiler_params=_cp(("arbitrary", "arbitrary")),
        name="ssd_scan",
    )(*([proj] * 8), init_p, conv_w, conv_b.reshape(1, cxbc), tri, jnp.asarray(expand), dtb, acont, dskip)
    return yf, yb, _unpair_states(fin)


def _headnorm(x, g_ref, w_ref):
    ms = _dot3(x * x, g_ref[...])
    return x * lax.rsqrt(ms + EPS) * w_ref[...]


def _rope(x, cos, sin_signed):
    lane = lax.broadcasted_iota(jnp.int32, x.shape, 1)
    w = x.shape[1]
    swapped = jnp.where(lane % 2 == 0, pltpu.roll(x, w - 1, axis=1), pltpu.roll(x, 1, axis=1))
    return x * cos + swapped * sin_signed


Q_SCALE = HEAD_DIM ** -0.5 * math.log2(math.e)


def _store_padded_heads(qo_ref, q):
    rep = N_HEADS // N_KV_HEADS
    lane = lax.broadcasted_iota(jnp.int32, (q.shape[0], LANES), 1)
    for h in range(N_HEADS):
        g = h // rep
        chunk = q[:, (h // 2) * LANES:(h // 2 + 1) * LANES]
        if h % 2 != g:
            chunk = pltpu.roll(chunk, HEAD_DIM, axis=1)
        keep = (lane >= g * HEAD_DIM) & (lane < (g + 1) * HEAD_DIM)
        qo_ref[:, h * LANES:(h + 1) * LANES] = jnp.where(keep, chunk, 0.0).astype(qo_ref.dtype)


def _qk_kernel(q_ref, k_ref, cos_ref, sin_ref, gq_ref, gk_ref, qw_ref, kw_ref, qo_ref, ko_ref, *, n_ctx_tiles):
    q = _headnorm(q_ref[...], gq_ref, qw_ref)
    k = _headnorm(k_ref[...], gk_ref, kw_ref)
    is_lat = pl.program_id(0) >= n_ctx_tiles

    @pl.when(is_lat)
    def _():
        cos, sin = cos_ref[...], sin_ref[...]
        _store_padded_heads(qo_ref, _rope(q, cos, sin) * Q_SCALE)
        ko_ref[...] = _rope(k, cos[:, :k.shape[1]], sin[:, :k.shape[1]])

    @pl.when(jnp.logical_not(is_lat))
    def _():
        _store_padded_heads(qo_ref, q * Q_SCALE)
        ko_ref[...] = k


def _rope_tables(seq_len):
    n_rows = seq_len // GRID_W
    row = jnp.repeat(jnp.arange(n_rows), GRID_W).astype(F32)
    col = jnp.tile(jnp.arange(GRID_W), n_rows).astype(F32)
    n_freq = HEAD_DIM // 4
    inv = ROPE_THETA ** (-jnp.arange(n_freq, dtype=F32) / n_freq)
    ang = jnp.concatenate([row[:, None] * inv, col[:, None] * inv], axis=-1)
    cos = jnp.repeat(jnp.cos(ang), 2, axis=1)
    sin = jnp.repeat(jnp.sin(ang), 2, axis=1) * jnp.tile(jnp.asarray([-1.0, 1.0], F32), HEAD_DIM // 2)
    return jnp.tile(cos, (1, N_HEADS)), jnp.tile(sin, (1, N_HEADS))


def qk_prep(proj, q_norm_w, k_norm_w, *, t_ctx, l_lat):
    t = proj.shape[0]
    tr = 2 * ROW_TILE
    qc, kc = N_HEADS * HEAD_DIM, N_KV_HEADS * HEAD_DIM
    cos, sin = _rope_tables(l_lat)
    group = lambda c: jnp.asarray(np.kron(np.eye(c // HEAD_DIM), np.ones((HEAD_DIM, HEAD_DIM))) / HEAD_DIM, F32)
    n_ctx_tiles = t_ctx // tr
    per = l_lat // tr
    tab = pl.BlockSpec((tr, qc), lambda i: (jnp.maximum(i - n_ctx_tiles, 0) % per, 0))
    full = lambda shape: pl.BlockSpec(shape, lambda i: (0, 0))
    return pl.pallas_call(
        functools.partial(_qk_kernel, n_ctx_tiles=n_ctx_tiles),
        out_shape=(jax.ShapeDtypeStruct((t, N_HEADS * LANES), BF16), jax.ShapeDtypeStruct((t, kc), F32)),
        grid=(t // tr,),
        in_specs=[pl.BlockSpec((tr, qc), lambda i: (i, P_Q // qc)), pl.BlockSpec((tr, kc), lambda i: (i, P_K // kc)),
                  tab, tab, full((qc, qc)), full((kc, kc)), full((1, qc)), full((1, kc))],
        out_specs=(pl.BlockSpec((tr, N_HEADS * LANES), lambda i: (i, 0)), pl.BlockSpec((tr, kc), lambda i: (i, 0))),
        compiler_params=_cp(("arbitrary",)),
        name="qk_prep",
    )(proj, proj, cos, sin, group(qc), group(kc), jnp.tile(q_norm_w, N_HEADS).reshape(1, qc),
      jnp.tile(k_norm_w, N_KV_HEADS).reshape(1, kc))


def _attention_kernel(q_ref, k_ref, vt_ref, o_ref, *, tq):
    rep = N_HEADS // N_KV_HEADS
    k = k_ref[0]
    vt = vt_ref[0]
    for g in range(N_KV_HEADS):
        qs = jnp.concatenate([q_ref[:, (g * rep + r) * LANES:(g * rep + r + 1) * LANES] for r in range(rep)],
                             axis=0)
        s = lax.dot_general(k, qs, (((1,), (1,)), ((), ())), preferred_element_type=F32)
        p = jnp.exp2(s - jnp.max(s, axis=0, keepdims=True))
        l = jnp.sum(p, axis=0, keepdims=True)
        out = (jnp.dot(vt, p.astype(BF16), preferred_element_type=F32) / l).T
        for r in range(rep):
            h = g * rep + r
            o_ref[:, h * HEAD_DIM:(h + 1) * HEAD_DIM] = (
                out[r * tq:(r + 1) * tq, g * HEAD_DIM:(g + 1) * HEAD_DIM].astype(o_ref.dtype))


def _values_t(v):
    return jnp.swapaxes(v, 1, 2).astype(BF16)


def attention(q, k, vt, *, row0, seq_len, tq):
    qc = q.shape[1]
    b, lk, kc = k.shape
    per = seq_len // tq
    base = row0 // tq
    return pl.pallas_call(
        functools.partial(_attention_kernel, tq=tq),
        out_shape=jax.ShapeDtypeStruct((b * seq_len, N_HEADS * HEAD_DIM), BF16),
        grid=(b, per),
        in_specs=[pl.BlockSpec((tq, qc), lambda s, i: (base + s * per + i, 0)),
                  pl.BlockSpec((1, lk, kc), lambda s, i: (s, 0, 0)),
                  pl.BlockSpec((1, kc, lk), lambda s, i: (s, 0, 0))],
        out_specs=pl.BlockSpec((tq, N_HEADS * HEAD_DIM), lambda s, i: (s * per + i, 0)),
        compiler_params=_cp(("arbitrary", "arbitrary")),
        name="attention",
    )(q, k, vt)


def _merge_kernel(x_ref, yhy_c, yhy_l, yf_c, yf_l, yb_c, yb_l, z_ref, yatt_c, yatt_l, gate_ref, snw_ref, whb_ref,
                  wsb_ref, wab_ref, wout_ref, g1_ref, n2w_ref, sh2_ref, sc2_ref, wr_ref, br_ref, ltri_ref,
                  xo_ref, h2_ref, ti_ref, tw_ref, hist_ref, *, n_ctx_tiles):
    d = x_ref.shape[1]
    is_ctx = pl.program_id(0) < n_ctx_tiles
    pick = lambda c_ref, l_ref: jnp.where(is_ctx, c_ref[...], l_ref[...])
    ys = (pick(yf_c, yf_l) + pick(yb_c, yb_l)) * _silu(z_ref[...])
    ys = ys * lax.rsqrt(jnp.mean(ys * ys, axis=-1, keepdims=True) + EPS) * snw_ref[...]
    gate = gate_ref[...].astype(F32)
    merged = (gate[:, :d] * jnp.dot(pick(yhy_c, yhy_l).astype(BF16), whb_ref[...], preferred_element_type=F32)
              + gate[:, d:2 * d] * jnp.dot(ys.astype(BF16), wsb_ref[...], preferred_element_type=F32)
              + gate[:, 2 * d:] * jnp.dot(pick(yatt_c, yatt_l), wab_ref[...], preferred_element_type=F32))
    mix = jnp.dot(merged.astype(BF16), wout_ref[...], preferred_element_type=F32)
    x = x_ref[...] + g1_ref[0] * mix
    xo_ref[...] = x
    h = x * lax.rsqrt(jnp.mean(x * x, axis=-1, keepdims=True) + EPS) * n2w_ref[...]
    h = h * (1.0 + sc2_ref[0]) + sh2_ref[0]
    _store_tiled(h2_ref, h)
    logits = _dot3(h, wr_ref[...]) + br_ref[...]
    lane = lax.broadcasted_iota(jnp.int32, logits.shape, 1)
    work = jnp.where(lane < N_EXPERTS, logits, -jnp.inf)
    idx_out = jnp.zeros(logits.shape, jnp.int32)
    val_out = jnp.full(logits.shape, -jnp.inf, F32)
    picks = []
    for j in range(TOP_K):
        mx = jnp.max(work, axis=-1, keepdims=True)
        am = jnp.min(jnp.where(work == mx, lane, LANES), axis=-1, keepdims=True)
        idx_out = jnp.where(lane == j, am, idx_out)
        val_out = jnp.where(lane == j, mx, val_out)
        picks.append(lane == am)
        work = jnp.where(picks[-1], -jnp.inf, work)
    e = jnp.exp(val_out - jnp.max(val_out, axis=-1, keepdims=True))
    tw_ref[...] = e / jnp.sum(e, axis=-1, keepdims=True)
    chosen = jnp.where(picks[0] | picks[1] | picks[2] | picks[3], 1.0, 0.0)
    before = jnp.dot(ltri_ref[...], chosen.astype(BF16), preferred_element_type=F32)
    rank_out = jnp.zeros(logits.shape, F32)
    for j in range(TOP_K):
        rank_j = jnp.sum(jnp.where(picks[j], before, 0.0), axis=-1, keepdims=True)
        rank_out = jnp.where(lane == TOP_K + j, rank_j, rank_out)
    ti_ref[...] = idx_out + rank_out.astype(jnp.int32)
    hist_ref[...] = jnp.broadcast_to(jnp.sum(chosen, axis=0, keepdims=True), hist_ref.shape)


def merge_router(x, y_hy, yf, yb, proj, y_att, gate, ssd_norm_w, whb, wsb, wab, wout, mods, n2w, wr, br,
                 *, t_ctx, l_lat, tm=512):
    t, d = x.shape
    nct = t_ctx // tm
    nlt = (t - t_ctx) // tm
    row = lambda tc, c0=0: pl.BlockSpec((tm, tc), lambda i: (i, c0))
    full = lambda shape: pl.BlockSpec(shape, lambda i: (0, 0))
    pair = lambda tc: [pl.BlockSpec((tm, tc), lambda i: (jnp.minimum(i, nct - 1), 0)),
                       pl.BlockSpec((tm, tc), lambda i: (jnp.clip(i - nct, 0, nlt - 1), 0))]
    ltri = jnp.asarray(np.tril(np.ones((tm, tm)), -1), BF16)
    return pl.pallas_call(
        functools.partial(_merge_kernel, n_ctx_tiles=nct),
        out_shape=(jax.ShapeDtypeStruct((t, d), F32), jax.ShapeDtypeStruct((t * ROW_SUB, LANES), F32),
                   jax.ShapeDtypeStruct((t, LANES), jnp.int32), jax.ShapeDtypeStruct((t, LANES), F32),
                   jax.ShapeDtypeStruct((t // tm * 8, LANES), F32)),
        grid=(t // tm,),
        in_specs=[row(d)] + pair(HY_DIM) + pair(SSD_INNER) + pair(SSD_INNER) + [row(SSD_INNER, P_Z // SSD_INNER)]
                 + pair(N_HEADS * HEAD_DIM) + [row(3 * d), full((1, SSD_INNER)),
                  full(whb.shape), full(wsb.shape), full(wab.shape), full(wout.shape),
                  _mod_spec(2, tm, t_ctx, l_lat), full((1, d)), _mod_spec(3, tm, t_ctx, l_lat),
                  _mod_spec(4, tm, t_ctx, l_lat), full((d, LANES)), full((1, LANES)), full((tm, tm))],
        out_specs=(row(d), pl.BlockSpec((tm * ROW_SUB, LANES), lambda i: (i, 0)), row(LANES), row(LANES),
                   pl.BlockSpec((8, LANES), lambda i: (i, 0))),
        compiler_params=_cp(("arbitrary",)),
        name="merge_router",
    )(x, *y_hy, *yf, *yb, proj, *y_att, gate, ssd_norm_w.reshape(1, -1), whb, wsb, wab, wout, mods, n2w, mods, mods,
      wr, br, ltri)


DEINT_BLOCK = 256


def _deinterleave(w_ref, perm_ref, g_ref, u_ref):
    half = DEINT_BLOCK // 2
    for blk in range(w_ref.shape[2] // DEINT_BLOCK):
        wb = w_ref[0, :, blk * DEINT_BLOCK:(blk + 1) * DEINT_BLOCK].astype(BF16)
        r = jnp.dot(wb, perm_ref[...], preferred_element_type=F32)
        g_ref[:, blk * half:(blk + 1) * half] = r[:, :half].astype(BF16)
        u_ref[:, blk * half:(blk + 1) * half] = r[:, half:].astype(BF16)


def _deinterleave_perm():
    perm = np.zeros((DEINT_BLOCK, DEINT_BLOCK), np.float32)
    half = DEINT_BLOCK // 2
    perm[2 * np.arange(half), np.arange(half)] = 1.0
    perm[2 * np.arange(half) + 1, half + np.arange(half)] = 1.0
    return jnp.asarray(perm, BF16)


DMA_CHUNK = 1024
DMA_UNROLL = 8
ROW_SUB = 8


def _tok_rows(t):
    return pl.ds(pl.multiple_of(t * ROW_SUB, ROW_SUB), ROW_SUB)


def _chunk_wait(ref, sem):
    n = DMA_CHUNK * ROW_SUB
    pltpu.make_async_copy(ref.at[pl.ds(0, n)], ref.at[pl.ds(0, n)], sem).wait()


def _issue_rows(copy_of):
    def body(it, carry):
        slot = it & (TOP_K - 1)
        tok0 = pl.multiple_of(lax.shift_right_logical(it, TOP_K_SHIFT) * DMA_UNROLL, DMA_UNROLL)
        for u in range(DMA_UNROLL):
            copy_of(slot, tok0, u).start(priority=u % 2)
        return carry
    lax.fori_loop(0, DMA_CHUNK // DMA_UNROLL, body, 0)


def _dispatch_kernel(zf_ref, idx_ref, x_ref, dst_ref, zbuf_ref, sem, zsem):
    tile_rows = zbuf_ref.shape[0]

    @pl.when(pl.program_id(0) == 0)
    def _():
        zbuf_ref[...] = jnp.zeros_like(zbuf_ref)
        fill = lambda i: pltpu.make_async_copy(
            zbuf_ref, dst_ref.at[pl.ds(pl.multiple_of(i * tile_rows, tile_rows), tile_rows)], zsem)

        def start(i, carry):
            @pl.when(zf_ref[i] == 1)
            def _():
                fill(i).start()
            return carry

        def wait(i, carry):
            @pl.when(zf_ref[i] == 1)
            def _():
                fill(i).wait()
            return carry

        lax.fori_loop(0, zf_ref.shape[0], start, 0)
        lax.fori_loop(0, zf_ref.shape[0], wait, 0)

    def copy_of(slot, tok0, u):
        j = idx_ref[0, 0, (tok0 + u) * TOP_K + slot]
        return pltpu.make_async_copy(x_ref.at[_tok_rows(tok0 + u)], dst_ref.at[_tok_rows(j)], sem)

    _issue_rows(copy_of)
    _chunk_wait(dst_ref, sem)


def moe_dispatch(dest, zero_tile, h2, n_dst):
    n = dest.shape[0]
    nc = n // DMA_CHUNK
    rows = DMA_CHUNK // TOP_K * ROW_SUB
    shape = (n_dst * ROW_SUB, LANES)
    return pl.pallas_call(
        _dispatch_kernel,
        out_shape=jax.ShapeDtypeStruct(shape, h2.dtype),
        grid_spec=pltpu.PrefetchScalarGridSpec(
            num_scalar_prefetch=1,
            grid=(nc,),
            in_specs=[pl.BlockSpec((1, 1, DMA_CHUNK), lambda c, zf: (c, 0, 0), memory_space=pltpu.SMEM),
                      pl.BlockSpec((rows, LANES), lambda c, zf: (c, 0))],
            out_specs=pl.BlockSpec(memory_space=pl.ANY),
            scratch_shapes=[pltpu.VMEM((MOE_TILE * ROW_SUB, LANES), h2.dtype), pltpu.SemaphoreType.DMA(()),
                            pltpu.SemaphoreType.DMA(())]),
        compiler_params=_cp(("arbitrary",)),
        name="moe_dispatch",
    )(zero_tile, dest.reshape(nc, 1, DMA_CHUNK), h2)


def _load_tiled(ref, n_tok, first_tok=0):
    return jnp.concatenate([ref[pl.ds(first_tok * ROW_SUB + s, n_tok, stride=ROW_SUB), :] for s in range(ROW_SUB)],
                           axis=1)


def _store_tiled(ref, x):
    for s in range(ROW_SUB):
        ref[pl.ds(s, x.shape[0], stride=ROW_SUB), :] = x[:, s * LANES:(s + 1) * LANES]


def _moe_kernel(te_ref, nv_ref, x_ref, w1_ref, perm_ref, b1g_ref, b1u_ref, w2_ref, b2_ref, o_ref,
                w1g_s, w1u_s, w2_s):
    i = pl.program_id(0)
    valid = i < nv_ref[0]
    new_expert = jnp.logical_or(i == 0, te_ref[i] != te_ref[jnp.maximum(i - 1, 0)])

    @pl.when(jnp.logical_and(valid, new_expert))
    def _():
        _deinterleave(w1_ref, perm_ref, w1g_s, w1u_s)
        w2_s[...] = w2_ref[0].astype(BF16)

    @pl.when(valid)
    def _():
        x = _load_tiled(x_ref, MOE_TILE).astype(BF16)
        gate = jnp.dot(x, w1g_s[...], preferred_element_type=F32) + b1g_ref[0]
        up = jnp.dot(x, w1u_s[...], preferred_element_type=F32) + b1u_ref[0]
        gate = jnp.minimum(gate, SWIGLU_LIMIT)
        up = jnp.clip(up, -SWIGLU_LIMIT, SWIGLU_LIMIT)
        act = (up + 1.0) * (gate * _sigmoid(SWIGLU_ALPHA * gate))
        _store_tiled(o_ref, jnp.dot(act.astype(BF16), w2_s[...], preferred_element_type=F32) + b2_ref[0])

    @pl.when(jnp.logical_not(valid))
    def _():
        o_ref[...] = jnp.zeros_like(o_ref)


def moe_experts(xs, tile_expert, n_valid, w1, b1g, b1u, w2, b2, *, layer):
    tm = MOE_TILE * ROW_SUB
    d, ff2 = w1.shape[1:]
    ff = ff2 // 2
    n_tiles = xs.shape[0] // tm
    rows = lambda i, te, nv: (jnp.minimum(i, nv[0] - 1), 0)
    wsel = lambda i, te, nv: (layer * N_EXPERTS + te[i], 0, 0)
    return pl.pallas_call(
        _moe_kernel,
        out_shape=jax.ShapeDtypeStruct(xs.shape, F32),
        grid_spec=pltpu.PrefetchScalarGridSpec(
            num_scalar_prefetch=2,
            grid=(n_tiles,),
            in_specs=[pl.BlockSpec((tm, LANES), rows),
                      pl.BlockSpec((1, d, ff2), wsel),
                      pl.BlockSpec((DEINT_BLOCK, DEINT_BLOCK), lambda i, te, nv: (0, 0)),
                      pl.BlockSpec((1, 1, ff), wsel), pl.BlockSpec((1, 1, ff), wsel),
                      pl.BlockSpec((1, ff, d), wsel), pl.BlockSpec((1, 1, d), wsel)],
            out_specs=pl.BlockSpec((tm, LANES), lambda i, te, nv: (i, 0)),
            scratch_shapes=[pltpu.VMEM((d, ff), BF16), pltpu.VMEM((d, ff), BF16), pltpu.VMEM((ff, d), BF16)]),
        compiler_params=_cp(("arbitrary",)),
        name="moe_experts",
    )(tile_expert, n_valid, xs, w1, _deinterleave_perm(), b1g, b1u, w2, b2)


def _combine_kernel(idx_ref, nxt_ref, x_ref, rows_ref, tw_ref, g2_ref, y_ref, buf_ref, sems):
    i = pl.program_id(0)
    slot = i % 2
    tm = x_ref.shape[0]

    def gather(ids_ref, s):
        def copy_of(slot, tok0, u):
            j = ids_ref[0, 0, (tok0 + u) * TOP_K + slot]
            return pltpu.make_async_copy(rows_ref.at[_tok_rows(j)], buf_ref.at[s, _tok_rows(slot * tm + tok0 + u)],
                                         sems.at[s])
        _issue_rows(copy_of)

    @pl.when(i == 0)
    def _():
        gather(idx_ref, 0)

    @pl.when(i + 1 < pl.num_programs(0))
    def _():
        gather(nxt_ref, 1 - slot)

    _chunk_wait(rows_ref, sems.at[slot])
    tw = tw_ref[...]
    cur = buf_ref.at[slot]
    acc = tw[:, 0:1] * _load_tiled(cur, tm, 0)
    for s in range(1, TOP_K):
        acc = acc + tw[:, s:s + 1] * _load_tiled(cur, tm, s * tm)
    y_ref[...] = x_ref[...] + g2_ref[0] * acc


def moe_combine(x, dest, rows, top_w, mods, *, t_ctx, l_lat):
    t, d = x.shape
    tm = DMA_CHUNK // TOP_K
    nt = t // tm
    ids = dest.reshape(nt, 1, DMA_CHUNK)
    smem = lambda fn: pl.BlockSpec((1, 1, DMA_CHUNK), fn, memory_space=pltpu.SMEM)
    return pl.pallas_call(
        _combine_kernel,
        out_shape=jax.ShapeDtypeStruct((t, d), F32),
        grid=(nt,),
        in_specs=[smem(lambda i: (i, 0, 0)), smem(lambda i: (jnp.minimum(i + 1, nt - 1), 0, 0)),
                  pl.BlockSpec((tm, d), lambda i: (i, 0)), pl.BlockSpec(memory_space=pl.ANY),
                  pl.BlockSpec((tm, LANES), lambda i: (i, 0)), _mod_spec(5, tm, t_ctx, l_lat)],
        out_specs=pl.BlockSpec((tm, d), lambda i: (i, 0)),
        scratch_shapes=[pltpu.VMEM((2, DMA_CHUNK * ROW_SUB, LANES), F32), pltpu.SemaphoreType.DMA((2,))],
        compiler_params=_cp(("arbitrary",)),
        name="moe_combine",
    )(ids, ids, x, rows, top_w, mods)


def _dispatch_plan(top_ir, hist, tm):
    t = top_ir.shape[0]
    n_rt = hist.shape[0]
    hist = hist.astype(jnp.int32)
    counts = jnp.sum(hist, axis=0)
    tiles = (counts + tm - 1) // tm
    tile_end = jnp.cumsum(tiles)
    base = (tile_end - tiles)[None, :] * tm + jnp.cumsum(hist, axis=0) - hist
    e = top_ir[:, :TOP_K].reshape(n_rt, -1)
    rank = top_ir[:, TOP_K:].reshape(n_rt, -1)
    pick = e[:, :, None] == jnp.arange(N_EXPERTS, dtype=jnp.int32)[None, None, :]
    dest = (jnp.sum(jnp.where(pick, base[:, None, :], 0), axis=2) + rank).reshape(-1).astype(jnp.int32)
    n_tiles = (t * TOP_K) // tm + N_EXPERTS
    tile_ids = jnp.arange(n_tiles, dtype=jnp.int32)
    tile_expert = jnp.minimum(jnp.sum((tile_ids[:, None] >= tile_end[None, :]).astype(jnp.int32), axis=1),
                              N_EXPERTS - 1).astype(jnp.int32)
    n_valid = tile_end[-1]
    partial = jnp.logical_or(tile_ids + 1 == tile_end[tile_expert], tile_ids >= n_valid)
    return dest, n_tiles, tile_expert, n_valid.reshape(1).astype(jnp.int32), partial.astype(jnp.int32)


def kernel(x_prompt, x_sample, c, cache_k, cache_v, state_ssd, c_ctx, norm1_w, norm2_w, w_mod, b_mod, w_in, w_gate,
           b_gate, hy_conv_w, hy_conv_b, hy_w1, hy_b1, hy_w2, hy_b2, hy_w3, hy_freq, hy_decay, hy_bias, ssd_conv_w,
           ssd_conv_b, ssd_a_log, ssd_dt_bias, ssd_d, ssd_norm_w, q_norm_w, k_norm_w, w_br_hy, w_br_ssd, w_br_att,
           w_out, w_router, b_router, w_e1, b_e1, w_e2, b_e2):
    n_ctx, l_ctx, d = x_prompt.shape
    n_lat, l_lat, _ = x_sample.shape
    depth = w_in.shape[0]
    t_ctx, t_lat = n_ctx * l_ctx, n_lat * l_lat
    t = t_ctx + t_lat
    kc = N_KV_HEADS * HEAD_DIM
    geo = dict(t_ctx=t_ctx, l_lat=l_lat)

    x = jnp.concatenate([x_prompt.reshape(t_ctx, d), x_sample.reshape(t_lat, d)], axis=0)
    cvec = jnp.zeros((8, d), F32).at[0].set(c_ctx).at[1:1 + n_lat].set(c)
    mods_all = modulation_all(cvec, w_mod, b_mod)
    b1_flat = b_e1.reshape(-1, 1, b_e1.shape[-1])
    b1g_all, b1u_all = b1_flat[:, :, 0::2], b1_flat[:, :, 1::2]
    new_k, new_v, new_s = [], [], []
    for l in range(depth):
        mods = mods_all[l].reshape(8 * 6, 1, d)
        wi = w_in[l]
        w_proj = jnp.concatenate([wi[:, 0:3072], wi[:, 3088:3856], wi[:, 3072:3088],
                                  jnp.zeros((d, P_COLS - 3856), F32)], axis=1).astype(BF16)
        nw1 = norm1_w[l].reshape(1, d)
        proj = norm_mod_matmul(x, nw1, mods, w_proj, jnp.zeros((1, P_COLS), F32), sigmoid=False, out_dtype=F32,
                               **geo)
        gate = norm_mod_matmul(x, nw1, mods, w_gate[l].astype(BF16), b_gate[l].reshape(1, -1), sigmoid=True,
                               out_dtype=BF16, **geo)

        x0, p = hyena_pre(proj, hy_conv_w[l], hy_conv_b[l], t_ctx=t_ctx, l_ctx=l_ctx, l_lat=l_lat)
        hy_args = (hy_w1[l], hy_b1[l], hy_w2[l], hy_b2[l], hy_w3[l], hy_freq[l], hy_decay[l])
        y_hy_ctx = hyena_ctx(x0, p, hyena_filter(l_ctx, *hy_args), hy_bias[l], n_seq=n_ctx, seq_len=l_ctx)
        y_hy_lat = hyena_lat(x0[t_ctx:], p[t_ctx:], hyena_filter(l_lat, *hy_args), hy_bias[l], n_seq=n_lat,
                             seq_len=l_lat)

        ssd_args = (ssd_conv_w[l], ssd_conv_b[l], ssd_dt_bias[l], ssd_a_log[l], ssd_d[l])
        yf_c, yb_c, fin_c = ssd_scan(proj, None, *ssd_args, row0=0, n_seq=n_ctx, seq_len=l_ctx)
        yf_l, yb_l, _ = ssd_scan(proj, state_ssd[:, l], *ssd_args, row0=t_ctx, n_seq=n_lat, seq_len=l_lat)

        qn, kn = qk_prep(proj, q_norm_w[l], k_norm_w[l], **geo)
        v_all = proj[:, P_V:P_V + kc]
        k_ctx = kn[:t_ctx].reshape(n_ctx, l_ctx, kc)
        v_ctx = v_all[:t_ctx].reshape(n_ctx, l_ctx, kc)
        att_ctx = attention(qn, k_ctx.astype(BF16), _values_t(v_ctx), row0=0, seq_len=l_ctx, tq=l_ctx)
        k_lat = jnp.concatenate([kn[t_ctx:].reshape(n_lat, l_lat, kc), cache_k[:, l].reshape(n_lat, -1, kc)], axis=1)
        v_lat = jnp.concatenate([v_all[t_ctx:].reshape(n_lat, l_lat, kc), cache_v[:, l].reshape(n_lat, -1, kc)],
                                axis=1)
        att_lat = attention(qn, k_lat.astype(BF16), _values_t(v_lat), row0=t_ctx, seq_len=l_lat, tq=128)

        wr = jnp.pad(w_router[l], ((0, 0), (0, LANES - N_EXPERTS)))
        br = jnp.pad(b_router[l], (0, LANES - N_EXPERTS)).reshape(1, LANES)
        x, h2, top_ir, top_w, hist = merge_router(
            x, (y_hy_ctx, y_hy_lat), (yf_c, yf_l), (yb_c, yb_l), proj, (att_ctx, att_lat), gate, ssd_norm_w[l],
            w_br_hy[l].astype(BF16), w_br_ssd[l].astype(BF16),
            w_br_att[l].astype(BF16), w_out[l].astype(BF16), mods, norm2_w[l].reshape(1, d), wr, br, **geo)

        dest, n_tiles, tile_expert, n_valid, partial = _dispatch_plan(
            top_ir[:, :2 * TOP_K], hist[::8, :N_EXPERTS], MOE_TILE)
        xs = moe_dispatch(dest, partial, h2, n_tiles * MOE_TILE)
        out_sorted = moe_experts(
            xs, tile_expert, n_valid, w_e1.reshape((-1,) + w_e1.shape[2:]), b1g_all, b1u_all,
            w_e2.reshape((-1,) + w_e2.shape[2:]), b_e2.reshape(-1, 1, d), layer=l)
        x = moe_combine(x, dest, out_sorted, top_w, mods, **geo)

        new_k.append(k_ctx.reshape(n_ctx, l_ctx, N_KV_HEADS, HEAD_DIM))
        new_v.append(v_ctx.reshape(n_ctx, l_ctx, N_KV_HEADS, HEAD_DIM))
        new_s.append(fin_c)

    y_prompt = x[:t_ctx].reshape(n_ctx, l_ctx, d)
    y_sample = x[t_ctx:].reshape(n_lat, l_lat, d)
    return (y_prompt, y_sample, jnp.stack(new_k, axis=1), jnp.stack(new_v, axis=1), jnp.stack(new_s, axis=1))
```
